```python
import math
import numpy as np
import jax
import jax.numpy as jnp
from jax import lax

D_MODEL = 1024
BATCH = 8
SEQ = 2048
DEPTH = 4
DEC_BATCH = 128
DEC_SEQ = 1
PAST_LEN = 2048
PAGE_SIZE = 128

N_A_LAYERS = DEPTH // 2
N_B_LAYERS = DEPTH - N_A_LAYERS
MIX_W = 3 * D_MODEL // 4
MEM_W = D_MODEL - MIX_W
HEAD_DIM = 64
MEM_HEADS = MEM_W // HEAD_DIM
MEM_TOKENS = 256
MLSTM_HEADS = 4
MLSTM_DH = MIX_W // MLSTM_HEADS
MLSTM_CHUNK = 64
NSA_HEADS = MIX_W // HEAD_DIM
NSA_KV_HEADS = 2
NSA_GROUP = NSA_HEADS // NSA_KV_HEADS
CMP_LEN = 32
CMP_STRIDE = 16
CMP_HIDDEN = 128
SLC_BLOCK = 64
N_SELECT = 8
WINDOW = 512
WIN_QBLOCK = 128
SLC_QBLOCK = 64
REL_BUCKETS = 32
REL_MAX_EXACT = 16
REL_MAX_DIST = 128
D_FF = -(-(8 * D_MODEL) // (3 * 256)) * 256
A_COLS = 4 * MIX_W + 2 * MLSTM_HEADS + MEM_W
B_COLS = MIX_W + 3 * NSA_HEADS + MEM_W
KV_COLS = 3 * 2 * NSA_KV_HEADS * HEAD_DIM
EPS = 1e-6
NEG = -1e30
FORCED = 1e6

kernel_name = 'yoco_mlstm_nsa_decode_step'


def rmsnorm(x, g):
    xf = x.astype(jnp.float32)
    y = xf * lax.rsqrt(jnp.mean(xf * xf, axis=-1, keepdims=True) + EPS)
    return (y * g.astype(jnp.float32)).astype(x.dtype)


def masked_softmax(s, mask):
    p = jax.nn.softmax(jnp.where(mask, s, NEG), axis=-1)
    return jnp.where(mask, p, 0.0)


def rel_bucket(dist):
    d = jnp.maximum(dist, 0)
    ratio = jnp.maximum(d, REL_MAX_EXACT).astype(jnp.float32) / REL_MAX_EXACT
    large = REL_MAX_EXACT + (jnp.log(ratio) / math.log(REL_MAX_DIST / REL_MAX_EXACT)
                             * (REL_BUCKETS - REL_MAX_EXACT)).astype(jnp.int32)
    return jnp.where(d < REL_MAX_EXACT, d, jnp.minimum(large, REL_BUCKETS - 1))


def swiglu(h, w_gu, w_down):
    gu = h @ w_gu
    return (jax.nn.silu(gu[..., :D_FF]) * gu[..., D_FF:]) @ w_down


def mem_kv_rows(mem, w, g_k):
    b, m, _ = mem.shape
    kv = (mem @ w).reshape(b, m, 2, MEM_HEADS, HEAD_DIM)
    return jnp.stack([rmsnorm(kv[:, :, 0], g_k), kv[:, :, 1]], axis=2)


def mem_attn(pm, mkv, g_q):
    b, s, _ = pm.shape
    q = rmsnorm(pm.reshape(b, s, MEM_HEADS, HEAD_DIM), g_q)
    sc = jnp.einsum('bshd,bmhd->bhsm', q, mkv[:, :, 0], preferred_element_type=jnp.float32) * HEAD_DIM ** -0.5
    p = jax.nn.softmax(sc, axis=-1)
    o = jnp.einsum('bhsm,bmhd->bshd', p.astype(mkv.dtype), mkv[:, :, 1])
    return o.reshape(b, s, MEM_W).astype(pm.dtype)


def mlstm_chunkwise(q, k, v, i_pre, logf, c0, n0, m0):
    b, s, h, d = q.shape
    L = MLSTM_CHUNK if s % MLSTM_CHUNK == 0 else s
    nc = s // L

    def chunks(a):
        return jnp.swapaxes(a.reshape(b, nc, L, *a.shape[2:]), 0, 1)

    causal = jnp.tril(jnp.ones((L, L), dtype=bool))

    def step(carry, xs):
        c, n, m = carry
        qc, kc, vc, ic, fc = xs
        bcum = jnp.swapaxes(jnp.cumsum(fc, axis=1), 1, 2)
        ig = jnp.swapaxes(ic, 1, 2)
        log_d = jnp.where(causal, bcum[..., :, None] - bcum[..., None, :] + ig[..., None, :], -jnp.inf)
        log_inter = bcum + m[..., None]
        m_t = jnp.maximum(jnp.max(log_d, axis=-1), log_inter)
        w_intra = jnp.exp(log_d - m_t[..., None])
        w_inter = jnp.exp(log_inter - m_t)
        sc = jnp.einsum('bthd,bshd->bhts', qc, kc) * w_intra
        num = jnp.einsum('bhts,bshd->bhtd', sc, vc) + w_inter[..., None] * jnp.einsum('bthd,bhde->bhte', qc, c)
        den = jnp.sum(sc, axis=-1) + w_inter * jnp.einsum('bthd,bhd->bht', qc, n)
        hh = num / jnp.maximum(jnp.abs(den), jnp.exp(-m_t))[..., None]
        g = bcum[..., -1]
        m_new = m_t[..., -1]
        w_s = jnp.exp(g[..., None] - bcum + ig - m_new[..., None])
        decay = jnp.exp(g + m - m_new)
        c_new = decay[..., None, None] * c + jnp.einsum('bhs,bshd,bshe->bhde', w_s, kc, vc)
        n_new = decay[..., None] * n + jnp.einsum('bhs,bshd->bhd', w_s, kc)
        return (c_new, n_new, m_new), jnp.swapaxes(hh, 1, 2)

    (c, n, m), hs = lax.scan(step, (c0, n0, m0), tuple(chunks(a) for a in (q, k, v, i_pre, logf)))
    return jnp.swapaxes(hs, 0, 1).reshape(b, s, h, d), (c, n, m)


def mlstm_mixer(p, state, b_if, g_hnorm):
    b, s, _ = p.shape
    f32 = jnp.float32

    def heads(a):
        return a.reshape(b, s, MLSTM_HEADS, MLSTM_DH).astype(f32)

    q = heads(p[..., :MIX_W])
    k = heads(p[..., MIX_W:2 * MIX_W]) * MLSTM_DH ** -0.5
    v = heads(p[..., 2 * MIX_W:3 * MIX_W])
    o = p[..., 3 * MIX_W:4 * MIX_W].astype(f32)
    gates = p[..., 4 * MIX_W:4 * MIX_W + 2 * MLSTM_HEADS].astype(f32) + b_if.astype(f32)
    i_pre = gates[..., :MLSTM_HEADS]
    logf = jax.nn.log_sigmoid(gates[..., MLSTM_HEADS:])
    c0, n0, m0 = (a.astype(f32) for a in state)
    h, new_state = mlstm_chunkwise(q, k, v, i_pre, logf, c0, n0, m0)
    h = rmsnorm(h, g_hnorm).reshape(b, s, MIX_W)
    return (jax.nn.sigmoid(o) * h).astype(p.dtype), new_state


def kv_rows(x, g_kv, w_kv, g_k_slc, g_k_win):
    b, s, _ = x.shape
    kv = (rmsnorm(x, g_kv) @ w_kv).reshape(b, s, 3, 2, NSA_KV_HEADS, HEAD_DIM)
    cmp = kv[:, :, 0]
    slc = jnp.stack([rmsnorm(kv[:, :, 1, 0], g_k_slc), kv[:, :, 1, 1]], axis=2)
    win = jnp.stack([rmsnorm(kv[:, :, 2, 0], g_k_win), kv[:, :, 2, 1]], axis=2)
    return cmp, slc, win


def compress(cmp_kv, w1, b1, w2, b2, pos_enc, g_k_cmp):
    b, t = cmp_kv.shape[:2]
    r_n = CMP_LEN // CMP_STRIDE
    nc = (t - CMP_LEN) // CMP_STRIDE + 1
    nseg = nc + r_n - 1
    seg = cmp_kv[:, :nseg * CMP_STRIDE].reshape(b, nseg, CMP_STRIDE, 2, NSA_KV_HEADS, HEAD_DIM)
    w1r = w1.reshape(2, r_n, CMP_STRIDE, HEAD_DIM, CMP_HIDDEN)
    pe = jnp.einsum('pcd,cpdh->ch', pos_enc, w1)
    pre = (b1 + pe)[None, None, :, None, :]
    for r in range(r_n):
        pre = pre + jnp.einsum('bnucgd,cudh->bncgh', seg[:, r:r + nc], w1r[:, r])
    out = jnp.einsum('bncgh,chd->bncgd', jax.nn.gelu(pre), w2) + b2[None, None, :, None, :]
    ck = rmsnorm(out[:, :, 0], g_k_cmp)
    cv = out[:, :, 1]
    c_end = jnp.arange(nc) * CMP_STRIDE + CMP_LEN - 1
    return ck, cv, c_end


def cmp_to_slc_matrix(n_cmp, n_slc):
    c0 = np.arange(n_cmp)[:, None] * CMP_STRIDE
    s0 = np.arange(n_slc)[None, :] * SLC_BLOCK
    ov = np.clip(np.minimum(c0 + CMP_LEN, s0 + SLC_BLOCK) - np.maximum(c0, s0), 0, None)
    return jnp.asarray(ov / CMP_STRIDE, dtype=jnp.float32)


def cmp_branch(q, qpos, ck, cv, c_end, table):
    sc = jnp.einsum('bsgzd,bngd->bsgzn', q, ck, preferred_element_type=jnp.float32) * HEAD_DIM ** -0.5
    dist = qpos[:, None] - c_end[None, :]
    bias = jnp.transpose(table[rel_bucket(dist)], (0, 2, 3, 1))
    p = masked_softmax(sc + bias[None], (dist >= 0)[:, None, None, :])
    o = jnp.einsum('bsgzn,bngd->bsgzd', p.astype(cv.dtype), cv)
    return o, p


def select_blocks(p_cmp, qpos, n_slc):
    imp = jnp.einsum('bsgzn,nj->bsgj', p_cmp, cmp_to_slc_matrix(p_cmp.shape[-1], n_slc))
    j = jnp.arange(n_slc)
    cur = (qpos // SLC_BLOCK)[:, None]
    valid = j <= cur
    forced = (j == 0) | (j == cur) | (j == cur - 1)
    score = jnp.where(forced[None, :, None], FORCED, jnp.where(valid[None, :, None], imp, -1.0))
    _, idx = lax.top_k(score, min(N_SELECT, n_slc))
    return idx


def slc_branch(q, qpos, idx, fetch, table):
    b, s = q.shape[:2]
    qb = SLC_QBLOCK if s % SLC_QBLOCK == 0 else s
    nb = s // qb
    g_i = jnp.arange(NSA_KV_HEADS)[None, None, :, None]

    def blk(args):
        qx, px, ix = args
        tok = (ix[..., None] * SLC_BLOCK + jnp.arange(SLC_BLOCK)).reshape(b, qb, NSA_KV_HEADS, -1)
        kx, vx = fetch(tok)
        sc = jnp.einsum('bsgzd,bsgjd->bsgzj', qx, kx, preferred_element_type=jnp.float32) * HEAD_DIM ** -0.5
        dist = px[None, :, None, None] - tok
        bias = jnp.swapaxes(table[rel_bucket(dist), g_i], -1, -2)
        p = masked_softmax(sc + bias, (dist >= 0)[:, :, :, None, :])
        return jnp.einsum('bsgzj,bsgjd->bsgzd', p.astype(vx.dtype), vx)

    qs = jnp.swapaxes(q.reshape(b, nb, qb, NSA_KV_HEADS, NSA_GROUP, HEAD_DIM), 0, 1)
    ids = jnp.swapaxes(idx.reshape(b, nb, qb, NSA_KV_HEADS, idx.shape[-1]), 0, 1)
    out = lax.map(blk, (qs, qpos.reshape(nb, qb), ids))
    return jnp.swapaxes(out, 0, 1).reshape(b, s, NSA_KV_HEADS, NSA_GROUP, HEAD_DIM)


def win_branch(q, q_pos0, wkv, wpos0, table):
    b, s = q.shape[:2]
    qb = WIN_QBLOCK if s % WIN_QBLOCK == 0 else s
    nb = s // qb
    lk = WINDOW + qb
    wpad = jnp.pad(wkv, ((0, 0), (WINDOW, 0), (0, 0), (0, 0), (0, 0)))
    starts = q_pos0 + jnp.arange(nb) * qb
    offs = jnp.arange(lk)
    kidx = starts[:, None] - wpos0 + offs[None, :]
    kpos = starts[:, None] - WINDOW + offs[None, :]

    def blk(args):
        qx, px, ki, kp = args
        kv = jnp.take(wpad, ki, axis=1)
        sc = jnp.einsum('bigzd,bjgd->bgzij', qx, kv[:, :, 0], preferred_element_type=jnp.float32) * HEAD_DIM ** -0.5
        dist = px[:, None] - kp[None, :]
        bias = jnp.transpose(table[rel_bucket(dist)], (2, 3, 0, 1))
        mask = (dist >= 0) & (dist < WINDOW) & (kp >= wpos0)[None, :]
        p = masked_softmax(sc + bias[None], mask)
        return jnp.einsum('bgzij,bjgd->bigzd', p.astype(kv.dtype), kv[:, :, 1])

    qs = jnp.swapaxes(q.reshape(b, nb, qb, NSA_KV_HEADS, NSA_GROUP, HEAD_DIM), 0, 1)
    qpos = (q_pos0 + jnp.arange(s)).reshape(nb, qb)
    out = lax.map(blk, (qs, qpos, kidx, kpos))
    return jnp.swapaxes(out, 0, 1).reshape(b, s, NSA_KV_HEADS, NSA_GROUP, HEAD_DIM)


def nsa_mixer(p, q_pos0, ctx, g_q, b_gate, rel_table):
    b, s, _ = p.shape
    q = rmsnorm(p[..., :MIX_W].reshape(b, s, NSA_HEADS, HEAD_DIM), g_q)
    q = q.reshape(b, s, NSA_KV_HEADS, NSA_GROUP, HEAD_DIM)
    gates = jax.nn.sigmoid(p[..., MIX_W:MIX_W + 3 * NSA_HEADS].astype(jnp.float32) + b_gate.astype(jnp.float32))
    gates = gates.reshape(b, s, 3, NSA_KV_HEADS, NSA_GROUP, 1)
    qpos = q_pos0 + jnp.arange(s)
    table = rel_table.astype(jnp.float32).reshape(REL_BUCKETS, NSA_KV_HEADS, NSA_GROUP)
    o_cmp, p_cmp = cmp_branch(q, qpos, ctx['ck'], ctx['cv'], ctx['c_end'], table)
    idx = select_blocks(p_cmp, qpos, ctx['n_slc'])
    o_slc = slc_branch(q, qpos, idx, ctx['fetch'], table)
    o_win = win_branch(q, q_pos0, ctx['wkv'], ctx['wpos0'], table)
    o = gates[:, :, 0] * o_cmp + gates[:, :, 1] * o_slc + gates[:, :, 2] * o_win
    return o.reshape(b, s, MIX_W).astype(p.dtype)


def run_trunk(x, q_pos0, mem_kv, init_states, build_ctx, prm):
    states = []
    ctx = None
    rows = None
    for l in range(DEPTH):
        h = rmsnorm(x, prm['g_mix'][l])
        if l < N_A_LAYERS:
            p = h @ prm['w_in_a'][l]
            mix, st = mlstm_mixer(p, init_states[l], prm['b_if'][l], prm['g_hnorm'][l])
            states.append(st)
        else:
            j = l - N_A_LAYERS
            p = h @ prm['w_in_b'][j]
            mix = nsa_mixer(p, q_pos0, ctx, prm['g_q'][j], prm['b_gate'][j], prm['rel_table'])
        mem_o = mem_attn(p[..., -MEM_W:], mem_kv[l], prm['g_mem_q'][l])
        x = x + jnp.concatenate([mix, mem_o], axis=-1) @ prm['w_out'][l]
        x = x + swiglu(rmsnorm(x, prm['g_ffn'][l]), prm['w_gu'][l], prm['w_down'][l])
        if l == N_A_LAYERS - 1:
            rows = kv_rows(x, prm['g_kv'], prm['w_kv'], prm['g_k_slc'], prm['g_k_win'])
            ctx = build_ctx(*rows)
    new_states = tuple(jnp.stack([st[i] for st in states]) for i in range(3))
    return x, new_states, rows


def setup_inputs(seed: int = 0) -> dict:
    key = jax.random.key(seed)
    keys = iter(jax.random.split(key, 48))

    def nrm(shape, scale=1.0):
        return jax.random.normal(next(keys), shape, jnp.float32) * scale

    def gain(shape):
        return 1.0 + nrm(shape, 0.02)

    n_pages = PAST_LEN // PAGE_SIZE
    n_pool = (DEC_BATCH * n_pages * 5 + 3) // 4
    wb = min(WINDOW, PAST_LEN)
    d_in = D_MODEL ** -0.5
    res = (2 * DEPTH) ** -0.5
    kvh, hd = NSA_KV_HEADS, HEAD_DIM
    page_table = jax.random.permutation(next(keys), n_pool)[:DEC_BATCH * n_pages].reshape(DEC_BATCH, n_pages).astype(jnp.int32)
    b_if = jnp.concatenate([nrm((N_A_LAYERS, MLSTM_HEADS), 0.1),
                            jnp.linspace(3.0, 6.0, MLSTM_HEADS, dtype=jnp.float32)[None, :] + nrm((N_A_LAYERS, MLSTM_HEADS), 0.1)], axis=-1)
    return {
        'x_prompt': nrm((BATCH, SEQ, D_MODEL)),
        'x_sample': nrm((DEC_BATCH, DEC_SEQ, D_MODEL)),
        'state_mlstm_c': nrm((N_A_LAYERS, DEC_BATCH, MLSTM_HEADS, MLSTM_DH, MLSTM_DH), 0.05),
        'state_mlstm_n': nrm((N_A_LAYERS, DEC_BATCH, MLSTM_HEADS, MLSTM_DH), 0.5),
        'state_mlstm_m': nrm((N_A_LAYERS, DEC_BATCH, MLSTM_HEADS), 1.0),
        'cache_mem_kv': nrm((DEPTH, DEC_BATCH, MEM_TOKENS, 2, MEM_HEADS, hd)),
        'cache_cmp_kv': nrm((n_pool, PAGE_SIZE, 2, kvh, hd)),
        'cache_slc_kv': nrm((n_pool, PAGE_SIZE, 2, kvh, hd)),
        'cache_win_kv': nrm((DEC_BATCH, wb, 2, kvh, hd)),
        'page_table': page_table,
        'mem_prompt': nrm((BATCH, MEM_TOKENS, D_MODEL)),
        'g_mix': gain((DEPTH, D_MODEL)),
        'w_in_a': nrm((N_A_LAYERS, D_MODEL, A_COLS), d_in),
        'b_if': b_if,
        'g_hnorm': gain((N_A_LAYERS, MLSTM_DH)),
        'w_in_b': nrm((N_B_LAYERS, D_MODEL, B_COLS), d_in),
        'g_q': gain((N_B_LAYERS, hd)),
        'b_gate': nrm((N_B_LAYERS, 3 * NSA_HEADS), 0.1),
        'rel_table': nrm((REL_BUCKETS, NSA_HEADS), 0.5),
        'g_kv': gain((D_MODEL,)),
        'w_kv': nrm((D_MODEL, KV_COLS), d_in),
        'g_k_slc': gain((hd,)),
        'g_k_win': gain((hd,)),
        'g_k_cmp': gain((hd,)),
        'cmp_pos': nrm((CMP_LEN, 2, hd), 0.1),
        'cmp_w1': nrm((2, CMP_LEN, hd, CMP_HIDDEN), (CMP_LEN * hd) ** -0.5),
        'cmp_b1': nrm((2, CMP_HIDDEN), 0.02),
        'cmp_w2': nrm((2, CMP_HIDDEN, hd), CMP_HIDDEN ** -0.5),
        'cmp_b2': nrm((2, hd), 0.02),
        'w_mem_kv': nrm((DEPTH, D_MODEL, 2 * MEM_W), d_in),
        'g_mem_k': gain((DEPTH, hd)),
        'g_mem_q': gain((DEPTH, hd)),
        'w_out': nrm((DEPTH, D_MODEL, D_MODEL), d_in * res),
        'g_ffn': gain((DEPTH, D_MODEL)),
        'w_gu': nrm((DEPTH, D_MODEL, 2 * D_FF), d_in),
        'w_down': nrm((DEPTH, D_FF, D_MODEL), D_FF ** -0.5 * res),
    }


def reference(x_prompt, x_sample, state_mlstm_c, state_mlstm_n, state_mlstm_m, cache_mem_kv,
              cache_cmp_kv, cache_slc_kv, cache_win_kv, page_table, mem_prompt,
              g_mix, w_in_a, b_if, g_hnorm, w_in_b, g_q, b_gate, rel_table, g_kv, w_kv,
              g_k_slc, g_k_win, g_k_cmp, cmp_pos, cmp_w1, cmp_b1, cmp_w2, cmp_b2,
              w_mem_kv, g_mem_k, g_mem_q, w_out, g_ffn, w_gu, w_down):
    prm = {'g_mix': g_mix, 'w_in_a': w_in_a, 'b_if': b_if, 'g_hnorm': g_hnorm, 'w_in_b': w_in_b,
           'g_q': g_q, 'b_gate': b_gate, 'rel_table': rel_table, 'g_kv': g_kv, 'w_kv': w_kv,
           'g_k_slc': g_k_slc, 'g_k_win': g_k_win, 'g_mem_q': g_mem_q, 'w_out': w_out,
           'g_ffn': g_ffn, 'w_gu': w_gu, 'w_down': w_down}
    g_i = jnp.arange(NSA_KV_HEADS)[None, None, :, None]

    b_p, s_p, _ = x_prompt.shape
    mem_kv_p = jnp.stack([mem_kv_rows(mem_prompt, w_mem_kv[l], g_mem_k[l]) for l in range(DEPTH)])
    init_p = [(jnp.zeros((b_p, MLSTM_HEADS, MLSTM_DH, MLSTM_DH), jnp.float32),
               jnp.zeros((b_p, MLSTM_HEADS, MLSTM_DH), jnp.float32),
               jnp.full((b_p, MLSTM_HEADS), NEG, jnp.float32)) for _ in range(N_A_LAYERS)]
    b_ip = jnp.arange(b_p)[:, None, None, None]

    def ctx_prompt(cmp_rows, slc_rows, win_rows):
        ck, cv, c_end = compress(cmp_rows, cmp_w1, cmp_b1, cmp_w2, cmp_b2, cmp_pos, g_k_cmp)

        def fetch(tok):
            kv = slc_rows[b_ip, jnp.minimum(tok, s_p - 1), :, g_i]
            return kv[..., 0, :], kv[..., 1, :]

        return {'ck': ck, 'cv': cv, 'c_end': c_end, 'n_slc': -(-s_p // SLC_BLOCK),
                'fetch': fetch, 'wkv': win_rows, 'wpos0': 0}

    y_p, st_p, rows_p = run_trunk(x_prompt, 0, mem_kv_p, init_p, ctx_prompt, prm)

    b_s, s_s, _ = x_sample.shape
    past = page_table.shape[1] * PAGE_SIZE
    wb = cache_win_kv.shape[1]
    init_s = [(state_mlstm_c[l], state_mlstm_n[l], state_mlstm_m[l]) for l in range(N_A_LAYERS)]
    b_is = jnp.arange(b_s)[:, None, None, None]

    def ctx_sample(cmp_new, slc_new, win_new):
        cmp_past = cache_cmp_kv[page_table].reshape(b_s, past, 2, NSA_KV_HEADS, HEAD_DIM)
        ck, cv, c_end = compress(jnp.concatenate([cmp_past, cmp_new], axis=1),
                                 cmp_w1, cmp_b1, cmp_w2, cmp_b2, cmp_pos, g_k_cmp)

        def fetch(tok):
            tp = jnp.minimum(tok, past - 1)
            phys = page_table[b_is, tp // PAGE_SIZE]
            kv_old = cache_slc_kv[phys, tp % PAGE_SIZE, :, g_i]
            kv_new = slc_new[b_is, jnp.clip(tok - past, 0, s_s - 1), :, g_i]
            kv = jnp.where((tok < past)[..., None, None], kv_old, kv_new)
            return kv[..., 0, :], kv[..., 1, :]

        return {'ck': ck, 'cv': cv, 'c_end': c_end, 'n_slc': -(-(past + s_s) // SLC_BLOCK),
                'fetch': fetch, 'wkv': jnp.concatenate([cache_win_kv, win_new], axis=1), 'wpos0': past - wb}

    y_s, st_s, rows_s = run_trunk(x_sample, past, cache_mem_kv, init_s, ctx_sample, prm)

    p_win = rows_p[2][:, s_p - min(WINDOW, s_p):]
    s_win = jnp.concatenate([cache_win_kv, rows_s[2]], axis=1)[:, -wb:]
    return (y_p, y_s, st_p[0], st_p[1], st_p[2], mem_kv_p, rows_p[0], rows_p[1], p_win,
            st_s[0], st_s[1], st_s[2], rows_s[0], rows_s[1], s_win)
```

```python
import math
from functools import partial

import numpy as np
import jax
import jax.numpy as jnp
from jax import lax
from jax.experimental import pallas as pl
from jax.experimental.pallas import tpu as pltpu

D_MODEL = 1024
DEPTH = 4
PAGE_SIZE = 128
N_A_LAYERS = DEPTH // 2
N_B_LAYERS = DEPTH - N_A_LAYERS
MIX_W = 3 * D_MODEL // 4
MEM_W = D_MODEL - MIX_W
HEAD_DIM = 64
MEM_HEADS = MEM_W // HEAD_DIM
MLSTM_HEADS = 4
MLSTM_DH = MIX_W // MLSTM_HEADS
MLSTM_CHUNK = 64
NSA_HEADS = MIX_W // HEAD_DIM
NSA_KV_HEADS = 2
NSA_GROUP = NSA_HEADS // NSA_KV_HEADS
CMP_LEN = 32
CMP_STRIDE = 16
CMP_HIDDEN = 128
SLC_BLOCK = 64
N_SELECT = 8
WINDOW = 512
WIN_QBLOCK = 128
SLC_QBLOCK = 64
REL_BUCKETS = 32
REL_MAX_EXACT = 16
REL_MAX_DIST = 128
D_FF = -(-(8 * D_MODEL) // (3 * 256)) * 256
EPS = 1e-6
NEG = -1e30
FORCED = 1e6

LANES = 128
VMEM_LIMIT = 48 * 1024 * 1024

BF16 = jnp.bfloat16
F32 = jnp.float32


def _rms_rows(x, g):
    return x * lax.rsqrt(jnp.mean(x * x, axis=-1, keepdims=True) + EPS) * g


def _norm_matmul_kernel(x_ref, g_ref, w_ref, o_ref, xn_ref, *, use_norm):
    @pl.when(pl.program_id(1) == 0)
    def _():
        x = x_ref[...]
        if use_norm:
            x = _rms_rows(x, g_ref[...])
        xn_ref[...] = x.astype(BF16)

    o_ref[...] = jnp.dot(xn_ref[...], w_ref[...], preferred_element_type=F32)


def _pick_tile(n, candidates):
    for c in candidates:
        if n % c == 0:
            return c
    return n


def norm_matmul(x, g, w, use_norm=True):
    m, k = x.shape
    n = w.shape[1]
    tm = _pick_tile(m, (512, 256, 128))
    tn = _pick_tile(n, (1152, 1024, 768, 512, 384, 256, 128))
    return pl.pallas_call(
        partial(_norm_matmul_kernel, use_norm=use_norm),
        grid=(m // tm, n // tn),
        in_specs=[pl.BlockSpec((tm, k), lambda i, j: (i, 0)),
                  pl.BlockSpec((1, k), lambda i, j: (0, 0)),
                  pl.BlockSpec((k, tn), lambda i, j: (0, j))],
        out_specs=pl.BlockSpec((tm, tn), lambda i, j: (i, j)),
        out_shape=jax.ShapeDtypeStruct((m, n), F32),
        scratch_shapes=[pltpu.VMEM((tm, k), BF16)],
        compiler_params=pltpu.CompilerParams(
            dimension_semantics=("parallel", "arbitrary"), vmem_limit_bytes=VMEM_LIMIT),
        name="norm_matmul",
    )(x, g.reshape(1, k), w)


def _matmul_res_kernel(a_ref, w_ref, r_ref, o_ref):
    o_ref[...] = r_ref[...] + jnp.dot(a_ref[...].astype(BF16), w_ref[...], preferred_element_type=F32)


def matmul_res(a, w, res):
    m, k = a.shape
    n = w.shape[1]
    tm = _pick_tile(m, (512, 256, 128))
    return pl.pallas_call(
        _matmul_res_kernel,
        grid=(m // tm,),
        in_specs=[pl.BlockSpec((tm, k), lambda i: (i, 0)),
                  pl.BlockSpec((k, n), lambda i: (0, 0)),
                  pl.BlockSpec((tm, n), lambda i: (i, 0))],
        out_specs=pl.BlockSpec((tm, n), lambda i: (i, 0)),
        out_shape=jax.ShapeDtypeStruct((m, n), F32),
        compiler_params=pltpu.CompilerParams(
            dimension_semantics=("parallel",), vmem_limit_bytes=VMEM_LIMIT),
        name="matmul_res",
    )(a, w, res)


def _ffn_kernel(x_ref, g_ref, wg_ref, wu_ref, wd_ref, o_ref, xn_ref, acc_ref):
    f = pl.program_id(1)

    @pl.when(f == 0)
    def _():
        xn_ref[...] = _rms_rows(x_ref[...], g_ref[...]).astype(BF16)
        acc_ref[...] = jnp.zeros_like(acc_ref)

    xn = xn_ref[...]
    gate = jnp.dot(xn, wg_ref[...], preferred_element_type=F32)
    up = jnp.dot(xn, wu_ref[...], preferred_element_type=F32)
    act = (gate * jax.nn.sigmoid(gate) * up).astype(BF16)
    acc_ref[...] += jnp.dot(act, wd_ref[...], preferred_element_type=F32)

    @pl.when(f == pl.num_programs(1) - 1)
    def _():
        o_ref[...] = x_ref[...] + acc_ref[...]


def ffn(x, g, w_gu, w_down):
    m, d = x.shape
    tm = _pick_tile(m, (512, 256, 128))
    tf = 1408
    nf = D_FF // tf
    return pl.pallas_call(
        _ffn_kernel,
        grid=(m // tm, nf),
        in_specs=[pl.BlockSpec((tm, d), lambda i, f: (i, 0)),
                  pl.BlockSpec((1, d), lambda i, f: (0, 0)),
                  pl.BlockSpec((d, tf), lambda i, f: (0, f)),
                  pl.BlockSpec((d, tf), lambda i, f: (0, f + nf)),
                  pl.BlockSpec((tf, d), lambda i, f: (f, 0))],
        out_specs=pl.BlockSpec((tm, d), lambda i, f: (i, 0)),
        out_shape=jax.ShapeDtypeStruct((m, d), F32),
        scratch_shapes=[pltpu.VMEM((tm, d), BF16), pltpu.VMEM((tm, d), F32)],
        compiler_params=pltpu.CompilerParams(
            dimension_semantics=("parallel", "arbitrary"), vmem_limit_bytes=VMEM_LIMIT),
        name="ffn",
    )(x, g.reshape(1, d), w_gu, w_gu, w_down)


def rmsnorm(x, g):
    xf = x.astype(jnp.float32)
    y = xf * lax.rsqrt(jnp.mean(xf * xf, axis=-1, keepdims=True) + EPS)
    return (y * g.astype(jnp.float32)).astype(x.dtype)


def masked_softmax(s, mask):
    p = jax.nn.softmax(jnp.where(mask, s, NEG), axis=-1)
    return jnp.where(mask, p, 0.0)


def rel_bucket(dist):
    d = jnp.maximum(dist, 0)
    ratio = jnp.maximum(d, REL_MAX_EXACT).astype(jnp.float32) / REL_MAX_EXACT
    large = REL_MAX_EXACT + (jnp.log(ratio) / math.log(REL_MAX_DIST / REL_MAX_EXACT)
                             * (REL_BUCKETS - REL_MAX_EXACT)).astype(jnp.int32)
    return jnp.where(d < REL_MAX_EXACT, d, jnp.minimum(large, REL_BUCKETS - 1))


def mem_attn(pm, mkv, g_q):
    b, s, _ = pm.shape
    q = rmsnorm(pm.reshape(b, s, MEM_HEADS, HEAD_DIM), g_q)
    sc = jnp.einsum('bshd,bmhd->bhsm', q, mkv[:, :, 0], preferred_element_type=jnp.float32) * HEAD_DIM ** -0.5
    p = jax.nn.softmax(sc, axis=-1)
    o = jnp.einsum('bhsm,bmhd->bshd', p.astype(mkv.dtype), mkv[:, :, 1])
    return o.reshape(b, s, MEM_W).astype(pm.dtype)


def mlstm_chunkwise(q, k, v, i_pre, logf, c0, n0, m0):
    b, s, h, d = q.shape
    L = MLSTM_CHUNK if s % MLSTM_CHUNK == 0 else s
    nc = s // L

    def chunks(a):
        return jnp.swapaxes(a.reshape(b, nc, L, *a.shape[2:]), 0, 1)

    causal = jnp.tril(jnp.ones((L, L), dtype=bool))

    def step(carry, xs):
        c, n, m = carry
        qc, kc, vc, ic, fc = xs
        bcum = jnp.swapaxes(jnp.cumsum(fc, axis=1), 1, 2)
        ig = jnp.swapaxes(ic, 1, 2)
        log_d = jnp.where(causal, bcum[..., :, None] - bcum[..., None, :] + ig[..., None, :], -jnp.inf)
        log_inter = bcum + m[..., None]
        m_t = jnp.maximum(jnp.max(log_d, axis=-1), log_inter)
        w_intra = jnp.exp(log_d - m_t[..., None])
        w_inter = jnp.exp(log_inter - m_t)
        sc = jnp.einsum('bthd,bshd->bhts', qc, kc) * w_intra
        num = jnp.einsum('bhts,bshd->bhtd', sc, vc) + w_inter[..., None] * jnp.einsum('bthd,bhde->bhte', qc, c)
        den = jnp.sum(sc, axis=-1) + w_inter * jnp.einsum('bthd,bhd->bht', qc, n)
        hh = num / jnp.maximum(jnp.abs(den), jnp.exp(-m_t))[..., None]
        g = bcum[..., -1]
        m_new = m_t[..., -1]
        w_s = jnp.exp(g[..., None] - bcum + ig - m_new[..., None])
        decay = jnp.exp(g + m - m_new)
        c_new = decay[..., None, None] * c + jnp.einsum('bhs,bshd,bshe->bhde', w_s, kc, vc)
        n_new = decay[..., None] * n + jnp.einsum('bhs,bshd->bhd', w_s, kc)
        return (c_new, n_new, m_new), jnp.swapaxes(hh, 1, 2)

    (c, n, m), hs = lax.scan(step, (c0, n0, m0), tuple(chunks(a) for a in (q, k, v, i_pre, logf)))
    return jnp.swapaxes(hs, 0, 1).reshape(b, s, h, d), (c, n, m)


def mlstm_mixer(p, gates_pre, state, b_if, g_hnorm):
    b, s, _ = p.shape

    def heads(a):
        return a.reshape(b, s, MLSTM_HEADS, MLSTM_DH).astype(F32)

    q = heads(p[..., :MIX_W])
    k = heads(p[..., MIX_W:2 * MIX_W]) * MLSTM_DH ** -0.5
    v = heads(p[..., 2 * MIX_W:3 * MIX_W])
    o = p[..., 3 * MIX_W:4 * MIX_W].astype(F32)
    gates = gates_pre + b_if.astype(F32)
    i_pre = gates[..., :MLSTM_HEADS]
    logf = jax.nn.log_sigmoid(gates[..., MLSTM_HEADS:])
    c0, n0, m0 = (a.astype(F32) for a in state)
    h, new_state = mlstm_chunkwise(q, k, v, i_pre, logf, c0, n0, m0)
    h = rmsnorm(h, g_hnorm).reshape(b, s, MIX_W)
    return (jax.nn.sigmoid(o) * h).astype(p.dtype), new_state


def split_kv_rows(kv, g_k_slc, g_k_win):
    b, s, _ = kv.shape
    kv = kv.reshape(b, s, 3, 2, NSA_KV_HEADS, HEAD_DIM)
    cmp = kv[:, :, 0]
    slc = jnp.stack([rmsnorm(kv[:, :, 1, 0], g_k_slc), kv[:, :, 1, 1]], axis=2)
    win = jnp.stack([rmsnorm(kv[:, :, 2, 0], g_k_win), kv[:, :, 2, 1]], axis=2)
    return cmp, slc, win


def compress(cmp_kv, w1, b1, w2, b2, pos_enc, g_k_cmp):
    b, t = cmp_kv.shape[:2]
    r_n = CMP_LEN // CMP_STRIDE
    nc = (t - CMP_LEN) // CMP_STRIDE + 1
    nseg = nc + r_n - 1
    seg = cmp_kv[:, :nseg * CMP_STRIDE].reshape(b, nseg, CMP_STRIDE, 2, NSA_KV_HEADS, HEAD_DIM)
    w1r = w1.reshape(2, r_n, CMP_STRIDE, HEAD_DIM, CMP_HIDDEN)
    pe = jnp.einsum('pcd,cpdh->ch', pos_enc, w1)
    pre = (b1 + pe)[None, None, :, None, :]
    for r in range(r_n):
        pre = pre + jnp.einsum('bnucgd,cudh->bncgh', seg[:, r:r + nc], w1r[:, r])
    out = jnp.einsum('bncgh,chd->bncgd', jax.nn.gelu(pre), w2) + b2[None, None, :, None, :]
    ck = rmsnorm(out[:, :, 0], g_k_cmp)
    cv = out[:, :, 1]
    c_end = jnp.arange(nc) * CMP_STRIDE + CMP_LEN - 1
    return ck, cv, c_end


def cmp_to_slc_matrix(n_cmp, n_slc):
    c0 = np.arange(n_cmp)[:, None] * CMP_STRIDE
    s0 = np.arange(n_slc)[None, :] * SLC_BLOCK
    ov = np.clip(np.minimum(c0 + CMP_LEN, s0 + SLC_BLOCK) - np.maximum(c0, s0), 0, None)
    return jnp.asarray(ov / CMP_STRIDE, dtype=jnp.float32)


def cmp_branch(q, qpos, ck, cv, c_end, table):
    sc = jnp.einsum('bsgzd,bngd->bsgzn', q, ck, preferred_element_type=jnp.float32) * HEAD_DIM ** -0.5
    dist = qpos[:, None] - c_end[None, :]
    bias = jnp.transpose(table[rel_bucket(dist)], (0, 2, 3, 1))
    p = masked_softmax(sc + bias[None], (dist >= 0)[:, None, None, :])
    o = jnp.einsum('bsgzn,bngd->bsgzd', p.astype(cv.dtype), cv)
    return o, p


def select_blocks(p_cmp, qpos, n_slc):
    imp = jnp.einsum('bsgzn,nj->bsgj', p_cmp, cmp_to_slc_matrix(p_cmp.shape[-1], n_slc))
    j = jnp.arange(n_slc)
    cur = (qpos // SLC_BLOCK)[:, None]
    valid = j <= cur
    forced = (j == 0) | (j == cur) | (j == cur - 1)
    score = jnp.where(forced[None, :, None], FORCED, jnp.where(valid[None, :, None], imp, -1.0))
    _, idx = lax.top_k(score, min(N_SELECT, n_slc))
    return idx


def slc_branch(q, qpos, idx, fetch, table):
    b, s = q.shape[:2]
    qb = SLC_QBLOCK if s % SLC_QBLOCK == 0 else s
    nb = s // qb
    g_i = jnp.arange(NSA_KV_HEADS)[None, None, :, None]

    def blk(args):
        qx, px, ix = args
        tok = (ix[..., None] * SLC_BLOCK + jnp.arange(SLC_BLOCK)).reshape(b, qb, NSA_KV_HEADS, -1)
        kx, vx = fetch(tok)
        sc = jnp.einsum('bsgzd,bsgjd->bsgzj', qx, kx, preferred_element_type=jnp.float32) * HEAD_DIM ** -0.5
        dist = px[None, :, None, None] - tok
        bias = jnp.swapaxes(table[rel_bucket(dist), g_i], -1, -2)
        p = masked_softmax(sc + bias, (dist >= 0)[:, :, :, None, :])
        return jnp.einsum('bsgzj,bsgjd->bsgzd', p.astype(vx.dtype), vx)

    qs = jnp.swapaxes(q.reshape(b, nb, qb, NSA_KV_HEADS, NSA_GROUP, HEAD_DIM), 0, 1)
    ids = jnp.swapaxes(idx.reshape(b, nb, qb, NSA_KV_HEADS, idx.shape[-1]), 0, 1)
    out = lax.map(blk, (qs, qpos.reshape(nb, qb), ids))
    return jnp.swapaxes(out, 0, 1).reshape(b, s, NSA_KV_HEADS, NSA_GROUP, HEAD_DIM)


def win_branch(q, q_pos0, wkv, wpos0, table):
    b, s = q.shape[:2]
    qb = WIN_QBLOCK if s % WIN_QBLOCK == 0 else s
    nb = s // qb
    lk = WINDOW + qb
    wpad = jnp.pad(wkv, ((0, 0), (WINDOW, 0), (0, 0), (0, 0), (0, 0)))
    starts = q_pos0 + jnp.arange(nb) * qb
    offs = jnp.arange(lk)
    kidx = starts[:, None] - wpos0 + offs[None, :]
    kpos = starts[:, None] - WINDOW + offs[None, :]

    def blk(args):
        qx, px, ki, kp = args
        kv = jnp.take(wpad, ki, axis=1)
        sc = jnp.einsum('bigzd,bjgd->bgzij', qx, kv[:, :, 0], preferred_element_type=jnp.float32) * HEAD_DIM ** -0.5
        dist = px[:, None] - kp[None, :]
        bias = jnp.transpose(table[rel_bucket(dist)], (2, 3, 0, 1))
        mask = (dist >= 0) & (dist < WINDOW) & (kp >= wpos0)[None, :]
        p = masked_softmax(sc + bias[None], mask)
        return jnp.einsum('bgzij,bjgd->bigzd', p.astype(kv.dtype), kv[:, :, 1])

    qs = jnp.swapaxes(q.reshape(b, nb, qb, NSA_KV_HEADS, NSA_GROUP, HEAD_DIM), 0, 1)
    qpos = (q_pos0 + jnp.arange(s)).reshape(nb, qb)
    out = lax.map(blk, (qs, qpos, kidx, kpos))
    return jnp.swapaxes(out, 0, 1).reshape(b, s, NSA_KV_HEADS, NSA_GROUP, HEAD_DIM)


def nsa_mixer(pq, gate_pre, q_pos0, ctx, g_q, b_gate, rel_table):
    b, s, _ = pq.shape
    q = rmsnorm(pq.reshape(b, s, NSA_HEADS, HEAD_DIM), g_q)
    q = q.reshape(b, s, NSA_KV_HEADS, NSA_GROUP, HEAD_DIM)
    gates = jax.nn.sigmoid(gate_pre + b_gate.astype(F32))
    gates = gates.reshape(b, s, 3, NSA_KV_HEADS, NSA_GROUP, 1)
    qpos = q_pos0 + jnp.arange(s)
    table = rel_table.astype(F32).reshape(REL_BUCKETS, NSA_KV_HEADS, NSA_GROUP)
    o_cmp, p_cmp = cmp_branch(q, qpos, ctx['ck'], ctx['cv'], ctx['c_end'], table)
    idx = select_blocks(p_cmp, qpos, ctx['n_slc'])
    o_slc = slc_branch(q, qpos, idx, ctx['fetch'], table)
    o_win = win_branch(q, q_pos0, ctx['wkv'], ctx['wpos0'], table)
    o = gates[:, :, 0] * o_cmp + gates[:, :, 1] * o_slc + gates[:, :, 2] * o_win
    return o.reshape(b, s, MIX_W)


GATE_PAD = LANES


def _pad_cols(w, n):
    return jnp.pad(w, ((0, 0), (0, n - w.shape[1])))


def prep_weights(prm):
    w = {}
    n_gate_a = 2 * MLSTM_HEADS
    n_gate_b = 3 * NSA_HEADS
    w['in_a'] = [jnp.concatenate([prm['w_in_a'][l][:, :4 * MIX_W],
                                  prm['w_in_a'][l][:, 4 * MIX_W + n_gate_a:],
                                  _pad_cols(prm['w_in_a'][l][:, 4 * MIX_W:4 * MIX_W + n_gate_a], GATE_PAD)],
                                 axis=1).astype(BF16) for l in range(N_A_LAYERS)]
    w['in_b'] = [jnp.concatenate([prm['w_in_b'][j][:, :MIX_W],
                                  prm['w_in_b'][j][:, MIX_W + n_gate_b:],
                                  _pad_cols(prm['w_in_b'][j][:, MIX_W:MIX_W + n_gate_b], GATE_PAD)],
                                 axis=1).astype(BF16) for j in range(N_B_LAYERS)]
    w['kv'] = prm['w_kv'].astype(BF16)
    w['out'] = prm['w_out'].astype(BF16)
    w['gu'] = prm['w_gu'].astype(BF16)
    w['down'] = prm['w_down'].astype(BF16)
    return w


def run_trunk(x, q_pos0, mem_kv, init_states, build_ctx, prm, w):
    b, s, d = x.shape
    x = x.reshape(b * s, d)
    states = []
    ctx = None
    rows = None
    for l in range(DEPTH):
        if l < N_A_LAYERS:
            p = norm_matmul(x, prm['g_mix'][l], w['in_a'][l]).reshape(b, s, -1)
            gates_pre = p[..., 4 * MIX_W + MEM_W:4 * MIX_W + MEM_W + 2 * MLSTM_HEADS]
            pm = p[..., 4 * MIX_W:4 * MIX_W + MEM_W]
            mix, st = mlstm_mixer(p[..., :4 * MIX_W], gates_pre, init_states[l], prm['b_if'][l], prm['g_hnorm'][l])
            states.append(st)
        else:
            j = l - N_A_LAYERS
            p = norm_matmul(x, prm['g_mix'][l], w['in_b'][j]).reshape(b, s, -1)
            pm = p[..., MIX_W:MIX_W + MEM_W]
            gate_pre = p[..., MIX_W + MEM_W:MIX_W + MEM_W + 3 * NSA_HEADS]
            mix = nsa_mixer(p[..., :MIX_W], gate_pre, q_pos0, ctx, prm['g_q'][j], prm['b_gate'][j], prm['rel_table'])
        mem_o = mem_attn(pm, mem_kv[l], prm['g_mem_q'][l])
        cat = jnp.concatenate([mix, mem_o], axis=-1).reshape(b * s, d)
        x = matmul_res(cat, w['out'][l], x)
        x = ffn(x, prm['g_ffn'][l], w['gu'][l], w['down'][l])
        if l == N_A_LAYERS - 1:
            kv = norm_matmul(x, prm['g_kv'], w['kv']).reshape(b, s, -1)
            rows = split_kv_rows(kv, prm['g_k_slc'], prm['g_k_win'])
            ctx = build_ctx(*rows)
    new_states = tuple(jnp.stack([st[i] for st in states]) for i in range(3))
    return x.reshape(b, s, d), new_states, rows


def kernel(x_prompt, x_sample, state_mlstm_c, state_mlstm_n, state_mlstm_m, cache_mem_kv,
           cache_cmp_kv, cache_slc_kv, cache_win_kv, page_table, mem_prompt,
           g_mix, w_in_a, b_if, g_hnorm, w_in_b, g_q, b_gate, rel_table, g_kv, w_kv,
           g_k_slc, g_k_win, g_k_cmp, cmp_pos, cmp_w1, cmp_b1, cmp_w2, cmp_b2,
           w_mem_kv, g_mem_k, g_mem_q, w_out, g_ffn, w_gu, w_down):
    prm = {'g_mix': g_mix, 'w_in_a': w_in_a, 'b_if': b_if, 'g_hnorm': g_hnorm, 'w_in_b': w_in_b,
           'g_q': g_q, 'b_gate': b_gate, 'rel_table': rel_table, 'g_kv': g_kv, 'w_kv': w_kv,
           'g_k_slc': g_k_slc, 'g_k_win': g_k_win, 'g_mem_q': g_mem_q, 'w_out': w_out,
           'g_ffn': g_ffn, 'w_gu': w_gu, 'w_down': w_down}
    w = prep_weights(prm)
    g_i = jnp.arange(NSA_KV_HEADS)[None, None, :, None]

    b_p, s_p, _ = x_prompt.shape
    m_tok = mem_prompt.shape[1]
    mem_flat = mem_prompt.reshape(b_p * m_tok, D_MODEL)
    ones = jnp.ones((D_MODEL,), F32)
    mem_kv_list = []
    for l in range(DEPTH):
        kv = norm_matmul(mem_flat, ones, w_mem_kv[l].astype(BF16), use_norm=False)
        kv = kv.reshape(b_p, m_tok, 2, MEM_HEADS, HEAD_DIM)
        mem_kv_list.append(jnp.stack([rmsnorm(kv[:, :, 0], g_mem_k[l]), kv[:, :, 1]], axis=2))
    mem_kv_p = jnp.stack(mem_kv_list)
    init_p = [(jnp.zeros((b_p, MLSTM_HEADS, MLSTM_DH, MLSTM_DH), F32),
               jnp.zeros((b_p, MLSTM_HEADS, MLSTM_DH), F32),
               jnp.full((b_p, MLSTM_HEADS), NEG, F32)) for _ in range(N_A_LAYERS)]
    b_ip = jnp.arange(b_p)[:, None, None, None]

    def ctx_prompt(cmp_rows, slc_rows, win_rows):
        ck, cv, c_end = compress(cmp_rows, cmp_w1, cmp_b1, cmp_w2, cmp_b2, cmp_pos, g_k_cmp)

        def fetch(tok):
            kv = slc_rows[b_ip, jnp.minimum(tok, s_p - 1), :, g_i]
            return kv[..., 0, :], kv[..., 1, :]

        return {'ck': ck, 'cv': cv, 'c_end': c_end, 'n_slc': -(-s_p // SLC_BLOCK),
                'fetch': fetch, 'wkv': win_rows, 'wpos0': 0}

    y_p, st_p, rows_p = run_trunk(x_prompt, 0, mem_kv_p, init_p, ctx_prompt, prm, w)

    b_s, s_s, _ = x_sample.shape
    past = page_table.shape[1] * PAGE_SIZE
    wb = cache_win_kv.shape[1]
    init_s = [(state_mlstm_c[l], state_mlstm_n[l], state_mlstm_m[l]) for l in range(N_A_LAYERS)]
    b_is = jnp.arange(b_s)[:, None, None, None]

    def ctx_sample(cmp_new, slc_new, win_new):
        cmp_past = cache_cmp_kv[page_table].reshape(b_s, past, 2, NSA_KV_HEADS, HEAD_DIM)
        ck, cv, c_end = compress(jnp.concatenate([cmp_past, cmp_new], axis=1),
                                 cmp_w1, cmp_b1, cmp_w2, cmp_b2, cmp_pos, g_k_cmp)

        def fetch(tok):
            tp = jnp.minimum(tok, past - 1)
            phys = page_table[b_is, tp // PAGE_SIZE]
            kv_old = cache_slc_kv[phys, tp % PAGE_SIZE, :, g_i]
            kv_new = slc_new[b_is, jnp.clip(tok - past, 0, s_s - 1), :, g_i]
            kv = jnp.where((tok < past)[..., None, None], kv_old, kv_new)
            return kv[..., 0, :], kv[..., 1, :]

        return {'ck': ck, 'cv': cv, 'c_end': c_end, 'n_slc': -(-(past + s_s) // SLC_BLOCK),
                'fetch': fetch, 'wkv': jnp.concatenate([cache_win_kv, win_new], axis=1), 'wpos0': past - wb}

    y_s, st_s, rows_s = run_trunk(x_sample, past, cache_mem_kv, init_s, ctx_sample, prm, w)

    p_win = rows_p[2][:, s_p - min(WINDOW, s_p):]
    s_win = jnp.concatenate([cache_win_kv, rows_s[2]], axis=1)[:, -wb:]
    return (y_p, y_s, st_p[0], st_p[1], st_p[2], mem_kv_p, rows_p[0], rows_p[1], p_win,
            st_s[0], st_s[1], st_s[2], rows_s[0], rows_s[1], s_win)
```

```python
import math
from functools import partial

import numpy as np
import jax
import jax.numpy as jnp
from jax import lax
from jax.experimental import pallas as pl
from jax.experimental.pallas import tpu as pltpu

D_MODEL = 1024
DEPTH = 4
PAGE_SIZE = 128
N_A_LAYERS = DEPTH // 2
N_B_LAYERS = DEPTH - N_A_LAYERS
MIX_W = 3 * D_MODEL // 4
MEM_W = D_MODEL - MIX_W
HEAD_DIM = 64
MEM_HEADS = MEM_W // HEAD_DIM
MLSTM_HEADS = 4
MLSTM_DH = MIX_W // MLSTM_HEADS
MLSTM_CHUNK = 64
NSA_HEADS = MIX_W // HEAD_DIM
NSA_KV_HEADS = 2
NSA_GROUP = NSA_HEADS // NSA_KV_HEADS
CMP_LEN = 32
CMP_STRIDE = 16
CMP_HIDDEN = 128
SLC_BLOCK = 64
N_SELECT = 8
WINDOW = 512
WIN_QBLOCK = 128
SLC_QBLOCK = 64
REL_BUCKETS = 32
REL_MAX_EXACT = 16
REL_MAX_DIST = 128
D_FF = -(-(8 * D_MODEL) // (3 * 256)) * 256
EPS = 1e-6
NEG = -1e30
FORCED = 1e6

LANES = 128
VMEM_LIMIT = 48 * 1024 * 1024

BF16 = jnp.bfloat16
F32 = jnp.float32


def _rms_rows(x, g):
    return x * lax.rsqrt(jnp.mean(x * x, axis=-1, keepdims=True) + EPS) * g


def _norm_matmul_kernel(x_ref, g_ref, w_ref, o_ref, xn_ref, *, use_norm):
    @pl.when(pl.program_id(1) == 0)
    def _():
        x = x_ref[...]
        if use_norm:
            x = _rms_rows(x, g_ref[...])
        xn_ref[...] = x.astype(BF16)

    o_ref[...] = jnp.dot(xn_ref[...], w_ref[...], preferred_element_type=F32)


def _pick_tile(n, candidates):
    for c in candidates:
        if n % c == 0:
            return c
    return n


def norm_matmul(x, g, w, use_norm=True):
    m, k = x.shape
    n = w.shape[1]
    tm = _pick_tile(m, (512, 256, 128))
    tn = _pick_tile(n, (1152, 1024, 768, 512, 384, 256, 128))
    return pl.pallas_call(
        partial(_norm_matmul_kernel, use_norm=use_norm),
        grid=(m // tm, n // tn),
        in_specs=[pl.BlockSpec((tm, k), lambda i, j: (i, 0)),
                  pl.BlockSpec((1, k), lambda i, j: (0, 0)),
                  pl.BlockSpec((k, tn), lambda i, j: (0, j))],
        out_specs=pl.BlockSpec((tm, tn), lambda i, j: (i, j)),
        out_shape=jax.ShapeDtypeStruct((m, n), F32),
        scratch_shapes=[pltpu.VMEM((tm, k), BF16)],
        compiler_params=pltpu.CompilerParams(
            dimension_semantics=("parallel", "arbitrary"), vmem_limit_bytes=VMEM_LIMIT),
        name="norm_matmul",
    )(x, g.reshape(1, k), w)


def _matmul_res_kernel(a_ref, w_ref, r_ref, o_ref):
    o_ref[...] = r_ref[...] + jnp.dot(a_ref[...].astype(BF16), w_ref[...], preferred_element_type=F32)


def matmul_res(a, w, res):
    m, k = a.shape
    n = w.shape[1]
    tm = _pick_tile(m, (512, 256, 128))
    return pl.pallas_call(
        _matmul_res_kernel,
        grid=(m // tm,),
        in_specs=[pl.BlockSpec((tm, k), lambda i: (i, 0)),
                  pl.BlockSpec((k, n), lambda i: (0, 0)),
                  pl.BlockSpec((tm, n), lambda i: (i, 0))],
        out_specs=pl.BlockSpec((tm, n), lambda i: (i, 0)),
        out_shape=jax.ShapeDtypeStruct((m, n), F32),
        compiler_params=pltpu.CompilerParams(
            dimension_semantics=("parallel",), vmem_limit_bytes=VMEM_LIMIT),
        name="matmul_res",
    )(a, w, res)


def _ffn_kernel(x_ref, g_ref, wg_ref, wu_ref, wd_ref, o_ref, xn_ref, acc_ref):
    f = pl.program_id(1)

    @pl.when(f == 0)
    def _():
        xn_ref[...] = _rms_rows(x_ref[...], g_ref[...]).astype(BF16)
        acc_ref[...] = jnp.zeros_like(acc_ref)

    xn = xn_ref[...]
    gate = jnp.dot(xn, wg_ref[...], preferred_element_type=F32)
    up = jnp.dot(xn, wu_ref[...], preferred_element_type=F32)
    act = (gate * jax.nn.sigmoid(gate) * up).astype(BF16)
    acc_ref[...] += jnp.dot(act, wd_ref[...], preferred_element_type=F32)

    @pl.when(f == pl.num_programs(1) - 1)
    def _():
        o_ref[...] = x_ref[...] + acc_ref[...]


def ffn(x, g, w_gu, w_down):
    m, d = x.shape
    tm = _pick_tile(m, (512, 256, 128))
    tf = 1408
    nf = D_FF // tf
    return pl.pallas_call(
        _ffn_kernel,
        grid=(m // tm, nf),
        in_specs=[pl.BlockSpec((tm, d), lambda i, f: (i, 0)),
                  pl.BlockSpec((1, d), lambda i, f: (0, 0)),
                  pl.BlockSpec((d, tf), lambda i, f: (0, f)),
                  pl.BlockSpec((d, tf), lambda i, f: (0, f + nf)),
                  pl.BlockSpec((tf, d), lambda i, f: (f, 0))],
        out_specs=pl.BlockSpec((tm, d), lambda i, f: (i, 0)),
        out_shape=jax.ShapeDtypeStruct((m, d), F32),
        scratch_shapes=[pltpu.VMEM((tm, d), BF16), pltpu.VMEM((tm, d), F32)],
        compiler_params=pltpu.CompilerParams(
            dimension_semantics=("parallel", "arbitrary"), vmem_limit_bytes=VMEM_LIMIT),
        name="ffn",
    )(x, g.reshape(1, d), w_gu, w_gu, w_down)


NSA_TQ = 128
MASK = -(2.0 ** 100)
SEL_LANE = HEAD_DIM
N_SEL_LANES = 32
CONST_LANE = SEL_LANE + N_SEL_LANES


def _rel_bucket_np(dist):
    d = np.maximum(dist, 0)
    ratio = np.maximum(d, REL_MAX_EXACT).astype(np.float64) / REL_MAX_EXACT
    large = REL_MAX_EXACT + (np.log(ratio) / math.log(REL_MAX_DIST / REL_MAX_EXACT)
                             * (REL_BUCKETS - REL_MAX_EXACT)).astype(np.int32)
    return np.where(d < REL_MAX_EXACT, d, np.minimum(large, REL_BUCKETS - 1)).astype(np.int32)


def _dot_nt(a, b):
    return lax.dot_general(a, b, (((1,), (1,)), ((), ())), preferred_element_type=F32)


def _nsa_prompt_kernel(rel_ref, q_ref, gate_ref, gq_ref, bg_ref, ck_ref, cv_ref, cb_ref, mt_ref,
                       sk_ref, sv_ref, wk_ref, wv_ref, dl_ref, o_ref,
                       qa_ref, m_ref, l_ref, acc_ref, out_ref):
    tq = NSA_TQ
    rows = NSA_GROUP * tq
    g = pl.program_id(1)
    qi = pl.program_id(2)
    lane = lax.broadcasted_iota(jnp.int32, (tq, LANES), 1)
    lo = lane < HEAD_DIM

    qn = []
    for k in range(NSA_GROUP // 2):
        x = q_ref[:, k * LANES:(k + 1) * LANES]
        x2 = x * x
        ss_lo = jnp.sum(jnp.where(lo, x2, 0.0), axis=1, keepdims=True)
        ss_hi = jnp.sum(jnp.where(lo, 0.0, x2), axis=1, keepdims=True)
        inv = jnp.where(lo, lax.rsqrt(ss_lo / HEAD_DIM + EPS), lax.rsqrt(ss_hi / HEAD_DIM + EPS))
        y = x * inv * gq_ref[...] * HEAD_DIM ** -0.5
        qn.append(jnp.where(lo, y, 0.0))
        qn.append(jnp.where(lo, pltpu.roll(y, HEAD_DIM, 1), 0.0))

    sig = jax.nn.sigmoid(gate_ref[...] + bg_ref[...])

    def gated(br, z, o):
        c = br * NSA_GROUP + z
        return sig[:, c:c + 1] * o

    qc = jnp.concatenate(qn, axis=0).astype(BF16)
    s = _dot_nt(qc, ck_ref[...]) + cb_ref[...].reshape(rows, LANES)
    m = jnp.maximum(jnp.max(s, axis=1, keepdims=True), NEG)
    p = jnp.exp(s - m)
    l = jnp.sum(p, axis=1, keepdims=True)
    pn = p * jnp.where(l > 0.0, 1.0 / l, 0.0)
    o_cmp = jnp.dot(pn.astype(BF16), cv_ref[...], preferred_element_type=F32)
    for z in range(NSA_GROUP):
        out_ref[z * tq:(z + 1) * tq, :] = gated(0, z, o_cmp[z * tq:(z + 1) * tq, :])

    psum = pn[0:tq, :]
    for z in range(1, NSA_GROUP):
        psum = psum + pn[z * tq:(z + 1) * tq, :]
    p_hi = psum.astype(BF16)
    p_lo = (psum - p_hi.astype(F32)).astype(BF16)
    imp = _dot_nt(mt_ref[...], p_hi) + _dot_nt(mt_ref[...], p_lo)
    jidx = lax.broadcasted_iota(jnp.int32, (N_SEL_LANES, tq), 0)
    tpos = qi * tq + lax.broadcasted_iota(jnp.int32, (N_SEL_LANES, tq), 1)
    cur = tpos // SLC_BLOCK
    forced = (jidx == 0) | (jidx == cur) | (jidx == cur - 1)
    score = jnp.where(forced, FORCED, jnp.where(jidx <= cur, imp, -1.0))
    rank = jnp.zeros((N_SEL_LANES, tq), F32)
    for i in range(N_SEL_LANES):
        ri = score[i:i + 1, :]
        beats = (ri > score) | ((ri == score) & (jidx > i))
        rank = rank + jnp.where(beats, 1.0, 0.0)
    selb = jnp.where(rank < N_SELECT, 0.0, MASK)
    selb = jnp.concatenate([selb, jnp.zeros((LANES - N_SEL_LANES, tq), F32)], axis=0)
    sel_cols = pltpu.roll(selb.T, SEL_LANE, 1)

    for z in range(NSA_GROUP):
        c = jnp.full((tq, LANES), rel_ref[REL_BUCKETS - 1, g * NSA_GROUP + z], F32)
        c_hi = c.astype(BF16).astype(F32)
        extra = jnp.where(lane == CONST_LANE, c_hi, jnp.where(lane == CONST_LANE + 1, c - c_hi, sel_cols))
        qa_ref[z * tq:(z + 1) * tq, :] = jnp.where(lo, qn[z], extra).astype(BF16)

    def flash_init():
        m_ref[...] = jnp.full((rows, LANES), NEG, F32)
        l_ref[...] = jnp.zeros((rows, LANES), F32)
        acc_ref[...] = jnp.zeros((rows, LANES), F32)

    def flash_block(k_ref, v_ref, kb, didx):
        start = pl.multiple_of(kb * tq, tq)
        s = _dot_nt(qa_ref[...], k_ref[pl.ds(start, tq), :]) + dl_ref[didx].reshape(rows, LANES)
        m_prev = m_ref[...]
        m_next = jnp.maximum(m_prev, jnp.max(s, axis=1, keepdims=True))
        p = jnp.exp(s - m_next)
        alpha = jnp.exp(m_prev - m_next)
        l_ref[...] = alpha * l_ref[...] + jnp.sum(p, axis=1, keepdims=True)
        acc_ref[...] = alpha * acc_ref[...] + jnp.dot(p.astype(BF16), v_ref[pl.ds(start, tq), :],
                                                      preferred_element_type=F32)
        m_ref[...] = m_next

    def flash_finish(br):
        o = acc_ref[...] / l_ref[...]
        for z in range(NSA_GROUP):
            out_ref[z * tq:(z + 1) * tq, :] += gated(br, z, o[z * tq:(z + 1) * tq, :])

    flash_init()

    def slc_body(kb, carry):
        didx = jnp.where(kb == qi, 2, jnp.where(kb == qi - 1, 1, 0))
        flash_block(sk_ref, sv_ref, kb, didx)
        return carry

    lax.fori_loop(0, qi + 1, slc_body, 0)
    flash_finish(1)

    flash_init()
    n_win = WINDOW // tq

    def win_body(kb, carry):
        d = qi - kb
        didx = jnp.where(d == 0, 2, jnp.where(d == 1, 1, jnp.where(d == n_win, 3, 0)))
        flash_block(wk_ref, wv_ref, kb, didx)
        return carry

    lax.fori_loop(jnp.maximum(qi - n_win, 0), qi + 1, win_body, 0)
    flash_finish(2)

    for k in range(NSA_GROUP // 2):
        even = out_ref[(2 * k) * tq:(2 * k + 1) * tq, :]
        odd = out_ref[(2 * k + 1) * tq:(2 * k + 2) * tq, :]
        o_ref[:, k * LANES:(k + 1) * LANES] = even + pltpu.roll(odd, HEAD_DIM, 1)


def nsa_bias_tiles(rel_table, s):
    tq = NSA_TQ
    table = rel_table.astype(F32)
    far = table[REL_BUCKETS - 1]
    r = np.arange(tq)[:, None]
    c = np.arange(tq)[None, :]

    def tile(off, masked):
        dist = off + r - c
        t = jnp.transpose(table[_rel_bucket_np(dist)], (2, 0, 1)) - far[:, None, None]
        return jnp.where(jnp.asarray(masked(dist))[None], MASK, t)

    zero = jnp.zeros((NSA_HEADS, tq, tq), F32)
    d128 = tile(tq, lambda d: d < 0)
    d0 = tile(0, lambda d: d < 0)
    dwin = tile(WINDOW, lambda d: d >= WINDOW)
    tiles = jnp.stack([zero, d128, d0, dwin])
    n_cmp = (s - CMP_LEN) // CMP_STRIDE + 1
    c_end = np.arange(LANES) * CMP_STRIDE + CMP_LEN - 1
    dist = np.arange(s)[:, None] - c_end[None, :]
    cb = jnp.transpose(table[_rel_bucket_np(dist)], (2, 0, 1))
    bad = (dist < 0) | (np.arange(LANES)[None, :] >= n_cmp)
    cb = jnp.where(jnp.asarray(bad)[None], MASK, cb)
    return tiles, cb


def nsa_prompt(p, b, s, g_q, b_gate, rel_table, ck, cv, ka, va, tiles, cb):
    tq = NSA_TQ
    nq = s // tq
    rows = NSA_GROUP * tq
    n_cmp = (s - CMP_LEN) // CMP_STRIDE + 1
    n_slc = -(-s // SLC_BLOCK)
    assert s % tq == 0 and n_cmp <= LANES and n_slc <= N_SEL_LANES
    mt = np.zeros((N_SEL_LANES, LANES), np.float32)
    c0 = np.arange(n_cmp)[None, :] * CMP_STRIDE
    s0 = np.arange(n_slc)[:, None] * SLC_BLOCK
    mt[:n_slc, :n_cmp] = np.clip(np.minimum(c0 + CMP_LEN, s0 + SLC_BLOCK) - np.maximum(c0, s0), 0, None) / CMP_STRIDE
    gq2 = jnp.concatenate([g_q, g_q]).reshape(1, LANES).astype(F32)
    bg = b_gate.astype(F32).reshape(3, NSA_KV_HEADS, NSA_GROUP).transpose(1, 0, 2).reshape(NSA_KV_HEADS, 1, 3 * NSA_GROUP)
    bg = jnp.pad(bg, ((0, 0), (0, 0), (0, LANES - 3 * NSA_GROUP)))
    qcol = (MIX_W + MEM_W) // LANES
    kmap = lambda off: (lambda bi, g, qi: (off + g, bi, 0, 0))
    return pl.pallas_call(
        _nsa_prompt_kernel,
        grid=(b, NSA_KV_HEADS, nq),
        in_specs=[pl.BlockSpec(memory_space=pltpu.SMEM),
                  pl.BlockSpec((tq, NSA_GROUP * HEAD_DIM), lambda bi, g, qi: (bi * nq + qi, g)),
                  pl.BlockSpec((tq, LANES), lambda bi, g, qi: (bi * nq + qi, qcol + g)),
                  pl.BlockSpec((1, LANES), lambda bi, g, qi: (0, 0)),
                  pl.BlockSpec((None, 1, LANES), lambda bi, g, qi: (g, 0, 0)),
                  pl.BlockSpec((None, None, LANES, LANES), lambda bi, g, qi: (bi, g, 0, 0)),
                  pl.BlockSpec((None, None, LANES, LANES), lambda bi, g, qi: (bi, g, 0, 0)),
                  pl.BlockSpec((NSA_GROUP, tq, LANES), lambda bi, g, qi: (g, qi, 0)),
                  pl.BlockSpec((N_SEL_LANES, LANES), lambda bi, g, qi: (0, 0)),
                  pl.BlockSpec((None, None, s, LANES), kmap(0)),
                  pl.BlockSpec((None, None, s, LANES), kmap(0)),
                  pl.BlockSpec((None, None, s, LANES), kmap(NSA_KV_HEADS)),
                  pl.BlockSpec((None, None, s, LANES), kmap(NSA_KV_HEADS)),
                  pl.BlockSpec((4, NSA_GROUP, tq, tq), lambda bi, g, qi: (0, g, 0, 0))],
        out_specs=pl.BlockSpec((tq, NSA_GROUP * HEAD_DIM), lambda bi, g, qi: (bi * nq + qi, g)),
        out_shape=jax.ShapeDtypeStruct((b * s, MIX_W), F32),
        scratch_shapes=[pltpu.VMEM((rows, LANES), BF16)] + [pltpu.VMEM((rows, LANES), F32)] * 4,
        compiler_params=pltpu.CompilerParams(
            dimension_semantics=("parallel", "parallel", "arbitrary"), vmem_limit_bytes=VMEM_LIMIT),
        name="nsa_prompt",
    )(rel_table.astype(F32), p, p, gq2, bg, ck, cv, cb, jnp.asarray(mt, BF16),
      ka.reshape(4, b, s, LANES), va.reshape(4, b, s, LANES),
      ka.reshape(4, b, s, LANES), va.reshape(4, b, s, LANES), tiles)


def _half_rms(x, g2, lo):
    x2 = x * x
    ss_lo = jnp.sum(jnp.where(lo, x2, 0.0), axis=1, keepdims=True)
    ss_hi = jnp.sum(jnp.where(lo, 0.0, x2), axis=1, keepdims=True)
    inv = jnp.where(lo, lax.rsqrt(ss_lo / HEAD_DIM + EPS), lax.rsqrt(ss_hi / HEAD_DIM + EPS))
    return x * inv * g2


def _kv_rows_kernel(x_ref, g_ref, w_ref, gs_ref, gw_ref, cmp_ref, slc_ref, win_ref, *aug_refs, seq_len):
    tm = x_ref.shape[0]
    xn = _rms_rows(x_ref[...], g_ref[...]).astype(BF16)
    kv = jnp.dot(xn, w_ref[...], preferred_element_type=F32)
    width = 2 * NSA_KV_HEADS * HEAD_DIM
    cmp_ref[...] = kv[:, :width]
    lane = lax.broadcasted_iota(jnp.int32, (tm, LANES), 1)
    lo = lane < HEAD_DIM
    if aug_refs:
        ka_ref, va_ref = aug_refs
        pos = (pl.program_id(0) * tm + lax.broadcasted_iota(jnp.int32, (tm, LANES), 0)) % seq_len
        ones = (lane == CONST_LANE) | (lane == CONST_LANE + 1)
        onehot = (lane - SEL_LANE) == pos // SLC_BLOCK
    for t, (rows_ref, gk_ref) in enumerate(((slc_ref, gs_ref), (win_ref, gw_ref))):
        base = width * (t + 1)
        kn = _half_rms(kv[:, base:base + LANES], gk_ref[...], lo)
        vv = kv[:, base + LANES:base + 2 * LANES]
        rows_ref[:, :LANES] = kn
        rows_ref[:, LANES:] = vv
        if aug_refs:
            extra = jnp.where((ones | onehot) if t == 0 else ones, 1.0, 0.0)
            for g in range(NSA_KV_HEADS):
                kg = kn if g == 0 else pltpu.roll(kn, HEAD_DIM, 1)
                vg = vv if g == 0 else pltpu.roll(vv, HEAD_DIM, 1)
                ka_ref[NSA_KV_HEADS * t + g] = jnp.where(lo, kg, extra).astype(BF16)
                va_ref[NSA_KV_HEADS * t + g] = jnp.where(lo, vg, 0.0).astype(BF16)


def kv_rows(x, g_kv, w_kv, g_k_slc, g_k_win, seq_len, aug):
    m, d = x.shape
    n = w_kv.shape[1]
    tm = _pick_tile(m, (512, 256, 128))
    width = 2 * NSA_KV_HEADS * HEAD_DIM
    row_spec = pl.BlockSpec((tm, width), lambda i: (i, 0))
    row_shape = jax.ShapeDtypeStruct((m, width), F32)
    out_specs = [row_spec] * 3
    out_shape = [row_shape] * 3
    if aug:
        out_specs += [pl.BlockSpec((2 * NSA_KV_HEADS, tm, LANES), lambda i: (0, i, 0))] * 2
        out_shape += [jax.ShapeDtypeStruct((2 * NSA_KV_HEADS, m, LANES), BF16)] * 2
    g2 = lambda g: jnp.concatenate([g, g]).reshape(1, LANES).astype(F32)
    return pl.pallas_call(
        partial(_kv_rows_kernel, seq_len=seq_len),
        grid=(m // tm,),
        in_specs=[pl.BlockSpec((tm, d), lambda i: (i, 0)),
                  pl.BlockSpec((1, d), lambda i: (0, 0)),
                  pl.BlockSpec((d, n), lambda i: (0, 0)),
                  pl.BlockSpec((1, LANES), lambda i: (0, 0)),
                  pl.BlockSpec((1, LANES), lambda i: (0, 0))],
        out_specs=out_specs,
        out_shape=out_shape,
        compiler_params=pltpu.CompilerParams(
            dimension_semantics=("parallel",), vmem_limit_bytes=VMEM_LIMIT),
        name="kv_rows",
    )(x, g_kv.reshape(1, d), w_kv, g2(g_k_slc), g2(g_k_win))


def rmsnorm(x, g):
    xf = x.astype(jnp.float32)
    y = xf * lax.rsqrt(jnp.mean(xf * xf, axis=-1, keepdims=True) + EPS)
    return (y * g.astype(jnp.float32)).astype(x.dtype)


def masked_softmax(s, mask):
    p = jax.nn.softmax(jnp.where(mask, s, NEG), axis=-1)
    return jnp.where(mask, p, 0.0)


def rel_bucket(dist):
    d = jnp.maximum(dist, 0)
    ratio = jnp.maximum(d, REL_MAX_EXACT).astype(jnp.float32) / REL_MAX_EXACT
    large = REL_MAX_EXACT + (jnp.log(ratio) / math.log(REL_MAX_DIST / REL_MAX_EXACT)
                             * (REL_BUCKETS - REL_MAX_EXACT)).astype(jnp.int32)
    return jnp.where(d < REL_MAX_EXACT, d, jnp.minimum(large, REL_BUCKETS - 1))


def mem_attn(pm, mkv, g_q):
    b, s, _ = pm.shape
    q = rmsnorm(pm.reshape(b, s, MEM_HEADS, HEAD_DIM), g_q)
    sc = jnp.einsum('bshd,bmhd->bhsm', q, mkv[:, :, 0], preferred_element_type=jnp.float32) * HEAD_DIM ** -0.5
    p = jax.nn.softmax(sc, axis=-1)
    o = jnp.einsum('bhsm,bmhd->bshd', p.astype(mkv.dtype), mkv[:, :, 1])
    return o.reshape(b, s, MEM_W).astype(pm.dtype)


def mlstm_chunkwise(q, k, v, i_pre, logf, c0, n0, m0):
    b, s, h, d = q.shape
    L = MLSTM_CHUNK if s % MLSTM_CHUNK == 0 else s
    nc = s // L

    def chunks(a):
        return jnp.swapaxes(a.reshape(b, nc, L, *a.shape[2:]), 0, 1)

    causal = jnp.tril(jnp.ones((L, L), dtype=bool))

    def step(carry, xs):
        c, n, m = carry
        qc, kc, vc, ic, fc = xs
        bcum = jnp.swapaxes(jnp.cumsum(fc, axis=1), 1, 2)
        ig = jnp.swapaxes(ic, 1, 2)
        log_d = jnp.where(causal, bcum[..., :, None] - bcum[..., None, :] + ig[..., None, :], -jnp.inf)
        log_inter = bcum + m[..., None]
        m_t = jnp.maximum(jnp.max(log_d, axis=-1), log_inter)
        w_intra = jnp.exp(log_d - m_t[..., None])
        w_inter = jnp.exp(log_inter - m_t)
        sc = jnp.einsum('bthd,bshd->bhts', qc, kc) * w_intra
        num = jnp.einsum('bhts,bshd->bhtd', sc, vc) + w_inter[..., None] * jnp.einsum('bthd,bhde->bhte', qc, c)
        den = jnp.sum(sc, axis=-1) + w_inter * jnp.einsum('bthd,bhd->bht', qc, n)
        hh = num / jnp.maximum(jnp.abs(den), jnp.exp(-m_t))[..., None]
        g = bcum[..., -1]
        m_new = m_t[..., -1]
        w_s = jnp.exp(g[..., None] - bcum + ig - m_new[..., None])
        decay = jnp.exp(g + m - m_new)
        c_new = decay[..., None, None] * c + jnp.einsum('bhs,bshd,bshe->bhde', w_s, kc, vc)
        n_new = decay[..., None] * n + jnp.einsum('bhs,bshd->bhd', w_s, kc)
        return (c_new, n_new, m_new), jnp.swapaxes(hh, 1, 2)

    (c, n, m), hs = lax.scan(step, (c0, n0, m0), tuple(chunks(a) for a in (q, k, v, i_pre, logf)))
    return jnp.swapaxes(hs, 0, 1).reshape(b, s, h, d), (c, n, m)


def mlstm_mixer(p, gates_pre, state, b_if, g_hnorm):
    b, s, _ = p.shape

    def heads(a):
        return a.reshape(b, s, MLSTM_HEADS, MLSTM_DH).astype(F32)

    q = heads(p[..., :MIX_W])
    k = heads(p[..., MIX_W:2 * MIX_W]) * MLSTM_DH ** -0.5
    v = heads(p[..., 2 * MIX_W:3 * MIX_W])
    o = p[..., 3 * MIX_W:4 * MIX_W].astype(F32)
    gates = gates_pre + b_if.astype(F32)
    i_pre = gates[..., :MLSTM_HEADS]
    logf = jax.nn.log_sigmoid(gates[..., MLSTM_HEADS:])
    c0, n0, m0 = (a.astype(F32) for a in state)
    h, new_state = mlstm_chunkwise(q, k, v, i_pre, logf, c0, n0, m0)
    h = rmsnorm(h, g_hnorm).reshape(b, s, MIX_W)
    return (jax.nn.sigmoid(o) * h).astype(p.dtype), new_state


def compress(cmp_kv, w1, b1, w2, b2, pos_enc, g_k_cmp):
    b, t = cmp_kv.shape[:2]
    r_n = CMP_LEN // CMP_STRIDE
    nc = (t - CMP_LEN) // CMP_STRIDE + 1
    nseg = nc + r_n - 1
    seg = cmp_kv[:, :nseg * CMP_STRIDE].reshape(b, nseg, CMP_STRIDE, 2, NSA_KV_HEADS, HEAD_DIM)
    w1r = w1.reshape(2, r_n, CMP_STRIDE, HEAD_DIM, CMP_HIDDEN)
    pe = jnp.einsum('pcd,cpdh->ch', pos_enc, w1)
    pre = (b1 + pe)[None, None, :, None, :]
    for r in range(r_n):
        pre = pre + jnp.einsum('bnucgd,cudh->bncgh', seg[:, r:r + nc], w1r[:, r])
    out = jnp.einsum('bncgh,chd->bncgd', jax.nn.gelu(pre), w2) + b2[None, None, :, None, :]
    ck = rmsnorm(out[:, :, 0], g_k_cmp)
    cv = out[:, :, 1]
    c_end = jnp.arange(nc) * CMP_STRIDE + CMP_LEN - 1
    return ck, cv, c_end


def cmp_to_slc_matrix(n_cmp, n_slc):
    c0 = np.arange(n_cmp)[:, None] * CMP_STRIDE
    s0 = np.arange(n_slc)[None, :] * SLC_BLOCK
    ov = np.clip(np.minimum(c0 + CMP_LEN, s0 + SLC_BLOCK) - np.maximum(c0, s0), 0, None)
    return jnp.asarray(ov / CMP_STRIDE, dtype=jnp.float32)


def cmp_branch(q, qpos, ck, cv, c_end, table):
    sc = jnp.einsum('bsgzd,bngd->bsgzn', q, ck, preferred_element_type=jnp.float32) * HEAD_DIM ** -0.5
    dist = qpos[:, None] - c_end[None, :]
    bias = jnp.transpose(table[rel_bucket(dist)], (0, 2, 3, 1))
    p = masked_softmax(sc + bias[None], (dist >= 0)[:, None, None, :])
    o = jnp.einsum('bsgzn,bngd->bsgzd', p.astype(cv.dtype), cv)
    return o, p


def select_blocks(p_cmp, qpos, n_slc):
    imp = jnp.einsum('bsgzn,nj->bsgj', p_cmp, cmp_to_slc_matrix(p_cmp.shape[-1], n_slc))
    j = jnp.arange(n_slc)
    cur = (qpos // SLC_BLOCK)[:, None]
    valid = j <= cur
    forced = (j == 0) | (j == cur) | (j == cur - 1)
    score = jnp.where(forced[None, :, None], FORCED, jnp.where(valid[None, :, None], imp, -1.0))
    _, idx = lax.top_k(score, min(N_SELECT, n_slc))
    return idx


def slc_branch(q, qpos, idx, fetch, table):
    b, s = q.shape[:2]
    qb = SLC_QBLOCK if s % SLC_QBLOCK == 0 else s
    nb = s // qb
    g_i = jnp.arange(NSA_KV_HEADS)[None, None, :, None]

    def blk(args):
        qx, px, ix = args
        tok = (ix[..., None] * SLC_BLOCK + jnp.arange(SLC_BLOCK)).reshape(b, qb, NSA_KV_HEADS, -1)
        kx, vx = fetch(tok)
        sc = jnp.einsum('bsgzd,bsgjd->bsgzj', qx, kx, preferred_element_type=jnp.float32) * HEAD_DIM ** -0.5
        dist = px[None, :, None, None] - tok
        bias = jnp.swapaxes(table[rel_bucket(dist), g_i], -1, -2)
        p = masked_softmax(sc + bias, (dist >= 0)[:, :, :, None, :])
        return jnp.einsum('bsgzj,bsgjd->bsgzd', p.astype(vx.dtype), vx)

    qs = jnp.swapaxes(q.reshape(b, nb, qb, NSA_KV_HEADS, NSA_GROUP, HEAD_DIM), 0, 1)
    ids = jnp.swapaxes(idx.reshape(b, nb, qb, NSA_KV_HEADS, idx.shape[-1]), 0, 1)
    out = lax.map(blk, (qs, qpos.reshape(nb, qb), ids))
    return jnp.swapaxes(out, 0, 1).reshape(b, s, NSA_KV_HEADS, NSA_GROUP, HEAD_DIM)


def win_branch(q, q_pos0, wkv, wpos0, table):
    b, s = q.shape[:2]
    qb = WIN_QBLOCK if s % WIN_QBLOCK == 0 else s
    nb = s // qb
    lk = WINDOW + qb
    wpad = jnp.pad(wkv, ((0, 0), (WINDOW, 0), (0, 0), (0, 0), (0, 0)))
    starts = q_pos0 + jnp.arange(nb) * qb
    offs = jnp.arange(lk)
    kidx = starts[:, None] - wpos0 + offs[None, :]
    kpos = starts[:, None] - WINDOW + offs[None, :]

    def blk(args):
        qx, px, ki, kp = args
        kv = jnp.take(wpad, ki, axis=1)
        sc = jnp.einsum('bigzd,bjgd->bgzij', qx, kv[:, :, 0], preferred_element_type=jnp.float32) * HEAD_DIM ** -0.5
        dist = px[:, None] - kp[None, :]
        bias = jnp.transpose(table[rel_bucket(dist)], (2, 3, 0, 1))
        mask = (dist >= 0) & (dist < WINDOW) & (kp >= wpos0)[None, :]
        p = masked_softmax(sc + bias[None], mask)
        return jnp.einsum('bgzij,bjgd->bigzd', p.astype(kv.dtype), kv[:, :, 1])

    qs = jnp.swapaxes(q.reshape(b, nb, qb, NSA_KV_HEADS, NSA_GROUP, HEAD_DIM), 0, 1)
    qpos = (q_pos0 + jnp.arange(s)).reshape(nb, qb)
    out = lax.map(blk, (qs, qpos, kidx, kpos))
    return jnp.swapaxes(out, 0, 1).reshape(b, s, NSA_KV_HEADS, NSA_GROUP, HEAD_DIM)


def nsa_mixer(pq, gate_pre, q_pos0, ctx, g_q, b_gate, rel_table):
    b, s, _ = pq.shape
    q = rmsnorm(pq.reshape(b, s, NSA_HEADS, HEAD_DIM), g_q)
    q = q.reshape(b, s, NSA_KV_HEADS, NSA_GROUP, HEAD_DIM)
    gates = jax.nn.sigmoid(gate_pre + b_gate.astype(F32))
    gates = gates.reshape(b, s, 3, NSA_KV_HEADS, NSA_GROUP, 1)
    qpos = q_pos0 + jnp.arange(s)
    table = rel_table.astype(F32).reshape(REL_BUCKETS, NSA_KV_HEADS, NSA_GROUP)
    o_cmp, p_cmp = cmp_branch(q, qpos, ctx['ck'], ctx['cv'], ctx['c_end'], table)
    idx = select_blocks(p_cmp, qpos, ctx['n_slc'])
    o_slc = slc_branch(q, qpos, idx, ctx['fetch'], table)
    o_win = win_branch(q, q_pos0, ctx['wkv'], ctx['wpos0'], table)
    o = gates[:, :, 0] * o_cmp + gates[:, :, 1] * o_slc + gates[:, :, 2] * o_win
    return o.reshape(b, s, MIX_W)


GATE_PAD = LANES


def _pad_cols(w, n):
    return jnp.pad(w, ((0, 0), (0, n - w.shape[1])))


def prep_weights(prm):
    w = {}
    n_gate_a = 2 * MLSTM_HEADS
    n_gate_b = 3 * NSA_HEADS
    w['in_a'] = [jnp.concatenate([prm['w_in_a'][l][:, :4 * MIX_W],
                                  prm['w_in_a'][l][:, 4 * MIX_W + n_gate_a:],
                                  _pad_cols(prm['w_in_a'][l][:, 4 * MIX_W:4 * MIX_W + n_gate_a], GATE_PAD)],
                                 axis=1).astype(BF16) for l in range(N_A_LAYERS)]
    w['in_b'] = []
    for j in range(N_B_LAYERS):
        wg = prm['w_in_b'][j][:, MIX_W:MIX_W + n_gate_b].reshape(D_MODEL, 3, NSA_KV_HEADS, NSA_GROUP)
        slabs = [_pad_cols(wg[:, :, g].reshape(D_MODEL, 3 * NSA_GROUP), GATE_PAD) for g in range(NSA_KV_HEADS)]
        w['in_b'].append(jnp.concatenate([prm['w_in_b'][j][:, :MIX_W],
                                          prm['w_in_b'][j][:, MIX_W + n_gate_b:]] + slabs, axis=1).astype(BF16))
    w['kv'] = prm['w_kv'].astype(BF16)
    w['out'] = prm['w_out'].astype(BF16)
    w['gu'] = prm['w_gu'].astype(BF16)
    w['down'] = prm['w_down'].astype(BF16)
    return w


def _nsa_gate_cols(p):
    base = MIX_W + MEM_W
    slabs = [p[..., base + g * GATE_PAD:base + g * GATE_PAD + 3 * NSA_GROUP].reshape(*p.shape[:-1], 3, 1, NSA_GROUP)
             for g in range(NSA_KV_HEADS)]
    return jnp.concatenate(slabs, axis=-2).reshape(*p.shape[:-1], 3 * NSA_HEADS)


def pack_cmp(ck, cv):
    def pack(a):
        a = jnp.transpose(a, (0, 2, 1, 3))
        return jnp.pad(a, ((0, 0), (0, 0), (0, LANES - a.shape[2]), (0, LANES - a.shape[3]))).astype(BF16)
    return pack(ck), pack(cv)


def run_trunk(x, q_pos0, mem_kv, init_states, build_ctx, prm, w, prompt):
    b, s, d = x.shape
    x = x.reshape(b * s, d)
    states = []
    ctx = None
    rows = None
    for l in range(DEPTH):
        if l < N_A_LAYERS:
            p = norm_matmul(x, prm['g_mix'][l], w['in_a'][l]).reshape(b, s, -1)
            gates_pre = p[..., 4 * MIX_W + MEM_W:4 * MIX_W + MEM_W + 2 * MLSTM_HEADS]
            pm = p[..., 4 * MIX_W:4 * MIX_W + MEM_W]
            mix, st = mlstm_mixer(p[..., :4 * MIX_W], gates_pre, init_states[l], prm['b_if'][l], prm['g_hnorm'][l])
            states.append(st)
        else:
            j = l - N_A_LAYERS
            p2 = norm_matmul(x, prm['g_mix'][l], w['in_b'][j])
            p = p2.reshape(b, s, -1)
            pm = p[..., MIX_W:MIX_W + MEM_W]
            if prompt:
                mix = nsa_prompt(p2, b, s, prm['g_q'][j], prm['b_gate'][j], prm['rel_table'], ctx['ck'], ctx['cv'],
                                 ctx['ka'], ctx['va'], ctx['tiles'], ctx['cb']).reshape(b, s, MIX_W)
            else:
                mix = nsa_mixer(p[..., :MIX_W], _nsa_gate_cols(p), q_pos0, ctx, prm['g_q'][j], prm['b_gate'][j],
                                prm['rel_table'])
        mem_o = mem_attn(pm, mem_kv[l], prm['g_mem_q'][l])
        cat = jnp.concatenate([mix, mem_o], axis=-1).reshape(b * s, d)
        x = matmul_res(cat, w['out'][l], x)
        x = ffn(x, prm['g_ffn'][l], w['gu'][l], w['down'][l])
        if l == N_A_LAYERS - 1:
            outs = kv_rows(x, prm['g_kv'], w['kv'], prm['g_k_slc'], prm['g_k_win'], s, aug=prompt)
            rows = tuple(r.reshape(b, s, 2, NSA_KV_HEADS, HEAD_DIM) for r in outs[:3])
            ctx = build_ctx(*rows)
            if prompt:
                ctx['ka'], ctx['va'] = outs[3], outs[4]
    new_states = tuple(jnp.stack([st[i] for st in states]) for i in range(3))
    return x.reshape(b, s, d), new_states, rows


def kernel(x_prompt, x_sample, state_mlstm_c, state_mlstm_n, state_mlstm_m, cache_mem_kv,
           cache_cmp_kv, cache_slc_kv, cache_win_kv, page_table, mem_prompt,
           g_mix, w_in_a, b_if, g_hnorm, w_in_b, g_q, b_gate, rel_table, g_kv, w_kv,
           g_k_slc, g_k_win, g_k_cmp, cmp_pos, cmp_w1, cmp_b1, cmp_w2, cmp_b2,
           w_mem_kv, g_mem_k, g_mem_q, w_out, g_ffn, w_gu, w_down):
    prm = {'g_mix': g_mix, 'w_in_a': w_in_a, 'b_if': b_if, 'g_hnorm': g_hnorm, 'w_in_b': w_in_b,
           'g_q': g_q, 'b_gate': b_gate, 'rel_table': rel_table, 'g_kv': g_kv, 'w_kv': w_kv,
           'g_k_slc': g_k_slc, 'g_k_win': g_k_win, 'g_mem_q': g_mem_q, 'w_out': w_out,
           'g_ffn': g_ffn, 'w_gu': w_gu, 'w_down': w_down}
    w = prep_weights(prm)
    g_i = jnp.arange(NSA_KV_HEADS)[None, None, :, None]

    b_p, s_p, _ = x_prompt.shape
    m_tok = mem_prompt.shape[1]
    mem_flat = mem_prompt.reshape(b_p * m_tok, D_MODEL)
    ones = jnp.ones((D_MODEL,), F32)
    mem_kv_list = []
    for l in range(DEPTH):
        kv = norm_matmul(mem_flat, ones, w_mem_kv[l].astype(BF16), use_norm=False)
        kv = kv.reshape(b_p, m_tok, 2, MEM_HEADS, HEAD_DIM)
        mem_kv_list.append(jnp.stack([rmsnorm(kv[:, :, 0], g_mem_k[l]), kv[:, :, 1]], axis=2))
    mem_kv_p = jnp.stack(mem_kv_list)
    init_p = [(jnp.zeros((b_p, MLSTM_HEADS, MLSTM_DH, MLSTM_DH), F32),
               jnp.zeros((b_p, MLSTM_HEADS, MLSTM_DH), F32),
               jnp.full((b_p, MLSTM_HEADS), NEG, F32)) for _ in range(N_A_LAYERS)]
    tiles, cb = nsa_bias_tiles(rel_table, s_p)

    def ctx_prompt(cmp_rows, slc_rows, win_rows):
        ck, cv, _ = compress(cmp_rows, cmp_w1, cmp_b1, cmp_w2, cmp_b2, cmp_pos, g_k_cmp)
        ck, cv = pack_cmp(ck, cv)
        return {'ck': ck, 'cv': cv, 'tiles': tiles, 'cb': cb}

    y_p, st_p, rows_p = run_trunk(x_prompt, 0, mem_kv_p, init_p, ctx_prompt, prm, w, True)

    b_s, s_s, _ = x_sample.shape
    past = page_table.shape[1] * PAGE_SIZE
    wb = cache_win_kv.shape[1]
    init_s = [(state_mlstm_c[l], state_mlstm_n[l], state_mlstm_m[l]) for l in range(N_A_LAYERS)]
    b_is = jnp.arange(b_s)[:, None, None, None]

    def ctx_sample(cmp_new, slc_new, win_new):
        cmp_past = cache_cmp_kv[page_table].reshape(b_s, past, 2, NSA_KV_HEADS, HEAD_DIM)
        ck, cv, c_end = compress(jnp.concatenate([cmp_past, cmp_new], axis=1),
                                 cmp_w1, cmp_b1, cmp_w2, cmp_b2, cmp_pos, g_k_cmp)

        def fetch(tok):
            tp = jnp.minimum(tok, past - 1)
            phys = page_table[b_is, tp // PAGE_SIZE]
            kv_old = cache_slc_kv[phys, tp % PAGE_SIZE, :, g_i]
            kv_new = slc_new[b_is, jnp.clip(tok - past, 0, s_s - 1), :, g_i]
            kv = jnp.where((tok < past)[..., None, None], kv_old, kv_new)
            return kv[..., 0, :], kv[..., 1, :]

        return {'ck': ck, 'cv': cv, 'c_end': c_end, 'n_slc': -(-(past + s_s) // SLC_BLOCK),
                'fetch': fetch, 'wkv': jnp.concatenate([cache_win_kv, win_new], axis=1), 'wpos0': past - wb}

    y_s, st_s, rows_s = run_trunk(x_sample, past, cache_mem_kv, init_s, ctx_sample, prm, w, False)

    p_win = rows_p[2][:, s_p - min(WINDOW, s_p):]
    s_win = jnp.concatenate([cache_win_kv, rows_s[2]], axis=1)[:, -wb:]
    return (y_p, y_s, st_p[0], st_p[1], st_p[2], mem_kv_p, rows_p[0], rows_p[1], p_win,
            st_s[0], st_s[1], st_s[2], rows_s[0], rows_s[1], s_win)
```

```python
import math
from functools import partial

import numpy as np
import jax
import jax.numpy as jnp
from jax import lax
from jax.experimental import pallas as pl
from jax.experimental.pallas import tpu as pltpu

D_MODEL = 1024
DEPTH = 4
PAGE_SIZE = 128
N_A_LAYERS = DEPTH // 2
N_B_LAYERS = DEPTH - N_A_LAYERS
MIX_W = 3 * D_MODEL // 4
MEM_W = D_MODEL - MIX_W
HEAD_DIM = 64
MEM_HEADS = MEM_W // HEAD_DIM
MLSTM_HEADS = 4
MLSTM_DH = MIX_W // MLSTM_HEADS
NSA_HEADS = MIX_W // HEAD_DIM
NSA_KV_HEADS = 2
NSA_GROUP = NSA_HEADS // NSA_KV_HEADS
CMP_LEN = 32
CMP_STRIDE = 16
CMP_HIDDEN = 128
SLC_BLOCK = 64
N_SELECT = 8
WINDOW = 512
REL_BUCKETS = 32
REL_MAX_EXACT = 16
REL_MAX_DIST = 128
D_FF = -(-(8 * D_MODEL) // (3 * 256)) * 256
EPS = 1e-6
NEG = -1e30
FORCED = 1e6

LANES = 128
VMEM_LIMIT = 48 * 1024 * 1024

BF16 = jnp.bfloat16
F32 = jnp.float32


def _pick_tile(n, candidates):
    for c in candidates:
        if n % c == 0:
            return c
    return n


def _pad_cols(w, n):
    return jnp.pad(w, ((0, 0), (0, n - w.shape[1])))


def _dot_nt(a, b):
    return lax.dot_general(a, b, (((1,), (1,)), ((), ())), preferred_element_type=F32)


def _rms_rows(x, g):
    return x * lax.rsqrt(jnp.mean(x * x, axis=-1, keepdims=True) + EPS) * g


def _half_rms(x, g2, lo):
    x2 = x * x
    ss_lo = jnp.sum(jnp.where(lo, x2, 0.0), axis=1, keepdims=True)
    ss_hi = jnp.sum(jnp.where(lo, 0.0, x2), axis=1, keepdims=True)
    inv = jnp.where(lo, lax.rsqrt(ss_lo / HEAD_DIM + EPS), lax.rsqrt(ss_hi / HEAD_DIM + EPS))
    return x * inv * g2


def _gain2(g):
    return jnp.concatenate([g, g]).reshape(1, LANES).astype(F32)


def _params(*sem):
    return pltpu.CompilerParams(dimension_semantics=sem, vmem_limit_bytes=VMEM_LIMIT)


def _norm_matmul_kernel(x_ref, g_ref, w_ref, o_ref, xn_ref):
    @pl.when(pl.program_id(1) == 0)
    def _():
        xn_ref[...] = _rms_rows(x_ref[...], g_ref[...]).astype(BF16)

    o_ref[...] = jnp.dot(xn_ref[...], w_ref[...], preferred_element_type=F32)


def norm_matmul(x, g, w):
    m, k = x.shape
    n = w.shape[1]
    tm = _pick_tile(m, (512, 256, 128))
    tn = _pick_tile(n, (1152, 1024, 896, 768, 640, 512, 384, 256, 128))
    return pl.pallas_call(
        _norm_matmul_kernel,
        grid=(m // tm, n // tn),
        in_specs=[pl.BlockSpec((tm, k), lambda i, j: (i, 0)),
                  pl.BlockSpec((1, k), lambda i, j: (0, 0)),
                  pl.BlockSpec((k, tn), lambda i, j: (0, j))],
        out_specs=pl.BlockSpec((tm, tn), lambda i, j: (i, j)),
        out_shape=jax.ShapeDtypeStruct((m, n), F32),
        scratch_shapes=[pltpu.VMEM((tm, k), BF16)],
        compiler_params=_params("parallel", "arbitrary"),
        name="norm_matmul",
    )(x, g.reshape(1, k), w)


def _mem_kv_kernel(x_ref, w_ref, g_ref, o_ref):
    tm = x_ref.shape[0]
    kv = jnp.dot(x_ref[...].astype(BF16), w_ref[...], preferred_element_type=F32)
    lo = lax.broadcasted_iota(jnp.int32, (tm, LANES), 1) < HEAD_DIM
    for k in range(MEM_HEADS // 2):
        o_ref[:, k * LANES:(k + 1) * LANES] = _half_rms(kv[:, k * LANES:(k + 1) * LANES], g_ref[...], lo)
    o_ref[:, MEM_W:] = kv[:, MEM_W:]


def mem_kv_rows(mem, w, g_k):
    m, d = mem.shape
    depth, _, n = w.shape
    tm = _pick_tile(m, (512, 256, 128))
    g2 = jnp.concatenate([g_k, g_k], axis=1).reshape(depth, 1, LANES).astype(F32)
    return pl.pallas_call(
        _mem_kv_kernel,
        grid=(depth, m // tm),
        in_specs=[pl.BlockSpec((tm, d), lambda l, i: (i, 0)),
                  pl.BlockSpec((None, d, n), lambda l, i: (l, 0, 0)),
                  pl.BlockSpec((None, 1, LANES), lambda l, i: (l, 0, 0))],
        out_specs=pl.BlockSpec((None, tm, n), lambda l, i: (l, i, 0)),
        out_shape=jax.ShapeDtypeStruct((depth, m, n), F32),
        compiler_params=_params("parallel", "parallel"),
        name="mem_kv_rows",
    )(mem, w, g2)


def _out_proj_kernel(a_ref, wa_ref, b_ref, wb_ref, r_ref, o_ref):
    o_ref[...] = (r_ref[...] + jnp.dot(a_ref[...].astype(BF16), wa_ref[...], preferred_element_type=F32)
                  + jnp.dot(b_ref[...].astype(BF16), wb_ref[...], preferred_element_type=F32))


def out_proj(a, wa, b, wb, res):
    m, n = res.shape
    tm = _pick_tile(m, (512, 256, 128))
    ka, kb = a.shape[1], b.shape[1]
    return pl.pallas_call(
        _out_proj_kernel,
        grid=(m // tm,),
        in_specs=[pl.BlockSpec((tm, ka), lambda i: (i, 0)),
                  pl.BlockSpec((ka, n), lambda i: (0, 0)),
                  pl.BlockSpec((tm, kb), lambda i: (i, 0)),
                  pl.BlockSpec((kb, n), lambda i: (0, 0)),
                  pl.BlockSpec((tm, n), lambda i: (i, 0))],
        out_specs=pl.BlockSpec((tm, n), lambda i: (i, 0)),
        out_shape=jax.ShapeDtypeStruct((m, n), F32),
        compiler_params=_params("parallel"),
        name="out_proj",
    )(a, wa, b, wb, res)


def _ffn_kernel(x_ref, g_ref, wg_ref, wu_ref, wd_ref, o_ref, xn_ref, acc_ref):
    f = pl.program_id(1)

    @pl.when(f == 0)
    def _():
        xn_ref[...] = _rms_rows(x_ref[...], g_ref[...]).astype(BF16)
        acc_ref[...] = jnp.zeros_like(acc_ref)

    xn = xn_ref[...]
    gate = jnp.dot(xn, wg_ref[...], preferred_element_type=F32)
    up = jnp.dot(xn, wu_ref[...], preferred_element_type=F32)
    act = (gate * jax.nn.sigmoid(gate) * up).astype(BF16)
    acc_ref[...] += jnp.dot(act, wd_ref[...], preferred_element_type=F32)

    @pl.when(f == pl.num_programs(1) - 1)
    def _():
        o_ref[...] = x_ref[...] + acc_ref[...]


def ffn(x, g, w_gu, w_down):
    m, d = x.shape
    tm = _pick_tile(m, (512, 256, 128))
    tf = 1408
    nf = D_FF // tf
    return pl.pallas_call(
        _ffn_kernel,
        grid=(m // tm, nf),
        in_specs=[pl.BlockSpec((tm, d), lambda i, f: (i, 0)),
                  pl.BlockSpec((1, d), lambda i, f: (0, 0)),
                  pl.BlockSpec((d, tf), lambda i, f: (0, f)),
                  pl.BlockSpec((d, tf), lambda i, f: (0, f + nf)),
                  pl.BlockSpec((tf, d), lambda i, f: (f, 0))],
        out_specs=pl.BlockSpec((tm, d), lambda i, f: (i, 0)),
        out_shape=jax.ShapeDtypeStruct((m, d), F32),
        scratch_shapes=[pltpu.VMEM((tm, d), BF16), pltpu.VMEM((tm, d), F32)],
        compiler_params=_params("parallel", "arbitrary"),
        name="ffn",
    )(x, g.reshape(1, d), w_gu, w_gu, w_down)


MLSTM_L = 128
MLSTM_PAD = 256
N_COL = MLSTM_DH
MLSTM_HW = MLSTM_HEADS * MLSTM_PAD
STEP_TB = 8


def _exact_tri_cumsum(tri, x):
    hi = x.astype(BF16)
    r1 = x - hi.astype(F32)
    mid = r1.astype(BF16)
    lo = (r1 - mid.astype(F32)).astype(BF16)
    return (jnp.dot(tri, hi, preferred_element_type=F32) + jnp.dot(tri, mid, preferred_element_type=F32)
            + jnp.dot(tri, lo, preferred_element_type=F32))


def _mlstm_kernel(q_ref, k_ref, v_ref, og_ref, gate_ref, bif_ref, gh_ref, c0_ref, m0_ref,
                  mix_ref, cout_ref, mout_ref, c_scr, m_scr):
    L = MLSTM_L
    P = MLSTM_PAD
    ci = pl.program_id(1)

    @pl.when(ci == 0)
    def _():
        c_scr[...] = c0_ref[...]
        m_scr[...] = m0_ref[...]

    row = lax.broadcasted_iota(jnp.int32, (L, L), 0)
    col = lax.broadcasted_iota(jnp.int32, (L, L), 1)
    causal = col <= row
    tri = jnp.where(causal, 1.0, 0.0).astype(BF16)
    lane_p = lax.broadcasted_iota(jnp.int32, (L, P), 1)

    gl = gate_ref[...] + bif_ref[...]
    logf = jax.nn.log_sigmoid(gl)
    bc = _exact_tri_cumsum(tri, logf)
    bt = bc.T
    gt = gl.T
    for h in range(MLSTM_HEADS):
        fh = MLSTM_HEADS + h
        b_col = bc[:, fh:fh + 1]
        b_row = bt[fh:fh + 1, :]
        i_row = gt[h:h + 1, :]
        i_col = gl[:, h:h + 1]
        m_prev = m_scr[0:1, h:h + 1]
        log_d = jnp.where(causal, b_col - b_row + i_row, -jnp.inf)
        log_inter = b_col + m_prev
        m_t = jnp.maximum(jnp.max(log_d, axis=1, keepdims=True), log_inter)
        w_intra = jnp.exp(log_d - m_t)
        w_inter = jnp.exp(log_inter - m_t)
        sl = slice(h * P, (h + 1) * P)
        q = q_ref[:, sl].astype(BF16)
        kf = k_ref[:, sl] * MLSTM_DH ** -0.5
        v_aug = jnp.where(lane_p == N_COL, 1.0, v_ref[:, sl])
        sc = _dot_nt(q, kf.astype(BF16)) * w_intra
        c = c_scr[h]
        nd = (jnp.dot(sc.astype(BF16), v_aug.astype(BF16), preferred_element_type=F32)
              + w_inter * jnp.dot(q, c.astype(BF16), preferred_element_type=F32))
        den = nd[:, N_COL:N_COL + 1]
        hh = jnp.where(lane_p < N_COL, nd / jnp.maximum(jnp.abs(den), jnp.exp(-m_t)), 0.0)
        y = hh * lax.rsqrt(jnp.sum(hh * hh, axis=1, keepdims=True) / MLSTM_DH + EPS) * gh_ref[...]
        mix_ref[:, sl] = jax.nn.sigmoid(og_ref[:, sl]) * y
        g_last = bc[L - 1:L, fh:fh + 1]
        m_new = m_t[L - 1:L, :]
        w_s = jnp.exp(g_last - b_col + i_col - m_new)
        decay = jnp.exp(g_last + m_prev - m_new)
        upd = jnp.dot(kf.T.astype(BF16), (w_s * v_aug).astype(BF16), preferred_element_type=F32)
        c_scr[h] = decay * c + upd
        m_scr[0:1, h:h + 1] = m_new

    @pl.when(ci == pl.num_programs(1) - 1)
    def _():
        cout_ref[...] = c_scr[...]
        mout_ref[...] = m_scr[...]


def mlstm_prompt(p, b, s, b_if, g_hnorm, c0, m0):
    L = MLSTM_L
    P = MLSTM_PAD
    hw = MLSTM_HW
    assert s % L == 0
    nc = s // L
    gcol = (4 * hw + MEM_W) // LANES
    bif = _pad_cols(b_if.astype(F32).reshape(1, -1), LANES)
    gh = _pad_cols(g_hnorm.astype(F32).reshape(1, -1), P)
    blk = lambda j: pl.BlockSpec((L, hw), lambda bi, ci: (bi * nc + ci, j))
    return pl.pallas_call(
        _mlstm_kernel,
        grid=(b, nc),
        in_specs=[blk(0), blk(1), blk(2), blk(3),
                  pl.BlockSpec((L, LANES), lambda bi, ci: (bi * nc + ci, gcol)),
                  pl.BlockSpec((1, LANES), lambda bi, ci: (0, 0)),
                  pl.BlockSpec((1, P), lambda bi, ci: (0, 0)),
                  pl.BlockSpec((None, MLSTM_HEADS, P, P), lambda bi, ci: (bi, 0, 0, 0)),
                  pl.BlockSpec((None, 1, LANES), lambda bi, ci: (bi, 0, 0))],
        out_specs=[pl.BlockSpec((L, hw), lambda bi, ci: (bi * nc + ci, 0)),
                   pl.BlockSpec((None, MLSTM_HEADS, P, P), lambda bi, ci: (bi, 0, 0, 0)),
                   pl.BlockSpec((None, 1, LANES), lambda bi, ci: (bi, 0, 0))],
        out_shape=[jax.ShapeDtypeStruct((b * s, hw), F32),
                   jax.ShapeDtypeStruct((b, MLSTM_HEADS, P, P), F32),
                   jax.ShapeDtypeStruct((b, 1, LANES), F32)],
        scratch_shapes=[pltpu.VMEM((MLSTM_HEADS, P, P), F32), pltpu.VMEM((1, LANES), F32)],
        compiler_params=_params("parallel", "arbitrary"),
        name="mlstm_prompt",
    )(p, p, p, p, p, bif, gh, c0, m0)


def pack_mlstm_state(c, n, m):
    pad = MLSTM_PAD - MLSTM_DH
    cn = jnp.concatenate([c, n[..., None]], axis=-1)
    cn = jnp.pad(cn, ((0, 0), (0, 0), (0, pad), (0, pad - 1)))
    return cn, _pad_cols(m, LANES)[:, None, :]


def unpack_mlstm_state(cn, m):
    return cn[:, :, :MLSTM_DH, :MLSTM_DH], cn[:, :, :MLSTM_DH, MLSTM_DH], m[:, 0, :MLSTM_HEADS]


def _mlstm_step_kernel(q_ref, v_ref, og_ref, gate_ref, qkt_ref, bif_ref, gh_ref, c_ref, n_ref, m_ref,
                       mix_ref, cout_ref, nout_ref, mout_ref):
    tb = q_ref.shape[0]
    dh = MLSTM_DH
    gl = gate_ref[...] + bif_ref[...]
    logf = jax.nn.log_sigmoid(gl)
    i4 = gl[:, :MLSTM_HEADS]
    f4 = logf[:, MLSTM_HEADS:2 * MLSTM_HEADS]
    m_prev = m_ref[...]
    m_t = jnp.maximum(i4, f4 + m_prev)
    w_in = jnp.exp(i4 - m_t)
    w_dec = jnp.exp(f4 + m_prev - m_t)
    floor = jnp.exp(-m_t)
    mout_ref[...] = m_t
    mix_ref[...] = jnp.zeros(mix_ref.shape, F32)
    for j in range(tb):
        for h in range(MLSTM_HEADS):
            sl = slice(h * MLSTM_PAD, h * MLSTM_PAD + dh)
            wi = w_in[j:j + 1, h:h + 1]
            wd = w_dec[j:j + 1, h:h + 1]
            q_row = q_ref[j:j + 1, sl]
            v_row = v_ref[j:j + 1, sl]
            q_col = qkt_ref[0, h, :, j:j + 1]
            k_col = qkt_ref[1, h, :, j:j + 1]
            k_row = qkt_ref[2, h, j:j + 1, :]
            c = c_ref[j, h]
            n_row = n_ref[j, h:h + 1, :]
            qk = jnp.sum(q_row * k_row, axis=1, keepdims=True) * wi
            qc = jnp.sum(q_col * c, axis=0, keepdims=True)
            qn = jnp.sum(q_row * n_row, axis=1, keepdims=True)
            num = qk * v_row + wd * qc
            den = qk + wd * qn
            hh = num / jnp.maximum(jnp.abs(den), floor[j:j + 1, h:h + 1])
            y = hh * lax.rsqrt(jnp.sum(hh * hh, axis=1, keepdims=True) / dh + EPS) * gh_ref[...]
            mix_ref[j:j + 1, sl] = jax.nn.sigmoid(og_ref[j:j + 1, sl]) * y
            cout_ref[j, h] = wd * c + k_col * (wi * v_row)
            nout_ref[j, h:h + 1, :] = wd * n_row + wi * k_row


def mlstm_step(p, b_if, g_hnorm, c, n, m):
    b = p.shape[0]
    tb = STEP_TB
    hw = MLSTM_HW
    dh = MLSTM_DH
    nb = b // tb
    gcol = (4 * hw + MEM_W) // LANES
    q = p[:, :hw].reshape(b, MLSTM_HEADS, MLSTM_PAD)[:, :, :dh]
    k = p[:, hw:2 * hw].reshape(b, MLSTM_HEADS, MLSTM_PAD)[:, :, :dh] * dh ** -0.5
    cols = lambda a: a.reshape(nb, tb, MLSTM_HEADS, dh).transpose(0, 2, 3, 1)
    pad_c = jnp.pad(jnp.stack([cols(q), cols(k)], axis=1), ((0, 0), (0, 0), (0, 0), (0, 0), (0, dh - tb)))
    k_rows = k.reshape(nb, tb, MLSTM_HEADS, dh).transpose(0, 2, 1, 3)
    pad_r = jnp.pad(k_rows, ((0, 0), (0, 0), (0, dh - tb), (0, 0)))
    qkt = jnp.concatenate([pad_c, pad_r[:, None]], axis=1)
    bif = _pad_cols(b_if.astype(F32).reshape(1, -1), LANES)
    gh = g_hnorm.astype(F32).reshape(1, dh)
    blk = lambda j: pl.BlockSpec((tb, hw), lambda i: (i, j))
    st4 = pl.BlockSpec((tb, MLSTM_HEADS, dh, dh), lambda i: (i, 0, 0, 0))
    st3 = pl.BlockSpec((tb, MLSTM_HEADS, dh), lambda i: (i, 0, 0))
    st2 = pl.BlockSpec((tb, MLSTM_HEADS), lambda i: (i, 0))
    return pl.pallas_call(
        _mlstm_step_kernel,
        grid=(nb,),
        in_specs=[blk(0), blk(2), blk(3),
                  pl.BlockSpec((tb, LANES), lambda i: (i, gcol)),
                  pl.BlockSpec((None, 3, MLSTM_HEADS, dh, dh), lambda i: (i, 0, 0, 0, 0)),
                  pl.BlockSpec((1, LANES), lambda i: (0, 0)),
                  pl.BlockSpec((1, dh), lambda i: (0, 0)),
                  st4, st3, st2],
        out_specs=[pl.BlockSpec((tb, hw), lambda i: (i, 0)), st4, st3, st2],
        out_shape=[jax.ShapeDtypeStruct((b, hw), F32), jax.ShapeDtypeStruct(c.shape, F32),
                   jax.ShapeDtypeStruct(n.shape, F32), jax.ShapeDtypeStruct(m.shape, F32)],
        compiler_params=_params("parallel"),
        name="mlstm_step",
    )(p, p, p, p, qkt, bif, gh, c, n, m)


def _mem_attn_kernel(q_ref, gq_ref, kv_ref, o_ref):
    tq = q_ref.shape[0]
    lane = lax.broadcasted_iota(jnp.int32, (tq, LANES), 1)
    lo = lane < HEAD_DIM
    for k in range(MEM_HEADS // 2):
        y = _half_rms(q_ref[:, k * LANES:(k + 1) * LANES], gq_ref[...], lo) * HEAD_DIM ** -0.5
        kp = kv_ref[:, k * LANES:(k + 1) * LANES].astype(BF16)
        vp = kv_ref[:, MEM_W + k * LANES:MEM_W + (k + 1) * LANES].astype(BF16)
        outs = []
        for half in range(2):
            qh = jnp.where(lo if half == 0 else ~lo, y, 0.0).astype(BF16)
            s = _dot_nt(qh, kp)
            p = jnp.exp(s - jnp.max(s, axis=1, keepdims=True))
            p = p / jnp.sum(p, axis=1, keepdims=True)
            outs.append(jnp.dot(p.astype(BF16), vp, preferred_element_type=F32))
        o_ref[:, k * LANES:(k + 1) * LANES] = jnp.where(lo, outs[0], outs[1])


def mem_attn_prompt(p, qcol, b, s, g_q, mem_kv):
    tq = _pick_tile(s, (512, 256, 128))
    nq = s // tq
    m_tok = mem_kv.shape[1]
    return pl.pallas_call(
        _mem_attn_kernel,
        grid=(b, nq),
        in_specs=[pl.BlockSpec((tq, MEM_W), lambda bi, qi: (bi * nq + qi, qcol)),
                  pl.BlockSpec((1, LANES), lambda bi, qi: (0, 0)),
                  pl.BlockSpec((None, m_tok, 2 * MEM_W), lambda bi, qi: (bi, 0, 0))],
        out_specs=pl.BlockSpec((tq, MEM_W), lambda bi, qi: (bi * nq + qi, 0)),
        out_shape=jax.ShapeDtypeStruct((b * s, MEM_W), F32),
        compiler_params=_params("parallel", "parallel"),
        name="mem_attn",
    )(p, _gain2(g_q), mem_kv)


def _mem_decode_kernel(q_ref, gq_ref, kv_ref, o_ref):
    tb = q_ref.shape[0]
    lane = lax.broadcasted_iota(jnp.int32, (tb, LANES), 1)
    lo = lane < HEAD_DIM
    y = jnp.concatenate([_half_rms(q_ref[:, k * LANES:(k + 1) * LANES], gq_ref[...], lo)
                         for k in range(MEM_HEADS // 2)], axis=1) * HEAD_DIM ** -0.5
    rowi = lax.broadcasted_iota(jnp.int32, (8, MEM_W), 0)
    own = (lax.broadcasted_iota(jnp.int32, (8, MEM_W), 1) // HEAD_DIM) == rowi
    for j in range(tb):
        qbd = jnp.where(own, y[j:j + 1, :], 0.0).astype(BF16)
        s = _dot_nt(qbd, kv_ref[j, :, :MEM_W].astype(BF16))
        p = jnp.exp(s - jnp.max(s, axis=1, keepdims=True))
        p = p / jnp.sum(p, axis=1, keepdims=True)
        o = jnp.dot(p.astype(BF16), kv_ref[j, :, MEM_W:].astype(BF16), preferred_element_type=F32)
        o_ref[j:j + 1, :] = jnp.sum(jnp.where(own, o, 0.0), axis=0, keepdims=True)


def mem_attn_decode(p, qcol, g_q, mem_kv):
    b = p.shape[0]
    tb = STEP_TB
    m_tok = mem_kv.shape[1]
    return pl.pallas_call(
        _mem_decode_kernel,
        grid=(b // tb,),
        in_specs=[pl.BlockSpec((tb, MEM_W), lambda i: (i, qcol)),
                  pl.BlockSpec((1, LANES), lambda i: (0, 0)),
                  pl.BlockSpec((tb, m_tok, 2 * MEM_W), lambda i: (i, 0, 0))],
        out_specs=pl.BlockSpec((tb, MEM_W), lambda i: (i, 0)),
        out_shape=jax.ShapeDtypeStruct((b, MEM_W), F32),
        compiler_params=_params("parallel"),
        name="mem_attn_decode",
    )(p, _gain2(g_q), mem_kv)


NSA_TQ = 128
MASK = -(2.0 ** 100)
SEL_LANE = HEAD_DIM
N_SEL_LANES = 32
CONST_LANE = SEL_LANE + N_SEL_LANES
KV_ROW = 2 * NSA_KV_HEADS * HEAD_DIM
SEG_PER_PAGE = PAGE_SIZE // CMP_STRIDE
SEG_ROW = CMP_STRIDE * KV_ROW
CMP_OUT = 2 * NSA_KV_HEADS * CMP_HIDDEN
QROWS = 16


def _rel_bucket_np(dist):
    d = np.maximum(dist, 0)
    ratio = np.maximum(d, REL_MAX_EXACT).astype(np.float64) / REL_MAX_EXACT
    large = REL_MAX_EXACT + (np.log(ratio) / math.log(REL_MAX_DIST / REL_MAX_EXACT)
                             * (REL_BUCKETS - REL_MAX_EXACT)).astype(np.int32)
    return np.where(d < REL_MAX_EXACT, d, np.minimum(large, REL_BUCKETS - 1)).astype(np.int32)


def _overlap_matrix(n_cmp, n_slc):
    c0 = np.arange(n_cmp)[:, None] * CMP_STRIDE
    s0 = np.arange(n_slc)[None, :] * SLC_BLOCK
    return np.clip(np.minimum(c0 + CMP_LEN, s0 + SLC_BLOCK) - np.maximum(c0, s0), 0, None) / CMP_STRIDE


def _kv_rows_kernel(x_ref, g_ref, w_ref, gs_ref, gw_ref, cmp_ref, slc_ref, win_ref, *aug_refs, seq_len):
    tm = x_ref.shape[0]
    xn = _rms_rows(x_ref[...], g_ref[...]).astype(BF16)
    kv = jnp.dot(xn, w_ref[...], preferred_element_type=F32)
    cmp_ref[...] = kv[:, :KV_ROW]
    lane = lax.broadcasted_iota(jnp.int32, (tm, LANES), 1)
    lo = lane < HEAD_DIM
    if aug_refs:
        ka_ref, va_ref = aug_refs
        pos = (pl.program_id(0) * tm + lax.broadcasted_iota(jnp.int32, (tm, LANES), 0)) % seq_len
        ones = (lane == CONST_LANE) | (lane == CONST_LANE + 1)
        onehot = (lane - SEL_LANE) == pos // SLC_BLOCK
    for t, (rows_ref, gk_ref) in enumerate(((slc_ref, gs_ref), (win_ref, gw_ref))):
        base = KV_ROW * (t + 1)
        kn = _half_rms(kv[:, base:base + LANES], gk_ref[...], lo)
        vv = kv[:, base + LANES:base + 2 * LANES]
        rows_ref[:, :LANES] = kn
        rows_ref[:, LANES:] = vv
        if aug_refs:
            extra = jnp.where((ones | onehot) if t == 0 else ones, 1.0, 0.0)
            for g in range(NSA_KV_HEADS):
                kg = kn if g == 0 else pltpu.roll(kn, HEAD_DIM, 1)
                vg = vv if g == 0 else pltpu.roll(vv, HEAD_DIM, 1)
                ka_ref[NSA_KV_HEADS * t + g] = jnp.where(lo, kg, extra).astype(BF16)
                va_ref[NSA_KV_HEADS * t + g] = jnp.where(lo, vg, 0.0).astype(BF16)


def kv_rows(x, g_kv, w_kv, g_k_slc, g_k_win, seq_len, aug):
    m, d = x.shape
    n = w_kv.shape[1]
    tm = _pick_tile(m, (512, 256, 128))
    out_specs = [pl.BlockSpec((tm, KV_ROW), lambda i: (i, 0))] * 3
    out_shape = [jax.ShapeDtypeStruct((m, KV_ROW), F32)] * 3
    if aug:
        out_specs += [pl.BlockSpec((2 * NSA_KV_HEADS, tm, LANES), lambda i: (0, i, 0))] * 2
        out_shape += [jax.ShapeDtypeStruct((2 * NSA_KV_HEADS, m, LANES), BF16)] * 2
    return pl.pallas_call(
        partial(_kv_rows_kernel, seq_len=seq_len),
        grid=(m // tm,),
        in_specs=[pl.BlockSpec((tm, d), lambda i: (i, 0)),
                  pl.BlockSpec((1, d), lambda i: (0, 0)),
                  pl.BlockSpec((d, n), lambda i: (0, 0)),
                  pl.BlockSpec((1, LANES), lambda i: (0, 0)),
                  pl.BlockSpec((1, LANES), lambda i: (0, 0))],
        out_specs=out_specs,
        out_shape=out_shape,
        compiler_params=_params("parallel"),
        name="kv_rows",
    )(x, g_kv.reshape(1, d), w_kv, _gain2(g_k_slc), _gain2(g_k_win))


def _compress_kernel(*refs, n_x, paged):
    if paged:
        refs = refs[1:]
    x_refs = refs[:n_x]
    w1_ref, b1_ref, w2_ref, b2_ref, gk_ref, o_ref = refs[n_x:]
    n_seg = o_ref.shape[0]
    acc = [jnp.zeros((n_seg, CMP_OUT), F32) for _ in range(CMP_LEN // CMP_STRIDE)]
    for u in range(CMP_STRIDE):
        pieces = [r[:, u * KV_ROW:(u + 1) * KV_ROW] for r in x_refs]
        xu = (pieces[0] if n_x == 1 else jnp.concatenate(pieces, axis=0)).astype(BF16)
        for r in range(len(acc)):
            acc[r] = acc[r] + jnp.dot(xu, w1_ref[r, u], preferred_element_type=F32)
    pre = acc[0] + pltpu.roll(acc[1], n_seg - 1, 0) + b1_ref[...]
    hid = jax.nn.gelu(pre).astype(BF16)
    out = jnp.dot(hid, w2_ref[...], preferred_element_type=F32) + b2_ref[...]
    lane = lax.broadcasted_iota(jnp.int32, (n_seg, LANES), 1)
    o_ref[:, :LANES] = _half_rms(out[:, :LANES], gk_ref[...], lane < HEAD_DIM).astype(BF16)
    o_ref[:, LANES:] = out[:, LANES:].astype(BF16)


def compress_weights(w1, b1, w2, b2, pos_enc, g_k_cmp):
    r_n = CMP_LEN // CMP_STRIDE
    eye = jnp.eye(NSA_KV_HEADS, dtype=F32)
    w1r = w1.reshape(2, r_n, CMP_STRIDE, HEAD_DIM, CMP_HIDDEN)
    w1b = jnp.einsum('crudh,cx,gy->rucgdxyh', w1r, eye, eye).reshape(r_n, CMP_STRIDE, KV_ROW, CMP_OUT).astype(BF16)
    pe = jnp.einsum('pcd,cpdh->ch', pos_enc, w1)
    b1b = jnp.broadcast_to((b1 + pe)[:, None, :], (2, NSA_KV_HEADS, CMP_HIDDEN)).reshape(1, CMP_OUT)
    w2b = jnp.einsum('chd,cx,gy->cghxyd', w2, eye, eye).reshape(CMP_OUT, KV_ROW).astype(BF16)
    b2b = jnp.broadcast_to(b2[:, None, :], (2, NSA_KV_HEADS, HEAD_DIM)).reshape(1, KV_ROW)
    return w1b, b1b, w2b, b2b, _gain2(g_k_cmp)


def compress_rows(x, cw, page_table=None):
    w1b, b1b, w2b, b2b, gk2 = cw
    paged = page_table is not None
    if paged:
        n_seq, n_x = page_table.shape
        x_specs = [pl.BlockSpec((None, SEG_PER_PAGE, SEG_ROW), partial(lambda k, b, pt: (pt[b, k], 0, 0), k))
                   for k in range(n_x)]
        n_seg = n_x * SEG_PER_PAGE
        cmap = lambda *idx: lambda b, pt: idx
    else:
        n_seq, n_seg, n_x = x.shape[0], x.shape[1], 1
        x_specs = [pl.BlockSpec((None, n_seg, SEG_ROW), lambda b: (b, 0, 0))]
        cmap = lambda *idx: lambda b: idx
    w_specs = [pl.BlockSpec(w1b.shape, cmap(0, 0, 0, 0)), pl.BlockSpec(b1b.shape, cmap(0, 0)),
               pl.BlockSpec(w2b.shape, cmap(0, 0)), pl.BlockSpec(b2b.shape, cmap(0, 0)),
               pl.BlockSpec(gk2.shape, cmap(0, 0))]
    out_spec = pl.BlockSpec((None, n_seg, KV_ROW), (lambda b, pt: (b, 0, 0)) if paged else (lambda b: (b, 0, 0)))
    out_shape = jax.ShapeDtypeStruct((n_seq, n_seg, KV_ROW), BF16)
    kern = partial(_compress_kernel, n_x=n_x, paged=paged)
    if paged:
        gs = pltpu.PrefetchScalarGridSpec(num_scalar_prefetch=1, grid=(n_seq,), in_specs=x_specs + w_specs,
                                          out_specs=out_spec)
        return pl.pallas_call(kern, grid_spec=gs, out_shape=out_shape, compiler_params=_params("parallel"),
                              name="compress_paged")(page_table, *([x] * n_x), w1b, b1b, w2b, b2b, gk2)
    return pl.pallas_call(kern, grid=(n_seq,), in_specs=x_specs + w_specs, out_specs=out_spec, out_shape=out_shape,
                          compiler_params=_params("parallel"), name="compress")(x, w1b, b1b, w2b, b2b, gk2)


def _nsa_prompt_kernel(rel_ref, q_ref, gate_ref, gq_ref, bg_ref, ck_ref, cv_ref, cb_ref, mt_ref,
                       sk_ref, sv_ref, wk_ref, wv_ref, dl_ref, o_ref,
                       qa_ref, m_ref, l_ref, acc_ref, out_ref):
    tq = NSA_TQ
    rows = NSA_GROUP * tq
    g = pl.program_id(1)
    qi = pl.program_id(2)
    lane = lax.broadcasted_iota(jnp.int32, (tq, LANES), 1)
    lo = lane < HEAD_DIM

    qn = []
    for k in range(NSA_GROUP // 2):
        y = _half_rms(q_ref[:, k * LANES:(k + 1) * LANES], gq_ref[...], lo) * HEAD_DIM ** -0.5
        qn.append(jnp.where(lo, y, 0.0))
        qn.append(jnp.where(lo, pltpu.roll(y, HEAD_DIM, 1), 0.0))

    sig = jax.nn.sigmoid(gate_ref[...] + bg_ref[...])

    def gated(br, z, o):
        c = br * NSA_GROUP + z
        return sig[:, c:c + 1] * o

    qc = jnp.concatenate(qn, axis=0).astype(BF16)
    s = _dot_nt(qc, ck_ref[...]) + cb_ref[...].reshape(rows, LANES)
    m = jnp.maximum(jnp.max(s, axis=1, keepdims=True), NEG)
    p = jnp.exp(s - m)
    l = jnp.sum(p, axis=1, keepdims=True)
    pn = p * jnp.where(l > 0.0, 1.0 / l, 0.0)
    o_cmp = jnp.dot(pn.astype(BF16), cv_ref[...], preferred_element_type=F32)
    for z in range(NSA_GROUP):
        out_ref[z * tq:(z + 1) * tq, :] = gated(0, z, o_cmp[z * tq:(z + 1) * tq, :])

    psum = pn[0:tq, :]
    for z in range(1, NSA_GROUP):
        psum = psum + pn[z * tq:(z + 1) * tq, :]
    p_hi = psum.astype(BF16)
    p_lo = (psum - p_hi.astype(F32)).astype(BF16)
    imp = _dot_nt(mt_ref[...], p_hi) + _dot_nt(mt_ref[...], p_lo)
    jidx = lax.broadcasted_iota(jnp.int32, (N_SEL_LANES, tq), 0)
    tpos = qi * tq + lax.broadcasted_iota(jnp.int32, (N_SEL_LANES, tq), 1)
    cur = tpos // SLC_BLOCK
    forced = (jidx == 0) | (jidx == cur) | (jidx == cur - 1)
    score = jnp.where(forced, FORCED, jnp.where(jidx <= cur, imp, -1.0))
    rank = jnp.zeros((N_SEL_LANES, tq), F32)
    for i in range(N_SEL_LANES):
        ri = score[i:i + 1, :]
        beats = (ri > score) | ((ri == score) & (jidx > i))
        rank = rank + jnp.where(beats, 1.0, 0.0)
    selb = jnp.where(rank < N_SELECT, 0.0, MASK)
    selb = jnp.concatenate([selb, jnp.zeros((LANES - N_SEL_LANES, tq), F32)], axis=0)
    sel_cols = pltpu.roll(selb.T, SEL_LANE, 1)

    for z in range(NSA_GROUP):
        c = jnp.full((tq, LANES), rel_ref[REL_BUCKETS - 1, g * NSA_GROUP + z], F32)
        c_hi = c.astype(BF16).astype(F32)
        extra = jnp.where(lane == CONST_LANE, c_hi, jnp.where(lane == CONST_LANE + 1, c - c_hi, sel_cols))
        qa_ref[z * tq:(z + 1) * tq, :] = jnp.where(lo, qn[z], extra).astype(BF16)

    def flash_init():
        m_ref[...] = jnp.full((rows, LANES), NEG, F32)
        l_ref[...] = jnp.zeros((rows, LANES), F32)
        acc_ref[...] = jnp.zeros((rows, LANES), F32)

    def flash_block(k_ref, v_ref, kb, didx):
        start = pl.multiple_of(kb * tq, tq)
        s = _dot_nt(qa_ref[...], k_ref[pl.ds(start, tq), :]) + dl_ref[didx].reshape(rows, LANES)
        m_prev = m_ref[...]
        m_next = jnp.maximum(m_prev, jnp.max(s, axis=1, keepdims=True))
        p = jnp.exp(s - m_next)
        alpha = jnp.exp(m_prev - m_next)
        l_ref[...] = alpha * l_ref[...] + jnp.sum(p, axis=1, keepdims=True)
        acc_ref[...] = alpha * acc_ref[...] + jnp.dot(p.astype(BF16), v_ref[pl.ds(start, tq), :],
                                                      preferred_element_type=F32)
        m_ref[...] = m_next

    def flash_finish(br):
        o = acc_ref[...] / l_ref[...]
        for z in range(NSA_GROUP):
            out_ref[z * tq:(z + 1) * tq, :] += gated(br, z, o[z * tq:(z + 1) * tq, :])

    flash_init()

    def slc_body(kb, carry):
        didx = jnp.where(kb == qi, 2, jnp.where(kb == qi - 1, 1, 0))
        flash_block(sk_ref, sv_ref, kb, didx)
        return carry

    lax.fori_loop(0, qi + 1, slc_body, 0)
    flash_finish(1)

    flash_init()
    n_win = WINDOW // tq

    def win_body(kb, carry):
        d = qi - kb
        didx = jnp.where(d == 0, 2, jnp.where(d == 1, 1, jnp.where(d == n_win, 3, 0)))
        flash_block(wk_ref, wv_ref, kb, didx)
        return carry

    lax.fori_loop(jnp.maximum(qi - n_win, 0), qi + 1, win_body, 0)
    flash_finish(2)

    for k in range(NSA_GROUP // 2):
        even = out_ref[(2 * k) * tq:(2 * k + 1) * tq, :]
        odd = out_ref[(2 * k + 1) * tq:(2 * k + 2) * tq, :]
        o_ref[:, k * LANES:(k + 1) * LANES] = even + pltpu.roll(odd, HEAD_DIM, 1)


def nsa_bias_tiles(rel_table, s):
    tq = NSA_TQ
    table = rel_table.astype(F32)
    far = table[REL_BUCKETS - 1]
    r = np.arange(tq)[:, None]
    c = np.arange(tq)[None, :]

    def tile(off, masked):
        dist = off + r - c
        t = jnp.transpose(table[_rel_bucket_np(dist)], (2, 0, 1)) - far[:, None, None]
        return jnp.where(jnp.asarray(masked(dist))[None], MASK, t)

    zero = jnp.zeros((NSA_HEADS, tq, tq), F32)
    d128 = tile(tq, lambda d: d < 0)
    d0 = tile(0, lambda d: d < 0)
    dwin = tile(WINDOW, lambda d: d >= WINDOW)
    tiles = jnp.stack([zero, d128, d0, dwin])
    n_cmp = (s - CMP_LEN) // CMP_STRIDE + 1
    c_end = np.arange(LANES) * CMP_STRIDE + CMP_LEN - 1
    dist = np.arange(s)[:, None] - c_end[None, :]
    cb = jnp.transpose(table[_rel_bucket_np(dist)], (2, 0, 1))
    bad = (dist < 0) | (np.arange(LANES)[None, :] >= n_cmp)
    cb = jnp.where(jnp.asarray(bad)[None], MASK, cb)
    return tiles, cb


def _gate_bias_slabs(b_gate):
    bg = b_gate.astype(F32).reshape(3, NSA_KV_HEADS, NSA_GROUP).transpose(1, 0, 2).reshape(NSA_KV_HEADS, 3 * NSA_GROUP)
    return _pad_cols(bg, LANES)


def nsa_prompt(p, b, s, g_q, b_gate, rel_table, ck, cv, ka, va, tiles, cb):
    tq = NSA_TQ
    nq = s // tq
    rows = NSA_GROUP * tq
    n_cmp = (s - CMP_LEN) // CMP_STRIDE + 1
    n_slc = -(-s // SLC_BLOCK)
    assert s % tq == 0 and n_cmp <= LANES and n_slc <= N_SEL_LANES
    mt = np.zeros((N_SEL_LANES, LANES), np.float32)
    mt[:n_slc, :n_cmp] = _overlap_matrix(n_cmp, n_slc).T
    bg = _gate_bias_slabs(b_gate).reshape(NSA_KV_HEADS, 1, LANES)
    qcol = (MIX_W + MEM_W) // LANES
    kmap = lambda off: (lambda bi, g, qi: (off + g, bi, 0, 0))
    return pl.pallas_call(
        _nsa_prompt_kernel,
        grid=(b, NSA_KV_HEADS, nq),
        in_specs=[pl.BlockSpec(memory_space=pltpu.SMEM),
                  pl.BlockSpec((tq, NSA_GROUP * HEAD_DIM), lambda bi, g, qi: (bi * nq + qi, g)),
                  pl.BlockSpec((tq, LANES), lambda bi, g, qi: (bi * nq + qi, qcol + g)),
                  pl.BlockSpec((1, LANES), lambda bi, g, qi: (0, 0)),
                  pl.BlockSpec((None, 1, LANES), lambda bi, g, qi: (g, 0, 0)),
                  pl.BlockSpec((None, None, LANES, LANES), lambda bi, g, qi: (bi, g, 0, 0)),
                  pl.BlockSpec((None, None, LANES, LANES), lambda bi, g, qi: (bi, g, 0, 0)),
                  pl.BlockSpec((NSA_GROUP, tq, LANES), lambda bi, g, qi: (g, qi, 0)),
                  pl.BlockSpec((N_SEL_LANES, LANES), lambda bi, g, qi: (0, 0)),
                  pl.BlockSpec((None, None, s, LANES), kmap(0)),
                  pl.BlockSpec((None, None, s, LANES), kmap(0)),
                  pl.BlockSpec((None, None, s, LANES), kmap(NSA_KV_HEADS)),
                  pl.BlockSpec((None, None, s, LANES), kmap(NSA_KV_HEADS)),
                  pl.BlockSpec((4, NSA_GROUP, tq, tq), lambda bi, g, qi: (0, g, 0, 0))],
        out_specs=pl.BlockSpec((tq, NSA_GROUP * HEAD_DIM), lambda bi, g, qi: (bi * nq + qi, g)),
        out_shape=jax.ShapeDtypeStruct((b * s, MIX_W), F32),
        scratch_shapes=[pltpu.VMEM((rows, LANES), BF16)] + [pltpu.VMEM((rows, LANES), F32)] * 4,
        compiler_params=_params("parallel", "parallel", "arbitrary"),
        name="nsa_prompt",
    )(rel_table.astype(F32), p, p, _gain2(g_q), bg, ck, cv, cb, jnp.asarray(mt, BF16),
      ka.reshape(4, b, s, LANES), va.reshape(4, b, s, LANES),
      ka.reshape(4, b, s, LANES), va.reshape(4, b, s, LANES), tiles)


def split_cmp_ctx(ctx):
    b, n, _ = ctx.shape
    parts = ctx.reshape(b, n, 2, NSA_KV_HEADS, HEAD_DIM).transpose(2, 0, 3, 1, 4)
    parts = jnp.pad(parts, ((0, 0), (0, 0), (0, 0), (0, 0), (0, LANES - HEAD_DIM)))
    return parts[0], parts[1]


def _softmax_parts(s, s_new):
    m = jnp.maximum(jnp.max(s, axis=1, keepdims=True), s_new)
    p = jnp.exp(s - m)
    p_new = jnp.exp(s_new - m)
    inv = 1.0 / (jnp.sum(p, axis=1, keepdims=True) + p_new)
    return p * inv, p_new * inv


def _nsa_decode_kernel(*refs, n_pages):
    (pt_ref, p_ref, gq_ref, bg_ref, cmp_ref, cb_ref, mt_ref) = refs[:7]
    page_refs = refs[7:7 + n_pages]
    (snew_ref, sb_ref, e_ref, win_ref, wnew_ref, wb_ref, b0_ref, o_ref) = refs[7 + n_pages:]
    del pt_ref
    n_keys = n_pages * PAGE_SIZE
    lo1 = lax.broadcasted_iota(jnp.int32, (1, LANES), 1) < HEAD_DIM
    rowi = lax.broadcasted_iota(jnp.int32, (QROWS, LANES), 0)
    lanei = lax.broadcasted_iota(jnp.int32, (QROWS, LANES), 1)
    row_g = rowi // 8
    own = (lanei // HEAD_DIM) == row_g

    qbd = jnp.zeros((QROWS, LANES), F32)
    for k in range(NSA_HEADS // 2):
        y = _half_rms(p_ref[:, k * LANES:(k + 1) * LANES], gq_ref[...], lo1) * HEAD_DIM ** -0.5
        yr = pltpu.roll(y, HEAD_DIM, 1)
        for half in range(2):
            g, z = divmod(2 * k + half, NSA_GROUP)
            qbd = jnp.where(rowi == 8 * g + z, y if half == g else yr, qbd)
    qbd = jnp.where(own, qbd, 0.0)
    qb = qbd.astype(BF16)

    s = _dot_nt(qb, cmp_ref[:, :LANES]) + cb_ref[...]
    m = jnp.maximum(jnp.max(s, axis=1, keepdims=True), NEG)
    p = jnp.exp(s - m)
    l = jnp.sum(p, axis=1, keepdims=True)
    pn = p * jnp.where(l > 0.0, 1.0 / l, 0.0)
    o_cmp = jnp.dot(pn.astype(BF16), cmp_ref[:, LANES:], preferred_element_type=F32)

    valid_row = (rowi % 8) < NSA_GROUP
    ps0 = jnp.sum(jnp.where(valid_row & (row_g == 0), pn, 0.0), axis=0, keepdims=True)
    ps1 = jnp.sum(jnp.where(valid_row & (row_g == 1), pn, 0.0), axis=0, keepdims=True)
    r8 = lax.broadcasted_iota(jnp.int32, (8, LANES), 0)
    j8 = lax.broadcasted_iota(jnp.int32, (8, LANES), 1)
    psum = jnp.where(r8 == 0, ps0, jnp.where(r8 == 1, ps1, 0.0))
    p_hi = psum.astype(BF16)
    p_lo = (psum - p_hi.astype(F32)).astype(BF16)
    imp = (jnp.dot(p_hi, mt_ref[...], preferred_element_type=F32)
           + jnp.dot(p_lo, mt_ref[...], preferred_element_type=F32))
    cur = n_keys // SLC_BLOCK
    forced = (j8 == 0) | (j8 == cur) | (j8 == cur - 1)
    score = jnp.where(forced, FORCED, jnp.where(j8 <= cur, imp, -1.0))
    rank = jnp.zeros((8, LANES), F32)
    for i in range(cur + 1):
        ri = score[:, i:i + 1]
        beats = (ri > score) | ((ri == score) & (j8 > i))
        rank = rank + jnp.where(beats, 1.0, 0.0)
    selb = jnp.where(rank < N_SELECT, 0.0, MASK)
    sel16 = jnp.where(row_g == 0, selb[0:1, :], selb[1:2, :]).astype(BF16)
    blockmask = jnp.dot(sel16, e_ref[...], preferred_element_type=F32)

    s = jnp.concatenate([_dot_nt(qb, r[:, :LANES].astype(BF16)) for r in page_refs], axis=1)
    s = s + sb_ref[...] + blockmask
    s_new = jnp.sum(qbd * snew_ref[:, :LANES], axis=1, keepdims=True) + b0_ref[...]
    pp, p_new = _softmax_parts(s, s_new)
    o_slc = p_new * snew_ref[:, LANES:]
    for k, r in enumerate(page_refs):
        o_slc = o_slc + jnp.dot(pp[:, k * PAGE_SIZE:(k + 1) * PAGE_SIZE].astype(BF16), r[:, LANES:].astype(BF16),
                                preferred_element_type=F32)

    n_wt = win_ref.shape[0] // LANES
    s = jnp.concatenate([_dot_nt(qb, win_ref[k * LANES:(k + 1) * LANES, :LANES].astype(BF16)) for k in range(n_wt)],
                        axis=1) + wb_ref[...]
    s_new = jnp.sum(qbd * wnew_ref[:, :LANES], axis=1, keepdims=True) + b0_ref[...]
    pp, p_new = _softmax_parts(s, s_new)
    o_win = p_new * wnew_ref[:, LANES:]
    for k in range(n_wt):
        o_win = o_win + jnp.dot(pp[:, k * LANES:(k + 1) * LANES].astype(BF16),
                                win_ref[k * LANES:(k + 1) * LANES, LANES:].astype(BF16), preferred_element_type=F32)

    gbase = MIX_W + MEM_W
    sig = jax.nn.sigmoid(p_ref[:, gbase:gbase + NSA_KV_HEADS * LANES] + bg_ref[...])
    gate = [jnp.zeros((QROWS, 1), F32) for _ in range(3)]
    rcol = lax.broadcasted_iota(jnp.int32, (QROWS, 1), 0)
    for br in range(3):
        for g in range(NSA_KV_HEADS):
            for z in range(NSA_GROUP):
                c = g * LANES + br * NSA_GROUP + z
                gate[br] = jnp.where(rcol == 8 * g + z, sig[:, c:c + 1], gate[br])
    comb = jnp.where(own, gate[0] * o_cmp + gate[1] * o_slc + gate[2] * o_win, 0.0)
    comb_r = pltpu.roll(comb, HEAD_DIM, 1)
    for k in range(NSA_HEADS // 2):
        acc = jnp.zeros((1, LANES), F32)
        for half in range(2):
            g, z = divmod(2 * k + half, NSA_GROUP)
            src = comb if half == g else comb_r
            acc = acc + src[8 * g + z:8 * g + z + 1, :]
        o_ref[:, k * LANES:(k + 1) * LANES] = acc


def nsa_decode_bias(rel_table, past, wb):
    table = rel_table.astype(F32)

    def rows16(t):
        z = jnp.zeros((2, t.shape[1]), F32)
        return jnp.concatenate([t[:NSA_GROUP], z, t[NSA_GROUP:], z], axis=0)

    n_cmp = (past + 1 - CMP_LEN) // CMP_STRIDE + 1
    c_end = np.arange(LANES) * CMP_STRIDE + CMP_LEN - 1
    dist = past - c_end
    cb = jnp.where(jnp.asarray((dist < 0) | (np.arange(LANES) >= n_cmp))[None], MASK, table[_rel_bucket_np(dist)].T)
    sb = table[_rel_bucket_np(past - np.arange(past))].T
    wpos = past - wb + np.arange(wb)
    wbias = jnp.where(jnp.asarray(past - wpos >= WINDOW)[None], MASK, table[_rel_bucket_np(past - wpos)].T)
    b0 = table[0][:, None]
    return rows16(cb), rows16(sb), rows16(wbias), rows16(b0)


def nsa_decode(p, g_q, b_gate, cmp_ctx, page_table, cache_slc, slc_new, cache_win, win_new, biases):
    b, n_pages = page_table.shape
    past = n_pages * PAGE_SIZE
    wb = cache_win.shape[1]
    cb, sb, wbias, b0 = biases
    n_cmp = (past + 1 - CMP_LEN) // CMP_STRIDE + 1
    n_slc = -(-(past + 1) // SLC_BLOCK)
    assert n_slc <= LANES and n_cmp <= LANES and wb % LANES == 0
    mt = np.zeros((LANES, LANES), np.float32)
    mt[:n_cmp, :n_slc] = _overlap_matrix(n_cmp, n_slc)
    e = (np.arange(past)[None, :] // SLC_BLOCK == np.arange(LANES)[:, None]).astype(np.float32)
    bg = _gate_bias_slabs(b_gate).reshape(1, NSA_KV_HEADS * LANES)
    c2 = lambda bi, pt: (0, 0)
    seq3 = lambda bi, pt: (bi, 0, 0)
    in_specs = ([pl.BlockSpec((None, 1, p.shape[1]), seq3),
                 pl.BlockSpec((1, LANES), c2),
                 pl.BlockSpec((1, NSA_KV_HEADS * LANES), c2),
                 pl.BlockSpec((None, LANES, KV_ROW), seq3),
                 pl.BlockSpec((QROWS, LANES), c2),
                 pl.BlockSpec((LANES, LANES), c2)]
                + [pl.BlockSpec((None, PAGE_SIZE, KV_ROW), partial(lambda k, bi, pt: (pt[bi, k], 0, 0), k))
                   for k in range(n_pages)]
                + [pl.BlockSpec((None, 1, KV_ROW), seq3),
                   pl.BlockSpec((QROWS, past), c2),
                   pl.BlockSpec((LANES, past), c2),
                   pl.BlockSpec((None, wb, KV_ROW), seq3),
                   pl.BlockSpec((None, 1, KV_ROW), seq3),
                   pl.BlockSpec((QROWS, wb), c2),
                   pl.BlockSpec((QROWS, 1), c2)])
    gs = pltpu.PrefetchScalarGridSpec(num_scalar_prefetch=1, grid=(b,), in_specs=in_specs,
                                      out_specs=pl.BlockSpec((None, 1, MIX_W), seq3))
    out = pl.pallas_call(
        partial(_nsa_decode_kernel, n_pages=n_pages),
        grid_spec=gs,
        out_shape=jax.ShapeDtypeStruct((b, 1, MIX_W), F32),
        compiler_params=_params("parallel"),
        name="nsa_decode",
    )(page_table, p.reshape(b, 1, -1), _gain2(g_q), bg, cmp_ctx, cb, jnp.asarray(mt, BF16),
      *([cache_slc] * n_pages), slc_new.reshape(b, 1, KV_ROW), sb, jnp.asarray(e, BF16),
      cache_win, win_new.reshape(b, 1, KV_ROW), wbias, b0)
    return out.reshape(b, MIX_W)


GATE_PAD = LANES


def _pad_heads_cols(w):
    d = w.shape[0]
    return jnp.pad(w.reshape(d, MLSTM_HEADS, MLSTM_DH), ((0, 0), (0, 0), (0, MLSTM_PAD - MLSTM_DH))).reshape(d, MLSTM_HW)


def prep_weights(w_in_a, w_in_b, w_kv, w_mem_kv, w_out, w_gu, w_down):
    w = {}
    n_gate_a = 2 * MLSTM_HEADS
    n_gate_b = 3 * NSA_HEADS
    w['in_a'] = []
    for l in range(N_A_LAYERS):
        wa = w_in_a[l]
        qkvo = [_pad_heads_cols(wa[:, j * MIX_W:(j + 1) * MIX_W]) for j in range(4)]
        w['in_a'].append(jnp.concatenate(qkvo + [wa[:, 4 * MIX_W + n_gate_a:],
                                                 _pad_cols(wa[:, 4 * MIX_W:4 * MIX_W + n_gate_a], GATE_PAD)],
                                         axis=1).astype(BF16))
    w['in_b'] = []
    for j in range(N_B_LAYERS):
        wg = w_in_b[j][:, MIX_W:MIX_W + n_gate_b].reshape(D_MODEL, 3, NSA_KV_HEADS, NSA_GROUP)
        slabs = [_pad_cols(wg[:, :, g].reshape(D_MODEL, 3 * NSA_GROUP), GATE_PAD) for g in range(NSA_KV_HEADS)]
        w['in_b'].append(jnp.concatenate([w_in_b[j][:, :MIX_W], w_in_b[j][:, MIX_W + n_gate_b:]] + slabs,
                                         axis=1).astype(BF16))
    w['out_mix'] = []
    for l in range(DEPTH):
        wm = w_out[l][:MIX_W]
        if l < N_A_LAYERS:
            wm = jnp.pad(wm.reshape(MLSTM_HEADS, MLSTM_DH, D_MODEL),
                         ((0, 0), (0, MLSTM_PAD - MLSTM_DH), (0, 0))).reshape(MLSTM_HW, D_MODEL)
        w['out_mix'].append(wm.astype(BF16))
    w['out_mem'] = w_out[:, MIX_W:].astype(BF16)
    w['kv'] = w_kv.astype(BF16)
    w['mem_kv'] = w_mem_kv.astype(BF16)
    w['gu'] = w_gu.astype(BF16)
    w['down'] = w_down.astype(BF16)
    return w


def kernel(x_prompt, x_sample, state_mlstm_c, state_mlstm_n, state_mlstm_m, cache_mem_kv,
           cache_cmp_kv, cache_slc_kv, cache_win_kv, page_table, mem_prompt,
           g_mix, w_in_a, b_if, g_hnorm, w_in_b, g_q, b_gate, rel_table, g_kv, w_kv,
           g_k_slc, g_k_win, g_k_cmp, cmp_pos, cmp_w1, cmp_b1, cmp_w2, cmp_b2,
           w_mem_kv, g_mem_k, g_mem_q, w_out, g_ffn, w_gu, w_down):
    w = prep_weights(w_in_a, w_in_b, w_kv, w_mem_kv, w_out, w_gu, w_down)
    cw = compress_weights(cmp_w1, cmp_b1, cmp_w2, cmp_b2, cmp_pos, g_k_cmp)
    qcol_a = 4 * MLSTM_HW // MEM_W
    qcol_b = MIX_W // MEM_W

    def finish_layer(l, x, mix, mem_o):
        x = out_proj(mix, w['out_mix'][l], mem_o, w['out_mem'][l], x)
        return ffn(x, g_ffn[l], w['gu'][l], w['down'][l])

    b_p, s_p, _ = x_prompt.shape
    m_tok = mem_prompt.shape[1]
    mem_kv_p = mem_kv_rows(mem_prompt.reshape(b_p * m_tok, D_MODEL), w['mem_kv'], g_mem_k)
    mem_kv_p = mem_kv_p.reshape(DEPTH, b_p, m_tok, 2 * MEM_W)
    tiles, cbias = nsa_bias_tiles(rel_table, s_p)
    c0, m0 = pack_mlstm_state(jnp.zeros((b_p, MLSTM_HEADS, MLSTM_DH, MLSTM_DH), F32),
                              jnp.zeros((b_p, MLSTM_HEADS, MLSTM_DH), F32),
                              jnp.full((b_p, MLSTM_HEADS), NEG, F32))
    x = x_prompt.reshape(b_p * s_p, D_MODEL)
    st_p = []
    for l in range(N_A_LAYERS):
        p = norm_matmul(x, g_mix[l], w['in_a'][l])
        mix, cn, mm = mlstm_prompt(p, b_p, s_p, b_if[l], g_hnorm[l], c0, m0)
        st_p.append(unpack_mlstm_state(cn, mm))
        mem_o = mem_attn_prompt(p, qcol_a, b_p, s_p, g_mem_q[l], mem_kv_p[l])
        x = finish_layer(l, x, mix, mem_o)
    cmp_p, slc_p, win_p, ka, va = kv_rows(x, g_kv, w['kv'], g_k_slc, g_k_win, s_p, aug=True)
    ck, cv = split_cmp_ctx(compress_rows(cmp_p.reshape(b_p, s_p // CMP_STRIDE, SEG_ROW), cw))
    for j in range(N_B_LAYERS):
        l = N_A_LAYERS + j
        p = norm_matmul(x, g_mix[l], w['in_b'][j])
        mix = nsa_prompt(p, b_p, s_p, g_q[j], b_gate[j], rel_table, ck, cv, ka, va, tiles, cbias)
        mem_o = mem_attn_prompt(p, qcol_b, b_p, s_p, g_mem_q[l], mem_kv_p[l])
        x = finish_layer(l, x, mix, mem_o)
    y_p = x.reshape(b_p, s_p, D_MODEL)

    b_s, s_s, _ = x_sample.shape
    assert s_s == 1
    n_pages = page_table.shape[1]
    past = n_pages * PAGE_SIZE
    wb = cache_win_kv.shape[1]
    n_pool = cache_cmp_kv.shape[0]
    x = x_sample.reshape(b_s, D_MODEL)
    st_s = []
    for l in range(N_A_LAYERS):
        p = norm_matmul(x, g_mix[l], w['in_a'][l])
        mix, c_new, n_new, m_new = mlstm_step(p, b_if[l], g_hnorm[l], state_mlstm_c[l], state_mlstm_n[l],
                                              state_mlstm_m[l])
        st_s.append((c_new, n_new, m_new))
        mem_o = mem_attn_decode(p, qcol_a, g_mem_q[l], cache_mem_kv[l].reshape(b_s, -1, 2 * MEM_W))
        x = finish_layer(l, x, mix, mem_o)
    cmp_s, slc_s, win_s = kv_rows(x, g_kv, w['kv'], g_k_slc, g_k_win, 1, aug=False)
    assert (past + 1 - CMP_LEN) // CMP_STRIDE + 1 == past // CMP_STRIDE - 1
    ctx_s = compress_rows(cache_cmp_kv.reshape(n_pool, SEG_PER_PAGE, SEG_ROW), cw, page_table)
    biases = nsa_decode_bias(rel_table, past, wb)
    cache_slc = cache_slc_kv.reshape(n_pool, PAGE_SIZE, KV_ROW)
    cache_win = cache_win_kv.reshape(b_s, wb, KV_ROW)
    for j in range(N_B_LAYERS):
        l = N_A_LAYERS + j
        p = norm_matmul(x, g_mix[l], w['in_b'][j])
        mix = nsa_decode(p, g_q[j], b_gate[j], ctx_s, page_table, cache_slc, slc_s, cache_win, win_s, biases)
        mem_o = mem_attn_decode(p, qcol_b, g_mem_q[l], cache_mem_kv[l].reshape(b_s, -1, 2 * MEM_W))
        x = finish_layer(l, x, mix, mem_o)
    y_s = x.reshape(b_s, 1, D_MODEL)

    rows5 = lambda r, b: r.reshape(b, -1, 2, NSA_KV_HEADS, HEAD_DIM)
    win_p5 = rows5(win_p, b_p)
    p_win = win_p5[:, s_p - min(WINDOW, s_p):]
    s_win = jnp.concatenate([cache_win_kv, rows5(win_s, b_s)], axis=1)[:, -wb:]
    stack = lambda sts, i: jnp.stack([st[i] for st in sts])
    return (y_p, y_s, stack(st_p, 0), stack(st_p, 1), stack(st_p, 2),
            mem_kv_p.reshape(DEPTH, b_p, m_tok, 2, MEM_HEADS, HEAD_DIM),
            rows5(cmp_p, b_p), rows5(slc_p, b_p), p_win,
            stack(st_s, 0), stack(st_s, 1), stack(st_s, 2),
            rows5(cmp_s, b_s), rows5(slc_s, b_s), s_win)
```

```python
import math
from functools import partial

import numpy as np
import jax
import jax.numpy as jnp
from jax import lax
from jax.experimental import pallas as pl
from jax.experimental.pallas import tpu as pltpu

D_MODEL = 1024
DEPTH = 4
PAGE_SIZE = 128
N_A_LAYERS = DEPTH // 2
N_B_LAYERS = DEPTH - N_A_LAYERS
MIX_W = 3 * D_MODEL // 4
MEM_W = D_MODEL - MIX_W
HEAD_DIM = 64
MEM_HEADS = MEM_W // HEAD_DIM
MLSTM_HEADS = 4
MLSTM_DH = MIX_W // MLSTM_HEADS
NSA_HEADS = MIX_W // HEAD_DIM
NSA_KV_HEADS = 2
NSA_GROUP = NSA_HEADS // NSA_KV_HEADS
CMP_LEN = 32
CMP_STRIDE = 16
CMP_HIDDEN = 128
SLC_BLOCK = 64
N_SELECT = 8
WINDOW = 512
REL_BUCKETS = 32
REL_MAX_EXACT = 16
REL_MAX_DIST = 128
D_FF = -(-(8 * D_MODEL) // (3 * 256)) * 256
EPS = 1e-6
NEG = -1e30
FORCED = 1e6

LANES = 128
VMEM_LIMIT = 48 * 1024 * 1024

BF16 = jnp.bfloat16
F32 = jnp.float32


def _pick_tile(n, candidates):
    for c in candidates:
        if n % c == 0:
            return c
    return n


def _pad_cols(w, n):
    return jnp.pad(w, ((0, 0), (0, n - w.shape[1])))


def _dot_nt(a, b):
    return lax.dot_general(a, b, (((1,), (1,)), ((), ())), preferred_element_type=F32)


def _rms_rows(x, g):
    return x * lax.rsqrt(jnp.mean(x * x, axis=-1, keepdims=True) + EPS) * g


def _half_rms(x, g2, lo):
    x2 = x * x
    ss_lo = jnp.sum(jnp.where(lo, x2, 0.0), axis=1, keepdims=True)
    ss_hi = jnp.sum(jnp.where(lo, 0.0, x2), axis=1, keepdims=True)
    inv = jnp.where(lo, lax.rsqrt(ss_lo / HEAD_DIM + EPS), lax.rsqrt(ss_hi / HEAD_DIM + EPS))
    return x * inv * g2


def _gain2(g):
    return jnp.concatenate([g, g]).reshape(1, LANES).astype(F32)


def _params(*sem):
    return pltpu.CompilerParams(dimension_semantics=sem, vmem_limit_bytes=VMEM_LIMIT)


def _norm_matmul_kernel(x_ref, g_ref, w_ref, o_ref, xn_ref):
    @pl.when(pl.program_id(1) == 0)
    def _():
        xn_ref[...] = _rms_rows(x_ref[...], g_ref[...]).astype(BF16)

    o_ref[...] = jnp.dot(xn_ref[...], w_ref[...], preferred_element_type=F32)


def norm_matmul(x, g, w):
    m, k = x.shape
    n = w.shape[1]
    tm = _pick_tile(m, (512, 256, 128))
    tn = _pick_tile(n, (1152, 1024, 896, 768, 640, 512, 384, 256, 128))
    return pl.pallas_call(
        _norm_matmul_kernel,
        grid=(m // tm, n // tn),
        in_specs=[pl.BlockSpec((tm, k), lambda i, j: (i, 0)),
                  pl.BlockSpec((1, k), lambda i, j: (0, 0)),
                  pl.BlockSpec((k, tn), lambda i, j: (0, j))],
        out_specs=pl.BlockSpec((tm, tn), lambda i, j: (i, j)),
        out_shape=jax.ShapeDtypeStruct((m, n), F32),
        scratch_shapes=[pltpu.VMEM((tm, k), BF16)],
        compiler_params=_params("parallel", "arbitrary"),
        name="norm_matmul",
    )(x, g.reshape(1, k), w)


def _mem_kv_kernel(x_ref, w_ref, g_ref, o_ref):
    tm = x_ref.shape[0]
    kv = jnp.dot(x_ref[...].astype(BF16), w_ref[...], preferred_element_type=F32)
    lo = lax.broadcasted_iota(jnp.int32, (tm, LANES), 1) < HEAD_DIM
    for k in range(MEM_HEADS // 2):
        o_ref[:, k * LANES:(k + 1) * LANES] = _half_rms(kv[:, k * LANES:(k + 1) * LANES], g_ref[...], lo)
    o_ref[:, MEM_W:] = kv[:, MEM_W:]


def mem_kv_rows(mem, w, g_k):
    m, d = mem.shape
    depth, _, n = w.shape
    tm = _pick_tile(m, (512, 256, 128))
    g2 = jnp.concatenate([g_k, g_k], axis=1).reshape(depth, 1, LANES).astype(F32)
    return pl.pallas_call(
        _mem_kv_kernel,
        grid=(depth, m // tm),
        in_specs=[pl.BlockSpec((tm, d), lambda l, i: (i, 0)),
                  pl.BlockSpec((None, d, n), lambda l, i: (l, 0, 0)),
                  pl.BlockSpec((None, 1, LANES), lambda l, i: (l, 0, 0))],
        out_specs=pl.BlockSpec((None, tm, n), lambda l, i: (l, i, 0)),
        out_shape=jax.ShapeDtypeStruct((depth, m, n), F32),
        compiler_params=_params("parallel", "parallel"),
        name="mem_kv_rows",
    )(mem, w, g2)


def _out_proj_kernel(a_ref, wa_ref, b_ref, wb_ref, r_ref, o_ref):
    o_ref[...] = (r_ref[...] + jnp.dot(a_ref[...].astype(BF16), wa_ref[...], preferred_element_type=F32)
                  + jnp.dot(b_ref[...].astype(BF16), wb_ref[...], preferred_element_type=F32))


def out_proj(a, wa, b, wb, res):
    m, n = res.shape
    tm = _pick_tile(m, (512, 256, 128))
    ka, kb = a.shape[1], b.shape[1]
    return pl.pallas_call(
        _out_proj_kernel,
        grid=(m // tm,),
        in_specs=[pl.BlockSpec((tm, ka), lambda i: (i, 0)),
                  pl.BlockSpec((ka, n), lambda i: (0, 0)),
                  pl.BlockSpec((tm, kb), lambda i: (i, 0)),
                  pl.BlockSpec((kb, n), lambda i: (0, 0)),
                  pl.BlockSpec((tm, n), lambda i: (i, 0))],
        out_specs=pl.BlockSpec((tm, n), lambda i: (i, 0)),
        out_shape=jax.ShapeDtypeStruct((m, n), F32),
        compiler_params=_params("parallel"),
        name="out_proj",
    )(a, wa, b, wb, res)


def _ffn_kernel(x_ref, g_ref, wg_ref, wu_ref, wd_ref, o_ref, xn_ref, acc_ref):
    f = pl.program_id(1)

    @pl.when(f == 0)
    def _():
        xn_ref[...] = _rms_rows(x_ref[...], g_ref[...]).astype(BF16)
        acc_ref[...] = jnp.zeros_like(acc_ref)

    xn = xn_ref[...]
    gate = jnp.dot(xn, wg_ref[...], preferred_element_type=F32)
    up = jnp.dot(xn, wu_ref[...], preferred_element_type=F32)
    act = (gate * jax.nn.sigmoid(gate) * up).astype(BF16)
    acc_ref[...] += jnp.dot(act, wd_ref[...], preferred_element_type=F32)

    @pl.when(f == pl.num_programs(1) - 1)
    def _():
        o_ref[...] = x_ref[...] + acc_ref[...]


def ffn(x, g, w_gu, w_down):
    m, d = x.shape
    tm = _pick_tile(m, (512, 256, 128))
    tf = 1408
    nf = D_FF // tf
    return pl.pallas_call(
        _ffn_kernel,
        grid=(m // tm, nf),
        in_specs=[pl.BlockSpec((tm, d), lambda i, f: (i, 0)),
                  pl.BlockSpec((1, d), lambda i, f: (0, 0)),
                  pl.BlockSpec((d, tf), lambda i, f: (0, f)),
                  pl.BlockSpec((d, tf), lambda i, f: (0, f + nf)),
                  pl.BlockSpec((tf, d), lambda i, f: (f, 0))],
        out_specs=pl.BlockSpec((tm, d), lambda i, f: (i, 0)),
        out_shape=jax.ShapeDtypeStruct((m, d), F32),
        scratch_shapes=[pltpu.VMEM((tm, d), BF16), pltpu.VMEM((tm, d), F32)],
        compiler_params=_params("parallel", "arbitrary"),
        name="ffn",
    )(x, g.reshape(1, d), w_gu, w_gu, w_down)


MLSTM_L = 128
MLSTM_PAD = 256
N_COL = MLSTM_DH
MLSTM_HW = MLSTM_HEADS * MLSTM_PAD
STEP_TB = 8


def _exact_tri_cumsum(tri, x):
    hi = x.astype(BF16)
    r1 = x - hi.astype(F32)
    mid = r1.astype(BF16)
    lo = (r1 - mid.astype(F32)).astype(BF16)
    return (jnp.dot(tri, hi, preferred_element_type=F32) + jnp.dot(tri, mid, preferred_element_type=F32)
            + jnp.dot(tri, lo, preferred_element_type=F32))


def _mlstm_kernel(q_ref, k_ref, v_ref, og_ref, gate_ref, bif_ref, gh_ref, c0_ref, m0_ref,
                  mix_ref, cout_ref, mout_ref, c_scr, m_scr):
    L = MLSTM_L
    P = MLSTM_PAD
    ci = pl.program_id(1)

    @pl.when(ci == 0)
    def _():
        c_scr[...] = c0_ref[...]
        m_scr[...] = m0_ref[...]

    row = lax.broadcasted_iota(jnp.int32, (L, L), 0)
    col = lax.broadcasted_iota(jnp.int32, (L, L), 1)
    causal = col <= row
    tri = jnp.where(causal, 1.0, 0.0).astype(BF16)
    lane_p = lax.broadcasted_iota(jnp.int32, (L, P), 1)

    gl = gate_ref[...] + bif_ref[...]
    logf = jax.nn.log_sigmoid(gl)
    bc = _exact_tri_cumsum(tri, logf)
    bt = bc.T
    gt = gl.T
    for h in range(MLSTM_HEADS):
        fh = MLSTM_HEADS + h
        b_col = bc[:, fh:fh + 1]
        b_row = bt[fh:fh + 1, :]
        i_row = gt[h:h + 1, :]
        i_col = gl[:, h:h + 1]
        m_prev = m_scr[0:1, h:h + 1]
        log_d = jnp.where(causal, b_col - b_row + i_row, -jnp.inf)
        log_inter = b_col + m_prev
        m_t = jnp.maximum(jnp.max(log_d, axis=1, keepdims=True), log_inter)
        w_intra = jnp.exp(log_d - m_t)
        w_inter = jnp.exp(log_inter - m_t)
        sl = slice(h * P, (h + 1) * P)
        q = q_ref[:, sl].astype(BF16)
        kf = k_ref[:, sl] * MLSTM_DH ** -0.5
        v_aug = jnp.where(lane_p == N_COL, 1.0, v_ref[:, sl])
        sc = _dot_nt(q, kf.astype(BF16)) * w_intra
        c = c_scr[h]
        nd = (jnp.dot(sc.astype(BF16), v_aug.astype(BF16), preferred_element_type=F32)
              + w_inter * jnp.dot(q, c.astype(BF16), preferred_element_type=F32))
        den = nd[:, N_COL:N_COL + 1]
        hh = jnp.where(lane_p < N_COL, nd / jnp.maximum(jnp.abs(den), jnp.exp(-m_t)), 0.0)
        y = hh * lax.rsqrt(jnp.sum(hh * hh, axis=1, keepdims=True) / MLSTM_DH + EPS) * gh_ref[...]
        mix_ref[:, sl] = jax.nn.sigmoid(og_ref[:, sl]) * y
        g_last = bc[L - 1:L, fh:fh + 1]
        m_new = m_t[L - 1:L, :]
        w_s = jnp.exp(g_last - b_col + i_col - m_new)
        decay = jnp.exp(g_last + m_prev - m_new)
        upd = jnp.dot(kf.T.astype(BF16), (w_s * v_aug).astype(BF16), preferred_element_type=F32)
        c_scr[h] = decay * c + upd
        m_scr[0:1, h:h + 1] = m_new

    @pl.when(ci == pl.num_programs(1) - 1)
    def _():
        cout_ref[...] = c_scr[...]
        mout_ref[...] = m_scr[...]


def mlstm_prompt(p, b, s, b_if, g_hnorm, c0, m0):
    L = MLSTM_L
    P = MLSTM_PAD
    hw = MLSTM_HW
    assert s % L == 0
    nc = s // L
    gcol = (4 * hw + MEM_W) // LANES
    bif = _pad_cols(b_if.astype(F32).reshape(1, -1), LANES)
    gh = _pad_cols(g_hnorm.astype(F32).reshape(1, -1), P)
    blk = lambda j: pl.BlockSpec((L, hw), lambda bi, ci: (bi * nc + ci, j))
    return pl.pallas_call(
        _mlstm_kernel,
        grid=(b, nc),
        in_specs=[blk(0), blk(1), blk(2), blk(3),
                  pl.BlockSpec((L, LANES), lambda bi, ci: (bi * nc + ci, gcol)),
                  pl.BlockSpec((1, LANES), lambda bi, ci: (0, 0)),
                  pl.BlockSpec((1, P), lambda bi, ci: (0, 0)),
                  pl.BlockSpec((None, MLSTM_HEADS, P, P), lambda bi, ci: (bi, 0, 0, 0)),
                  pl.BlockSpec((None, 1, LANES), lambda bi, ci: (bi, 0, 0))],
        out_specs=[pl.BlockSpec((L, hw), lambda bi, ci: (bi * nc + ci, 0)),
                   pl.BlockSpec((None, MLSTM_HEADS, P, P), lambda bi, ci: (bi, 0, 0, 0)),
                   pl.BlockSpec((None, 1, LANES), lambda bi, ci: (bi, 0, 0))],
        out_shape=[jax.ShapeDtypeStruct((b * s, hw), F32),
                   jax.ShapeDtypeStruct((b, MLSTM_HEADS, P, P), F32),
                   jax.ShapeDtypeStruct((b, 1, LANES), F32)],
        scratch_shapes=[pltpu.VMEM((MLSTM_HEADS, P, P), F32), pltpu.VMEM((1, LANES), F32)],
        compiler_params=_params("parallel", "arbitrary"),
        name="mlstm_prompt",
    )(p, p, p, p, p, bif, gh, c0, m0)


def pack_mlstm_state(c, n, m):
    pad = MLSTM_PAD - MLSTM_DH
    cn = jnp.concatenate([c, n[..., None]], axis=-1)
    cn = jnp.pad(cn, ((0, 0), (0, 0), (0, pad), (0, pad - 1)))
    return cn, _pad_cols(m, LANES)[:, None, :]


def unpack_mlstm_state(cn, m):
    return cn[:, :, :MLSTM_DH, :MLSTM_DH], cn[:, :, :MLSTM_DH, MLSTM_DH], m[:, 0, :MLSTM_HEADS]


def _mlstm_step_kernel(q_ref, v_ref, og_ref, gate_ref, qkt_ref, bif_ref, gh_ref, c_ref, n_ref, m_ref,
                       mix_ref, cout_ref, nout_ref, mout_ref):
    tb = q_ref.shape[0]
    dh = MLSTM_DH
    gl = gate_ref[...] + bif_ref[...]
    logf = jax.nn.log_sigmoid(gl)
    i4 = gl[:, :MLSTM_HEADS]
    f4 = logf[:, MLSTM_HEADS:2 * MLSTM_HEADS]
    m_prev = m_ref[...]
    m_t = jnp.maximum(i4, f4 + m_prev)
    w_in = jnp.exp(i4 - m_t)
    w_dec = jnp.exp(f4 + m_prev - m_t)
    floor = jnp.exp(-m_t)
    mout_ref[...] = m_t
    mix_ref[...] = jnp.zeros(mix_ref.shape, F32)
    for j in range(tb):
        for h in range(MLSTM_HEADS):
            sl = slice(h * MLSTM_PAD, h * MLSTM_PAD + dh)
            wi = w_in[j:j + 1, h:h + 1]
            wd = w_dec[j:j + 1, h:h + 1]
            q_row = q_ref[j:j + 1, sl]
            v_row = v_ref[j:j + 1, sl]
            q_col = qkt_ref[0, h, :, j:j + 1]
            k_col = qkt_ref[1, h, :, j:j + 1]
            k_row = qkt_ref[2, h, j:j + 1, :]
            c = c_ref[j, h]
            n_row = n_ref[j, h:h + 1, :]
            qk = jnp.sum(q_row * k_row, axis=1, keepdims=True) * wi
            qc = jnp.sum(q_col * c, axis=0, keepdims=True)
            qn = jnp.sum(q_row * n_row, axis=1, keepdims=True)
            num = qk * v_row + wd * qc
            den = qk + wd * qn
            hh = num / jnp.maximum(jnp.abs(den), floor[j:j + 1, h:h + 1])
            y = hh * lax.rsqrt(jnp.sum(hh * hh, axis=1, keepdims=True) / dh + EPS) * gh_ref[...]
            mix_ref[j:j + 1, sl] = jax.nn.sigmoid(og_ref[j:j + 1, sl]) * y
            cout_ref[j, h] = wd * c + k_col * (wi * v_row)
            nout_ref[j, h:h + 1, :] = wd * n_row + wi * k_row


def mlstm_step(p, b_if, g_hnorm, c, n, m):
    b = p.shape[0]
    tb = STEP_TB
    hw = MLSTM_HW
    dh = MLSTM_DH
    nb = b // tb
    gcol = (4 * hw + MEM_W) // LANES
    q = p[:, :hw].reshape(b, MLSTM_HEADS, MLSTM_PAD)[:, :, :dh]
    k = p[:, hw:2 * hw].reshape(b, MLSTM_HEADS, MLSTM_PAD)[:, :, :dh] * dh ** -0.5
    cols = lambda a: a.reshape(nb, tb, MLSTM_HEADS, dh).transpose(0, 2, 3, 1)
    pad_c = jnp.pad(jnp.stack([cols(q), cols(k)], axis=1), ((0, 0), (0, 0), (0, 0), (0, 0), (0, dh - tb)))
    k_rows = k.reshape(nb, tb, MLSTM_HEADS, dh).transpose(0, 2, 1, 3)
    pad_r = jnp.pad(k_rows, ((0, 0), (0, 0), (0, dh - tb), (0, 0)))
    qkt = jnp.concatenate([pad_c, pad_r[:, None]], axis=1)
    bif = _pad_cols(b_if.astype(F32).reshape(1, -1), LANES)
    gh = g_hnorm.astype(F32).reshape(1, dh)
    blk = lambda j: pl.BlockSpec((tb, hw), lambda i: (i, j))
    st4 = pl.BlockSpec((tb, MLSTM_HEADS, dh, dh), lambda i: (i, 0, 0, 0))
    st3 = pl.BlockSpec((tb, MLSTM_HEADS, dh), lambda i: (i, 0, 0))
    st2 = pl.BlockSpec((tb, MLSTM_HEADS), lambda i: (i, 0))
    return pl.pallas_call(
        _mlstm_step_kernel,
        grid=(nb,),
        in_specs=[blk(0), blk(2), blk(3),
                  pl.BlockSpec((tb, LANES), lambda i: (i, gcol)),
                  pl.BlockSpec((None, 3, MLSTM_HEADS, dh, dh), lambda i: (i, 0, 0, 0, 0)),
                  pl.BlockSpec((1, LANES), lambda i: (0, 0)),
                  pl.BlockSpec((1, dh), lambda i: (0, 0)),
                  st4, st3, st2],
        out_specs=[pl.BlockSpec((tb, hw), lambda i: (i, 0)), st4, st3, st2],
        out_shape=[jax.ShapeDtypeStruct((b, hw), F32), jax.ShapeDtypeStruct(c.shape, F32),
                   jax.ShapeDtypeStruct(n.shape, F32), jax.ShapeDtypeStruct(m.shape, F32)],
        compiler_params=_params("parallel"),
        name="mlstm_step",
    )(p, p, p, p, qkt, bif, gh, c, n, m)


def _mem_attn_kernel(q_ref, gq_ref, kv_ref, o_ref):
    tq = q_ref.shape[0]
    lane = lax.broadcasted_iota(jnp.int32, (tq, LANES), 1)
    lo = lane < HEAD_DIM
    for k in range(MEM_HEADS // 2):
        y = _half_rms(q_ref[:, k * LANES:(k + 1) * LANES], gq_ref[...], lo) * HEAD_DIM ** -0.5
        kp = kv_ref[:, k * LANES:(k + 1) * LANES].astype(BF16)
        vp = kv_ref[:, MEM_W + k * LANES:MEM_W + (k + 1) * LANES].astype(BF16)
        outs = []
        for half in range(2):
            qh = jnp.where(lo if half == 0 else ~lo, y, 0.0).astype(BF16)
            s = _dot_nt(qh, kp)
            p = jnp.exp(s - jnp.max(s, axis=1, keepdims=True))
            p = p / jnp.sum(p, axis=1, keepdims=True)
            outs.append(jnp.dot(p.astype(BF16), vp, preferred_element_type=F32))
        o_ref[:, k * LANES:(k + 1) * LANES] = jnp.where(lo, outs[0], outs[1])


def mem_attn_prompt(p, qcol, b, s, g_q, mem_kv):
    tq = _pick_tile(s, (512, 256, 128))
    nq = s // tq
    m_tok = mem_kv.shape[1]
    return pl.pallas_call(
        _mem_attn_kernel,
        grid=(b, nq),
        in_specs=[pl.BlockSpec((tq, MEM_W), lambda bi, qi: (bi * nq + qi, qcol)),
                  pl.BlockSpec((1, LANES), lambda bi, qi: (0, 0)),
                  pl.BlockSpec((None, m_tok, 2 * MEM_W), lambda bi, qi: (bi, 0, 0))],
        out_specs=pl.BlockSpec((tq, MEM_W), lambda bi, qi: (bi * nq + qi, 0)),
        out_shape=jax.ShapeDtypeStruct((b * s, MEM_W), F32),
        compiler_params=_params("parallel", "parallel"),
        name="mem_attn",
    )(p, _gain2(g_q), mem_kv)


def _mem_decode_kernel(q_ref, gq_ref, kv_ref, o_ref):
    tb = q_ref.shape[0]
    lane = lax.broadcasted_iota(jnp.int32, (tb, LANES), 1)
    lo = lane < HEAD_DIM
    y = jnp.concatenate([_half_rms(q_ref[:, k * LANES:(k + 1) * LANES], gq_ref[...], lo)
                         for k in range(MEM_HEADS // 2)], axis=1) * HEAD_DIM ** -0.5
    rowi = lax.broadcasted_iota(jnp.int32, (8, MEM_W), 0)
    own = (lax.broadcasted_iota(jnp.int32, (8, MEM_W), 1) // HEAD_DIM) == rowi
    for j in range(tb):
        qbd = jnp.where(own, y[j:j + 1, :], 0.0).astype(BF16)
        s = _dot_nt(qbd, kv_ref[j, :, :MEM_W].astype(BF16))
        p = jnp.exp(s - jnp.max(s, axis=1, keepdims=True))
        p = p / jnp.sum(p, axis=1, keepdims=True)
        o = jnp.dot(p.astype(BF16), kv_ref[j, :, MEM_W:].astype(BF16), preferred_element_type=F32)
        o_ref[j:j + 1, :] = jnp.sum(jnp.where(own, o, 0.0), axis=0, keepdims=True)


def mem_attn_decode(p, qcol, g_q, mem_kv):
    b = p.shape[0]
    tb = STEP_TB
    m_tok = mem_kv.shape[1]
    return pl.pallas_call(
        _mem_decode_kernel,
        grid=(b // tb,),
        in_specs=[pl.BlockSpec((tb, MEM_W), lambda i: (i, qcol)),
                  pl.BlockSpec((1, LANES), lambda i: (0, 0)),
                  pl.BlockSpec((tb, m_tok, 2 * MEM_W), lambda i: (i, 0, 0))],
        out_specs=pl.BlockSpec((tb, MEM_W), lambda i: (i, 0)),
        out_shape=jax.ShapeDtypeStruct((b, MEM_W), F32),
        compiler_params=_params("parallel"),
        name="mem_attn_decode",
    )(p, _gain2(g_q), mem_kv)


NSA_TQ = 128
MASK = -(2.0 ** 100)
SEL_LANE = HEAD_DIM
N_SEL_LANES = 32
CONST_LANE = SEL_LANE + N_SEL_LANES
KV_ROW = 2 * NSA_KV_HEADS * HEAD_DIM
SEG_PER_PAGE = PAGE_SIZE // CMP_STRIDE
SEG_ROW = CMP_STRIDE * KV_ROW
CMP_OUT = 2 * NSA_KV_HEADS * CMP_HIDDEN
QROWS = 16


def _rel_bucket_np(dist):
    d = np.maximum(dist, 0)
    ratio = np.maximum(d, REL_MAX_EXACT).astype(np.float64) / REL_MAX_EXACT
    large = REL_MAX_EXACT + (np.log(ratio) / math.log(REL_MAX_DIST / REL_MAX_EXACT)
                             * (REL_BUCKETS - REL_MAX_EXACT)).astype(np.int32)
    return np.where(d < REL_MAX_EXACT, d, np.minimum(large, REL_BUCKETS - 1)).astype(np.int32)


def _overlap_matrix(n_cmp, n_slc):
    c0 = np.arange(n_cmp)[:, None] * CMP_STRIDE
    s0 = np.arange(n_slc)[None, :] * SLC_BLOCK
    return np.clip(np.minimum(c0 + CMP_LEN, s0 + SLC_BLOCK) - np.maximum(c0, s0), 0, None) / CMP_STRIDE


def _kv_rows_kernel(x_ref, g_ref, w_ref, gs_ref, gw_ref, cmp_ref, slc_ref, win_ref, *aug_refs, seq_len):
    tm = x_ref.shape[0]
    xn = _rms_rows(x_ref[...], g_ref[...]).astype(BF16)
    kv = jnp.dot(xn, w_ref[...], preferred_element_type=F32)
    cmp_ref[...] = kv[:, :KV_ROW]
    lane = lax.broadcasted_iota(jnp.int32, (tm, LANES), 1)
    lo = lane < HEAD_DIM
    if aug_refs:
        ka_ref, va_ref = aug_refs
        pos = (pl.program_id(0) * tm + lax.broadcasted_iota(jnp.int32, (tm, LANES), 0)) % seq_len
        ones = (lane == CONST_LANE) | (lane == CONST_LANE + 1)
        onehot = (lane - SEL_LANE) == pos // SLC_BLOCK
    for t, (rows_ref, gk_ref) in enumerate(((slc_ref, gs_ref), (win_ref, gw_ref))):
        base = KV_ROW * (t + 1)
        kn = _half_rms(kv[:, base:base + LANES], gk_ref[...], lo)
        vv = kv[:, base + LANES:base + 2 * LANES]
        rows_ref[:, :LANES] = kn
        rows_ref[:, LANES:] = vv
        if aug_refs:
            extra = jnp.where((ones | onehot) if t == 0 else ones, 1.0, 0.0)
            rowt = lax.broadcasted_iota(jnp.int32, (VT_ROWS, tm), 0)
            for g in range(NSA_KV_HEADS):
                kg = kn if g == 0 else pltpu.roll(kn, HEAD_DIM, 1)
                vg = vv if g == 0 else pltpu.roll(vv, HEAD_DIM, 1)
                ka_ref[NSA_KV_HEADS * t + g] = jnp.where(lo, kg, extra).astype(BF16)
                vt = jnp.where(lo, vg, 0.0).T[:VT_ROWS, :]
                va_ref[NSA_KV_HEADS * t + g] = jnp.where(rowt == HEAD_DIM, 1.0, vt).astype(BF16)


def kv_rows(x, g_kv, w_kv, g_k_slc, g_k_win, seq_len, aug):
    m, d = x.shape
    n = w_kv.shape[1]
    tm = _pick_tile(m, (512, 256, 128))
    out_specs = [pl.BlockSpec((tm, KV_ROW), lambda i: (i, 0))] * 3
    out_shape = [jax.ShapeDtypeStruct((m, KV_ROW), F32)] * 3
    if aug:
        out_specs += [pl.BlockSpec((2 * NSA_KV_HEADS, tm, LANES), lambda i: (0, i, 0)),
                      pl.BlockSpec((2 * NSA_KV_HEADS, VT_ROWS, tm), lambda i: (0, 0, i))]
        out_shape += [jax.ShapeDtypeStruct((2 * NSA_KV_HEADS, m, LANES), BF16),
                      jax.ShapeDtypeStruct((2 * NSA_KV_HEADS, VT_ROWS, m), BF16)]
    return pl.pallas_call(
        partial(_kv_rows_kernel, seq_len=seq_len),
        grid=(m // tm,),
        in_specs=[pl.BlockSpec((tm, d), lambda i: (i, 0)),
                  pl.BlockSpec((1, d), lambda i: (0, 0)),
                  pl.BlockSpec((d, n), lambda i: (0, 0)),
                  pl.BlockSpec((1, LANES), lambda i: (0, 0)),
                  pl.BlockSpec((1, LANES), lambda i: (0, 0))],
        out_specs=out_specs,
        out_shape=out_shape,
        compiler_params=_params("parallel"),
        name="kv_rows",
    )(x, g_kv.reshape(1, d), w_kv, _gain2(g_k_slc), _gain2(g_k_win))


def _compress_kernel(*refs, n_x, paged):
    if paged:
        refs = refs[1:]
    x_refs = refs[:n_x]
    w1_ref, b1_ref, w2_ref, b2_ref, gk_ref, o_ref = refs[n_x:]
    n_seg = o_ref.shape[0]
    acc = [jnp.zeros((n_seg, CMP_OUT), F32) for _ in range(CMP_LEN // CMP_STRIDE)]
    for u in range(CMP_STRIDE):
        pieces = [r[:, u * KV_ROW:(u + 1) * KV_ROW] for r in x_refs]
        xu = (pieces[0] if n_x == 1 else jnp.concatenate(pieces, axis=0)).astype(BF16)
        for r in range(len(acc)):
            acc[r] = acc[r] + jnp.dot(xu, w1_ref[r, u], preferred_element_type=F32)
    pre = acc[0] + pltpu.roll(acc[1], n_seg - 1, 0) + b1_ref[...]
    hid = jax.nn.gelu(pre).astype(BF16)
    out = jnp.dot(hid, w2_ref[...], preferred_element_type=F32) + b2_ref[...]
    lane = lax.broadcasted_iota(jnp.int32, (n_seg, LANES), 1)
    o_ref[:, :LANES] = _half_rms(out[:, :LANES], gk_ref[...], lane < HEAD_DIM).astype(BF16)
    o_ref[:, LANES:] = out[:, LANES:].astype(BF16)


def compress_weights(w1, b1, w2, b2, pos_enc, g_k_cmp):
    r_n = CMP_LEN // CMP_STRIDE
    eye = jnp.eye(NSA_KV_HEADS, dtype=F32)
    w1r = w1.reshape(2, r_n, CMP_STRIDE, HEAD_DIM, CMP_HIDDEN)
    w1b = jnp.einsum('crudh,cx,gy->rucgdxyh', w1r, eye, eye).reshape(r_n, CMP_STRIDE, KV_ROW, CMP_OUT).astype(BF16)
    pe = jnp.einsum('pcd,cpdh->ch', pos_enc, w1)
    b1b = jnp.broadcast_to((b1 + pe)[:, None, :], (2, NSA_KV_HEADS, CMP_HIDDEN)).reshape(1, CMP_OUT)
    w2b = jnp.einsum('chd,cx,gy->cghxyd', w2, eye, eye).reshape(CMP_OUT, KV_ROW).astype(BF16)
    b2b = jnp.broadcast_to(b2[:, None, :], (2, NSA_KV_HEADS, HEAD_DIM)).reshape(1, KV_ROW)
    return w1b, b1b, w2b, b2b, _gain2(g_k_cmp)


def compress_rows(x, cw, page_table=None):
    w1b, b1b, w2b, b2b, gk2 = cw
    paged = page_table is not None
    if paged:
        n_seq, n_x = page_table.shape
        x_specs = [pl.BlockSpec((None, SEG_PER_PAGE, SEG_ROW), partial(lambda k, b, pt: (pt[b, k], 0, 0), k))
                   for k in range(n_x)]
        n_seg = n_x * SEG_PER_PAGE
        cmap = lambda *idx: lambda b, pt: idx
    else:
        n_seq, n_seg, n_x = x.shape[0], x.shape[1], 1
        x_specs = [pl.BlockSpec((None, n_seg, SEG_ROW), lambda b: (b, 0, 0))]
        cmap = lambda *idx: lambda b: idx
    w_specs = [pl.BlockSpec(w1b.shape, cmap(0, 0, 0, 0)), pl.BlockSpec(b1b.shape, cmap(0, 0)),
               pl.BlockSpec(w2b.shape, cmap(0, 0)), pl.BlockSpec(b2b.shape, cmap(0, 0)),
               pl.BlockSpec(gk2.shape, cmap(0, 0))]
    out_spec = pl.BlockSpec((None, n_seg, KV_ROW), (lambda b, pt: (b, 0, 0)) if paged else (lambda b: (b, 0, 0)))
    out_shape = jax.ShapeDtypeStruct((n_seq, n_seg, KV_ROW), BF16)
    kern = partial(_compress_kernel, n_x=n_x, paged=paged)
    if paged:
        gs = pltpu.PrefetchScalarGridSpec(num_scalar_prefetch=1, grid=(n_seq,), in_specs=x_specs + w_specs,
                                          out_specs=out_spec)
        return pl.pallas_call(kern, grid_spec=gs, out_shape=out_shape, compiler_params=_params("parallel"),
                              name="compress_paged")(page_table, *([x] * n_x), w1b, b1b, w2b, b2b, gk2)
    return pl.pallas_call(kern, grid=(n_seq,), in_specs=x_specs + w_specs, out_specs=out_spec, out_shape=out_shape,
                          compiler_params=_params("parallel"), name="compress")(x, w1b, b1b, w2b, b2b, gk2)


def _nsa_prompt_kernel(rel_ref, q_ref, gate_ref, gq_ref, bg_ref, ck_ref, cv_ref, cb_ref, mt_ref,
                       sk_ref, sv_ref, wk_ref, wv_ref, dl_ref, o_ref,
                       qa_ref, m_ref, l_ref, acc_ref, out_ref):
    tq = NSA_TQ
    rows = NSA_GROUP * tq
    g = pl.program_id(1)
    qi = pl.program_id(2)
    lane = lax.broadcasted_iota(jnp.int32, (tq, LANES), 1)
    lo = lane < HEAD_DIM

    qn = []
    for k in range(NSA_GROUP // 2):
        y = _half_rms(q_ref[:, k * LANES:(k + 1) * LANES], gq_ref[...], lo) * HEAD_DIM ** -0.5
        qn.append(jnp.where(lo, y, 0.0))
        qn.append(jnp.where(lo, pltpu.roll(y, HEAD_DIM, 1), 0.0))

    sig = jax.nn.sigmoid(gate_ref[...] + bg_ref[...])

    def gated(br, z, o):
        c = br * NSA_GROUP + z
        return sig[:, c:c + 1] * o

    qc = jnp.concatenate(qn, axis=0).astype(BF16)
    s = _dot_nt(qc, ck_ref[...]) + cb_ref[...].reshape(rows, LANES)
    m = jnp.maximum(jnp.max(s, axis=1, keepdims=True), NEG)
    p = jnp.exp(s - m)
    l = jnp.sum(p, axis=1, keepdims=True)
    pn = p * jnp.where(l > 0.0, 1.0 / l, 0.0)
    o_cmp = jnp.dot(pn.astype(BF16), cv_ref[...], preferred_element_type=F32)
    for z in range(NSA_GROUP):
        out_ref[z * tq:(z + 1) * tq, :] = gated(0, z, o_cmp[z * tq:(z + 1) * tq, :])

    psum = pn[0:tq, :]
    for z in range(1, NSA_GROUP):
        psum = psum + pn[z * tq:(z + 1) * tq, :]
    p_hi = psum.astype(BF16)
    p_lo = (psum - p_hi.astype(F32)).astype(BF16)
    imp = _dot_nt(mt_ref[...], p_hi) + _dot_nt(mt_ref[...], p_lo)
    jidx = lax.broadcasted_iota(jnp.int32, (N_SEL_LANES, tq), 0)
    tpos = qi * tq + lax.broadcasted_iota(jnp.int32, (N_SEL_LANES, tq), 1)
    cur = tpos // SLC_BLOCK
    forced = (jidx == 0) | (jidx == cur) | (jidx == cur - 1)
    score = jnp.where(forced, FORCED, jnp.where(jidx <= cur, imp, -1.0))
    rank = jnp.zeros((N_SEL_LANES, tq), F32)
    for i in range(N_SEL_LANES):
        ri = score[i:i + 1, :]
        beats = (ri > score) | ((ri == score) & (jidx > i))
        rank = rank + jnp.where(beats, 1.0, 0.0)
    selb = jnp.where(rank < N_SELECT, 0.0, MASK)
    selb = jnp.concatenate([selb, jnp.zeros((LANES - N_SEL_LANES, tq), F32)], axis=0)
    sel_cols = pltpu.roll(selb.T, SEL_LANE, 1)

    for z in range(NSA_GROUP):
        c = jnp.full((tq, LANES), rel_ref[REL_BUCKETS - 1, g * NSA_GROUP + z], F32)
        c_hi = c.astype(BF16).astype(F32)
        extra = jnp.where(lane == CONST_LANE, c_hi, jnp.where(lane == CONST_LANE + 1, c - c_hi, sel_cols))
        qa_ref[z * tq:(z + 1) * tq, :] = jnp.where(lo, qn[z], extra).astype(BF16)

    def flash_init():
        m_ref[...] = jnp.full((rows, LANES), NEG, F32)
        l_ref[...] = jnp.zeros((rows, LANES), F32)
        acc_ref[...] = jnp.zeros((rows, LANES), F32)

    def flash_block(k_ref, v_ref, kb, didx):
        start = pl.multiple_of(kb * tq, tq)
        s = _dot_nt(qa_ref[...], k_ref[pl.ds(start, tq), :]) + dl_ref[didx].reshape(rows, LANES)
        m_prev = m_ref[...]
        m_next = jnp.maximum(m_prev, jnp.max(s, axis=1, keepdims=True))
        p = jnp.exp(s - m_next)
        alpha = jnp.exp(m_prev - m_next)
        l_ref[...] = alpha * l_ref[...] + jnp.sum(p, axis=1, keepdims=True)
        acc_ref[...] = alpha * acc_ref[...] + jnp.dot(p.astype(BF16), v_ref[pl.ds(start, tq), :],
                                                      preferred_element_type=F32)
        m_ref[...] = m_next

    def flash_finish(br):
        o = acc_ref[...] / l_ref[...]
        for z in range(NSA_GROUP):
            out_ref[z * tq:(z + 1) * tq, :] += gated(br, z, o[z * tq:(z + 1) * tq, :])

    flash_init()

    def slc_body(kb, carry):
        didx = jnp.where(kb == qi, 2, jnp.where(kb == qi - 1, 1, 0))
        flash_block(sk_ref, sv_ref, kb, didx)
        return carry

    lax.fori_loop(0, qi + 1, slc_body, 0)
    flash_finish(1)

    flash_init()
    n_win = WINDOW // tq

    def win_body(kb, carry):
        d = qi - kb
        didx = jnp.where(d == 0, 2, jnp.where(d == 1, 1, jnp.where(d == n_win, 3, 0)))
        flash_block(wk_ref, wv_ref, kb, didx)
        return carry

    lax.fori_loop(jnp.maximum(qi - n_win, 0), qi + 1, win_body, 0)
    flash_finish(2)

    for k in range(NSA_GROUP // 2):
        even = out_ref[(2 * k) * tq:(2 * k + 1) * tq, :]
        odd = out_ref[(2 * k + 1) * tq:(2 * k + 2) * tq, :]
        o_ref[:, k * LANES:(k + 1) * LANES] = even + pltpu.roll(odd, HEAD_DIM, 1)


VT_ROWS = 80
CMP_BAND = 16


def _nsa_prompt_t_kernel(rel_ref, q_ref, gate_ref, gq_ref, bg_ref, ck_ref, cvt_ref, band_ref, far_ref, mt_ref,
                         sk_ref, svt_ref, wk_ref, wvt_ref, dl_ref, o_ref,
                         qa_ref, m_ref, acc_ref, out_ref, cb_ref, *, n_cmp):
    tq = NSA_TQ
    cols = NSA_GROUP * tq
    g = pl.program_id(1)
    qi = pl.program_id(2)
    lane = lax.broadcasted_iota(jnp.int32, (tq, LANES), 1)
    lo = lane < HEAD_DIM

    qn = []
    for k in range(NSA_GROUP // 2):
        y = _half_rms(q_ref[:, k * LANES:(k + 1) * LANES], gq_ref[...], lo) * HEAD_DIM ** -0.5
        qn.append(jnp.where(lo, y, 0.0))
        qn.append(jnp.where(lo, pltpu.roll(y, HEAD_DIM, 1), 0.0))
    for z in range(NSA_GROUP):
        qa_ref[z * tq:(z + 1) * tq, :] = qn[z].astype(BF16)

    sig_t = jax.nn.sigmoid(gate_ref[...] + bg_ref[...]).T

    def gate_row(br):
        return jnp.concatenate([sig_t[br * NSA_GROUP + z:br * NSA_GROUP + z + 1, :] for z in range(NSA_GROUP)], axis=1)

    r_cb = lax.broadcasted_iota(jnp.int32, (LANES + CMP_BAND, cols), 0)
    cb_ref[...] = jnp.where(r_cb < (CMP_BAND // 2) * qi, far_ref[...], MASK)
    cb_ref[pl.ds(pl.multiple_of((CMP_BAND // 2) * qi, CMP_BAND // 2), CMP_BAND), :] = band_ref[...]
    nidx = lax.broadcasted_iota(jnp.int32, (LANES, cols), 0)
    cbias = jnp.where(nidx >= n_cmp, MASK, cb_ref[CMP_BAND // 2:CMP_BAND // 2 + LANES, :])
    s = _dot_nt(ck_ref[...], qa_ref[...]) + cbias
    m = jnp.maximum(jnp.max(s, axis=0, keepdims=True), NEG)
    p = jnp.exp(s - m)
    l = jnp.sum(p, axis=0, keepdims=True)
    pn = p * jnp.where(l > 0.0, 1.0 / l, 0.0)
    o_cmp = jnp.dot(cvt_ref[...], pn.astype(BF16), preferred_element_type=F32)
    out_ref[...] = gate_row(0) * o_cmp

    psum = pn[:, 0:tq]
    for z in range(1, NSA_GROUP):
        psum = psum + pn[:, z * tq:(z + 1) * tq]
    p_hi = psum.astype(BF16)
    p_lo = (psum - p_hi.astype(F32)).astype(BF16)
    imp = (jnp.dot(mt_ref[...], p_hi, preferred_element_type=F32)
           + jnp.dot(mt_ref[...], p_lo, preferred_element_type=F32))
    jidx = lax.broadcasted_iota(jnp.int32, (N_SEL_LANES, tq), 0)
    tpos = qi * tq + lax.broadcasted_iota(jnp.int32, (N_SEL_LANES, tq), 1)
    cur = tpos // SLC_BLOCK
    forced = (jidx == 0) | (jidx == cur) | (jidx == cur - 1)
    score = jnp.where(forced, FORCED, jnp.where(jidx <= cur, imp, -1.0))
    rank = jnp.zeros((N_SEL_LANES, tq), F32)
    for i in range(N_SEL_LANES):
        ri = score[i:i + 1, :]
        beats = (ri > score) | ((ri == score) & (jidx > i))
        rank = rank + jnp.where(beats, 1.0, 0.0)
    selb = jnp.where(rank < N_SELECT, 0.0, MASK)
    selb = jnp.concatenate([selb, jnp.zeros((LANES - N_SEL_LANES, tq), F32)], axis=0)
    sel_cols = pltpu.roll(selb.T, SEL_LANE, 1)

    for z in range(NSA_GROUP):
        c = jnp.full((tq, LANES), rel_ref[REL_BUCKETS - 1, g * NSA_GROUP + z], F32)
        c_hi = c.astype(BF16).astype(F32)
        extra = jnp.where(lane == CONST_LANE, c_hi, jnp.where(lane == CONST_LANE + 1, c - c_hi, sel_cols))
        qa_ref[z * tq:(z + 1) * tq, :] = jnp.where(lo, qn[z], extra).astype(BF16)

    def flash_init():
        m_ref[...] = jnp.full((1, cols), NEG, F32)
        acc_ref[...] = jnp.zeros((VT_ROWS, cols), F32)

    def flash_block(k_ref, vt_ref, kb, didx):
        start = pl.multiple_of(kb * tq, tq)
        s = _dot_nt(k_ref[pl.ds(start, tq), :], qa_ref[...]) + dl_ref[didx]
        m_prev = m_ref[...]
        m_next = jnp.maximum(m_prev, jnp.max(s, axis=0, keepdims=True))
        p = jnp.exp(s - m_next).astype(BF16)
        acc_ref[...] = (jnp.exp(m_prev - m_next) * acc_ref[...]
                        + jnp.dot(vt_ref[:, pl.ds(start, tq)], p, preferred_element_type=F32))
        m_ref[...] = m_next

    def flash_finish(br):
        out_ref[...] += gate_row(br) * (acc_ref[:HEAD_DIM, :] / acc_ref[HEAD_DIM:HEAD_DIM + 1, :])

    flash_init()

    def slc_body(kb, carry):
        didx = jnp.where(kb == qi, 2, jnp.where(kb == qi - 1, 1, 0))
        flash_block(sk_ref, svt_ref, kb, didx)
        return carry

    lax.fori_loop(0, qi + 1, slc_body, 0)
    flash_finish(1)

    flash_init()
    n_win = WINDOW // tq

    def win_body(kb, carry):
        d = qi - kb
        didx = jnp.where(d == 0, 2, jnp.where(d == 1, 1, jnp.where(d == n_win, 3, 0)))
        flash_block(wk_ref, wvt_ref, kb, didx)
        return carry

    lax.fori_loop(jnp.maximum(qi - n_win, 0), qi + 1, win_body, 0)
    flash_finish(2)

    for k in range(NSA_GROUP // 2):
        o_ref[:, k * LANES:(k + 1) * LANES] = jnp.concatenate(
            [out_ref[:, (2 * k) * tq:(2 * k + 1) * tq], out_ref[:, (2 * k + 1) * tq:(2 * k + 2) * tq]], axis=0).T


FAR_BLOCKS = 4
NEAR_BLOCKS = 5
FULL_MASK_TILE = 4


def _nsa_prompt_kernel2(rel_ref, q_ref, gate_ref, gq_ref, bg_ref, ck_ref, cvt_ref, band_ref, far_ref, mt_ref,
                        ka_ref, vat_ref, dl_ref, o_ref, qa_ref, m_ref, acc_ref, cb_ref, *, n_cmp, n_tiles):
    tq = NSA_TQ
    cols = NSA_GROUP * tq
    qi = pl.program_id(1)
    lane = lax.broadcasted_iota(jnp.int32, (tq, LANES), 1)
    lo = lane < HEAD_DIM
    nidx = lax.broadcasted_iota(jnp.int32, (LANES, cols), 0)
    r_cb = lax.broadcasted_iota(jnp.int32, (LANES + CMP_BAND, cols), 0)
    jidx = lax.broadcasted_iota(jnp.int32, (N_SEL_LANES, tq), 0)
    tpos = qi * tq + lax.broadcasted_iota(jnp.int32, (N_SEL_LANES, tq), 1)
    cur = tpos // SLC_BLOCK
    forced = (jidx == 0) | (jidx == cur) | (jidx == cur - 1)
    half_band = CMP_BAND // 2

    def gate_rows(g):
        sig_t = jax.nn.sigmoid(gate_ref[:, g * LANES:(g + 1) * LANES] + bg_ref[:, g * LANES:(g + 1) * LANES]).T
        return [jnp.concatenate([sig_t[br * NSA_GROUP + z:br * NSA_GROUP + z + 1, :] for z in range(NSA_GROUP)], axis=1)
                for br in range(3)]

    def prologue(g):
        qn = []
        for k in range(NSA_GROUP // 2):
            kk = g * (NSA_GROUP // 2) + k
            y = _half_rms(q_ref[:, kk * LANES:(kk + 1) * LANES], gq_ref[...], lo) * HEAD_DIM ** -0.5
            qn.append(jnp.where(lo, y, 0.0))
            qn.append(jnp.where(lo, pltpu.roll(y, HEAD_DIM, 1), 0.0))
        qc = jnp.concatenate(qn, axis=0).astype(BF16)

        cb_ref[g] = jnp.where(r_cb < half_band * qi, far_ref[g], MASK)
        cb_ref[g, pl.ds(pl.multiple_of(half_band * qi, half_band), CMP_BAND), :] = band_ref[g]
        cbias = jnp.where(nidx >= n_cmp, MASK, cb_ref[g, half_band:half_band + LANES, :])
        s = _dot_nt(ck_ref[g], qc) + cbias
        m = jnp.maximum(jnp.max(s, axis=0, keepdims=True), NEG)
        p = jnp.exp(s - m)
        l = jnp.sum(p, axis=0, keepdims=True)
        pn = p * jnp.where(l > 0.0, 1.0 / l, 0.0)
        o_cmp = jnp.dot(cvt_ref[g], pn.astype(BF16), preferred_element_type=F32)

        psum = pn[:, 0:tq]
        for z in range(1, NSA_GROUP):
            psum = psum + pn[:, z * tq:(z + 1) * tq]
        p_hi = psum.astype(BF16)
        p_lo = (psum - p_hi.astype(F32)).astype(BF16)
        imp = (jnp.dot(mt_ref[...], p_hi, preferred_element_type=F32)
               + jnp.dot(mt_ref[...], p_lo, preferred_element_type=F32))
        score = jnp.where(forced, FORCED, jnp.where(jidx <= cur, imp, -1.0))
        rank = jnp.zeros((N_SEL_LANES, tq), F32)
        for i in range(N_SEL_LANES):
            ri = score[i:i + 1, :]
            beats = (ri > score) | ((ri == score) & (jidx > i))
            rank = rank + jnp.where(beats, 1.0, 0.0)
        selb = jnp.where((rank < N_SELECT) & (jidx <= cur), 0.0, MASK)
        selb = jnp.concatenate([selb, jnp.zeros((LANES - N_SEL_LANES, tq), F32)], axis=0)
        sel_cols = pltpu.roll(selb.T, SEL_LANE, 1)

        for z in range(NSA_GROUP):
            c = jnp.full((tq, LANES), rel_ref[REL_BUCKETS - 1, g * NSA_GROUP + z], F32)
            c_hi = c.astype(BF16).astype(F32)
            extra = jnp.where(lane == CONST_LANE, c_hi, jnp.where(lane == CONST_LANE + 1, c - c_hi, sel_cols))
            qa_ref[g, z * tq:(z + 1) * tq, :] = jnp.where(lo, qn[z], extra).astype(BF16)
        return o_cmp

    o_cmp = [prologue(g) for g in range(NSA_KV_HEADS)]

    n_far = jnp.maximum(qi - 1, 0) // FAR_BLOCKS
    far_rows = FAR_BLOCKS * tq
    m_ref[...] = jnp.full(m_ref.shape, NEG, F32)
    acc_ref[...] = jnp.zeros(acc_ref.shape, F32)

    def far_body(c, carry):
        start = pl.multiple_of(c * far_rows, far_rows)
        for g in range(NSA_KV_HEADS):
            s = _dot_nt(ka_ref[g, pl.ds(start, far_rows), :], qa_ref[g])
            m_prev = m_ref[g]
            m_next = jnp.maximum(m_prev, jnp.max(s, axis=0, keepdims=True))
            p = jnp.exp(s - m_next).astype(BF16)
            acc_ref[g] = (jnp.exp(m_prev - m_next) * acc_ref[g]
                          + jnp.dot(vat_ref[g, :, pl.ds(start, far_rows)], p, preferred_element_type=F32))
            m_ref[g] = m_next
        return carry

    lax.fori_loop(0, n_far, far_body, 0)

    def near_stage(idx, g, start_blk, tile_of, m_prev, acc_prev):
        ss = []
        for i in range(NEAR_BLOCKS):
            start = pl.multiple_of((start_blk + i) * tq, tq)
            ss.append(_dot_nt(ka_ref[idx, pl.ds(start, tq), :], qa_ref[g]) + dl_ref[tile_of(start_blk + i), g])
        m_cur = ss[0].max(axis=0, keepdims=True)
        for s in ss[1:]:
            m_cur = jnp.maximum(m_cur, s.max(axis=0, keepdims=True))
        m_next = m_cur if m_prev is None else jnp.maximum(m_prev, m_cur)
        acc = None if acc_prev is None else jnp.exp(m_prev - m_next) * acc_prev
        for i, s in enumerate(ss):
            start = pl.multiple_of((start_blk + i) * tq, tq)
            pv = jnp.dot(vat_ref[idx, :, pl.ds(start, tq)], jnp.exp(s - m_next).astype(BF16),
                         preferred_element_type=F32)
            acc = pv if acc is None else acc + pv
        return acc[:HEAD_DIM, :] / acc[HEAD_DIM:HEAD_DIM + 1, :]

    far_cov = FAR_BLOCKS * n_far
    slc_start = jnp.minimum(far_cov, n_tiles - NEAR_BLOCKS)
    win_start = jnp.maximum(qi - WINDOW // tq, 0)

    def slc_tile(j):
        d = qi - j
        return jnp.where((d < 0) | (j < far_cov), FULL_MASK_TILE, jnp.where(d == 0, 2, jnp.where(d == 1, 1, 0)))

    def win_tile(j):
        d = qi - j
        return jnp.where(d < 0, FULL_MASK_TILE,
                         jnp.where(d == 0, 2, jnp.where(d == 1, 1, jnp.where(d == WINDOW // tq, 3, 0))))

    outs = []
    for g in range(NSA_KV_HEADS):
        gr = gate_rows(g)
        o_slc = near_stage(g, g, slc_start, slc_tile, m_ref[g], acc_ref[g])
        o_win = near_stage(NSA_KV_HEADS + g, g, win_start, win_tile, None, None)
        outs.append(gr[0] * o_cmp[g] + gr[1] * o_slc + gr[2] * o_win)

    for g in range(NSA_KV_HEADS):
        for k in range(NSA_GROUP // 2):
            kk = g * (NSA_GROUP // 2) + k
            o_ref[:, kk * LANES:(kk + 1) * LANES] = jnp.concatenate(
                [outs[g][:, (2 * k) * tq:(2 * k + 1) * tq], outs[g][:, (2 * k + 1) * tq:(2 * k + 2) * tq]], axis=0).T


def nsa_prompt2(p, b, s, g_q, b_gate, rel_table, ck, cvt, ka, vat, tables):
    tq = NSA_TQ
    nq = s // tq
    cols = NSA_GROUP * tq
    n_cmp = (s - CMP_LEN) // CMP_STRIDE + 1
    n_slc = -(-s // SLC_BLOCK)
    assert s % tq == 0 and nq >= NEAR_BLOCKS and n_cmp <= LANES and n_slc <= N_SEL_LANES and tq == 8 * CMP_STRIDE
    tiles, band, far_row = tables
    tiles = jnp.concatenate([tiles, jnp.full((1,) + tiles.shape[1:], MASK, F32)], axis=0)
    mt = np.zeros((N_SEL_LANES, LANES), np.float32)
    mt[:n_slc, :n_cmp] = _overlap_matrix(n_cmp, n_slc).T
    bg = _gate_bias_slabs(b_gate).reshape(1, NSA_KV_HEADS * LANES)
    gcol = (MIX_W + MEM_W) // (NSA_KV_HEADS * LANES)
    ka4 = ka.reshape(2 * NSA_KV_HEADS, b, s, LANES)
    c2 = lambda bi, qi: (0, 0)
    c3 = lambda bi, qi: (0, 0, 0)
    return pl.pallas_call(
        partial(_nsa_prompt_kernel2, n_cmp=n_cmp, n_tiles=nq),
        grid=(b, nq),
        in_specs=[pl.BlockSpec(memory_space=pltpu.SMEM),
                  pl.BlockSpec((tq, MIX_W), lambda bi, qi: (bi * nq + qi, 0)),
                  pl.BlockSpec((tq, NSA_KV_HEADS * LANES), lambda bi, qi: (bi * nq + qi, gcol)),
                  pl.BlockSpec((1, LANES), c2),
                  pl.BlockSpec((1, NSA_KV_HEADS * LANES), c2),
                  pl.BlockSpec((None, NSA_KV_HEADS, LANES, LANES), lambda bi, qi: (bi, 0, 0, 0)),
                  pl.BlockSpec((None, NSA_KV_HEADS, HEAD_DIM, LANES), lambda bi, qi: (bi, 0, 0, 0)),
                  pl.BlockSpec((NSA_KV_HEADS, CMP_BAND, cols), c3),
                  pl.BlockSpec((NSA_KV_HEADS, 1, cols), c3),
                  pl.BlockSpec((N_SEL_LANES, LANES), c2),
                  pl.BlockSpec((2 * NSA_KV_HEADS, None, s, LANES), lambda bi, qi: (0, bi, 0, 0)),
                  pl.BlockSpec((2 * NSA_KV_HEADS, VT_ROWS, s), lambda bi, qi: (0, 0, bi)),
                  pl.BlockSpec((FULL_MASK_TILE + 1, NSA_KV_HEADS, tq, cols), lambda bi, qi: (0, 0, 0, 0))],
        out_specs=pl.BlockSpec((tq, MIX_W), lambda bi, qi: (bi * nq + qi, 0)),
        out_shape=jax.ShapeDtypeStruct((b * s, MIX_W), F32),
        scratch_shapes=[pltpu.VMEM((NSA_KV_HEADS, cols, LANES), BF16), pltpu.VMEM((NSA_KV_HEADS, 1, cols), F32),
                        pltpu.VMEM((NSA_KV_HEADS, VT_ROWS, cols), F32),
                        pltpu.VMEM((NSA_KV_HEADS, LANES + CMP_BAND, cols), F32)],
        compiler_params=_params("parallel", "arbitrary"),
        name="nsa_prompt",
    )(rel_table.astype(F32), p, p, _gain2(g_q), bg, ck, cvt, band, far_row, jnp.asarray(mt, BF16), ka4, vat, tiles)


def nsa_bias_tables(rel_table):
    tq = NSA_TQ
    table = rel_table.astype(F32)
    far = table[REL_BUCKETS - 1]
    r = np.arange(tq)[None, :]
    c = np.arange(tq)[:, None]

    def by_group(t):
        k = t.shape[0]
        return jnp.transpose(t.reshape(k, tq, NSA_KV_HEADS, NSA_GROUP), (2, 0, 3, 1)).reshape(NSA_KV_HEADS, k, -1)

    def tile(off, masked):
        dist = off + r - c
        t = table[_rel_bucket_np(dist)] - far
        return by_group(jnp.where(jnp.asarray(masked(dist))[..., None], MASK, t))

    zero = jnp.zeros((NSA_KV_HEADS, tq, NSA_GROUP * tq), F32)
    tiles = jnp.stack([zero, tile(tq, lambda d: d < 0), tile(0, lambda d: d < 0),
                       tile(WINDOW, lambda d: d >= WINDOW)])
    n_rel = np.arange(CMP_BAND)[:, None] - CMP_BAND // 2
    dist = r - CMP_STRIDE * n_rel - (CMP_LEN - 1)
    band = by_group(jnp.where(jnp.asarray(dist < 0)[..., None], MASK, table[_rel_bucket_np(dist)]))
    far_row = by_group(jnp.broadcast_to(far, (1, tq, NSA_HEADS)))
    return tiles, band, far_row


def nsa_prompt_t(p, b, s, g_q, b_gate, rel_table, ck, cvt, ka, vat, tables):
    tq = NSA_TQ
    nq = s // tq
    cols = NSA_GROUP * tq
    n_cmp = (s - CMP_LEN) // CMP_STRIDE + 1
    n_slc = -(-s // SLC_BLOCK)
    assert s % tq == 0 and n_cmp <= LANES and n_slc <= N_SEL_LANES and tq == 8 * CMP_STRIDE
    tiles, band, far_row = tables
    mt = np.zeros((N_SEL_LANES, LANES), np.float32)
    mt[:n_slc, :n_cmp] = _overlap_matrix(n_cmp, n_slc).T
    bg = _gate_bias_slabs(b_gate).reshape(NSA_KV_HEADS, 1, LANES)
    qcol = (MIX_W + MEM_W) // LANES
    ka4 = ka.reshape(2 * NSA_KV_HEADS, b, s, LANES)
    kmap = lambda off: (lambda bi, g, qi: (off + g, bi, 0, 0))
    vmap = lambda off: (lambda bi, g, qi: (off + g, 0, bi))
    gmap = lambda bi, g, qi: (g, 0, 0)
    return pl.pallas_call(
        partial(_nsa_prompt_t_kernel, n_cmp=n_cmp),
        grid=(b, NSA_KV_HEADS, nq),
        in_specs=[pl.BlockSpec(memory_space=pltpu.SMEM),
                  pl.BlockSpec((tq, NSA_GROUP * HEAD_DIM), lambda bi, g, qi: (bi * nq + qi, g)),
                  pl.BlockSpec((tq, LANES), lambda bi, g, qi: (bi * nq + qi, qcol + g)),
                  pl.BlockSpec((1, LANES), lambda bi, g, qi: (0, 0)),
                  pl.BlockSpec((None, 1, LANES), gmap),
                  pl.BlockSpec((None, None, LANES, LANES), lambda bi, g, qi: (bi, g, 0, 0)),
                  pl.BlockSpec((None, None, HEAD_DIM, LANES), lambda bi, g, qi: (bi, g, 0, 0)),
                  pl.BlockSpec((None, CMP_BAND, cols), gmap),
                  pl.BlockSpec((None, 1, cols), gmap),
                  pl.BlockSpec((N_SEL_LANES, LANES), lambda bi, g, qi: (0, 0)),
                  pl.BlockSpec((None, None, s, LANES), kmap(0)),
                  pl.BlockSpec((None, VT_ROWS, s), vmap(0)),
                  pl.BlockSpec((None, None, s, LANES), kmap(NSA_KV_HEADS)),
                  pl.BlockSpec((None, VT_ROWS, s), vmap(NSA_KV_HEADS)),
                  pl.BlockSpec((4, None, tq, cols), lambda bi, g, qi: (0, g, 0, 0))],
        out_specs=pl.BlockSpec((tq, NSA_GROUP * HEAD_DIM), lambda bi, g, qi: (bi * nq + qi, g)),
        out_shape=jax.ShapeDtypeStruct((b * s, MIX_W), F32),
        scratch_shapes=[pltpu.VMEM((cols, LANES), BF16), pltpu.VMEM((1, cols), F32), pltpu.VMEM((VT_ROWS, cols), F32),
                        pltpu.VMEM((HEAD_DIM, cols), F32), pltpu.VMEM((LANES + CMP_BAND, cols), F32)],
        compiler_params=_params("parallel", "parallel", "arbitrary"),
        name="nsa_prompt",
    )(rel_table.astype(F32), p, p, _gain2(g_q), bg, ck, cvt, band, far_row, jnp.asarray(mt, BF16),
      ka4, vat, ka4, vat, tiles)


def nsa_bias_tiles(rel_table, s):
    tq = NSA_TQ
    table = rel_table.astype(F32)
    far = table[REL_BUCKETS - 1]
    r = np.arange(tq)[:, None]
    c = np.arange(tq)[None, :]

    def tile(off, masked):
        dist = off + r - c
        t = jnp.transpose(table[_rel_bucket_np(dist)], (2, 0, 1)) - far[:, None, None]
        return jnp.where(jnp.asarray(masked(dist))[None], MASK, t)

    zero = jnp.zeros((NSA_HEADS, tq, tq), F32)
    d128 = tile(tq, lambda d: d < 0)
    d0 = tile(0, lambda d: d < 0)
    dwin = tile(WINDOW, lambda d: d >= WINDOW)
    tiles = jnp.stack([zero, d128, d0, dwin])
    n_cmp = (s - CMP_LEN) // CMP_STRIDE + 1
    c_end = np.arange(LANES) * CMP_STRIDE + CMP_LEN - 1
    dist = np.arange(s)[:, None] - c_end[None, :]
    cb = jnp.transpose(table[_rel_bucket_np(dist)], (2, 0, 1))
    bad = (dist < 0) | (np.arange(LANES)[None, :] >= n_cmp)
    cb = jnp.where(jnp.asarray(bad)[None], MASK, cb)
    return tiles, cb


def _gate_bias_slabs(b_gate):
    bg = b_gate.astype(F32).reshape(3, NSA_KV_HEADS, NSA_GROUP).transpose(1, 0, 2).reshape(NSA_KV_HEADS, 3 * NSA_GROUP)
    return _pad_cols(bg, LANES)


def nsa_prompt(p, b, s, g_q, b_gate, rel_table, ck, cv, ka, va, tiles, cb):
    tq = NSA_TQ
    nq = s // tq
    rows = NSA_GROUP * tq
    n_cmp = (s - CMP_LEN) // CMP_STRIDE + 1
    n_slc = -(-s // SLC_BLOCK)
    assert s % tq == 0 and n_cmp <= LANES and n_slc <= N_SEL_LANES
    mt = np.zeros((N_SEL_LANES, LANES), np.float32)
    mt[:n_slc, :n_cmp] = _overlap_matrix(n_cmp, n_slc).T
    bg = _gate_bias_slabs(b_gate).reshape(NSA_KV_HEADS, 1, LANES)
    qcol = (MIX_W + MEM_W) // LANES
    kmap = lambda off: (lambda bi, g, qi: (off + g, bi, 0, 0))
    return pl.pallas_call(
        _nsa_prompt_kernel,
        grid=(b, NSA_KV_HEADS, nq),
        in_specs=[pl.BlockSpec(memory_space=pltpu.SMEM),
                  pl.BlockSpec((tq, NSA_GROUP * HEAD_DIM), lambda bi, g, qi: (bi * nq + qi, g)),
                  pl.BlockSpec((tq, LANES), lambda bi, g, qi: (bi * nq + qi, qcol + g)),
                  pl.BlockSpec((1, LANES), lambda bi, g, qi: (0, 0)),
                  pl.BlockSpec((None, 1, LANES), lambda bi, g, qi: (g, 0, 0)),
                  pl.BlockSpec((None, None, LANES, LANES), lambda bi, g, qi: (bi, g, 0, 0)),
                  pl.BlockSpec((None, None, LANES, LANES), lambda bi, g, qi: (bi, g, 0, 0)),
                  pl.BlockSpec((NSA_GROUP, tq, LANES), lambda bi, g, qi: (g, qi, 0)),
                  pl.BlockSpec((N_SEL_LANES, LANES), lambda bi, g, qi: (0, 0)),
                  pl.BlockSpec((None, None, s, LANES), kmap(0)),
                  pl.BlockSpec((None, None, s, LANES), kmap(0)),
                  pl.BlockSpec((None, None, s, LANES), kmap(NSA_KV_HEADS)),
                  pl.BlockSpec((None, None, s, LANES), kmap(NSA_KV_HEADS)),
                  pl.BlockSpec((4, NSA_GROUP, tq, tq), lambda bi, g, qi: (0, g, 0, 0))],
        out_specs=pl.BlockSpec((tq, NSA_GROUP * HEAD_DIM), lambda bi, g, qi: (bi * nq + qi, g)),
        out_shape=jax.ShapeDtypeStruct((b * s, MIX_W), F32),
        scratch_shapes=[pltpu.VMEM((rows, LANES), BF16)] + [pltpu.VMEM((rows, LANES), F32)] * 4,
        compiler_params=_params("parallel", "parallel", "arbitrary"),
        name="nsa_prompt",
    )(rel_table.astype(F32), p, p, _gain2(g_q), bg, ck, cv, cb, jnp.asarray(mt, BF16),
      ka.reshape(4, b, s, LANES), va.reshape(4, b, s, LANES),
      ka.reshape(4, b, s, LANES), va.reshape(4, b, s, LANES), tiles)


def split_cmp_ctx(ctx):
    b, n, _ = ctx.shape
    parts = ctx.reshape(b, n, 2, NSA_KV_HEADS, HEAD_DIM).transpose(2, 0, 3, 1, 4)
    parts = jnp.pad(parts, ((0, 0), (0, 0), (0, 0), (0, LANES - n), (0, 0)))
    ck = jnp.pad(parts[0], ((0, 0), (0, 0), (0, 0), (0, LANES - HEAD_DIM)))
    return ck, jnp.swapaxes(parts[1], -1, -2)


def _softmax_parts(s, s_new):
    m = jnp.maximum(jnp.max(s, axis=1, keepdims=True), s_new)
    p = jnp.exp(s - m)
    p_new = jnp.exp(s_new - m)
    inv = 1.0 / (jnp.sum(p, axis=1, keepdims=True) + p_new)
    return p * inv, p_new * inv


def _nsa_decode_kernel(*refs, n_pages):
    (pt_ref, p_ref, gq_ref, bg_ref, cmp_ref, cb_ref, mt_ref) = refs[:7]
    page_refs = refs[7:7 + n_pages]
    (snew_ref, sb_ref, e_ref, win_ref, wnew_ref, wb_ref, b0_ref, o_ref) = refs[7 + n_pages:]
    del pt_ref
    n_keys = n_pages * PAGE_SIZE
    lo1 = lax.broadcasted_iota(jnp.int32, (1, LANES), 1) < HEAD_DIM
    rowi = lax.broadcasted_iota(jnp.int32, (QROWS, LANES), 0)
    lanei = lax.broadcasted_iota(jnp.int32, (QROWS, LANES), 1)
    row_g = rowi // 8
    own = (lanei // HEAD_DIM) == row_g

    qbd = jnp.zeros((QROWS, LANES), F32)
    for k in range(NSA_HEADS // 2):
        y = _half_rms(p_ref[:, k * LANES:(k + 1) * LANES], gq_ref[...], lo1) * HEAD_DIM ** -0.5
        yr = pltpu.roll(y, HEAD_DIM, 1)
        for half in range(2):
            g, z = divmod(2 * k + half, NSA_GROUP)
            qbd = jnp.where(rowi == 8 * g + z, y if half == g else yr, qbd)
    qbd = jnp.where(own, qbd, 0.0)
    qb = qbd.astype(BF16)

    s = _dot_nt(qb, cmp_ref[:, :LANES]) + cb_ref[...]
    m = jnp.maximum(jnp.max(s, axis=1, keepdims=True), NEG)
    p = jnp.exp(s - m)
    l = jnp.sum(p, axis=1, keepdims=True)
    pn = p * jnp.where(l > 0.0, 1.0 / l, 0.0)
    o_cmp = jnp.dot(pn.astype(BF16), cmp_ref[:, LANES:], preferred_element_type=F32)

    valid_row = (rowi % 8) < NSA_GROUP
    ps0 = jnp.sum(jnp.where(valid_row & (row_g == 0), pn, 0.0), axis=0, keepdims=True)
    ps1 = jnp.sum(jnp.where(valid_row & (row_g == 1), pn, 0.0), axis=0, keepdims=True)
    r8 = lax.broadcasted_iota(jnp.int32, (8, LANES), 0)
    j8 = lax.broadcasted_iota(jnp.int32, (8, LANES), 1)
    psum = jnp.where(r8 == 0, ps0, jnp.where(r8 == 1, ps1, 0.0))
    p_hi = psum.astype(BF16)
    p_lo = (psum - p_hi.astype(F32)).astype(BF16)
    imp = (jnp.dot(p_hi, mt_ref[...], preferred_element_type=F32)
           + jnp.dot(p_lo, mt_ref[...], preferred_element_type=F32))
    cur = n_keys // SLC_BLOCK
    forced = (j8 == 0) | (j8 == cur) | (j8 == cur - 1)
    score = jnp.where(forced, FORCED, jnp.where(j8 <= cur, imp, -1.0))
    rank = jnp.zeros((8, LANES), F32)
    for i in range(cur + 1):
        ri = score[:, i:i + 1]
        beats = (ri > score) | ((ri == score) & (j8 > i))
        rank = rank + jnp.where(beats, 1.0, 0.0)
    selb = jnp.where(rank < N_SELECT, 0.0, MASK)
    sel16 = jnp.where(row_g == 0, selb[0:1, :], selb[1:2, :]).astype(BF16)
    blockmask = jnp.dot(sel16, e_ref[...], preferred_element_type=F32)

    s = jnp.concatenate([_dot_nt(qb, r[:, :LANES].astype(BF16)) for r in page_refs], axis=1)
    s = s + sb_ref[...] + blockmask
    s_new = jnp.sum(qbd * snew_ref[:, :LANES], axis=1, keepdims=True) + b0_ref[...]
    pp, p_new = _softmax_parts(s, s_new)
    o_slc = p_new * snew_ref[:, LANES:]
    for k, r in enumerate(page_refs):
        o_slc = o_slc + jnp.dot(pp[:, k * PAGE_SIZE:(k + 1) * PAGE_SIZE].astype(BF16), r[:, LANES:].astype(BF16),
                                preferred_element_type=F32)

    n_wt = win_ref.shape[0] // LANES
    s = jnp.concatenate([_dot_nt(qb, win_ref[k * LANES:(k + 1) * LANES, :LANES].astype(BF16)) for k in range(n_wt)],
                        axis=1) + wb_ref[...]
    s_new = jnp.sum(qbd * wnew_ref[:, :LANES], axis=1, keepdims=True) + b0_ref[...]
    pp, p_new = _softmax_parts(s, s_new)
    o_win = p_new * wnew_ref[:, LANES:]
    for k in range(n_wt):
        o_win = o_win + jnp.dot(pp[:, k * LANES:(k + 1) * LANES].astype(BF16),
                                win_ref[k * LANES:(k + 1) * LANES, LANES:].astype(BF16), preferred_element_type=F32)

    gbase = MIX_W + MEM_W
    sig = jax.nn.sigmoid(p_ref[:, gbase:gbase + NSA_KV_HEADS * LANES] + bg_ref[...])
    gate = [jnp.zeros((QROWS, 1), F32) for _ in range(3)]
    rcol = lax.broadcasted_iota(jnp.int32, (QROWS, 1), 0)
    for br in range(3):
        for g in range(NSA_KV_HEADS):
            for z in range(NSA_GROUP):
                c = g * LANES + br * NSA_GROUP + z
                gate[br] = jnp.where(rcol == 8 * g + z, sig[:, c:c + 1], gate[br])
    comb = jnp.where(own, gate[0] * o_cmp + gate[1] * o_slc + gate[2] * o_win, 0.0)
    comb_r = pltpu.roll(comb, HEAD_DIM, 1)
    for k in range(NSA_HEADS // 2):
        acc = jnp.zeros((1, LANES), F32)
        for half in range(2):
            g, z = divmod(2 * k + half, NSA_GROUP)
            src = comb if half == g else comb_r
            acc = acc + src[8 * g + z:8 * g + z + 1, :]
        o_ref[:, k * LANES:(k + 1) * LANES] = acc


def nsa_decode_bias(rel_table, past, wb):
    table = rel_table.astype(F32)

    def rows16(t):
        z = jnp.zeros((2, t.shape[1]), F32)
        return jnp.concatenate([t[:NSA_GROUP], z, t[NSA_GROUP:], z], axis=0)

    n_cmp = (past + 1 - CMP_LEN) // CMP_STRIDE + 1
    c_end = np.arange(LANES) * CMP_STRIDE + CMP_LEN - 1
    dist = past - c_end
    cb = jnp.where(jnp.asarray((dist < 0) | (np.arange(LANES) >= n_cmp))[None], MASK, table[_rel_bucket_np(dist)].T)
    sb = table[_rel_bucket_np(past - np.arange(past))].T
    wpos = past - wb + np.arange(wb)
    wbias = jnp.where(jnp.asarray(past - wpos >= WINDOW)[None], MASK, table[_rel_bucket_np(past - wpos)].T)
    b0 = table[0][:, None]
    return rows16(cb), rows16(sb), rows16(wbias), rows16(b0)


def nsa_decode(p, g_q, b_gate, cmp_ctx, page_table, cache_slc, slc_new, cache_win, win_new, biases):
    b, n_pages = page_table.shape
    past = n_pages * PAGE_SIZE
    wb = cache_win.shape[1]
    cb, sb, wbias, b0 = biases
    n_cmp = (past + 1 - CMP_LEN) // CMP_STRIDE + 1
    n_slc = -(-(past + 1) // SLC_BLOCK)
    assert n_slc <= LANES and n_cmp <= LANES and wb % LANES == 0
    mt = np.zeros((LANES, LANES), np.float32)
    mt[:n_cmp, :n_slc] = _overlap_matrix(n_cmp, n_slc)
    e = (np.arange(past)[None, :] // SLC_BLOCK == np.arange(LANES)[:, None]).astype(np.float32)
    bg = _gate_bias_slabs(b_gate).reshape(1, NSA_KV_HEADS * LANES)
    c2 = lambda bi, pt: (0, 0)
    seq3 = lambda bi, pt: (bi, 0, 0)
    in_specs = ([pl.BlockSpec((None, 1, p.shape[1]), seq3),
                 pl.BlockSpec((1, LANES), c2),
                 pl.BlockSpec((1, NSA_KV_HEADS * LANES), c2),
                 pl.BlockSpec((None, LANES, KV_ROW), seq3),
                 pl.BlockSpec((QROWS, LANES), c2),
                 pl.BlockSpec((LANES, LANES), c2)]
                + [pl.BlockSpec((None, PAGE_SIZE, KV_ROW), partial(lambda k, bi, pt: (pt[bi, k], 0, 0), k))
                   for k in range(n_pages)]
                + [pl.BlockSpec((None, 1, KV_ROW), seq3),
                   pl.BlockSpec((QROWS, past), c2),
                   pl.BlockSpec((LANES, past), c2),
                   pl.BlockSpec((None, wb, KV_ROW), seq3),
                   pl.BlockSpec((None, 1, KV_ROW), seq3),
                   pl.BlockSpec((QROWS, wb), c2),
                   pl.BlockSpec((QROWS, 1), c2)])
    gs = pltpu.PrefetchScalarGridSpec(num_scalar_prefetch=1, grid=(b,), in_specs=in_specs,
                                      out_specs=pl.BlockSpec((None, 1, MIX_W), seq3))
    out = pl.pallas_call(
        partial(_nsa_decode_kernel, n_pages=n_pages),
        grid_spec=gs,
        out_shape=jax.ShapeDtypeStruct((b, 1, MIX_W), F32),
        compiler_params=_params("parallel"),
        name="nsa_decode",
    )(page_table, p.reshape(b, 1, -1), _gain2(g_q), bg, cmp_ctx, cb, jnp.asarray(mt, BF16),
      *([cache_slc] * n_pages), slc_new.reshape(b, 1, KV_ROW), sb, jnp.asarray(e, BF16),
      cache_win, win_new.reshape(b, 1, KV_ROW), wbias, b0)
    return out.reshape(b, MIX_W)


GATE_PAD = LANES


def _pad_heads_cols(w):
    d = w.shape[0]
    return jnp.pad(w.reshape(d, MLSTM_HEADS, MLSTM_DH), ((0, 0), (0, 0), (0, MLSTM_PAD - MLSTM_DH))).reshape(d, MLSTM_HW)


def prep_weights(w_in_a, w_in_b, w_kv, w_mem_kv, w_out, w_gu, w_down):
    w = {}
    n_gate_a = 2 * MLSTM_HEADS
    n_gate_b = 3 * NSA_HEADS
    w['in_a'] = []
    for l in range(N_A_LAYERS):
        wa = w_in_a[l]
        qkvo = [_pad_heads_cols(wa[:, j * MIX_W:(j + 1) * MIX_W]) for j in range(4)]
        w['in_a'].append(jnp.concatenate(qkvo + [wa[:, 4 * MIX_W + n_gate_a:],
                                                 _pad_cols(wa[:, 4 * MIX_W:4 * MIX_W + n_gate_a], GATE_PAD)],
                                         axis=1).astype(BF16))
    w['in_b'] = []
    for j in range(N_B_LAYERS):
        wg = w_in_b[j][:, MIX_W:MIX_W + n_gate_b].reshape(D_MODEL, 3, NSA_KV_HEADS, NSA_GROUP)
        slabs = [_pad_cols(wg[:, :, g].reshape(D_MODEL, 3 * NSA_GROUP), GATE_PAD) for g in range(NSA_KV_HEADS)]
        w['in_b'].append(jnp.concatenate([w_in_b[j][:, :MIX_W], w_in_b[j][:, MIX_W + n_gate_b:]] + slabs,
                                         axis=1).astype(BF16))
    w['out_mix'] = []
    for l in range(DEPTH):
        wm = w_out[l][:MIX_W]
        if l < N_A_LAYERS:
            wm = jnp.pad(wm.reshape(MLSTM_HEADS, MLSTM_DH, D_MODEL),
                         ((0, 0), (0, MLSTM_PAD - MLSTM_DH), (0, 0))).reshape(MLSTM_HW, D_MODEL)
        w['out_mix'].append(wm.astype(BF16))
    w['out_mem'] = w_out[:, MIX_W:].astype(BF16)
    w['kv'] = w_kv.astype(BF16)
    w['mem_kv'] = w_mem_kv.astype(BF16)
    w['gu'] = w_gu.astype(BF16)
    w['down'] = w_down.astype(BF16)
    return w


def kernel(x_prompt, x_sample, state_mlstm_c, state_mlstm_n, state_mlstm_m, cache_mem_kv,
           cache_cmp_kv, cache_slc_kv, cache_win_kv, page_table, mem_prompt,
           g_mix, w_in_a, b_if, g_hnorm, w_in_b, g_q, b_gate, rel_table, g_kv, w_kv,
           g_k_slc, g_k_win, g_k_cmp, cmp_pos, cmp_w1, cmp_b1, cmp_w2, cmp_b2,
           w_mem_kv, g_mem_k, g_mem_q, w_out, g_ffn, w_gu, w_down):
    w = prep_weights(w_in_a, w_in_b, w_kv, w_mem_kv, w_out, w_gu, w_down)
    cw = compress_weights(cmp_w1, cmp_b1, cmp_w2, cmp_b2, cmp_pos, g_k_cmp)
    qcol_a = 4 * MLSTM_HW // MEM_W
    qcol_b = MIX_W // MEM_W

    def finish_layer(l, x, mix, mem_o):
        x = out_proj(mix, w['out_mix'][l], mem_o, w['out_mem'][l], x)
        return ffn(x, g_ffn[l], w['gu'][l], w['down'][l])

    b_p, s_p, _ = x_prompt.shape
    m_tok = mem_prompt.shape[1]
    mem_kv_p = mem_kv_rows(mem_prompt.reshape(b_p * m_tok, D_MODEL), w['mem_kv'], g_mem_k)
    mem_kv_p = mem_kv_p.reshape(DEPTH, b_p, m_tok, 2 * MEM_W)
    tables = nsa_bias_tables(rel_table)
    c0, m0 = pack_mlstm_state(jnp.zeros((b_p, MLSTM_HEADS, MLSTM_DH, MLSTM_DH), F32),
                              jnp.zeros((b_p, MLSTM_HEADS, MLSTM_DH), F32),
                              jnp.full((b_p, MLSTM_HEADS), NEG, F32))
    x = x_prompt.reshape(b_p * s_p, D_MODEL)
    st_p = []
    for l in range(N_A_LAYERS):
        p = norm_matmul(x, g_mix[l], w['in_a'][l])
        mix, cn, mm = mlstm_prompt(p, b_p, s_p, b_if[l], g_hnorm[l], c0, m0)
        st_p.append(unpack_mlstm_state(cn, mm))
        mem_o = mem_attn_prompt(p, qcol_a, b_p, s_p, g_mem_q[l], mem_kv_p[l])
        x = finish_layer(l, x, mix, mem_o)
    cmp_p, slc_p, win_p, ka, vat = kv_rows(x, g_kv, w['kv'], g_k_slc, g_k_win, s_p, aug=True)
    ck, cvt = split_cmp_ctx(compress_rows(cmp_p.reshape(b_p, s_p // CMP_STRIDE, SEG_ROW), cw))
    for j in range(N_B_LAYERS):
        l = N_A_LAYERS + j
        p = norm_matmul(x, g_mix[l], w['in_b'][j])
        mix = nsa_prompt2(p, b_p, s_p, g_q[j], b_gate[j], rel_table, ck, cvt, ka, vat, tables)
        mem_o = mem_attn_prompt(p, qcol_b, b_p, s_p, g_mem_q[l], mem_kv_p[l])
        x = finish_layer(l, x, mix, mem_o)
    y_p = x.reshape(b_p, s_p, D_MODEL)

    b_s, s_s, _ = x_sample.shape
    assert s_s == 1
    n_pages = page_table.shape[1]
    past = n_pages * PAGE_SIZE
    wb = cache_win_kv.shape[1]
    n_pool = cache_cmp_kv.shape[0]
    x = x_sample.reshape(b_s, D_MODEL)
    st_s = []
    for l in range(N_A_LAYERS):
        p = norm_matmul(x, g_mix[l], w['in_a'][l])
        mix, c_new, n_new, m_new = mlstm_step(p, b_if[l], g_hnorm[l], state_mlstm_c[l], state_mlstm_n[l],
                                              state_mlstm_m[l])
        st_s.append((c_new, n_new, m_new))
        mem_o = mem_attn_decode(p, qcol_a, g_mem_q[l], cache_mem_kv[l].reshape(b_s, -1, 2 * MEM_W))
        x = finish_layer(l, x, mix, mem_o)
    cmp_s, slc_s, win_s = kv_rows(x, g_kv, w['kv'], g_k_slc, g_k_win, 1, aug=False)
    assert (past + 1 - CMP_LEN) // CMP_STRIDE + 1 == past // CMP_STRIDE - 1
    ctx_s = compress_rows(cache_cmp_kv.reshape(n_pool, SEG_PER_PAGE, SEG_ROW), cw, page_table)
    biases = nsa_decode_bias(rel_table, past, wb)
    cache_slc = cache_slc_kv.reshape(n_pool, PAGE_SIZE, KV_ROW)
    cache_win = cache_win_kv.reshape(b_s, wb, KV_ROW)
    for j in range(N_B_LAYERS):
        l = N_A_LAYERS + j
        p = norm_matmul(x, g_mix[l], w['in_b'][j])
        mix = nsa_decode(p, g_q[j], b_gate[j], ctx_s, page_table, cache_slc, slc_s, cache_win, win_s, biases)
        mem_o = mem_attn_decode(p, qcol_b, g_mem_q[l], cache_mem_kv[l].reshape(b_s, -1, 2 * MEM_W))
        x = finish_layer(l, x, mix, mem_o)
    y_s = x.reshape(b_s, 1, D_MODEL)

    rows5 = lambda r, b: r.reshape(b, -1, 2, NSA_KV_HEADS, HEAD_DIM)
    win_p5 = rows5(win_p, b_p)
    p_win = win_p5[:, s_p - min(WINDOW, s_p):]
    s_win = jnp.concatenate([cache_win_kv, rows5(win_s, b_s)], axis=1)[:, -wb:]
    stack = lambda sts, i: jnp.stack([st[i] for st in sts])
    return (y_p, y_s, stack(st_p, 0), stack(st_p, 1), stack(st_p, 2),
            mem_kv_p.reshape(DEPTH, b_p, m_tok, 2, MEM_HEADS, HEAD_DIM),
            rows5(cmp_p, b_p), rows5(slc_p, b_p), p_win,
            stack(st_s, 0), stack(st_s, 1), stack(st_s, 2),
            rows5(cmp_s, b_s), rows5(slc_s, b_s), s_win)
```

```python
import math
from functools import partial

import numpy as np
import jax
import jax.numpy as jnp
from jax import lax
from jax.experimental import pallas as pl
from jax.experimental.pallas import tpu as pltpu

D_MODEL = 1024
DEPTH = 4
PAGE_SIZE = 128
N_A_LAYERS = DEPTH // 2
N_B_LAYERS = DEPTH - N_A_LAYERS
MIX_W = 3 * D_MODEL // 4
MEM_W = D_MODEL - MIX_W
HEAD_DIM = 64
MEM_HEADS = MEM_W // HEAD_DIM
MLSTM_HEADS = 4
MLSTM_DH = MIX_W // MLSTM_HEADS
NSA_HEADS = MIX_W // HEAD_DIM
NSA_KV_HEADS = 2
NSA_GROUP = NSA_HEADS // NSA_KV_HEADS
CMP_LEN = 32
CMP_STRIDE = 16
CMP_HIDDEN = 128
SLC_BLOCK = 64
N_SELECT = 8
WINDOW = 512
REL_BUCKETS = 32
REL_MAX_EXACT = 16
REL_MAX_DIST = 128
D_FF = -(-(8 * D_MODEL) // (3 * 256)) * 256
EPS = 1e-6
NEG = -1e30
FORCED = 1e6

LANES = 128
VMEM_LIMIT = 48 * 1024 * 1024

BF16 = jnp.bfloat16
F32 = jnp.float32


def _pick_tile(n, candidates):
    for c in candidates:
        if n % c == 0:
            return c
    return n


def _pad_cols(w, n):
    return jnp.pad(w, ((0, 0), (0, n - w.shape[1])))


def _dot_nt(a, b):
    return lax.dot_general(a, b, (((1,), (1,)), ((), ())), preferred_element_type=F32)


def _rms_rows(x, g):
    return x * lax.rsqrt(jnp.mean(x * x, axis=-1, keepdims=True) + EPS) * g


def _half_rms(x, g2, lo):
    x2 = x * x
    ss_lo = jnp.sum(jnp.where(lo, x2, 0.0), axis=1, keepdims=True)
    ss_hi = jnp.sum(jnp.where(lo, 0.0, x2), axis=1, keepdims=True)
    inv = jnp.where(lo, lax.rsqrt(ss_lo / HEAD_DIM + EPS), lax.rsqrt(ss_hi / HEAD_DIM + EPS))
    return x * inv * g2


def _gain2(g):
    return jnp.concatenate([g, g]).reshape(1, LANES).astype(F32)


def _params(*sem):
    return pltpu.CompilerParams(dimension_semantics=sem, vmem_limit_bytes=VMEM_LIMIT)


def _norm_matmul_kernel(x_ref, g_ref, w_ref, o_ref, xn_ref):
    @pl.when(pl.program_id(1) == 0)
    def _():
        xn_ref[...] = _rms_rows(x_ref[...], g_ref[...]).astype(BF16)

    o_ref[...] = jnp.dot(xn_ref[...], w_ref[...], preferred_element_type=F32)


def norm_matmul(x, g, w):
    m, k = x.shape
    n = w.shape[1]
    tm = _pick_tile(m, (1024, 512, 256, 128))
    tn = _pick_tile(n, (1280, 1152, 1024, 896, 768, 640, 512, 384, 256, 128))
    return pl.pallas_call(
        _norm_matmul_kernel,
        grid=(m // tm, n // tn),
        in_specs=[pl.BlockSpec((tm, k), lambda i, j: (i, 0)),
                  pl.BlockSpec((1, k), lambda i, j: (0, 0)),
                  pl.BlockSpec((k, tn), lambda i, j: (0, j))],
        out_specs=pl.BlockSpec((tm, tn), lambda i, j: (i, j)),
        out_shape=jax.ShapeDtypeStruct((m, n), F32),
        scratch_shapes=[pltpu.VMEM((tm, k), BF16)],
        compiler_params=_params("parallel", "arbitrary"),
        name="norm_matmul",
    )(x, g.reshape(1, k), w)


def _mem_kv_kernel(x_ref, w_ref, g_ref, o_ref):
    tm = x_ref.shape[0]
    kv = jnp.dot(x_ref[...].astype(BF16), w_ref[...], preferred_element_type=F32)
    lo = lax.broadcasted_iota(jnp.int32, (tm, LANES), 1) < HEAD_DIM
    for k in range(MEM_HEADS // 2):
        o_ref[:, k * LANES:(k + 1) * LANES] = _half_rms(kv[:, k * LANES:(k + 1) * LANES], g_ref[...], lo)
    o_ref[:, MEM_W:] = kv[:, MEM_W:]


def mem_kv_rows(mem, w, g_k):
    m, d = mem.shape
    depth, _, n = w.shape
    tm = _pick_tile(m, (512, 256, 128))
    g2 = jnp.concatenate([g_k, g_k], axis=1).reshape(depth, 1, LANES).astype(F32)
    return pl.pallas_call(
        _mem_kv_kernel,
        grid=(depth, m // tm),
        in_specs=[pl.BlockSpec((tm, d), lambda l, i: (i, 0)),
                  pl.BlockSpec((None, d, n), lambda l, i: (l, 0, 0)),
                  pl.BlockSpec((None, 1, LANES), lambda l, i: (l, 0, 0))],
        out_specs=pl.BlockSpec((None, tm, n), lambda l, i: (l, i, 0)),
        out_shape=jax.ShapeDtypeStruct((depth, m, n), F32),
        compiler_params=_params("parallel", "parallel"),
        name="mem_kv_rows",
    )(mem, w, g2)


def _out_proj_kernel(a_ref, wa_ref, b_ref, wb_ref, r_ref, o_ref):
    o_ref[...] = (r_ref[...] + jnp.dot(a_ref[...].astype(BF16), wa_ref[...], preferred_element_type=F32)
                  + jnp.dot(b_ref[...].astype(BF16), wb_ref[...], preferred_element_type=F32))


def out_proj(a, wa, b, wb, res):
    m, n = res.shape
    tm = _pick_tile(m, (512, 256, 128))
    ka, kb = a.shape[1], b.shape[1]
    return pl.pallas_call(
        _out_proj_kernel,
        grid=(m // tm,),
        in_specs=[pl.BlockSpec((tm, ka), lambda i: (i, 0)),
                  pl.BlockSpec((ka, n), lambda i: (0, 0)),
                  pl.BlockSpec((tm, kb), lambda i: (i, 0)),
                  pl.BlockSpec((kb, n), lambda i: (0, 0)),
                  pl.BlockSpec((tm, n), lambda i: (i, 0))],
        out_specs=pl.BlockSpec((tm, n), lambda i: (i, 0)),
        out_shape=jax.ShapeDtypeStruct((m, n), F32),
        compiler_params=_params("parallel"),
        name="out_proj",
    )(a, wa, b, wb, res)


def _ffn_kernel(x_ref, g_ref, wg_ref, wu_ref, wd_ref, o_ref, xn_ref, acc_ref):
    f = pl.program_id(1)

    @pl.when(f == 0)
    def _():
        xn_ref[...] = _rms_rows(x_ref[...], g_ref[...]).astype(BF16)
        acc_ref[...] = jnp.zeros_like(acc_ref)

    xn = xn_ref[...]
    gate = jnp.dot(xn, wg_ref[...], preferred_element_type=F32)
    up = jnp.dot(xn, wu_ref[...], preferred_element_type=F32)
    act = (gate * jax.nn.sigmoid(gate) * up).astype(BF16)
    acc_ref[...] += jnp.dot(act, wd_ref[...], preferred_element_type=F32)

    @pl.when(f == pl.num_programs(1) - 1)
    def _():
        o_ref[...] = x_ref[...] + acc_ref[...]


def ffn(x, g, w_gu, w_down):
    m, d = x.shape
    tm = _pick_tile(m, (512, 256, 128))
    tf = 1408
    nf = D_FF // tf
    return pl.pallas_call(
        _ffn_kernel,
        grid=(m // tm, nf),
        in_specs=[pl.BlockSpec((tm, d), lambda i, f: (i, 0)),
                  pl.BlockSpec((1, d), lambda i, f: (0, 0)),
                  pl.BlockSpec((d, tf), lambda i, f: (0, f)),
                  pl.BlockSpec((d, tf), lambda i, f: (0, f + nf)),
                  pl.BlockSpec((tf, d), lambda i, f: (f, 0))],
        out_specs=pl.BlockSpec((tm, d), lambda i, f: (i, 0)),
        out_shape=jax.ShapeDtypeStruct((m, d), F32),
        scratch_shapes=[pltpu.VMEM((tm, d), BF16), pltpu.VMEM((tm, d), F32)],
        compiler_params=_params("parallel", "arbitrary"),
        name="ffn",
    )(x, g.reshape(1, d), w_gu, w_gu, w_down)


MLSTM_L = 128
MLSTM_PAD = 256
N_COL = MLSTM_DH
MLSTM_HW = MLSTM_HEADS * MLSTM_PAD
STEP_TB = 8


def _exact_tri_cumsum(tri, x):
    hi = x.astype(BF16)
    r1 = x - hi.astype(F32)
    mid = r1.astype(BF16)
    lo = (r1 - mid.astype(F32)).astype(BF16)
    return (jnp.dot(tri, hi, preferred_element_type=F32) + jnp.dot(tri, mid, preferred_element_type=F32)
            + jnp.dot(tri, lo, preferred_element_type=F32))


def _mlstm_kernel(q_ref, k_ref, v_ref, og_ref, gate_ref, bif_ref, gh_ref, c0_ref, m0_ref,
                  mix_ref, cout_ref, mout_ref, c_scr, m_scr):
    L = MLSTM_L
    P = MLSTM_PAD
    ci = pl.program_id(1)

    @pl.when(ci == 0)
    def _():
        c_scr[...] = c0_ref[...]
        m_scr[...] = m0_ref[...]

    row = lax.broadcasted_iota(jnp.int32, (L, L), 0)
    col = lax.broadcasted_iota(jnp.int32, (L, L), 1)
    causal = col <= row
    tri = jnp.where(causal, 1.0, 0.0).astype(BF16)
    lane_p = lax.broadcasted_iota(jnp.int32, (L, P), 1)

    gl = gate_ref[...] + bif_ref[...]
    logf = jax.nn.log_sigmoid(gl)
    bc = _exact_tri_cumsum(tri, logf)
    bt = bc.T
    gt = gl.T
    for h in range(MLSTM_HEADS):
        fh = MLSTM_HEADS + h
        b_col = bc[:, fh:fh + 1]
        b_row = bt[fh:fh + 1, :]
        i_row = gt[h:h + 1, :]
        i_col = gl[:, h:h + 1]
        m_prev = m_scr[0:1, h:h + 1]
        log_d = jnp.where(causal, b_col - b_row + i_row, -jnp.inf)
        log_inter = b_col + m_prev
        m_t = jnp.maximum(jnp.max(log_d, axis=1, keepdims=True), log_inter)
        w_intra = jnp.exp(log_d - m_t)
        w_inter = jnp.exp(log_inter - m_t)
        sl = slice(h * P, (h + 1) * P)
        q = q_ref[:, sl].astype(BF16)
        kf = k_ref[:, sl] * MLSTM_DH ** -0.5
        v_aug = jnp.where(lane_p == N_COL, 1.0, v_ref[:, sl])
        sc = _dot_nt(q, kf.astype(BF16)) * w_intra
        c = c_scr[h]
        nd = (jnp.dot(sc.astype(BF16), v_aug.astype(BF16), preferred_element_type=F32)
              + w_inter * jnp.dot(q, c.astype(BF16), preferred_element_type=F32))
        den = nd[:, N_COL:N_COL + 1]
        hh = jnp.where(lane_p < N_COL, nd / jnp.maximum(jnp.abs(den), jnp.exp(-m_t)), 0.0)
        y = hh * lax.rsqrt(jnp.sum(hh * hh, axis=1, keepdims=True) / MLSTM_DH + EPS) * gh_ref[...]
        mix_ref[:, sl] = jax.nn.sigmoid(og_ref[:, sl]) * y
        g_last = bc[L - 1:L, fh:fh + 1]
        m_new = m_t[L - 1:L, :]
        w_s = jnp.exp(g_last - b_col + i_col - m_new)
        decay = jnp.exp(g_last + m_prev - m_new)
        upd = jnp.dot(kf.T.astype(BF16), (w_s * v_aug).astype(BF16), preferred_element_type=F32)
        c_scr[h] = decay * c + upd
        m_scr[0:1, h:h + 1] = m_new

    @pl.when(ci == pl.num_programs(1) - 1)
    def _():
        cout_ref[...] = c_scr[...]
        mout_ref[...] = m_scr[...]


def mlstm_prompt(p, b, s, b_if, g_hnorm, c0, m0):
    L = MLSTM_L
    P = MLSTM_PAD
    hw = MLSTM_HW
    assert s % L == 0
    nc = s // L
    gcol = (4 * hw + MEM_W) // LANES
    bif = _pad_cols(b_if.astype(F32).reshape(1, -1), LANES)
    gh = _pad_cols(g_hnorm.astype(F32).reshape(1, -1), P)
    blk = lambda j: pl.BlockSpec((L, hw), lambda bi, ci: (bi * nc + ci, j))
    return pl.pallas_call(
        _mlstm_kernel,
        grid=(b, nc),
        in_specs=[blk(0), blk(1), blk(2), blk(3),
                  pl.BlockSpec((L, LANES), lambda bi, ci: (bi * nc + ci, gcol)),
                  pl.BlockSpec((1, LANES), lambda bi, ci: (0, 0)),
                  pl.BlockSpec((1, P), lambda bi, ci: (0, 0)),
                  pl.BlockSpec((None, MLSTM_HEADS, P, P), lambda bi, ci: (bi, 0, 0, 0)),
                  pl.BlockSpec((None, 1, LANES), lambda bi, ci: (bi, 0, 0))],
        out_specs=[pl.BlockSpec((L, hw), lambda bi, ci: (bi * nc + ci, 0)),
                   pl.BlockSpec((None, MLSTM_HEADS, P, P), lambda bi, ci: (bi, 0, 0, 0)),
                   pl.BlockSpec((None, 1, LANES), lambda bi, ci: (bi, 0, 0))],
        out_shape=[jax.ShapeDtypeStruct((b * s, hw), F32),
                   jax.ShapeDtypeStruct((b, MLSTM_HEADS, P, P), F32),
                   jax.ShapeDtypeStruct((b, 1, LANES), F32)],
        scratch_shapes=[pltpu.VMEM((MLSTM_HEADS, P, P), F32), pltpu.VMEM((1, LANES), F32)],
        compiler_params=_params("parallel", "arbitrary"),
        name="mlstm_prompt",
    )(p, p, p, p, p, bif, gh, c0, m0)


def pack_mlstm_state(c, n, m):
    pad = MLSTM_PAD - MLSTM_DH
    cn = jnp.concatenate([c, n[..., None]], axis=-1)
    cn = jnp.pad(cn, ((0, 0), (0, 0), (0, pad), (0, pad - 1)))
    return cn, _pad_cols(m, LANES)[:, None, :]


def unpack_mlstm_state(cn, m):
    return cn[:, :, :MLSTM_DH, :MLSTM_DH], cn[:, :, :MLSTM_DH, MLSTM_DH], m[:, 0, :MLSTM_HEADS]


def _mlstm_step_kernel(q_ref, v_ref, og_ref, gate_ref, qkt_ref, bif_ref, gh_ref, c_ref, n_ref, m_ref,
                       mix_ref, cout_ref, nout_ref, mout_ref):
    tb = q_ref.shape[0]
    dh = MLSTM_DH
    gl = gate_ref[...] + bif_ref[...]
    logf = jax.nn.log_sigmoid(gl)
    i4 = gl[:, :MLSTM_HEADS]
    f4 = logf[:, MLSTM_HEADS:2 * MLSTM_HEADS]
    m_prev = m_ref[...]
    m_t = jnp.maximum(i4, f4 + m_prev)
    w_in = jnp.exp(i4 - m_t)
    w_dec = jnp.exp(f4 + m_prev - m_t)
    floor = jnp.exp(-m_t)
    mout_ref[...] = m_t
    mix_ref[...] = jnp.zeros(mix_ref.shape, F32)
    for j in range(tb):
        for h in range(MLSTM_HEADS):
            sl = slice(h * MLSTM_PAD, h * MLSTM_PAD + dh)
            wi = w_in[j:j + 1, h:h + 1]
            wd = w_dec[j:j + 1, h:h + 1]
            q_row = q_ref[j:j + 1, sl]
            v_row = v_ref[j:j + 1, sl]
            q_col = qkt_ref[0, h, :, j:j + 1]
            k_col = qkt_ref[1, h, :, j:j + 1]
            k_row = qkt_ref[2, h, j:j + 1, :]
            c = c_ref[j, h]
            n_row = n_ref[j, h:h + 1, :]
            qk = jnp.sum(q_row * k_row, axis=1, keepdims=True) * wi
            qc = jnp.sum(q_col * c, axis=0, keepdims=True)
            qn = jnp.sum(q_row * n_row, axis=1, keepdims=True)
            num = qk * v_row + wd * qc
            den = qk + wd * qn
            hh = num / jnp.maximum(jnp.abs(den), floor[j:j + 1, h:h + 1])
            y = hh * lax.rsqrt(jnp.sum(hh * hh, axis=1, keepdims=True) / dh + EPS) * gh_ref[...]
            mix_ref[j:j + 1, sl] = jax.nn.sigmoid(og_ref[j:j + 1, sl]) * y
            cout_ref[j, h] = wd * c + k_col * (wi * v_row)
            nout_ref[j, h:h + 1, :] = wd * n_row + wi * k_row


def mlstm_step(p, b_if, g_hnorm, c, n, m, layer):
    b = p.shape[0]
    tb = STEP_TB
    hw = MLSTM_HW
    dh = MLSTM_DH
    nb = b // tb
    gcol = (4 * hw + MEM_W) // LANES
    q = p[:, :hw].reshape(b, MLSTM_HEADS, MLSTM_PAD)[:, :, :dh]
    k = p[:, hw:2 * hw].reshape(b, MLSTM_HEADS, MLSTM_PAD)[:, :, :dh] * dh ** -0.5
    cols = lambda a: a.reshape(nb, tb, MLSTM_HEADS, dh).transpose(0, 2, 3, 1)
    pad_c = jnp.pad(jnp.stack([cols(q), cols(k)], axis=1), ((0, 0), (0, 0), (0, 0), (0, 0), (0, dh - tb)))
    k_rows = k.reshape(nb, tb, MLSTM_HEADS, dh).transpose(0, 2, 1, 3)
    pad_r = jnp.pad(k_rows, ((0, 0), (0, 0), (0, dh - tb), (0, 0)))
    qkt = jnp.concatenate([pad_c, pad_r[:, None]], axis=1)
    bif = _pad_cols(b_if.astype(F32).reshape(1, -1), LANES)
    gh = g_hnorm.astype(F32).reshape(1, dh)
    blk = lambda j: pl.BlockSpec((tb, hw), lambda i: (i, j))
    st4 = pl.BlockSpec((tb, MLSTM_HEADS, dh, dh), lambda i: (i, 0, 0, 0))
    st3 = pl.BlockSpec((tb, MLSTM_HEADS, dh), lambda i: (i, 0, 0))
    st2 = pl.BlockSpec((tb, MLSTM_HEADS), lambda i: (i, 0))
    return pl.pallas_call(
        _mlstm_step_kernel,
        grid=(nb,),
        in_specs=[blk(0), blk(2), blk(3),
                  pl.BlockSpec((tb, LANES), lambda i: (i, gcol)),
                  pl.BlockSpec((None, 3, MLSTM_HEADS, dh, dh), lambda i: (i, 0, 0, 0, 0)),
                  pl.BlockSpec((1, LANES), lambda i: (0, 0)),
                  pl.BlockSpec((1, dh), lambda i: (0, 0)),
                  pl.BlockSpec((None, tb, MLSTM_HEADS, dh, dh), lambda i: (layer, i, 0, 0, 0)),
                  pl.BlockSpec((None, tb, MLSTM_HEADS, dh), lambda i: (layer, i, 0, 0)),
                  pl.BlockSpec((None, tb, MLSTM_HEADS), lambda i: (layer, i, 0))],
        out_specs=[pl.BlockSpec((tb, hw), lambda i: (i, 0)), st4, st3, st2],
        out_shape=[jax.ShapeDtypeStruct((b, hw), F32), jax.ShapeDtypeStruct(c.shape[1:], F32),
                   jax.ShapeDtypeStruct(n.shape[1:], F32), jax.ShapeDtypeStruct(m.shape[1:], F32)],
        compiler_params=_params("parallel"),
        name="mlstm_step",
    )(p, p, p, p, qkt, bif, gh, c, n, m)


def _mem_attn_kernel(q_ref, gq_ref, kv_ref, o_ref):
    tq = q_ref.shape[0]
    lane = lax.broadcasted_iota(jnp.int32, (tq, LANES), 1)
    lo = lane < HEAD_DIM
    for k in range(MEM_HEADS // 2):
        y = _half_rms(q_ref[:, k * LANES:(k + 1) * LANES], gq_ref[...], lo) * HEAD_DIM ** -0.5
        kp = kv_ref[:, k * LANES:(k + 1) * LANES].astype(BF16)
        vp = kv_ref[:, MEM_W + k * LANES:MEM_W + (k + 1) * LANES].astype(BF16)
        outs = []
        for half in range(2):
            qh = jnp.where(lo if half == 0 else ~lo, y, 0.0).astype(BF16)
            s = _dot_nt(qh, kp)
            p = jnp.exp(s - jnp.max(s, axis=1, keepdims=True))
            p = p / jnp.sum(p, axis=1, keepdims=True)
            outs.append(jnp.dot(p.astype(BF16), vp, preferred_element_type=F32))
        o_ref[:, k * LANES:(k + 1) * LANES] = jnp.where(lo, outs[0], outs[1])


def mem_attn_prompt(p, qcol, b, s, g_q, mem_kv, layer):
    tq = _pick_tile(s, (512, 256, 128))
    nq = s // tq
    m_tok = mem_kv.shape[2]
    return pl.pallas_call(
        _mem_attn_kernel,
        grid=(b, nq),
        in_specs=[pl.BlockSpec((tq, MEM_W), lambda bi, qi: (bi * nq + qi, qcol)),
                  pl.BlockSpec((1, LANES), lambda bi, qi: (0, 0)),
                  pl.BlockSpec((None, None, m_tok, 2 * MEM_W), lambda bi, qi: (layer, bi, 0, 0))],
        out_specs=pl.BlockSpec((tq, MEM_W), lambda bi, qi: (bi * nq + qi, 0)),
        out_shape=jax.ShapeDtypeStruct((b * s, MEM_W), F32),
        compiler_params=_params("parallel", "parallel"),
        name="mem_attn",
    )(p, _gain2(g_q), mem_kv)


def _mem_decode_kernel(q_ref, gq_ref, kv_ref, o_ref):
    tb = q_ref.shape[0]
    lane = lax.broadcasted_iota(jnp.int32, (tb, LANES), 1)
    lo = lane < HEAD_DIM
    y = jnp.concatenate([_half_rms(q_ref[:, k * LANES:(k + 1) * LANES], gq_ref[...], lo)
                         for k in range(MEM_HEADS // 2)], axis=1) * HEAD_DIM ** -0.5
    rowi = lax.broadcasted_iota(jnp.int32, (8, MEM_W), 0)
    own = (lax.broadcasted_iota(jnp.int32, (8, MEM_W), 1) // HEAD_DIM) == rowi
    for j in range(tb):
        qbd = jnp.where(own, y[j:j + 1, :], 0.0).astype(BF16)
        s = _dot_nt(qbd, kv_ref[j, :, :MEM_W].astype(BF16))
        p = jnp.exp(s - jnp.max(s, axis=1, keepdims=True))
        p = p / jnp.sum(p, axis=1, keepdims=True)
        o = jnp.dot(p.astype(BF16), kv_ref[j, :, MEM_W:].astype(BF16), preferred_element_type=F32)
        o_ref[j:j + 1, :] = jnp.sum(jnp.where(own, o, 0.0), axis=0, keepdims=True)


def mem_attn_decode(p, qcol, g_q, mem_kv, layer):
    b = p.shape[0]
    tb = STEP_TB
    m_tok = mem_kv.shape[2]
    return pl.pallas_call(
        _mem_decode_kernel,
        grid=(b // tb,),
        in_specs=[pl.BlockSpec((tb, MEM_W), lambda i: (i, qcol)),
                  pl.BlockSpec((1, LANES), lambda i: (0, 0)),
                  pl.BlockSpec((None, tb, m_tok, 2 * MEM_W), lambda i: (layer, i, 0, 0))],
        out_specs=pl.BlockSpec((tb, MEM_W), lambda i: (i, 0)),
        out_shape=jax.ShapeDtypeStruct((b, MEM_W), F32),
        compiler_params=_params("parallel"),
        name="mem_attn_decode",
    )(p, _gain2(g_q), mem_kv)


NSA_TQ = 128
MASK = -(2.0 ** 100)
SEL_LANE = HEAD_DIM
N_SEL_LANES = 32
CONST_LANE = SEL_LANE + N_SEL_LANES
KV_ROW = 2 * NSA_KV_HEADS * HEAD_DIM
SEG_PER_PAGE = PAGE_SIZE // CMP_STRIDE
SEG_ROW = CMP_STRIDE * KV_ROW
CMP_OUT = 2 * NSA_KV_HEADS * CMP_HIDDEN
QROWS = 16


def _rel_bucket_np(dist):
    d = np.maximum(dist, 0)
    ratio = np.maximum(d, REL_MAX_EXACT).astype(np.float64) / REL_MAX_EXACT
    large = REL_MAX_EXACT + (np.log(ratio) / math.log(REL_MAX_DIST / REL_MAX_EXACT)
                             * (REL_BUCKETS - REL_MAX_EXACT)).astype(np.int32)
    return np.where(d < REL_MAX_EXACT, d, np.minimum(large, REL_BUCKETS - 1)).astype(np.int32)


def _overlap_matrix(n_cmp, n_slc):
    c0 = np.arange(n_cmp)[:, None] * CMP_STRIDE
    s0 = np.arange(n_slc)[None, :] * SLC_BLOCK
    return np.clip(np.minimum(c0 + CMP_LEN, s0 + SLC_BLOCK) - np.maximum(c0, s0), 0, None) / CMP_STRIDE


def _kv_rows_kernel(x_ref, g_ref, w_ref, gs_ref, gw_ref, cmp_ref, slc_ref, win_ref, *aug_refs, seq_len):
    tm = x_ref.shape[0]
    xn = _rms_rows(x_ref[...], g_ref[...]).astype(BF16)
    kv = jnp.dot(xn, w_ref[...], preferred_element_type=F32)
    cmp_ref[...] = kv[:, :KV_ROW]
    lane = lax.broadcasted_iota(jnp.int32, (tm, LANES), 1)
    lo = lane < HEAD_DIM
    if aug_refs:
        ka_ref, va_ref = aug_refs
        pos = (pl.program_id(0) * tm + lax.broadcasted_iota(jnp.int32, (tm, LANES), 0)) % seq_len
        ones = (lane == CONST_LANE) | (lane == CONST_LANE + 1)
        onehot = (lane - SEL_LANE) == pos // SLC_BLOCK
    for t, (rows_ref, gk_ref) in enumerate(((slc_ref, gs_ref), (win_ref, gw_ref))):
        base = KV_ROW * (t + 1)
        kn = _half_rms(kv[:, base:base + LANES], gk_ref[...], lo)
        vv = kv[:, base + LANES:base + 2 * LANES]
        rows_ref[:, :LANES] = kn
        rows_ref[:, LANES:] = vv
        if aug_refs:
            extra = jnp.where((ones | onehot) if t == 0 else ones, 1.0, 0.0)
            rowt = lax.broadcasted_iota(jnp.int32, (VT_ROWS, tm), 0)
            for g in range(NSA_KV_HEADS):
                kg = kn if g == 0 else pltpu.roll(kn, HEAD_DIM, 1)
                vg = vv if g == 0 else pltpu.roll(vv, HEAD_DIM, 1)
                ka_ref[NSA_KV_HEADS * t + g] = jnp.where(lo, kg, extra).astype(BF16)
                vt = jnp.where(lo, vg, 0.0).T[:VT_ROWS, :]
                va_ref[NSA_KV_HEADS * t + g] = jnp.where(rowt == HEAD_DIM, 1.0, vt).astype(BF16)


def kv_rows(x, g_kv, w_kv, g_k_slc, g_k_win, seq_len, aug):
    m, d = x.shape
    n = w_kv.shape[1]
    tm = _pick_tile(m, (512, 256, 128))
    out_specs = [pl.BlockSpec((tm, KV_ROW), lambda i: (i, 0))] * 3
    out_shape = [jax.ShapeDtypeStruct((m, KV_ROW), F32)] * 3
    if aug:
        out_specs += [pl.BlockSpec((2 * NSA_KV_HEADS, tm, LANES), lambda i: (0, i, 0)),
                      pl.BlockSpec((2 * NSA_KV_HEADS, VT_ROWS, tm), lambda i: (0, 0, i))]
        out_shape += [jax.ShapeDtypeStruct((2 * NSA_KV_HEADS, m, LANES), BF16),
                      jax.ShapeDtypeStruct((2 * NSA_KV_HEADS, VT_ROWS, m), BF16)]
    return pl.pallas_call(
        partial(_kv_rows_kernel, seq_len=seq_len),
        grid=(m // tm,),
        in_specs=[pl.BlockSpec((tm, d), lambda i: (i, 0)),
                  pl.BlockSpec((1, d), lambda i: (0, 0)),
                  pl.BlockSpec((d, n), lambda i: (0, 0)),
                  pl.BlockSpec((1, LANES), lambda i: (0, 0)),
                  pl.BlockSpec((1, LANES), lambda i: (0, 0))],
        out_specs=out_specs,
        out_shape=out_shape,
        compiler_params=_params("parallel"),
        name="kv_rows",
    )(x, g_kv.reshape(1, d), w_kv, _gain2(g_k_slc), _gain2(g_k_win))


def _compress_kernel(*refs, n_x, paged):
    if paged:
        refs = refs[1:]
    x_refs = refs[:n_x]
    w1_ref, b1_ref, w2_ref, b2_ref, gk_ref, o_ref = refs[n_x:]
    n_seg = o_ref.shape[0]
    acc = [jnp.zeros((n_seg, CMP_OUT), F32) for _ in range(CMP_LEN // CMP_STRIDE)]
    for u in range(CMP_STRIDE):
        pieces = [r[:, u * KV_ROW:(u + 1) * KV_ROW] for r in x_refs]
        xu = (pieces[0] if n_x == 1 else jnp.concatenate(pieces, axis=0)).astype(BF16)
        for r in range(len(acc)):
            acc[r] = acc[r] + jnp.dot(xu, w1_ref[r, u], preferred_element_type=F32)
    pre = acc[0] + pltpu.roll(acc[1], n_seg - 1, 0) + b1_ref[...]
    hid = jax.nn.gelu(pre).astype(BF16)
    out = jnp.dot(hid, w2_ref[...], preferred_element_type=F32) + b2_ref[...]
    lane = lax.broadcasted_iota(jnp.int32, (n_seg, LANES), 1)
    o_ref[:, :LANES] = _half_rms(out[:, :LANES], gk_ref[...], lane < HEAD_DIM).astype(BF16)
    o_ref[:, LANES:] = out[:, LANES:].astype(BF16)


def compress_weights(w1, b1, w2, b2, pos_enc, g_k_cmp):
    r_n = CMP_LEN // CMP_STRIDE
    eye = jnp.eye(NSA_KV_HEADS, dtype=F32)
    w1r = w1.reshape(2, r_n, CMP_STRIDE, HEAD_DIM, CMP_HIDDEN)
    w1b = jnp.einsum('crudh,cx,gy->rucgdxyh', w1r, eye, eye).reshape(r_n, CMP_STRIDE, KV_ROW, CMP_OUT).astype(BF16)
    pe = jnp.einsum('pcd,cpdh->ch', pos_enc, w1)
    b1b = jnp.broadcast_to((b1 + pe)[:, None, :], (2, NSA_KV_HEADS, CMP_HIDDEN)).reshape(1, CMP_OUT)
    w2b = jnp.einsum('chd,cx,gy->cghxyd', w2, eye, eye).reshape(CMP_OUT, KV_ROW).astype(BF16)
    b2b = jnp.broadcast_to(b2[:, None, :], (2, NSA_KV_HEADS, HEAD_DIM)).reshape(1, KV_ROW)
    return w1b, b1b, w2b, b2b, _gain2(g_k_cmp)


def compress_rows(x, cw, page_table=None):
    w1b, b1b, w2b, b2b, gk2 = cw
    paged = page_table is not None
    if paged:
        n_seq, n_x = page_table.shape
        x_specs = [pl.BlockSpec((None, SEG_PER_PAGE, SEG_ROW), partial(lambda k, b, pt: (pt[b, k], 0, 0), k))
                   for k in range(n_x)]
        n_seg = n_x * SEG_PER_PAGE
        cmap = lambda *idx: lambda b, pt: idx
    else:
        n_seq, n_seg, n_x = x.shape[0], x.shape[1], 1
        x_specs = [pl.BlockSpec((None, n_seg, SEG_ROW), lambda b: (b, 0, 0))]
        cmap = lambda *idx: lambda b: idx
    w_specs = [pl.BlockSpec(w1b.shape, cmap(0, 0, 0, 0)), pl.BlockSpec(b1b.shape, cmap(0, 0)),
               pl.BlockSpec(w2b.shape, cmap(0, 0)), pl.BlockSpec(b2b.shape, cmap(0, 0)),
               pl.BlockSpec(gk2.shape, cmap(0, 0))]
    out_spec = pl.BlockSpec((None, n_seg, KV_ROW), (lambda b, pt: (b, 0, 0)) if paged else (lambda b: (b, 0, 0)))
    out_shape = jax.ShapeDtypeStruct((n_seq, n_seg, KV_ROW), BF16)
    kern = partial(_compress_kernel, n_x=n_x, paged=paged)
    if paged:
        gs = pltpu.PrefetchScalarGridSpec(num_scalar_prefetch=1, grid=(n_seq,), in_specs=x_specs + w_specs,
                                          out_specs=out_spec)
        return pl.pallas_call(kern, grid_spec=gs, out_shape=out_shape, compiler_params=_params("parallel"),
                              name="compress_paged")(page_table, *([x] * n_x), w1b, b1b, w2b, b2b, gk2)
    return pl.pallas_call(kern, grid=(n_seq,), in_specs=x_specs + w_specs, out_specs=out_spec, out_shape=out_shape,
                          compiler_params=_params("parallel"), name="compress")(x, w1b, b1b, w2b, b2b, gk2)


def _nsa_prompt_kernel(rel_ref, q_ref, gate_ref, gq_ref, bg_ref, ck_ref, cv_ref, cb_ref, mt_ref,
                       sk_ref, sv_ref, wk_ref, wv_ref, dl_ref, o_ref,
                       qa_ref, m_ref, l_ref, acc_ref, out_ref):
    tq = NSA_TQ
    rows = NSA_GROUP * tq
    g = pl.program_id(1)
    qi = pl.program_id(2)
    lane = lax.broadcasted_iota(jnp.int32, (tq, LANES), 1)
    lo = lane < HEAD_DIM

    qn = []
    for k in range(NSA_GROUP // 2):
        y = _half_rms(q_ref[:, k * LANES:(k + 1) * LANES], gq_ref[...], lo) * HEAD_DIM ** -0.5
        qn.append(jnp.where(lo, y, 0.0))
        qn.append(jnp.where(lo, pltpu.roll(y, HEAD_DIM, 1), 0.0))

    sig = jax.nn.sigmoid(gate_ref[...] + bg_ref[...])

    def gated(br, z, o):
        c = br * NSA_GROUP + z
        return sig[:, c:c + 1] * o

    qc = jnp.concatenate(qn, axis=0).astype(BF16)
    s = _dot_nt(qc, ck_ref[...]) + cb_ref[...].reshape(rows, LANES)
    m = jnp.maximum(jnp.max(s, axis=1, keepdims=True), NEG)
    p = jnp.exp(s - m)
    l = jnp.sum(p, axis=1, keepdims=True)
    pn = p * jnp.where(l > 0.0, 1.0 / l, 0.0)
    o_cmp = jnp.dot(pn.astype(BF16), cv_ref[...], preferred_element_type=F32)
    for z in range(NSA_GROUP):
        out_ref[z * tq:(z + 1) * tq, :] = gated(0, z, o_cmp[z * tq:(z + 1) * tq, :])

    psum = pn[0:tq, :]
    for z in range(1, NSA_GROUP):
        psum = psum + pn[z * tq:(z + 1) * tq, :]
    p_hi = psum.astype(BF16)
    p_lo = (psum - p_hi.astype(F32)).astype(BF16)
    imp = _dot_nt(mt_ref[...], p_hi) + _dot_nt(mt_ref[...], p_lo)
    jidx = lax.broadcasted_iota(jnp.int32, (N_SEL_LANES, tq), 0)
    tpos = qi * tq + lax.broadcasted_iota(jnp.int32, (N_SEL_LANES, tq), 1)
    cur = tpos // SLC_BLOCK
    forced = (jidx == 0) | (jidx == cur) | (jidx == cur - 1)
    score = jnp.where(forced, FORCED, jnp.where(jidx <= cur, imp, -1.0))
    rank = jnp.zeros((N_SEL_LANES, tq), F32)
    for i in range(N_SEL_LANES):
        ri = score[i:i + 1, :]
        beats = (ri > score) | ((ri == score) & (jidx > i))
        rank = rank + jnp.where(beats, 1.0, 0.0)
    selb = jnp.where(rank < N_SELECT, 0.0, MASK)
    selb = jnp.concatenate([selb, jnp.zeros((LANES - N_SEL_LANES, tq), F32)], axis=0)
    sel_cols = pltpu.roll(selb.T, SEL_LANE, 1)

    for z in range(NSA_GROUP):
        c = jnp.full((tq, LANES), rel_ref[REL_BUCKETS - 1, g * NSA_GROUP + z], F32)
        c_hi = c.astype(BF16).astype(F32)
        extra = jnp.where(lane == CONST_LANE, c_hi, jnp.where(lane == CONST_LANE + 1, c - c_hi, sel_cols))
        qa_ref[z * tq:(z + 1) * tq, :] = jnp.where(lo, qn[z], extra).astype(BF16)

    def flash_init():
        m_ref[...] = jnp.full((rows, LANES), NEG, F32)
        l_ref[...] = jnp.zeros((rows, LANES), F32)
        acc_ref[...] = jnp.zeros((rows, LANES), F32)

    def flash_block(k_ref, v_ref, kb, didx):
        start = pl.multiple_of(kb * tq, tq)
        s = _dot_nt(qa_ref[...], k_ref[pl.ds(start, tq), :]) + dl_ref[didx].reshape(rows, LANES)
        m_prev = m_ref[...]
        m_next = jnp.maximum(m_prev, jnp.max(s, axis=1, keepdims=True))
        p = jnp.exp(s - m_next)
        alpha = jnp.exp(m_prev - m_next)
        l_ref[...] = alpha * l_ref[...] + jnp.sum(p, axis=1, keepdims=True)
        acc_ref[...] = alpha * acc_ref[...] + jnp.dot(p.astype(BF16), v_ref[pl.ds(start, tq), :],
                                                      preferred_element_type=F32)
        m_ref[...] = m_next

    def flash_finish(br):
        o = acc_ref[...] / l_ref[...]
        for z in range(NSA_GROUP):
            out_ref[z * tq:(z + 1) * tq, :] += gated(br, z, o[z * tq:(z + 1) * tq, :])

    flash_init()

    def slc_body(kb, carry):
        didx = jnp.where(kb == qi, 2, jnp.where(kb == qi - 1, 1, 0))
        flash_block(sk_ref, sv_ref, kb, didx)
        return carry

    lax.fori_loop(0, qi + 1, slc_body, 0)
    flash_finish(1)

    flash_init()
    n_win = WINDOW // tq

    def win_body(kb, carry):
        d = qi - kb
        didx = jnp.where(d == 0, 2, jnp.where(d == 1, 1, jnp.where(d == n_win, 3, 0)))
        flash_block(wk_ref, wv_ref, kb, didx)
        return carry

    lax.fori_loop(jnp.maximum(qi - n_win, 0), qi + 1, win_body, 0)
    flash_finish(2)

    for k in range(NSA_GROUP // 2):
        even = out_ref[(2 * k) * tq:(2 * k + 1) * tq, :]
        odd = out_ref[(2 * k + 1) * tq:(2 * k + 2) * tq, :]
        o_ref[:, k * LANES:(k + 1) * LANES] = even + pltpu.roll(odd, HEAD_DIM, 1)


VT_ROWS = 80
CMP_BAND = 16


def _nsa_prompt_t_kernel(rel_ref, q_ref, gate_ref, gq_ref, bg_ref, ck_ref, cvt_ref, band_ref, far_ref, mt_ref,
                         sk_ref, svt_ref, wk_ref, wvt_ref, dl_ref, o_ref,
                         qa_ref, m_ref, acc_ref, out_ref, cb_ref, *, n_cmp):
    tq = NSA_TQ
    cols = NSA_GROUP * tq
    g = pl.program_id(1)
    qi = pl.program_id(2)
    lane = lax.broadcasted_iota(jnp.int32, (tq, LANES), 1)
    lo = lane < HEAD_DIM

    qn = []
    for k in range(NSA_GROUP // 2):
        y = _half_rms(q_ref[:, k * LANES:(k + 1) * LANES], gq_ref[...], lo) * HEAD_DIM ** -0.5
        qn.append(jnp.where(lo, y, 0.0))
        qn.append(jnp.where(lo, pltpu.roll(y, HEAD_DIM, 1), 0.0))
    for z in range(NSA_GROUP):
        qa_ref[z * tq:(z + 1) * tq, :] = qn[z].astype(BF16)

    sig_t = jax.nn.sigmoid(gate_ref[...] + bg_ref[...]).T

    def gate_row(br):
        return jnp.concatenate([sig_t[br * NSA_GROUP + z:br * NSA_GROUP + z + 1, :] for z in range(NSA_GROUP)], axis=1)

    r_cb = lax.broadcasted_iota(jnp.int32, (LANES + CMP_BAND, cols), 0)
    cb_ref[...] = jnp.where(r_cb < (CMP_BAND // 2) * qi, far_ref[...], MASK)
    cb_ref[pl.ds(pl.multiple_of((CMP_BAND // 2) * qi, CMP_BAND // 2), CMP_BAND), :] = band_ref[...]
    nidx = lax.broadcasted_iota(jnp.int32, (LANES, cols), 0)
    cbias = jnp.where(nidx >= n_cmp, MASK, cb_ref[CMP_BAND // 2:CMP_BAND // 2 + LANES, :])
    s = _dot_nt(ck_ref[...], qa_ref[...]) + cbias
    m = jnp.maximum(jnp.max(s, axis=0, keepdims=True), NEG)
    p = jnp.exp(s - m)
    l = jnp.sum(p, axis=0, keepdims=True)
    pn = p * jnp.where(l > 0.0, 1.0 / l, 0.0)
    o_cmp = jnp.dot(cvt_ref[...], pn.astype(BF16), preferred_element_type=F32)
    out_ref[...] = gate_row(0) * o_cmp

    psum = pn[:, 0:tq]
    for z in range(1, NSA_GROUP):
        psum = psum + pn[:, z * tq:(z + 1) * tq]
    p_hi = psum.astype(BF16)
    p_lo = (psum - p_hi.astype(F32)).astype(BF16)
    imp = (jnp.dot(mt_ref[...], p_hi, preferred_element_type=F32)
           + jnp.dot(mt_ref[...], p_lo, preferred_element_type=F32))
    jidx = lax.broadcasted_iota(jnp.int32, (N_SEL_LANES, tq), 0)
    tpos = qi * tq + lax.broadcasted_iota(jnp.int32, (N_SEL_LANES, tq), 1)
    cur = tpos // SLC_BLOCK
    forced = (jidx == 0) | (jidx == cur) | (jidx == cur - 1)
    score = jnp.where(forced, FORCED, jnp.where(jidx <= cur, imp, -1.0))
    rank = jnp.zeros((N_SEL_LANES, tq), F32)
    for i in range(N_SEL_LANES):
        ri = score[i:i + 1, :]
        beats = (ri > score) | ((ri == score) & (jidx > i))
        rank = rank + jnp.where(beats, 1.0, 0.0)
    selb = jnp.where(rank < N_SELECT, 0.0, MASK)
    selb = jnp.concatenate([selb, jnp.zeros((LANES - N_SEL_LANES, tq), F32)], axis=0)
    sel_cols = pltpu.roll(selb.T, SEL_LANE, 1)

    for z in range(NSA_GROUP):
        c = jnp.full((tq, LANES), rel_ref[REL_BUCKETS - 1, g * NSA_GROUP + z], F32)
        c_hi = c.astype(BF16).astype(F32)
        extra = jnp.where(lane == CONST_LANE, c_hi, jnp.where(lane == CONST_LANE + 1, c - c_hi, sel_cols))
        qa_ref[z * tq:(z + 1) * tq, :] = jnp.where(lo, qn[z], extra).astype(BF16)

    def flash_init():
        m_ref[...] = jnp.full((1, cols), NEG, F32)
        acc_ref[...] = jnp.zeros((VT_ROWS, cols), F32)

    def flash_block(k_ref, vt_ref, kb, didx):
        start = pl.multiple_of(kb * tq, tq)
        s = _dot_nt(k_ref[pl.ds(start, tq), :], qa_ref[...]) + dl_ref[didx]
        m_prev = m_ref[...]
        m_next = jnp.maximum(m_prev, jnp.max(s, axis=0, keepdims=True))
        p = jnp.exp(s - m_next).astype(BF16)
        acc_ref[...] = (jnp.exp(m_prev - m_next) * acc_ref[...]
                        + jnp.dot(vt_ref[:, pl.ds(start, tq)], p, preferred_element_type=F32))
        m_ref[...] = m_next

    def flash_finish(br):
        out_ref[...] += gate_row(br) * (acc_ref[:HEAD_DIM, :] / acc_ref[HEAD_DIM:HEAD_DIM + 1, :])

    flash_init()

    def slc_body(kb, carry):
        didx = jnp.where(kb == qi, 2, jnp.where(kb == qi - 1, 1, 0))
        flash_block(sk_ref, svt_ref, kb, didx)
        return carry

    lax.fori_loop(0, qi + 1, slc_body, 0)
    flash_finish(1)

    flash_init()
    n_win = WINDOW // tq

    def win_body(kb, carry):
        d = qi - kb
        didx = jnp.where(d == 0, 2, jnp.where(d == 1, 1, jnp.where(d == n_win, 3, 0)))
        flash_block(wk_ref, wvt_ref, kb, didx)
        return carry

    lax.fori_loop(jnp.maximum(qi - n_win, 0), qi + 1, win_body, 0)
    flash_finish(2)

    for k in range(NSA_GROUP // 2):
        o_ref[:, k * LANES:(k + 1) * LANES] = jnp.concatenate(
            [out_ref[:, (2 * k) * tq:(2 * k + 1) * tq], out_ref[:, (2 * k + 1) * tq:(2 * k + 2) * tq]], axis=0).T


LOG2E = 1.0 / math.log(2.0)
FAR_BLOCKS = 4
NEAR_BLOCKS = 5
FULL_MASK_TILE = 4


def _nsa_prompt_kernel2(rel_ref, q_ref, gate_ref, gq_ref, bg_ref, ck_ref, cvt_ref, band_ref, far_ref, mt_ref,
                        ka_ref, vat_ref, dl_ref, o_ref, qa_ref, m_ref, acc_ref, cb_ref, *, n_cmp, n_tiles):
    tq = NSA_TQ
    cols = NSA_GROUP * tq
    qi = pl.program_id(1)
    lane = lax.broadcasted_iota(jnp.int32, (tq, LANES), 1)
    lo = lane < HEAD_DIM
    nidx = lax.broadcasted_iota(jnp.int32, (LANES, cols), 0)
    r_cb = lax.broadcasted_iota(jnp.int32, (LANES + CMP_BAND, cols), 0)
    jidx = lax.broadcasted_iota(jnp.int32, (N_SEL_LANES, tq), 0)
    tpos = qi * tq + lax.broadcasted_iota(jnp.int32, (N_SEL_LANES, tq), 1)
    cur = tpos // SLC_BLOCK
    forced = (jidx == 0) | (jidx == cur) | (jidx == cur - 1)
    half_band = CMP_BAND // 2

    def gate_rows(g):
        sig_t = jax.nn.sigmoid(gate_ref[:, g * LANES:(g + 1) * LANES] + bg_ref[:, g * LANES:(g + 1) * LANES]).T
        return [jnp.concatenate([sig_t[br * NSA_GROUP + z:br * NSA_GROUP + z + 1, :] for z in range(NSA_GROUP)], axis=1)
                for br in range(3)]

    def prologue(g):
        qn = []
        for k in range(NSA_GROUP // 2):
            kk = g * (NSA_GROUP // 2) + k
            y = _half_rms(q_ref[:, kk * LANES:(kk + 1) * LANES], gq_ref[...], lo) * (HEAD_DIM ** -0.5 * LOG2E)
            qn.append(jnp.where(lo, y, 0.0))
            qn.append(jnp.where(lo, pltpu.roll(y, HEAD_DIM, 1), 0.0))
        qc = jnp.concatenate(qn, axis=0).astype(BF16)

        cb_ref[g] = jnp.where(r_cb < half_band * qi, far_ref[g], MASK)
        cb_ref[g, pl.ds(pl.multiple_of(half_band * qi, half_band), CMP_BAND), :] = band_ref[g]
        cbias = jnp.where(nidx >= n_cmp, MASK, cb_ref[g, half_band:half_band + LANES, :])
        s = _dot_nt(ck_ref[g], qc) + cbias
        m = jnp.maximum(jnp.max(s, axis=0, keepdims=True), NEG)
        p = jnp.exp2(s - m)
        l = jnp.sum(p, axis=0, keepdims=True)
        pn = p * jnp.where(l > 0.0, 1.0 / l, 0.0)
        o_cmp = jnp.dot(cvt_ref[g], pn.astype(BF16), preferred_element_type=F32)

        psum = pn[:, 0:tq]
        for z in range(1, NSA_GROUP):
            psum = psum + pn[:, z * tq:(z + 1) * tq]
        p_hi = psum.astype(BF16)
        p_lo = (psum - p_hi.astype(F32)).astype(BF16)
        imp = (jnp.dot(mt_ref[...], p_hi, preferred_element_type=F32)
               + jnp.dot(mt_ref[...], p_lo, preferred_element_type=F32))
        score = jnp.where(forced, FORCED, jnp.where(jidx <= cur, imp, -1.0))
        rank = jnp.zeros((N_SEL_LANES, tq), F32)
        for i in range(N_SEL_LANES):
            ri = score[i:i + 1, :]
            beats = (ri > score) | ((ri == score) & (jidx > i))
            rank = rank + jnp.where(beats, 1.0, 0.0)
        selb = jnp.where((rank < N_SELECT) & (jidx <= cur), 0.0, MASK)
        selb = jnp.concatenate([selb, jnp.zeros((LANES - N_SEL_LANES, tq), F32)], axis=0)
        sel_cols = pltpu.roll(selb.T, SEL_LANE, 1)

        for z in range(NSA_GROUP):
            c = jnp.full((tq, LANES), rel_ref[REL_BUCKETS - 1, g * NSA_GROUP + z], F32)
            c_hi = c.astype(BF16).astype(F32)
            extra = jnp.where(lane == CONST_LANE, c_hi, jnp.where(lane == CONST_LANE + 1, c - c_hi, sel_cols))
            qa_ref[g, z * tq:(z + 1) * tq, :] = jnp.where(lo, qn[z], extra).astype(BF16)
        return o_cmp

    o_cmp = [prologue(g) for g in range(NSA_KV_HEADS)]

    n_far = jnp.maximum(qi - 1, 0) // FAR_BLOCKS
    far_rows = FAR_BLOCKS * tq
    m_ref[...] = jnp.full(m_ref.shape, NEG, F32)
    acc_ref[...] = jnp.zeros(acc_ref.shape, F32)

    def far_body(c, carry):
        start = pl.multiple_of(c * far_rows, far_rows)
        for g in range(NSA_KV_HEADS):
            s = _dot_nt(ka_ref[g, pl.ds(start, far_rows), :], qa_ref[g])
            m_prev = m_ref[g]
            m_next = jnp.maximum(m_prev, jnp.max(s, axis=0, keepdims=True))
            p = jnp.exp2(s - m_next).astype(BF16)
            acc_ref[g] = (jnp.exp2(m_prev - m_next) * acc_ref[g]
                          + jnp.dot(vat_ref[g, :, pl.ds(start, far_rows)], p, preferred_element_type=F32))
            m_ref[g] = m_next
        return carry

    lax.fori_loop(0, n_far, far_body, 0)

    def near_stage(idx, g, start_blk, tile_of, m_prev, acc_prev):
        start = pl.multiple_of(start_blk * tq, tq)
        s = _dot_nt(ka_ref[idx, pl.ds(start, NEAR_BLOCKS * tq), :], qa_ref[g])
        s = s + jnp.concatenate([dl_ref[tile_of(start_blk + i), g] for i in range(NEAR_BLOCKS)], axis=0)
        m_next = jnp.max(s, axis=0, keepdims=True)
        if m_prev is not None:
            m_next = jnp.maximum(m_prev, m_next)
        acc = jnp.dot(vat_ref[idx, :, pl.ds(start, NEAR_BLOCKS * tq)], jnp.exp2(s - m_next).astype(BF16),
                      preferred_element_type=F32)
        if acc_prev is not None:
            acc = acc + jnp.exp2(m_prev - m_next) * acc_prev
        return acc[:HEAD_DIM, :] / acc[HEAD_DIM:HEAD_DIM + 1, :]

    far_cov = FAR_BLOCKS * n_far
    slc_start = jnp.minimum(far_cov, n_tiles - NEAR_BLOCKS)
    win_start = jnp.maximum(qi - WINDOW // tq, 0)

    def slc_tile(j):
        d = qi - j
        return jnp.where((d < 0) | (j < far_cov), FULL_MASK_TILE, jnp.where(d == 0, 2, jnp.where(d == 1, 1, 0)))

    def win_tile(j):
        d = qi - j
        return jnp.where(d < 0, FULL_MASK_TILE,
                         jnp.where(d == 0, 2, jnp.where(d == 1, 1, jnp.where(d == WINDOW // tq, 3, 0))))

    outs = []
    for g in range(NSA_KV_HEADS):
        gr = gate_rows(g)
        o_slc = near_stage(g, g, slc_start, slc_tile, m_ref[g], acc_ref[g])
        o_win = near_stage(NSA_KV_HEADS + g, g, win_start, win_tile, None, None)
        outs.append(gr[0] * o_cmp[g] + gr[1] * o_slc + gr[2] * o_win)

    for g in range(NSA_KV_HEADS):
        for k in range(NSA_GROUP // 2):
            kk = g * (NSA_GROUP // 2) + k
            o_ref[:, kk * LANES:(kk + 1) * LANES] = jnp.concatenate(
                [outs[g][:, (2 * k) * tq:(2 * k + 1) * tq], outs[g][:, (2 * k + 1) * tq:(2 * k + 2) * tq]], axis=0).T


def nsa_prompt2(p, b, s, g_q, b_gate, rel_table, ck, cvt, ka, vat, tables):
    tq = NSA_TQ
    nq = s // tq
    cols = NSA_GROUP * tq
    n_cmp = (s - CMP_LEN) // CMP_STRIDE + 1
    n_slc = -(-s // SLC_BLOCK)
    assert s % tq == 0 and nq >= NEAR_BLOCKS and n_cmp <= LANES and n_slc <= N_SEL_LANES and tq == 8 * CMP_STRIDE
    tiles, band, far_row = tables
    tiles = jnp.concatenate([tiles, jnp.full((1,) + tiles.shape[1:], MASK, F32)], axis=0)
    mt = np.zeros((N_SEL_LANES, LANES), np.float32)
    mt[:n_slc, :n_cmp] = _overlap_matrix(n_cmp, n_slc).T
    bg = _gate_bias_slabs(b_gate).reshape(1, NSA_KV_HEADS * LANES)
    gcol = (MIX_W + MEM_W) // (NSA_KV_HEADS * LANES)
    ka4 = ka.reshape(2 * NSA_KV_HEADS, b, s, LANES)
    c2 = lambda bi, qi: (0, 0)
    c3 = lambda bi, qi: (0, 0, 0)
    return pl.pallas_call(
        partial(_nsa_prompt_kernel2, n_cmp=n_cmp, n_tiles=nq),
        grid=(b, nq),
        in_specs=[pl.BlockSpec(memory_space=pltpu.SMEM),
                  pl.BlockSpec((tq, MIX_W), lambda bi, qi: (bi * nq + qi, 0)),
                  pl.BlockSpec((tq, NSA_KV_HEADS * LANES), lambda bi, qi: (bi * nq + qi, gcol)),
                  pl.BlockSpec((1, LANES), c2),
                  pl.BlockSpec((1, NSA_KV_HEADS * LANES), c2),
                  pl.BlockSpec((None, NSA_KV_HEADS, LANES, LANES), lambda bi, qi: (bi, 0, 0, 0)),
                  pl.BlockSpec((None, NSA_KV_HEADS, HEAD_DIM, LANES), lambda bi, qi: (bi, 0, 0, 0)),
                  pl.BlockSpec((NSA_KV_HEADS, CMP_BAND, cols), c3),
                  pl.BlockSpec((NSA_KV_HEADS, 1, cols), c3),
                  pl.BlockSpec((N_SEL_LANES, LANES), c2),
                  pl.BlockSpec((2 * NSA_KV_HEADS, None, s, LANES), lambda bi, qi: (0, bi, 0, 0)),
                  pl.BlockSpec((2 * NSA_KV_HEADS, VT_ROWS, s), lambda bi, qi: (0, 0, bi)),
                  pl.BlockSpec((FULL_MASK_TILE + 1, NSA_KV_HEADS, tq, cols), lambda bi, qi: (0, 0, 0, 0))],
        out_specs=pl.BlockSpec((tq, MIX_W), lambda bi, qi: (bi * nq + qi, 0)),
        out_shape=jax.ShapeDtypeStruct((b * s, MIX_W), F32),
        scratch_shapes=[pltpu.VMEM((NSA_KV_HEADS, cols, LANES), BF16), pltpu.VMEM((NSA_KV_HEADS, 1, cols), F32),
                        pltpu.VMEM((NSA_KV_HEADS, VT_ROWS, cols), F32),
                        pltpu.VMEM((NSA_KV_HEADS, LANES + CMP_BAND, cols), F32)],
        compiler_params=_params("parallel", "arbitrary"),
        name="nsa_prompt",
    )(rel_table.astype(F32), p, p, _gain2(g_q), bg, ck, cvt, band, far_row, jnp.asarray(mt, BF16), ka4, vat, tiles)


def nsa_bias_tables(rel_table):
    tq = NSA_TQ
    table = rel_table.astype(F32)
    far = table[REL_BUCKETS - 1]
    r = np.arange(tq)[None, :]
    c = np.arange(tq)[:, None]

    def by_group(t):
        k = t.shape[0]
        return jnp.transpose(t.reshape(k, tq, NSA_KV_HEADS, NSA_GROUP), (2, 0, 3, 1)).reshape(NSA_KV_HEADS, k, -1)

    def tile(off, masked):
        d = off + np.arange(-(tq - 1), tq)
        v = jnp.where(jnp.asarray(masked(d))[:, None], MASK, table[_rel_bucket_np(d)] - far)
        return by_group(jnp.stack([v[tq - 1 - k:2 * tq - 1 - k] for k in range(tq)]))

    zero = jnp.zeros((NSA_KV_HEADS, tq, NSA_GROUP * tq), F32)
    tiles = jnp.stack([zero, tile(tq, lambda d: d < 0), tile(0, lambda d: d < 0),
                       tile(WINDOW, lambda d: d >= WINDOW)])
    n_rel = np.arange(CMP_BAND)[:, None] - CMP_BAND // 2
    dist = r - CMP_STRIDE * n_rel - (CMP_LEN - 1)
    band = by_group(jnp.where(jnp.asarray(dist < 0)[..., None], MASK, table[_rel_bucket_np(dist)]))
    far_row = by_group(jnp.broadcast_to(far, (1, tq, NSA_HEADS)))
    return tiles, band, far_row


def nsa_prompt_t(p, b, s, g_q, b_gate, rel_table, ck, cvt, ka, vat, tables):
    tq = NSA_TQ
    nq = s // tq
    cols = NSA_GROUP * tq
    n_cmp = (s - CMP_LEN) // CMP_STRIDE + 1
    n_slc = -(-s // SLC_BLOCK)
    assert s % tq == 0 and n_cmp <= LANES and n_slc <= N_SEL_LANES and tq == 8 * CMP_STRIDE
    tiles, band, far_row = tables
    mt = np.zeros((N_SEL_LANES, LANES), np.float32)
    mt[:n_slc, :n_cmp] = _overlap_matrix(n_cmp, n_slc).T
    bg = _gate_bias_slabs(b_gate).reshape(NSA_KV_HEADS, 1, LANES)
    qcol = (MIX_W + MEM_W) // LANES
    ka4 = ka.reshape(2 * NSA_KV_HEADS, b, s, LANES)
    kmap = lambda off: (lambda bi, g, qi: (off + g, bi, 0, 0))
    vmap = lambda off: (lambda bi, g, qi: (off + g, 0, bi))
    gmap = lambda bi, g, qi: (g, 0, 0)
    return pl.pallas_call(
        partial(_nsa_prompt_t_kernel, n_cmp=n_cmp),
        grid=(b, NSA_KV_HEADS, nq),
        in_specs=[pl.BlockSpec(memory_space=pltpu.SMEM),
                  pl.BlockSpec((tq, NSA_GROUP * HEAD_DIM), lambda bi, g, qi: (bi * nq + qi, g)),
                  pl.BlockSpec((tq, LANES), lambda bi, g, qi: (bi * nq + qi, qcol + g)),
                  pl.BlockSpec((1, LANES), lambda bi, g, qi: (0, 0)),
                  pl.BlockSpec((None, 1, LANES), gmap),
                  pl.BlockSpec((None, None, LANES, LANES), lambda bi, g, qi: (bi, g, 0, 0)),
                  pl.BlockSpec((None, None, HEAD_DIM, LANES), lambda bi, g, qi: (bi, g, 0, 0)),
                  pl.BlockSpec((None, CMP_BAND, cols), gmap),
                  pl.BlockSpec((None, 1, cols), gmap),
                  pl.BlockSpec((N_SEL_LANES, LANES), lambda bi, g, qi: (0, 0)),
                  pl.BlockSpec((None, None, s, LANES), kmap(0)),
                  pl.BlockSpec((None, VT_ROWS, s), vmap(0)),
                  pl.BlockSpec((None, None, s, LANES), kmap(NSA_KV_HEADS)),
                  pl.BlockSpec((None, VT_ROWS, s), vmap(NSA_KV_HEADS)),
                  pl.BlockSpec((4, None, tq, cols), lambda bi, g, qi: (0, g, 0, 0))],
        out_specs=pl.BlockSpec((tq, NSA_GROUP * HEAD_DIM), lambda bi, g, qi: (bi * nq + qi, g)),
        out_shape=jax.ShapeDtypeStruct((b * s, MIX_W), F32),
        scratch_shapes=[pltpu.VMEM((cols, LANES), BF16), pltpu.VMEM((1, cols), F32), pltpu.VMEM((VT_ROWS, cols), F32),
                        pltpu.VMEM((HEAD_DIM, cols), F32), pltpu.VMEM((LANES + CMP_BAND, cols), F32)],
        compiler_params=_params("parallel", "parallel", "arbitrary"),
        name="nsa_prompt",
    )(rel_table.astype(F32), p, p, _gain2(g_q), bg, ck, cvt, band, far_row, jnp.asarray(mt, BF16),
      ka4, vat, ka4, vat, tiles)


def nsa_bias_tiles(rel_table, s):
    tq = NSA_TQ
    table = rel_table.astype(F32)
    far = table[REL_BUCKETS - 1]
    r = np.arange(tq)[:, None]
    c = np.arange(tq)[None, :]

    def tile(off, masked):
        dist = off + r - c
        t = jnp.transpose(table[_rel_bucket_np(dist)], (2, 0, 1)) - far[:, None, None]
        return jnp.where(jnp.asarray(masked(dist))[None], MASK, t)

    zero = jnp.zeros((NSA_HEADS, tq, tq), F32)
    d128 = tile(tq, lambda d: d < 0)
    d0 = tile(0, lambda d: d < 0)
    dwin = tile(WINDOW, lambda d: d >= WINDOW)
    tiles = jnp.stack([zero, d128, d0, dwin])
    n_cmp = (s - CMP_LEN) // CMP_STRIDE + 1
    c_end = np.arange(LANES) * CMP_STRIDE + CMP_LEN - 1
    dist = np.arange(s)[:, None] - c_end[None, :]
    cb = jnp.transpose(table[_rel_bucket_np(dist)], (2, 0, 1))
    bad = (dist < 0) | (np.arange(LANES)[None, :] >= n_cmp)
    cb = jnp.where(jnp.asarray(bad)[None], MASK, cb)
    return tiles, cb


def _gate_bias_slabs(b_gate):
    bg = b_gate.astype(F32).reshape(3, NSA_KV_HEADS, NSA_GROUP).transpose(1, 0, 2).reshape(NSA_KV_HEADS, 3 * NSA_GROUP)
    return _pad_cols(bg, LANES)


def nsa_prompt(p, b, s, g_q, b_gate, rel_table, ck, cv, ka, va, tiles, cb):
    tq = NSA_TQ
    nq = s // tq
    rows = NSA_GROUP * tq
    n_cmp = (s - CMP_LEN) // CMP_STRIDE + 1
    n_slc = -(-s // SLC_BLOCK)
    assert s % tq == 0 and n_cmp <= LANES and n_slc <= N_SEL_LANES
    mt = np.zeros((N_SEL_LANES, LANES), np.float32)
    mt[:n_slc, :n_cmp] = _overlap_matrix(n_cmp, n_slc).T
    bg = _gate_bias_slabs(b_gate).reshape(NSA_KV_HEADS, 1, LANES)
    qcol = (MIX_W + MEM_W) // LANES
    kmap = lambda off: (lambda bi, g, qi: (off + g, bi, 0, 0))
    return pl.pallas_call(
        _nsa_prompt_kernel,
        grid=(b, NSA_KV_HEADS, nq),
        in_specs=[pl.BlockSpec(memory_space=pltpu.SMEM),
                  pl.BlockSpec((tq, NSA_GROUP * HEAD_DIM), lambda bi, g, qi: (bi * nq + qi, g)),
                  pl.BlockSpec((tq, LANES), lambda bi, g, qi: (bi * nq + qi, qcol + g)),
                  pl.BlockSpec((1, LANES), lambda bi, g, qi: (0, 0)),
                  pl.BlockSpec((None, 1, LANES), lambda bi, g, qi: (g, 0, 0)),
                  pl.BlockSpec((None, None, LANES, LANES), lambda bi, g, qi: (bi, g, 0, 0)),
                  pl.BlockSpec((None, None, LANES, LANES), lambda bi, g, qi: (bi, g, 0, 0)),
                  pl.BlockSpec((NSA_GROUP, tq, LANES), lambda bi, g, qi: (g, qi, 0)),
                  pl.BlockSpec((N_SEL_LANES, LANES), lambda bi, g, qi: (0, 0)),
                  pl.BlockSpec((None, None, s, LANES), kmap(0)),
                  pl.BlockSpec((None, None, s, LANES), kmap(0)),
                  pl.BlockSpec((None, None, s, LANES), kmap(NSA_KV_HEADS)),
                  pl.BlockSpec((None, None, s, LANES), kmap(NSA_KV_HEADS)),
                  pl.BlockSpec((4, NSA_GROUP, tq, tq), lambda bi, g, qi: (0, g, 0, 0))],
        out_specs=pl.BlockSpec((tq, NSA_GROUP * HEAD_DIM), lambda bi, g, qi: (bi * nq + qi, g)),
        out_shape=jax.ShapeDtypeStruct((b * s, MIX_W), F32),
        scratch_shapes=[pltpu.VMEM((rows, LANES), BF16)] + [pltpu.VMEM((rows, LANES), F32)] * 4,
        compiler_params=_params("parallel", "parallel", "arbitrary"),
        name="nsa_prompt",
    )(rel_table.astype(F32), p, p, _gain2(g_q), bg, ck, cv, cb, jnp.asarray(mt, BF16),
      ka.reshape(4, b, s, LANES), va.reshape(4, b, s, LANES),
      ka.reshape(4, b, s, LANES), va.reshape(4, b, s, LANES), tiles)


def split_cmp_ctx(ctx):
    b, n, _ = ctx.shape
    parts = ctx.reshape(b, n, 2, NSA_KV_HEADS, HEAD_DIM).transpose(2, 0, 3, 1, 4)
    parts = jnp.pad(parts, ((0, 0), (0, 0), (0, 0), (0, LANES - n), (0, 0)))
    ck = jnp.pad(parts[0], ((0, 0), (0, 0), (0, 0), (0, LANES - HEAD_DIM)))
    return ck, jnp.swapaxes(parts[1], -1, -2)


def _softmax_parts(s, s_new):
    m = jnp.maximum(jnp.max(s, axis=1, keepdims=True), s_new)
    p = jnp.exp(s - m)
    p_new = jnp.exp(s_new - m)
    inv = 1.0 / (jnp.sum(p, axis=1, keepdims=True) + p_new)
    return p * inv, p_new * inv


DECODE_SEQS = 2


def _nsa_decode_kernel(*refs, n_pages):
    (pt_ref, p_ref, gq_ref, bg_ref, cmp_ref, cb_ref, mt_ref) = refs[:7]
    n_in = DECODE_SEQS * n_pages
    page_refs = refs[7:7 + n_in]
    (snew_ref, sb_ref, e_ref, win_ref, wnew_ref, wb_ref, b0_ref, o_ref) = refs[7 + n_in:]
    del pt_ref
    for s in range(DECODE_SEQS):
        _nsa_decode_seq(p_ref.at[s], gq_ref, bg_ref, cmp_ref.at[s], cb_ref, mt_ref,
                        page_refs[s * n_pages:(s + 1) * n_pages], snew_ref.at[s], sb_ref, e_ref, win_ref.at[s],
                        wnew_ref.at[s], wb_ref, b0_ref, o_ref.at[s])


def _nsa_decode_seq(p_ref, gq_ref, bg_ref, cmp_ref, cb_ref, mt_ref, page_refs, snew_ref, sb_ref, e_ref, win_ref,
                    wnew_ref, wb_ref, b0_ref, o_ref):
    n_pages = len(page_refs)
    n_keys = n_pages * PAGE_SIZE
    lo1 = lax.broadcasted_iota(jnp.int32, (1, LANES), 1) < HEAD_DIM
    rowi = lax.broadcasted_iota(jnp.int32, (QROWS, LANES), 0)
    lanei = lax.broadcasted_iota(jnp.int32, (QROWS, LANES), 1)
    row_g = rowi // 8
    own = (lanei // HEAD_DIM) == row_g

    qbd = jnp.zeros((QROWS, LANES), F32)
    for k in range(NSA_HEADS // 2):
        y = _half_rms(p_ref[:, k * LANES:(k + 1) * LANES], gq_ref[...], lo1) * HEAD_DIM ** -0.5
        yr = pltpu.roll(y, HEAD_DIM, 1)
        for half in range(2):
            g, z = divmod(2 * k + half, NSA_GROUP)
            qbd = jnp.where(rowi == 8 * g + z, y if half == g else yr, qbd)
    qbd = jnp.where(own, qbd, 0.0)
    qb = qbd.astype(BF16)

    s = _dot_nt(qb, cmp_ref[:, :LANES]) + cb_ref[...]
    m = jnp.maximum(jnp.max(s, axis=1, keepdims=True), NEG)
    p = jnp.exp(s - m)
    l = jnp.sum(p, axis=1, keepdims=True)
    pn = p * jnp.where(l > 0.0, 1.0 / l, 0.0)
    o_cmp = jnp.dot(pn.astype(BF16), cmp_ref[:, LANES:], preferred_element_type=F32)

    valid_row = (rowi % 8) < NSA_GROUP
    ps0 = jnp.sum(jnp.where(valid_row & (row_g == 0), pn, 0.0), axis=0, keepdims=True)
    ps1 = jnp.sum(jnp.where(valid_row & (row_g == 1), pn, 0.0), axis=0, keepdims=True)
    r8 = lax.broadcasted_iota(jnp.int32, (8, LANES), 0)
    j8 = lax.broadcasted_iota(jnp.int32, (8, LANES), 1)
    psum = jnp.where(r8 == 0, ps0, jnp.where(r8 == 1, ps1, 0.0))
    p_hi = psum.astype(BF16)
    p_lo = (psum - p_hi.astype(F32)).astype(BF16)
    imp = (jnp.dot(p_hi, mt_ref[...], preferred_element_type=F32)
           + jnp.dot(p_lo, mt_ref[...], preferred_element_type=F32))
    cur = n_keys // SLC_BLOCK
    forced = (j8 == 0) | (j8 == cur) | (j8 == cur - 1)
    score = jnp.where(forced, FORCED, jnp.where(j8 <= cur, imp, -1.0))
    rank = jnp.zeros((8, LANES), F32)
    for i in range(cur + 1):
        ri = score[:, i:i + 1]
        beats = (ri > score) | ((ri == score) & (j8 > i))
        rank = rank + jnp.where(beats, 1.0, 0.0)
    selb = jnp.where(rank < N_SELECT, 0.0, MASK)
    sel16 = jnp.where(row_g == 0, selb[0:1, :], selb[1:2, :]).astype(BF16)
    blockmask = jnp.dot(sel16, e_ref[...], preferred_element_type=F32)

    s = jnp.concatenate([_dot_nt(qb, r[:, :LANES].astype(BF16)) for r in page_refs], axis=1)
    s = s + sb_ref[...] + blockmask
    s_new = jnp.sum(qbd * snew_ref[:, :LANES], axis=1, keepdims=True) + b0_ref[...]
    pp, p_new = _softmax_parts(s, s_new)
    o_slc = p_new * snew_ref[:, LANES:]
    for k, r in enumerate(page_refs):
        o_slc = o_slc + jnp.dot(pp[:, k * PAGE_SIZE:(k + 1) * PAGE_SIZE].astype(BF16), r[:, LANES:].astype(BF16),
                                preferred_element_type=F32)

    n_wt = win_ref.shape[0] // LANES
    s = jnp.concatenate([_dot_nt(qb, win_ref[k * LANES:(k + 1) * LANES, :LANES].astype(BF16)) for k in range(n_wt)],
                        axis=1) + wb_ref[...]
    s_new = jnp.sum(qbd * wnew_ref[:, :LANES], axis=1, keepdims=True) + b0_ref[...]
    pp, p_new = _softmax_parts(s, s_new)
    o_win = p_new * wnew_ref[:, LANES:]
    for k in range(n_wt):
        o_win = o_win + jnp.dot(pp[:, k * LANES:(k + 1) * LANES].astype(BF16),
                                win_ref[k * LANES:(k + 1) * LANES, LANES:].astype(BF16), preferred_element_type=F32)

    gbase = MIX_W + MEM_W
    sig = jax.nn.sigmoid(p_ref[:, gbase:gbase + NSA_KV_HEADS * LANES] + bg_ref[...])
    gate = [jnp.zeros((QROWS, 1), F32) for _ in range(3)]
    rcol = lax.broadcasted_iota(jnp.int32, (QROWS, 1), 0)
    for br in range(3):
        for g in range(NSA_KV_HEADS):
            for z in range(NSA_GROUP):
                c = g * LANES + br * NSA_GROUP + z
                gate[br] = jnp.where(rcol == 8 * g + z, sig[:, c:c + 1], gate[br])
    comb = jnp.where(own, gate[0] * o_cmp + gate[1] * o_slc + gate[2] * o_win, 0.0)
    comb_r = pltpu.roll(comb, HEAD_DIM, 1)
    for k in range(NSA_HEADS // 2):
        acc = jnp.zeros((1, LANES), F32)
        for half in range(2):
            g, z = divmod(2 * k + half, NSA_GROUP)
            src = comb if half == g else comb_r
            acc = acc + src[8 * g + z:8 * g + z + 1, :]
        o_ref[:, k * LANES:(k + 1) * LANES] = acc


def nsa_decode_bias(rel_table, past, wb):
    table = rel_table.astype(F32)

    def rows16(t):
        z = jnp.zeros((2, t.shape[1]), F32)
        return jnp.concatenate([t[:NSA_GROUP], z, t[NSA_GROUP:], z], axis=0)

    n_cmp = (past + 1 - CMP_LEN) // CMP_STRIDE + 1
    c_end = np.arange(LANES) * CMP_STRIDE + CMP_LEN - 1
    dist = past - c_end
    cb = jnp.where(jnp.asarray((dist < 0) | (np.arange(LANES) >= n_cmp))[None], MASK, table[_rel_bucket_np(dist)].T)
    sb = table[_rel_bucket_np(past - np.arange(past))].T
    wpos = past - wb + np.arange(wb)
    wbias = jnp.where(jnp.asarray(past - wpos >= WINDOW)[None], MASK, table[_rel_bucket_np(past - wpos)].T)
    b0 = table[0][:, None]
    return rows16(cb), rows16(sb), rows16(wbias), rows16(b0)


def nsa_decode(p, g_q, b_gate, cmp_ctx, page_table, cache_slc, slc_new, cache_win, win_new, biases):
    b, n_pages = page_table.shape
    past = n_pages * PAGE_SIZE
    wb = cache_win.shape[1]
    cb, sb, wbias, b0 = biases
    n_cmp = (past + 1 - CMP_LEN) // CMP_STRIDE + 1
    n_slc = -(-(past + 1) // SLC_BLOCK)
    assert n_slc <= LANES and n_cmp <= LANES and wb % LANES == 0
    mt = np.zeros((LANES, LANES), np.float32)
    mt[:n_cmp, :n_slc] = _overlap_matrix(n_cmp, n_slc)
    e = (np.arange(past)[None, :] // SLC_BLOCK == np.arange(LANES)[:, None]).astype(np.float32)
    bg = _gate_bias_slabs(b_gate).reshape(1, NSA_KV_HEADS * LANES)
    ns = DECODE_SEQS
    assert b % ns == 0
    c2 = lambda bi, pt: (0, 0)
    seq3 = lambda bi, pt: (bi, 0, 0)
    in_specs = ([pl.BlockSpec((ns, 1, p.shape[1]), seq3),
                 pl.BlockSpec((1, LANES), c2),
                 pl.BlockSpec((1, NSA_KV_HEADS * LANES), c2),
                 pl.BlockSpec((ns, LANES, KV_ROW), seq3),
                 pl.BlockSpec((QROWS, LANES), c2),
                 pl.BlockSpec((LANES, LANES), c2)]
                + [pl.BlockSpec((None, PAGE_SIZE, KV_ROW), partial(lambda s, k, bi, pt: (pt[ns * bi + s, k], 0, 0), s, k))
                   for s in range(ns) for k in range(n_pages)]
                + [pl.BlockSpec((ns, 1, KV_ROW), seq3),
                   pl.BlockSpec((QROWS, past), c2),
                   pl.BlockSpec((LANES, past), c2),
                   pl.BlockSpec((ns, wb, KV_ROW), seq3),
                   pl.BlockSpec((ns, 1, KV_ROW), seq3),
                   pl.BlockSpec((QROWS, wb), c2),
                   pl.BlockSpec((QROWS, 1), c2)])
    gs = pltpu.PrefetchScalarGridSpec(num_scalar_prefetch=1, grid=(b // ns,), in_specs=in_specs,
                                      out_specs=pl.BlockSpec((ns, 1, MIX_W), seq3))
    out = pl.pallas_call(
        partial(_nsa_decode_kernel, n_pages=n_pages),
        grid_spec=gs,
        out_shape=jax.ShapeDtypeStruct((b, 1, MIX_W), F32),
        compiler_params=_params("parallel"),
        name="nsa_decode",
    )(page_table, p.reshape(b, 1, -1), _gain2(g_q), bg, cmp_ctx, cb, jnp.asarray(mt, BF16),
      *([cache_slc] * (ns * n_pages)), slc_new.reshape(b, 1, KV_ROW), sb, jnp.asarray(e, BF16),
      cache_win, win_new.reshape(b, 1, KV_ROW), wbias, b0)
    return out.reshape(b, MIX_W)


GATE_PAD = LANES


def _pad_heads_cols(w):
    d = w.shape[0]
    return jnp.pad(w.reshape(d, MLSTM_HEADS, MLSTM_DH), ((0, 0), (0, 0), (0, MLSTM_PAD - MLSTM_DH))).reshape(d, MLSTM_HW)


def prep_weights(w_in_a, w_in_b, w_kv, w_mem_kv, w_out, w_gu, w_down):
    w = {}
    n_gate_a = 2 * MLSTM_HEADS
    n_gate_b = 3 * NSA_HEADS
    w['in_a'] = []
    for l in range(N_A_LAYERS):
        wa = w_in_a[l]
        qkvo = [_pad_heads_cols(wa[:, j * MIX_W:(j + 1) * MIX_W]) for j in range(4)]
        w['in_a'].append(jnp.concatenate(qkvo + [wa[:, 4 * MIX_W + n_gate_a:],
                                                 _pad_cols(wa[:, 4 * MIX_W:4 * MIX_W + n_gate_a], GATE_PAD)],
                                         axis=1).astype(BF16))
    w['in_b'] = []
    for j in range(N_B_LAYERS):
        wg = w_in_b[j][:, MIX_W:MIX_W + n_gate_b].reshape(D_MODEL, 3, NSA_KV_HEADS, NSA_GROUP)
        slabs = [_pad_cols(wg[:, :, g].reshape(D_MODEL, 3 * NSA_GROUP), GATE_PAD) for g in range(NSA_KV_HEADS)]
        w['in_b'].append(jnp.concatenate([w_in_b[j][:, :MIX_W], w_in_b[j][:, MIX_W + n_gate_b:]] + slabs,
                                         axis=1).astype(BF16))
    w['out_mix'] = []
    for l in range(DEPTH):
        wm = w_out[l][:MIX_W]
        if l < N_A_LAYERS:
            wm = jnp.pad(wm.reshape(MLSTM_HEADS, MLSTM_DH, D_MODEL),
                         ((0, 0), (0, MLSTM_PAD - MLSTM_DH), (0, 0))).reshape(MLSTM_HW, D_MODEL)
        w['out_mix'].append(wm.astype(BF16))
    w['out_mem'] = w_out[:, MIX_W:].astype(BF16)
    w['kv'] = w_kv.astype(BF16)
    w['mem_kv'] = w_mem_kv.astype(BF16)
    w['gu'] = w_gu.astype(BF16)
    w['down'] = w_down.astype(BF16)
    return w


def kernel(x_prompt, x_sample, state_mlstm_c, state_mlstm_n, state_mlstm_m, cache_mem_kv,
           cache_cmp_kv, cache_slc_kv, cache_win_kv, page_table, mem_prompt,
           g_mix, w_in_a, b_if, g_hnorm, w_in_b, g_q, b_gate, rel_table, g_kv, w_kv,
           g_k_slc, g_k_win, g_k_cmp, cmp_pos, cmp_w1, cmp_b1, cmp_w2, cmp_b2,
           w_mem_kv, g_mem_k, g_mem_q, w_out, g_ffn, w_gu, w_down):
    w = prep_weights(w_in_a, w_in_b, w_kv, w_mem_kv, w_out, w_gu, w_down)
    cw = compress_weights(cmp_w1, cmp_b1, cmp_w2, cmp_b2, cmp_pos, g_k_cmp)
    qcol_a = 4 * MLSTM_HW // MEM_W
    qcol_b = MIX_W // MEM_W

    def finish_layer(l, x, mix, mem_o):
        x = out_proj(mix, w['out_mix'][l], mem_o, w['out_mem'][l], x)
        return ffn(x, g_ffn[l], w['gu'][l], w['down'][l])

    b_p, s_p, _ = x_prompt.shape
    m_tok = mem_prompt.shape[1]
    mem_kv_p = mem_kv_rows(mem_prompt.reshape(b_p * m_tok, D_MODEL), w['mem_kv'], g_mem_k)
    mem_kv_p = mem_kv_p.reshape(DEPTH, b_p, m_tok, 2 * MEM_W)
    rel2 = rel_table.astype(F32) * LOG2E
    tables = nsa_bias_tables(rel2)
    c0, m0 = pack_mlstm_state(jnp.zeros((b_p, MLSTM_HEADS, MLSTM_DH, MLSTM_DH), F32),
                              jnp.zeros((b_p, MLSTM_HEADS, MLSTM_DH), F32),
                              jnp.full((b_p, MLSTM_HEADS), NEG, F32))
    x = x_prompt.reshape(b_p * s_p, D_MODEL)
    st_p = []
    for l in range(N_A_LAYERS):
        p = norm_matmul(x, g_mix[l], w['in_a'][l])
        mix, cn, mm = mlstm_prompt(p, b_p, s_p, b_if[l], g_hnorm[l], c0, m0)
        st_p.append(unpack_mlstm_state(cn, mm))
        mem_o = mem_attn_prompt(p, qcol_a, b_p, s_p, g_mem_q[l], mem_kv_p, l)
        x = finish_layer(l, x, mix, mem_o)
    cmp_p, slc_p, win_p, ka, vat = kv_rows(x, g_kv, w['kv'], g_k_slc, g_k_win, s_p, aug=True)
    ck, cvt = split_cmp_ctx(compress_rows(cmp_p.reshape(b_p, s_p // CMP_STRIDE, SEG_ROW), cw))
    for j in range(N_B_LAYERS):
        l = N_A_LAYERS + j
        p = norm_matmul(x, g_mix[l], w['in_b'][j])
        mix = nsa_prompt2(p, b_p, s_p, g_q[j], b_gate[j], rel2, ck, cvt, ka, vat, tables)
        mem_o = mem_attn_prompt(p, qcol_b, b_p, s_p, g_mem_q[l], mem_kv_p, l)
        x = finish_layer(l, x, mix, mem_o)
    y_p = x.reshape(b_p, s_p, D_MODEL)

    b_s, s_s, _ = x_sample.shape
    assert s_s == 1
    n_pages = page_table.shape[1]
    past = n_pages * PAGE_SIZE
    wb = cache_win_kv.shape[1]
    n_pool = cache_cmp_kv.shape[0]
    x = x_sample.reshape(b_s, D_MODEL)
    cache_mem = cache_mem_kv.reshape(DEPTH, b_s, -1, 2 * MEM_W)
    st_s = []
    for l in range(N_A_LAYERS):
        p = norm_matmul(x, g_mix[l], w['in_a'][l])
        mix, c_new, n_new, m_new = mlstm_step(p, b_if[l], g_hnorm[l], state_mlstm_c, state_mlstm_n, state_mlstm_m, l)
        st_s.append((c_new, n_new, m_new))
        mem_o = mem_attn_decode(p, qcol_a, g_mem_q[l], cache_mem, l)
        x = finish_layer(l, x, mix, mem_o)
    cmp_s, slc_s, win_s = kv_rows(x, g_kv, w['kv'], g_k_slc, g_k_win, 1, aug=False)
    assert (past + 1 - CMP_LEN) // CMP_STRIDE + 1 == past // CMP_STRIDE - 1
    ctx_s = compress_rows(cache_cmp_kv.reshape(n_pool, SEG_PER_PAGE, SEG_ROW), cw, page_table)
    biases = nsa_decode_bias(rel_table, past, wb)
    cache_slc = cache_slc_kv.reshape(n_pool, PAGE_SIZE, KV_ROW)
    cache_win = cache_win_kv.reshape(b_s, wb, KV_ROW)
    for j in range(N_B_LAYERS):
        l = N_A_LAYERS + j
        p = norm_matmul(x, g_mix[l], w['in_b'][j])
        mix = nsa_decode(p, g_q[j], b_gate[j], ctx_s, page_table, cache_slc, slc_s, cache_win, win_s, biases)
        mem_o = mem_attn_decode(p, qcol_b, g_mem_q[l], cache_mem, l)
        x = finish_layer(l, x, mix, mem_o)
    y_s = x.reshape(b_s, 1, D_MODEL)

    rows5 = lambda r, b: r.reshape(b, -1, 2, NSA_KV_HEADS, HEAD_DIM)
    win_p5 = rows5(win_p, b_p)
    p_win = win_p5[:, s_p - min(WINDOW, s_p):]
    s_win = jnp.concatenate([cache_win_kv, rows5(win_s, b_s)], axis=1)[:, -wb:]
    stack = lambda sts, i: jnp.stack([st[i] for st in sts])
    return (y_p, y_s, stack(st_p, 0), stack(st_p, 1), stack(st_p, 2),
            mem_kv_p.reshape(DEPTH, b_p, m_tok, 2, MEM_HEADS, HEAD_DIM),
            rows5(cmp_p, b_p), rows5(slc_p, b_p), p_win,
            stack(st_s, 0), stack(st_s, 1), stack(st_s, 2),
            rows5(cmp_s, b_s), rows5(slc_s, b_s), s_win)
```

```python
import math
from functools import partial

import numpy as np
import jax
import jax.numpy as jnp
from jax import lax
from jax.experimental import pallas as pl
from jax.experimental.pallas import tpu as pltpu

D_MODEL = 1024
DEPTH = 4
PAGE_SIZE = 128
N_A_LAYERS = DEPTH // 2
N_B_LAYERS = DEPTH - N_A_LAYERS
MIX_W = 3 * D_MODEL // 4
MEM_W = D_MODEL - MIX_W
HEAD_DIM = 64
MEM_HEADS = MEM_W // HEAD_DIM
MLSTM_HEADS = 4
MLSTM_DH = MIX_W // MLSTM_HEADS
NSA_HEADS = MIX_W // HEAD_DIM
NSA_KV_HEADS = 2
NSA_GROUP = NSA_HEADS // NSA_KV_HEADS
CMP_LEN = 32
CMP_STRIDE = 16
CMP_HIDDEN = 128
SLC_BLOCK = 64
N_SELECT = 8
WINDOW = 512
REL_BUCKETS = 32
REL_MAX_EXACT = 16
REL_MAX_DIST = 128
D_FF = -(-(8 * D_MODEL) // (3 * 256)) * 256
EPS = 1e-6
NEG = -1e30
FORCED = 1e6

LANES = 128
VMEM_LIMIT = 48 * 1024 * 1024
TAIL_VMEM_LIMIT = 56 * 1024 * 1024

BF16 = jnp.bfloat16
F32 = jnp.float32


def _pick_tile(n, candidates):
    for c in candidates:
        if n % c == 0:
            return c
    return n


def _pad_cols(w, n):
    return jnp.pad(w, ((0, 0), (0, n - w.shape[1])))


def _dot_nt(a, b):
    return lax.dot_general(a, b, (((1,), (1,)), ((), ())), preferred_element_type=F32)


def _rms_rows(x, g):
    return x * lax.rsqrt(jnp.mean(x * x, axis=-1, keepdims=True) + EPS) * g


def _half_rms(x, g2, lo):
    x2 = x * x
    ss_lo = jnp.sum(jnp.where(lo, x2, 0.0), axis=1, keepdims=True)
    ss_hi = jnp.sum(jnp.where(lo, 0.0, x2), axis=1, keepdims=True)
    inv = jnp.where(lo, lax.rsqrt(ss_lo / HEAD_DIM + EPS), lax.rsqrt(ss_hi / HEAD_DIM + EPS))
    return x * inv * g2


def _gain2(g):
    return jnp.concatenate([g, g]).reshape(1, LANES).astype(F32)


def _params(*sem):
    return pltpu.CompilerParams(dimension_semantics=sem, vmem_limit_bytes=VMEM_LIMIT)


def _norm_matmul_kernel(x_ref, g_ref, w_ref, o_ref, xn_ref):
    @pl.when(pl.program_id(1) == 0)
    def _():
        xn_ref[...] = _rms_rows(x_ref[...], g_ref[...]).astype(BF16)

    o_ref[...] = jnp.dot(xn_ref[...], w_ref[...], preferred_element_type=F32)


def norm_matmul(x, g, w):
    m, k = x.shape
    n = w.shape[1]
    tm = _pick_tile(m, (1024, 512, 256, 128))
    tn = _pick_tile(n, (1280, 1152, 1024, 896, 768, 640, 512, 384, 256, 128))
    return pl.pallas_call(
        _norm_matmul_kernel,
        grid=(m // tm, n // tn),
        in_specs=[pl.BlockSpec((tm, k), lambda i, j: (i, 0)),
                  pl.BlockSpec((1, k), lambda i, j: (0, 0)),
                  pl.BlockSpec((k, tn), lambda i, j: (0, j))],
        out_specs=pl.BlockSpec((tm, tn), lambda i, j: (i, j)),
        out_shape=jax.ShapeDtypeStruct((m, n), F32),
        scratch_shapes=[pltpu.VMEM((tm, k), BF16)],
        compiler_params=_params("parallel", "arbitrary"),
        name="norm_matmul",
    )(x, g.reshape(1, k), w)


def _mem_kv_kernel(x_ref, w_ref, g_ref, o_ref):
    tm = x_ref.shape[0]
    kv = jnp.dot(x_ref[...].astype(BF16), w_ref[...], preferred_element_type=F32)
    lo = lax.broadcasted_iota(jnp.int32, (tm, LANES), 1) < HEAD_DIM
    for k in range(MEM_HEADS // 2):
        o_ref[:, k * LANES:(k + 1) * LANES] = _half_rms(kv[:, k * LANES:(k + 1) * LANES], g_ref[...], lo)
    o_ref[:, MEM_W:] = kv[:, MEM_W:]


def mem_kv_rows(mem, w, g_k):
    m, d = mem.shape
    depth, _, n = w.shape
    tm = _pick_tile(m, (512, 256, 128))
    g2 = jnp.concatenate([g_k, g_k], axis=1).reshape(depth, 1, LANES).astype(F32)
    return pl.pallas_call(
        _mem_kv_kernel,
        grid=(depth, m // tm),
        in_specs=[pl.BlockSpec((tm, d), lambda l, i: (i, 0)),
                  pl.BlockSpec((None, d, n), lambda l, i: (l, 0, 0)),
                  pl.BlockSpec((None, 1, LANES), lambda l, i: (l, 0, 0))],
        out_specs=pl.BlockSpec((None, tm, n), lambda l, i: (l, i, 0)),
        out_shape=jax.ShapeDtypeStruct((depth, m, n), F32),
        compiler_params=_params("parallel", "parallel"),
        name="mem_kv_rows",
    )(mem, w, g2)


def _out_proj_kernel(a_ref, wa_ref, b_ref, wb_ref, r_ref, o_ref):
    o_ref[...] = (r_ref[...] + jnp.dot(a_ref[...].astype(BF16), wa_ref[...], preferred_element_type=F32)
                  + jnp.dot(b_ref[...].astype(BF16), wb_ref[...], preferred_element_type=F32))


def out_proj(a, wa, b, wb, res):
    m, n = res.shape
    tm = _pick_tile(m, (512, 256, 128))
    ka, kb = a.shape[1], b.shape[1]
    return pl.pallas_call(
        _out_proj_kernel,
        grid=(m // tm,),
        in_specs=[pl.BlockSpec((tm, ka), lambda i: (i, 0)),
                  pl.BlockSpec((ka, n), lambda i: (0, 0)),
                  pl.BlockSpec((tm, kb), lambda i: (i, 0)),
                  pl.BlockSpec((kb, n), lambda i: (0, 0)),
                  pl.BlockSpec((tm, n), lambda i: (i, 0))],
        out_specs=pl.BlockSpec((tm, n), lambda i: (i, 0)),
        out_shape=jax.ShapeDtypeStruct((m, n), F32),
        compiler_params=_params("parallel"),
        name="out_proj",
    )(a, wa, b, wb, res)


def _layer_tail_kernel(a_ref, wa_ref, b_ref, wb_ref, x_ref, g_ref, wg_ref, wu_ref, wd_ref, o_ref, xn_ref):
    f = pl.program_id(1)

    @pl.when(f == 0)
    def _():
        x1 = (x_ref[...] + jnp.dot(a_ref[...].astype(BF16), wa_ref[...], preferred_element_type=F32)
              + jnp.dot(b_ref[...].astype(BF16), wb_ref[...], preferred_element_type=F32))
        o_ref[...] = x1
        xn_ref[...] = _rms_rows(x1, g_ref[...]).astype(BF16)

    xn = xn_ref[...]
    gate = jnp.dot(xn, wg_ref[...], preferred_element_type=F32)
    up = jnp.dot(xn, wu_ref[...], preferred_element_type=F32)
    act = (gate * jax.nn.sigmoid(gate) * up).astype(BF16)
    o_ref[...] += jnp.dot(act, wd_ref[...], preferred_element_type=F32)


def layer_tail(a, wa, b, wb, x, g, w_gu, w_down):
    m, d = x.shape
    tm = _pick_tile(m, (512, 256, 128))
    tf = 1408
    nf = D_FF // tf
    ka, kb = a.shape[1], b.shape[1]
    row = lambda k: pl.BlockSpec((tm, k), lambda i, f: (i, 0))
    const = lambda shape: pl.BlockSpec(shape, lambda i, f: (0, 0))
    return pl.pallas_call(
        _layer_tail_kernel,
        grid=(m // tm, nf),
        in_specs=[row(ka), const((ka, d)), row(kb), const((kb, d)), row(d), const((1, d)),
                  pl.BlockSpec((d, tf), lambda i, f: (0, f)),
                  pl.BlockSpec((d, tf), lambda i, f: (0, f + nf)),
                  pl.BlockSpec((tf, d), lambda i, f: (f, 0))],
        out_specs=row(d),
        out_shape=jax.ShapeDtypeStruct((m, d), F32),
        scratch_shapes=[pltpu.VMEM((tm, d), BF16)],
        compiler_params=pltpu.CompilerParams(dimension_semantics=("parallel", "arbitrary"),
                                             vmem_limit_bytes=TAIL_VMEM_LIMIT),
        name="layer_tail",
    )(a, wa, b, wb, x, g.reshape(1, d), w_gu, w_gu, w_down)


def _ffn_kernel(x_ref, g_ref, wg_ref, wu_ref, wd_ref, o_ref, xn_ref, acc_ref):
    f = pl.program_id(1)

    @pl.when(f == 0)
    def _():
        xn_ref[...] = _rms_rows(x_ref[...], g_ref[...]).astype(BF16)
        acc_ref[...] = jnp.zeros_like(acc_ref)

    xn = xn_ref[...]
    gate = jnp.dot(xn, wg_ref[...], preferred_element_type=F32)
    up = jnp.dot(xn, wu_ref[...], preferred_element_type=F32)
    act = (gate * jax.nn.sigmoid(gate) * up).astype(BF16)
    acc_ref[...] += jnp.dot(act, wd_ref[...], preferred_element_type=F32)

    @pl.when(f == pl.num_programs(1) - 1)
    def _():
        o_ref[...] = x_ref[...] + acc_ref[...]


def ffn(x, g, w_gu, w_down):
    m, d = x.shape
    tm = _pick_tile(m, (512, 256, 128))
    tf = 1408
    nf = D_FF // tf
    return pl.pallas_call(
        _ffn_kernel,
        grid=(m // tm, nf),
        in_specs=[pl.BlockSpec((tm, d), lambda i, f: (i, 0)),
                  pl.BlockSpec((1, d), lambda i, f: (0, 0)),
                  pl.BlockSpec((d, tf), lambda i, f: (0, f)),
                  pl.BlockSpec((d, tf), lambda i, f: (0, f + nf)),
                  pl.BlockSpec((tf, d), lambda i, f: (f, 0))],
        out_specs=pl.BlockSpec((tm, d), lambda i, f: (i, 0)),
        out_shape=jax.ShapeDtypeStruct((m, d), F32),
        scratch_shapes=[pltpu.VMEM((tm, d), BF16), pltpu.VMEM((tm, d), F32)],
        compiler_params=_params("parallel", "arbitrary"),
        name="ffn",
    )(x, g.reshape(1, d), w_gu, w_gu, w_down)


MLSTM_L = 128
MLSTM_PAD = 256
N_COL = MLSTM_DH
MLSTM_HW = MLSTM_HEADS * MLSTM_PAD
STEP_TB = 8


def _exact_tri_cumsum(tri, x):
    hi = x.astype(BF16)
    r1 = x - hi.astype(F32)
    mid = r1.astype(BF16)
    lo = (r1 - mid.astype(F32)).astype(BF16)
    return (jnp.dot(tri, hi, preferred_element_type=F32) + jnp.dot(tri, mid, preferred_element_type=F32)
            + jnp.dot(tri, lo, preferred_element_type=F32))


def _mlstm_kernel(q_ref, k_ref, v_ref, og_ref, gate_ref, bif_ref, gh_ref, c0_ref, m0_ref,
                  mix_ref, cout_ref, mout_ref, c_scr, m_scr):
    L = MLSTM_L
    P = MLSTM_PAD
    ci = pl.program_id(1)

    @pl.when(ci == 0)
    def _():
        c_scr[...] = c0_ref[...]
        m_scr[...] = m0_ref[...]

    row = lax.broadcasted_iota(jnp.int32, (L, L), 0)
    col = lax.broadcasted_iota(jnp.int32, (L, L), 1)
    causal = col <= row
    tri = jnp.where(causal, 1.0, 0.0).astype(BF16)
    lane_p = lax.broadcasted_iota(jnp.int32, (L, P), 1)

    gl = gate_ref[...] + bif_ref[...]
    logf = jax.nn.log_sigmoid(gl)
    bc = _exact_tri_cumsum(tri, logf)
    bt = bc.T
    gt = gl.T
    for h in range(MLSTM_HEADS):
        fh = MLSTM_HEADS + h
        b_col = bc[:, fh:fh + 1]
        b_row = bt[fh:fh + 1, :]
        i_row = gt[h:h + 1, :]
        i_col = gl[:, h:h + 1]
        m_prev = m_scr[0:1, h:h + 1]
        log_d = jnp.where(causal, b_col - b_row + i_row, -jnp.inf)
        log_inter = b_col + m_prev
        m_t = jnp.maximum(jnp.max(log_d, axis=1, keepdims=True), log_inter)
        w_intra = jnp.exp(log_d - m_t)
        w_inter = jnp.exp(log_inter - m_t)
        sl = slice(h * P, (h + 1) * P)
        q = q_ref[:, sl].astype(BF16)
        kf = k_ref[:, sl] * MLSTM_DH ** -0.5
        v_aug = jnp.where(lane_p == N_COL, 1.0, v_ref[:, sl])
        sc = _dot_nt(q, kf.astype(BF16)) * w_intra
        c = c_scr[h]
        nd = (jnp.dot(sc.astype(BF16), v_aug.astype(BF16), preferred_element_type=F32)
              + w_inter * jnp.dot(q, c.astype(BF16), preferred_element_type=F32))
        den = nd[:, N_COL:N_COL + 1]
        hh = jnp.where(lane_p < N_COL, nd / jnp.maximum(jnp.abs(den), jnp.exp(-m_t)), 0.0)
        y = hh * lax.rsqrt(jnp.sum(hh * hh, axis=1, keepdims=True) / MLSTM_DH + EPS) * gh_ref[...]
        mix_ref[:, sl] = jax.nn.sigmoid(og_ref[:, sl]) * y
        g_last = bc[L - 1:L, fh:fh + 1]
        m_new = m_t[L - 1:L, :]
        w_s = jnp.exp(g_last - b_col + i_col - m_new)
        decay = jnp.exp(g_last + m_prev - m_new)
        upd = jnp.dot(kf.T.astype(BF16), (w_s * v_aug).astype(BF16), preferred_element_type=F32)
        c_scr[h] = decay * c + upd
        m_scr[0:1, h:h + 1] = m_new

    @pl.when(ci == pl.num_programs(1) - 1)
    def _():
        cout_ref[...] = c_scr[...]
        mout_ref[...] = m_scr[...]


def mlstm_prompt(p, b, s, b_if, g_hnorm, c0, m0):
    L = MLSTM_L
    P = MLSTM_PAD
    hw = MLSTM_HW
    assert s % L == 0
    nc = s // L
    gcol = (4 * hw + MEM_W) // LANES
    bif = _pad_cols(b_if.astype(F32).reshape(1, -1), LANES)
    gh = _pad_cols(g_hnorm.astype(F32).reshape(1, -1), P)
    blk = lambda j: pl.BlockSpec((L, hw), lambda bi, ci: (bi * nc + ci, j))
    return pl.pallas_call(
        _mlstm_kernel,
        grid=(b, nc),
        in_specs=[blk(0), blk(1), blk(2), blk(3),
                  pl.BlockSpec((L, LANES), lambda bi, ci: (bi * nc + ci, gcol)),
                  pl.BlockSpec((1, LANES), lambda bi, ci: (0, 0)),
                  pl.BlockSpec((1, P), lambda bi, ci: (0, 0)),
                  pl.BlockSpec((None, MLSTM_HEADS, P, P), lambda bi, ci: (bi, 0, 0, 0)),
                  pl.BlockSpec((None, 1, LANES), lambda bi, ci: (bi, 0, 0))],
        out_specs=[pl.BlockSpec((L, hw), lambda bi, ci: (bi * nc + ci, 0)),
                   pl.BlockSpec((None, MLSTM_HEADS, P, P), lambda bi, ci: (bi, 0, 0, 0)),
                   pl.BlockSpec((None, 1, LANES), lambda bi, ci: (bi, 0, 0))],
        out_shape=[jax.ShapeDtypeStruct((b * s, hw), F32),
                   jax.ShapeDtypeStruct((b, MLSTM_HEADS, P, P), F32),
                   jax.ShapeDtypeStruct((b, 1, LANES), F32)],
        scratch_shapes=[pltpu.VMEM((MLSTM_HEADS, P, P), F32), pltpu.VMEM((1, LANES), F32)],
        compiler_params=_params("parallel", "arbitrary"),
        name="mlstm_prompt",
    )(p, p, p, p, p, bif, gh, c0, m0)


def pack_mlstm_state(c, n, m):
    pad = MLSTM_PAD - MLSTM_DH
    cn = jnp.concatenate([c, n[..., None]], axis=-1)
    cn = jnp.pad(cn, ((0, 0), (0, 0), (0, pad), (0, pad - 1)))
    return cn, _pad_cols(m, LANES)[:, None, :]


def unpack_mlstm_state(cn, m):
    return cn[:, :, :MLSTM_DH, :MLSTM_DH], cn[:, :, :MLSTM_DH, MLSTM_DH], m[:, 0, :MLSTM_HEADS]


def _mlstm_step_kernel(q_ref, v_ref, og_ref, gate_ref, qkt_ref, bif_ref, gh_ref, c_ref, n_ref, m_ref,
                       mix_ref, cout_ref, nout_ref, mout_ref):
    tb = q_ref.shape[0]
    dh = MLSTM_DH
    gl = gate_ref[...] + bif_ref[...]
    logf = jax.nn.log_sigmoid(gl)
    i4 = gl[:, :MLSTM_HEADS]
    f4 = logf[:, MLSTM_HEADS:2 * MLSTM_HEADS]
    m_prev = m_ref[...]
    m_t = jnp.maximum(i4, f4 + m_prev)
    w_in = jnp.exp(i4 - m_t)
    w_dec = jnp.exp(f4 + m_prev - m_t)
    floor = jnp.exp(-m_t)
    mout_ref[...] = m_t
    mix_ref[...] = jnp.zeros(mix_ref.shape, F32)
    rowi = lax.broadcasted_iota(jnp.int32, (LANES, dh), 0)
    for j in range(tb):
        for h in range(MLSTM_HEADS):
            sl = slice(h * MLSTM_PAD, h * MLSTM_PAD + dh)
            wi = w_in[j:j + 1, h:h + 1]
            wd = w_dec[j:j + 1, h:h + 1]
            q_row = q_ref[j:j + 1, sl]
            v_row = v_ref[j:j + 1, sl]
            k_row = qkt_ref[2, h, j:j + 1, :]
            c = c_ref[j, h]
            n_row = n_ref[j, h:h + 1, :]
            qk = jnp.sum(q_row * k_row, axis=1, keepdims=True) * wi
            qc = jnp.dot(q_ref[:, sl].astype(BF16), c.astype(BF16), preferred_element_type=F32)[j:j + 1, :]
            qn = jnp.sum(q_row * n_row, axis=1, keepdims=True)
            v_sel = jnp.where(rowi == j, wi * v_row, 0.0).astype(BF16)
            kv = jnp.dot(qkt_ref[1, h, :, :LANES].astype(BF16), v_sel, preferred_element_type=F32)
            num = qk * v_row + wd * qc
            den = qk + wd * qn
            hh = num / jnp.maximum(jnp.abs(den), floor[j:j + 1, h:h + 1])
            y = hh * lax.rsqrt(jnp.sum(hh * hh, axis=1, keepdims=True) / dh + EPS) * gh_ref[...]
            mix_ref[j:j + 1, sl] = jax.nn.sigmoid(og_ref[j:j + 1, sl]) * y
            cout_ref[j, h] = wd * c + kv
            nout_ref[j, h:h + 1, :] = wd * n_row + wi * k_row


def mlstm_step(p, b_if, g_hnorm, c, n, m, layer):
    b = p.shape[0]
    tb = STEP_TB
    hw = MLSTM_HW
    dh = MLSTM_DH
    nb = b // tb
    gcol = (4 * hw + MEM_W) // LANES
    q = p[:, :hw].reshape(b, MLSTM_HEADS, MLSTM_PAD)[:, :, :dh]
    k = p[:, hw:2 * hw].reshape(b, MLSTM_HEADS, MLSTM_PAD)[:, :, :dh] * dh ** -0.5
    cols = lambda a: a.reshape(nb, tb, MLSTM_HEADS, dh).transpose(0, 2, 3, 1)
    pad_c = jnp.pad(jnp.stack([cols(q), cols(k)], axis=1), ((0, 0), (0, 0), (0, 0), (0, 0), (0, dh - tb)))
    k_rows = k.reshape(nb, tb, MLSTM_HEADS, dh).transpose(0, 2, 1, 3)
    pad_r = jnp.pad(k_rows, ((0, 0), (0, 0), (0, dh - tb), (0, 0)))
    qkt = jnp.concatenate([pad_c, pad_r[:, None]], axis=1)
    bif = _pad_cols(b_if.astype(F32).reshape(1, -1), LANES)
    gh = g_hnorm.astype(F32).reshape(1, dh)
    blk = lambda j: pl.BlockSpec((tb, hw), lambda i: (i, j))
    st4 = pl.BlockSpec((tb, MLSTM_HEADS, dh, dh), lambda i: (i, 0, 0, 0))
    st3 = pl.BlockSpec((tb, MLSTM_HEADS, dh), lambda i: (i, 0, 0))
    st2 = pl.BlockSpec((tb, MLSTM_HEADS), lambda i: (i, 0))
    return pl.pallas_call(
        _mlstm_step_kernel,
        grid=(nb,),
        in_specs=[blk(0), blk(2), blk(3),
                  pl.BlockSpec((tb, LANES), lambda i: (i, gcol)),
                  pl.BlockSpec((None, 3, MLSTM_HEADS, dh, dh), lambda i: (i, 0, 0, 0, 0)),
                  pl.BlockSpec((1, LANES), lambda i: (0, 0)),
                  pl.BlockSpec((1, dh), lambda i: (0, 0)),
                  pl.BlockSpec((None, tb, MLSTM_HEADS, dh, dh), lambda i: (layer, i, 0, 0, 0)),
                  pl.BlockSpec((None, tb, MLSTM_HEADS, dh), lambda i: (layer, i, 0, 0)),
                  pl.BlockSpec((None, tb, MLSTM_HEADS), lambda i: (layer, i, 0))],
        out_specs=[pl.BlockSpec((tb, hw), lambda i: (i, 0)), st4, st3, st2],
        out_shape=[jax.ShapeDtypeStruct((b, hw), F32), jax.ShapeDtypeStruct(c.shape[1:], F32),
                   jax.ShapeDtypeStruct(n.shape[1:], F32), jax.ShapeDtypeStruct(m.shape[1:], F32)],
        compiler_params=_params("parallel"),
        name="mlstm_step",
    )(p, p, p, p, qkt, bif, gh, c, n, m)


def _mem_attn_kernel(q_ref, gq_ref, kv_ref, o_ref):
    tq = q_ref.shape[0]
    lane = lax.broadcasted_iota(jnp.int32, (tq, LANES), 1)
    lo = lane < HEAD_DIM
    for k in range(MEM_HEADS // 2):
        y = _half_rms(q_ref[:, k * LANES:(k + 1) * LANES], gq_ref[...], lo) * HEAD_DIM ** -0.5
        kp = kv_ref[:, k * LANES:(k + 1) * LANES].astype(BF16)
        vp = kv_ref[:, MEM_W + k * LANES:MEM_W + (k + 1) * LANES].astype(BF16)
        outs = []
        for half in range(2):
            qh = jnp.where(lo if half == 0 else ~lo, y, 0.0).astype(BF16)
            s = _dot_nt(qh, kp)
            p = jnp.exp(s - jnp.max(s, axis=1, keepdims=True))
            p = p / jnp.sum(p, axis=1, keepdims=True)
            outs.append(jnp.dot(p.astype(BF16), vp, preferred_element_type=F32))
        o_ref[:, k * LANES:(k + 1) * LANES] = jnp.where(lo, outs[0], outs[1])


def mem_attn_prompt(p, qcol, b, s, g_q, mem_kv, layer):
    tq = _pick_tile(s, (512, 256, 128))
    nq = s // tq
    m_tok = mem_kv.shape[2]
    return pl.pallas_call(
        _mem_attn_kernel,
        grid=(b, nq),
        in_specs=[pl.BlockSpec((tq, MEM_W), lambda bi, qi: (bi * nq + qi, qcol)),
                  pl.BlockSpec((1, LANES), lambda bi, qi: (0, 0)),
                  pl.BlockSpec((None, None, m_tok, 2 * MEM_W), lambda bi, qi: (layer, bi, 0, 0))],
        out_specs=pl.BlockSpec((tq, MEM_W), lambda bi, qi: (bi * nq + qi, 0)),
        out_shape=jax.ShapeDtypeStruct((b * s, MEM_W), F32),
        compiler_params=_params("parallel", "parallel"),
        name="mem_attn",
    )(p, _gain2(g_q), mem_kv)


def _mem_decode_kernel(q_ref, gq_ref, kv_ref, o_ref):
    tb = q_ref.shape[0]
    lane = lax.broadcasted_iota(jnp.int32, (tb, LANES), 1)
    lo = lane < HEAD_DIM
    y = jnp.concatenate([_half_rms(q_ref[:, k * LANES:(k + 1) * LANES], gq_ref[...], lo)
                         for k in range(MEM_HEADS // 2)], axis=1) * HEAD_DIM ** -0.5
    rowi = lax.broadcasted_iota(jnp.int32, (8, MEM_W), 0)
    own = (lax.broadcasted_iota(jnp.int32, (8, MEM_W), 1) // HEAD_DIM) == rowi
    for j in range(tb):
        qbd = jnp.where(own, y[j:j + 1, :], 0.0).astype(BF16)
        s = _dot_nt(qbd, kv_ref[j, :, :MEM_W].astype(BF16))
        p = jnp.exp(s - jnp.max(s, axis=1, keepdims=True))
        p = p / jnp.sum(p, axis=1, keepdims=True)
        o = jnp.dot(p.astype(BF16), kv_ref[j, :, MEM_W:].astype(BF16), preferred_element_type=F32)
        o_ref[j:j + 1, :] = jnp.sum(jnp.where(own, o, 0.0), axis=0, keepdims=True)


def mem_attn_decode(p, qcol, g_q, mem_kv, layer):
    b = p.shape[0]
    tb = STEP_TB
    m_tok = mem_kv.shape[2]
    return pl.pallas_call(
        _mem_decode_kernel,
        grid=(b // tb,),
        in_specs=[pl.BlockSpec((tb, MEM_W), lambda i: (i, qcol)),
                  pl.BlockSpec((1, LANES), lambda i: (0, 0)),
                  pl.BlockSpec((None, tb, m_tok, 2 * MEM_W), lambda i: (layer, i, 0, 0))],
        out_specs=pl.BlockSpec((tb, MEM_W), lambda i: (i, 0)),
        out_shape=jax.ShapeDtypeStruct((b, MEM_W), F32),
        compiler_params=_params("parallel"),
        name="mem_attn_decode",
    )(p, _gain2(g_q), mem_kv)


NSA_TQ = 128
MASK = -(2.0 ** 100)
SEL_LANE = HEAD_DIM
N_SEL_LANES = 32
CONST_LANE = SEL_LANE + N_SEL_LANES
KV_ROW = 2 * NSA_KV_HEADS * HEAD_DIM
SEG_PER_PAGE = PAGE_SIZE // CMP_STRIDE
SEG_ROW = CMP_STRIDE * KV_ROW
CMP_OUT = 2 * NSA_KV_HEADS * CMP_HIDDEN
QROWS = 16


def _rel_bucket_np(dist):
    d = np.maximum(dist, 0)
    ratio = np.maximum(d, REL_MAX_EXACT).astype(np.float64) / REL_MAX_EXACT
    large = REL_MAX_EXACT + (np.log(ratio) / math.log(REL_MAX_DIST / REL_MAX_EXACT)
                             * (REL_BUCKETS - REL_MAX_EXACT)).astype(np.int32)
    return np.where(d < REL_MAX_EXACT, d, np.minimum(large, REL_BUCKETS - 1)).astype(np.int32)


def _overlap_matrix(n_cmp, n_slc):
    c0 = np.arange(n_cmp)[:, None] * CMP_STRIDE
    s0 = np.arange(n_slc)[None, :] * SLC_BLOCK
    return np.clip(np.minimum(c0 + CMP_LEN, s0 + SLC_BLOCK) - np.maximum(c0, s0), 0, None) / CMP_STRIDE


def _kv_rows_kernel(x_ref, g_ref, w_ref, gs_ref, gw_ref, cmp_ref, slc_ref, win_ref, *aug_refs, seq_len):
    tm = x_ref.shape[0]
    xn = _rms_rows(x_ref[...], g_ref[...]).astype(BF16)
    kv = jnp.dot(xn, w_ref[...], preferred_element_type=F32)
    cmp_ref[...] = kv[:, :KV_ROW]
    lane = lax.broadcasted_iota(jnp.int32, (tm, LANES), 1)
    lo = lane < HEAD_DIM
    if aug_refs:
        ka_ref, va_ref = aug_refs
        pos = (pl.program_id(0) * tm + lax.broadcasted_iota(jnp.int32, (tm, LANES), 0)) % seq_len
        ones = (lane == CONST_LANE) | (lane == CONST_LANE + 1)
        onehot = (lane - SEL_LANE) == pos // SLC_BLOCK
    for t, (rows_ref, gk_ref) in enumerate(((slc_ref, gs_ref), (win_ref, gw_ref))):
        base = KV_ROW * (t + 1)
        kn = _half_rms(kv[:, base:base + LANES], gk_ref[...], lo)
        vv = kv[:, base + LANES:base + 2 * LANES]
        rows_ref[:, :LANES] = kn
        rows_ref[:, LANES:] = vv
        if aug_refs:
            extra = jnp.where((ones | onehot) if t == 0 else ones, 1.0, 0.0)
            rowt = lax.broadcasted_iota(jnp.int32, (VT_ROWS, tm), 0)
            for g in range(NSA_KV_HEADS):
                kg = kn if g == 0 else pltpu.roll(kn, HEAD_DIM, 1)
                vg = vv if g == 0 else pltpu.roll(vv, HEAD_DIM, 1)
                ka_ref[NSA_KV_HEADS * t + g] = jnp.where(lo, kg, extra).astype(BF16)
                vt = jnp.where(lo, vg, 0.0).T[:VT_ROWS, :]
                va_ref[NSA_KV_HEADS * t + g] = jnp.where(rowt == HEAD_DIM, 1.0, vt).astype(BF16)


def kv_rows(x, g_kv, w_kv, g_k_slc, g_k_win, seq_len, aug):
    m, d = x.shape
    n = w_kv.shape[1]
    tm = _pick_tile(m, (512, 256, 128))
    out_specs = [pl.BlockSpec((tm, KV_ROW), lambda i: (i, 0))] * 3
    out_shape = [jax.ShapeDtypeStruct((m, KV_ROW), F32)] * 3
    if aug:
        out_specs += [pl.BlockSpec((2 * NSA_KV_HEADS, tm, LANES), lambda i: (0, i, 0)),
                      pl.BlockSpec((2 * NSA_KV_HEADS, VT_ROWS, tm), lambda i: (0, 0, i))]
        out_shape += [jax.ShapeDtypeStruct((2 * NSA_KV_HEADS, m, LANES), BF16),
                      jax.ShapeDtypeStruct((2 * NSA_KV_HEADS, VT_ROWS, m), BF16)]
    return pl.pallas_call(
        partial(_kv_rows_kernel, seq_len=seq_len),
        grid=(m // tm,),
        in_specs=[pl.BlockSpec((tm, d), lambda i: (i, 0)),
                  pl.BlockSpec((1, d), lambda i: (0, 0)),
                  pl.BlockSpec((d, n), lambda i: (0, 0)),
                  pl.BlockSpec((1, LANES), lambda i: (0, 0)),
                  pl.BlockSpec((1, LANES), lambda i: (0, 0))],
        out_specs=out_specs,
        out_shape=out_shape,
        compiler_params=_params("parallel"),
        name="kv_rows",
    )(x, g_kv.reshape(1, d), w_kv, _gain2(g_k_slc), _gain2(g_k_win))


def _compress_kernel(*refs, n_x, paged):
    if paged:
        refs = refs[1:]
    x_refs = refs[:n_x]
    w1_ref, b1_ref, w2_ref, b2_ref, gk_ref, o_ref = refs[n_x:]
    n_seg = o_ref.shape[0]
    r_n = CMP_LEN // CMP_STRIDE
    row_stride = 2 * CMP_STRIDE
    lane = lax.broadcasted_iota(jnp.int32, (n_seg, LANES), 1)
    for c in range(2):
        acc = [jnp.zeros((n_seg, CMP_OUT // 2), F32) for _ in range(r_n)]
        for u in range(CMP_STRIDE):
            pieces = [r[pl.ds(2 * u + c, r.shape[0] // row_stride, stride=row_stride), :] for r in x_refs]
            xu = (pieces[0] if n_x == 1 else jnp.concatenate(pieces, axis=0)).astype(BF16)
            for r in range(r_n):
                acc[r] = acc[r] + jnp.dot(xu, w1_ref[r, u, c], preferred_element_type=F32)
        pre = acc[0] + pltpu.roll(acc[1], n_seg - 1, 0) + b1_ref[c]
        out = jnp.dot(jax.nn.gelu(pre).astype(BF16), w2_ref[c], preferred_element_type=F32) + b2_ref[c]
        if c == 0:
            out = _half_rms(out, gk_ref[...], lane < HEAD_DIM)
        o_ref[:, c * LANES:(c + 1) * LANES] = out.astype(BF16)


def compress_weights(w1, b1, w2, b2, pos_enc, g_k_cmp):
    r_n = CMP_LEN // CMP_STRIDE
    eye = jnp.eye(NSA_KV_HEADS, dtype=F32)
    w1r = w1.reshape(2, r_n, CMP_STRIDE, HEAD_DIM, CMP_HIDDEN)
    w1b = jnp.einsum('crudh,gy->rucgdyh', w1r, eye).reshape(r_n, CMP_STRIDE, 2, LANES, CMP_OUT // 2).astype(BF16)
    pe = jnp.einsum('pcd,cpdh->ch', pos_enc, w1)
    b1b = jnp.broadcast_to((b1 + pe)[:, None, :], (2, NSA_KV_HEADS, CMP_HIDDEN)).reshape(2, 1, CMP_OUT // 2)
    w2b = jnp.einsum('chd,gy->cghyd', w2, eye).reshape(2, CMP_OUT // 2, LANES).astype(BF16)
    b2b = jnp.broadcast_to(b2[:, None, :], (2, NSA_KV_HEADS, HEAD_DIM)).reshape(2, 1, LANES)
    return w1b, b1b, w2b, b2b, _gain2(g_k_cmp)


def compress_rows(x, cw, page_table=None):
    w1b, b1b, w2b, b2b, gk2 = cw
    paged = page_table is not None
    if paged:
        n_seq, n_x = page_table.shape
        x_specs = [pl.BlockSpec((None, 2 * PAGE_SIZE, LANES), partial(lambda k, b, pt: (pt[b, k], 0, 0), k))
                   for k in range(n_x)]
        n_seg = n_x * SEG_PER_PAGE
        cmap = lambda *idx: lambda b, pt: idx
    else:
        n_seq, n_seg, n_x = x.shape[0], x.shape[1] // (2 * CMP_STRIDE), 1
        x_specs = [pl.BlockSpec((None, x.shape[1], LANES), lambda b: (b, 0, 0))]
        cmap = lambda *idx: lambda b: idx
    w_specs = [pl.BlockSpec(w1b.shape, cmap(0, 0, 0, 0, 0)), pl.BlockSpec(b1b.shape, cmap(0, 0, 0)),
               pl.BlockSpec(w2b.shape, cmap(0, 0, 0)), pl.BlockSpec(b2b.shape, cmap(0, 0, 0)),
               pl.BlockSpec(gk2.shape, cmap(0, 0))]
    out_spec = pl.BlockSpec((None, n_seg, KV_ROW), (lambda b, pt: (b, 0, 0)) if paged else (lambda b: (b, 0, 0)))
    out_shape = jax.ShapeDtypeStruct((n_seq, n_seg, KV_ROW), BF16)
    kern = partial(_compress_kernel, n_x=n_x, paged=paged)
    if paged:
        gs = pltpu.PrefetchScalarGridSpec(num_scalar_prefetch=1, grid=(n_seq,), in_specs=x_specs + w_specs,
                                          out_specs=out_spec)
        return pl.pallas_call(kern, grid_spec=gs, out_shape=out_shape, compiler_params=_params("parallel"),
                              name="compress_paged")(page_table, *([x] * n_x), w1b, b1b, w2b, b2b, gk2)
    return pl.pallas_call(kern, grid=(n_seq,), in_specs=x_specs + w_specs, out_specs=out_spec, out_shape=out_shape,
                          compiler_params=_params("parallel"), name="compress")(x, w1b, b1b, w2b, b2b, gk2)


def _nsa_prompt_kernel(rel_ref, q_ref, gate_ref, gq_ref, bg_ref, ck_ref, cv_ref, cb_ref, mt_ref,
                       sk_ref, sv_ref, wk_ref, wv_ref, dl_ref, o_ref,
                       qa_ref, m_ref, l_ref, acc_ref, out_ref):
    tq = NSA_TQ
    rows = NSA_GROUP * tq
    g = pl.program_id(1)
    qi = pl.program_id(2)
    lane = lax.broadcasted_iota(jnp.int32, (tq, LANES), 1)
    lo = lane < HEAD_DIM

    qn = []
    for k in range(NSA_GROUP // 2):
        y = _half_rms(q_ref[:, k * LANES:(k + 1) * LANES], gq_ref[...], lo) * HEAD_DIM ** -0.5
        qn.append(jnp.where(lo, y, 0.0))
        qn.append(jnp.where(lo, pltpu.roll(y, HEAD_DIM, 1), 0.0))

    sig = jax.nn.sigmoid(gate_ref[...] + bg_ref[...])

    def gated(br, z, o):
        c = br * NSA_GROUP + z
        return sig[:, c:c + 1] * o

    qc = jnp.concatenate(qn, axis=0).astype(BF16)
    s = _dot_nt(qc, ck_ref[...]) + cb_ref[...].reshape(rows, LANES)
    m = jnp.maximum(jnp.max(s, axis=1, keepdims=True), NEG)
    p = jnp.exp(s - m)
    l = jnp.sum(p, axis=1, keepdims=True)
    pn = p * jnp.where(l > 0.0, 1.0 / l, 0.0)
    o_cmp = jnp.dot(pn.astype(BF16), cv_ref[...], preferred_element_type=F32)
    for z in range(NSA_GROUP):
        out_ref[z * tq:(z + 1) * tq, :] = gated(0, z, o_cmp[z * tq:(z + 1) * tq, :])

    psum = pn[0:tq, :]
    for z in range(1, NSA_GROUP):
        psum = psum + pn[z * tq:(z + 1) * tq, :]
    p_hi = psum.astype(BF16)
    p_lo = (psum - p_hi.astype(F32)).astype(BF16)
    imp = _dot_nt(mt_ref[...], p_hi) + _dot_nt(mt_ref[...], p_lo)
    jidx = lax.broadcasted_iota(jnp.int32, (N_SEL_LANES, tq), 0)
    tpos = qi * tq + lax.broadcasted_iota(jnp.int32, (N_SEL_LANES, tq), 1)
    cur = tpos // SLC_BLOCK
    forced = (jidx == 0) | (jidx == cur) | (jidx == cur - 1)
    score = jnp.where(forced, FORCED, jnp.where(jidx <= cur, imp, -1.0))
    rank = jnp.zeros((N_SEL_LANES, tq), F32)
    for i in range(N_SEL_LANES):
        ri = score[i:i + 1, :]
        beats = (ri > score) | ((ri == score) & (jidx > i))
        rank = rank + jnp.where(beats, 1.0, 0.0)
    selb = jnp.where(rank < N_SELECT, 0.0, MASK)
    selb = jnp.concatenate([selb, jnp.zeros((LANES - N_SEL_LANES, tq), F32)], axis=0)
    sel_cols = pltpu.roll(selb.T, SEL_LANE, 1)

    for z in range(NSA_GROUP):
        c = jnp.full((tq, LANES), rel_ref[REL_BUCKETS - 1, g * NSA_GROUP + z], F32)
        c_hi = c.astype(BF16).astype(F32)
        extra = jnp.where(lane == CONST_LANE, c_hi, jnp.where(lane == CONST_LANE + 1, c - c_hi, sel_cols))
        qa_ref[z * tq:(z + 1) * tq, :] = jnp.where(lo, qn[z], extra).astype(BF16)

    def flash_init():
        m_ref[...] = jnp.full((rows, LANES), NEG, F32)
        l_ref[...] = jnp.zeros((rows, LANES), F32)
        acc_ref[...] = jnp.zeros((rows, LANES), F32)

    def flash_block(k_ref, v_ref, kb, didx):
        start = pl.multiple_of(kb * tq, tq)
        s = _dot_nt(qa_ref[...], k_ref[pl.ds(start, tq), :]) + dl_ref[didx].reshape(rows, LANES)
        m_prev = m_ref[...]
        m_next = jnp.maximum(m_prev, jnp.max(s, axis=1, keepdims=True))
        p = jnp.exp(s - m_next)
        alpha = jnp.exp(m_prev - m_next)
        l_ref[...] = alpha * l_ref[...] + jnp.sum(p, axis=1, keepdims=True)
        acc_ref[...] = alpha * acc_ref[...] + jnp.dot(p.astype(BF16), v_ref[pl.ds(start, tq), :],
                                                      preferred_element_type=F32)
        m_ref[...] = m_next

    def flash_finish(br):
        o = acc_ref[...] / l_ref[...]
        for z in range(NSA_GROUP):
            out_ref[z * tq:(z + 1) * tq, :] += gated(br, z, o[z * tq:(z + 1) * tq, :])

    flash_init()

    def slc_body(kb, carry):
        didx = jnp.where(kb == qi, 2, jnp.where(kb == qi - 1, 1, 0))
        flash_block(sk_ref, sv_ref, kb, didx)
        return carry

    lax.fori_loop(0, qi + 1, slc_body, 0)
    flash_finish(1)

    flash_init()
    n_win = WINDOW // tq

    def win_body(kb, carry):
        d = qi - kb
        didx = jnp.where(d == 0, 2, jnp.where(d == 1, 1, jnp.where(d == n_win, 3, 0)))
        flash_block(wk_ref, wv_ref, kb, didx)
        return carry

    lax.fori_loop(jnp.maximum(qi - n_win, 0), qi + 1, win_body, 0)
    flash_finish(2)

    for k in range(NSA_GROUP // 2):
        even = out_ref[(2 * k) * tq:(2 * k + 1) * tq, :]
        odd = out_ref[(2 * k + 1) * tq:(2 * k + 2) * tq, :]
        o_ref[:, k * LANES:(k + 1) * LANES] = even + pltpu.roll(odd, HEAD_DIM, 1)


VT_ROWS = 80
CMP_BAND = 16


def _nsa_prompt_t_kernel(rel_ref, q_ref, gate_ref, gq_ref, bg_ref, ck_ref, cvt_ref, band_ref, far_ref, mt_ref,
                         sk_ref, svt_ref, wk_ref, wvt_ref, dl_ref, o_ref,
                         qa_ref, m_ref, acc_ref, out_ref, cb_ref, *, n_cmp):
    tq = NSA_TQ
    cols = NSA_GROUP * tq
    g = pl.program_id(1)
    qi = pl.program_id(2)
    lane = lax.broadcasted_iota(jnp.int32, (tq, LANES), 1)
    lo = lane < HEAD_DIM

    qn = []
    for k in range(NSA_GROUP // 2):
        y = _half_rms(q_ref[:, k * LANES:(k + 1) * LANES], gq_ref[...], lo) * HEAD_DIM ** -0.5
        qn.append(jnp.where(lo, y, 0.0))
        qn.append(jnp.where(lo, pltpu.roll(y, HEAD_DIM, 1), 0.0))
    for z in range(NSA_GROUP):
        qa_ref[z * tq:(z + 1) * tq, :] = qn[z].astype(BF16)

    sig_t = jax.nn.sigmoid(gate_ref[...] + bg_ref[...]).T

    def gate_row(br):
        return jnp.concatenate([sig_t[br * NSA_GROUP + z:br * NSA_GROUP + z + 1, :] for z in range(NSA_GROUP)], axis=1)

    r_cb = lax.broadcasted_iota(jnp.int32, (LANES + CMP_BAND, cols), 0)
    cb_ref[...] = jnp.where(r_cb < (CMP_BAND // 2) * qi, far_ref[...], MASK)
    cb_ref[pl.ds(pl.multiple_of((CMP_BAND // 2) * qi, CMP_BAND // 2), CMP_BAND), :] = band_ref[...]
    nidx = lax.broadcasted_iota(jnp.int32, (LANES, cols), 0)
    cbias = jnp.where(nidx >= n_cmp, MASK, cb_ref[CMP_BAND // 2:CMP_BAND // 2 + LANES, :])
    s = _dot_nt(ck_ref[...], qa_ref[...]) + cbias
    m = jnp.maximum(jnp.max(s, axis=0, keepdims=True), NEG)
    p = jnp.exp(s - m)
    l = jnp.sum(p, axis=0, keepdims=True)
    pn = p * jnp.where(l > 0.0, 1.0 / l, 0.0)
    o_cmp = jnp.dot(cvt_ref[...], pn.astype(BF16), preferred_element_type=F32)
    out_ref[...] = gate_row(0) * o_cmp

    psum = pn[:, 0:tq]
    for z in range(1, NSA_GROUP):
        psum = psum + pn[:, z * tq:(z + 1) * tq]
    p_hi = psum.astype(BF16)
    p_lo = (psum - p_hi.astype(F32)).astype(BF16)
    imp = (jnp.dot(mt_ref[...], p_hi, preferred_element_type=F32)
           + jnp.dot(mt_ref[...], p_lo, preferred_element_type=F32))
    jidx = lax.broadcasted_iota(jnp.int32, (N_SEL_LANES, tq), 0)
    tpos = qi * tq + lax.broadcasted_iota(jnp.int32, (N_SEL_LANES, tq), 1)
    cur = tpos // SLC_BLOCK
    forced = (jidx == 0) | (jidx == cur) | (jidx == cur - 1)
    score = jnp.where(forced, FORCED, jnp.where(jidx <= cur, imp, -1.0))
    rank = jnp.zeros((N_SEL_LANES, tq), F32)
    for i in range(N_SEL_LANES):
        ri = score[i:i + 1, :]
        beats = (ri > score) | ((ri == score) & (jidx > i))
        rank = rank + jnp.where(beats, 1.0, 0.0)
    selb = jnp.where(rank < N_SELECT, 0.0, MASK)
    selb = jnp.concatenate([selb, jnp.zeros((LANES - N_SEL_LANES, tq), F32)], axis=0)
    sel_cols = pltpu.roll(selb.T, SEL_LANE, 1)

    for z in range(NSA_GROUP):
        c = jnp.full((tq, LANES), rel_ref[REL_BUCKETS - 1, g * NSA_GROUP + z], F32)
        c_hi = c.astype(BF16).astype(F32)
        extra = jnp.where(lane == CONST_LANE, c_hi, jnp.where(lane == CONST_LANE + 1, c - c_hi, sel_cols))
        qa_ref[z * tq:(z + 1) * tq, :] = jnp.where(lo, qn[z], extra).astype(BF16)

    def flash_init():
        m_ref[...] = jnp.full((1, cols), NEG, F32)
        acc_ref[...] = jnp.zeros((VT_ROWS, cols), F32)

    def flash_block(k_ref, vt_ref, kb, didx):
        start = pl.multiple_of(kb * tq, tq)
        s = _dot_nt(k_ref[pl.ds(start, tq), :], qa_ref[...]) + dl_ref[didx]
        m_prev = m_ref[...]
        m_next = jnp.maximum(m_prev, jnp.max(s, axis=0, keepdims=True))
        p = jnp.exp(s - m_next).astype(BF16)
        acc_ref[...] = (jnp.exp(m_prev - m_next) * acc_ref[...]
                        + jnp.dot(vt_ref[:, pl.ds(start, tq)], p, preferred_element_type=F32))
        m_ref[...] = m_next

    def flash_finish(br):
        out_ref[...] += gate_row(br) * (acc_ref[:HEAD_DIM, :] / acc_ref[HEAD_DIM:HEAD_DIM + 1, :])

    flash_init()

    def slc_body(kb, carry):
        didx = jnp.where(kb == qi, 2, jnp.where(kb == qi - 1, 1, 0))
        flash_block(sk_ref, svt_ref, kb, didx)
        return carry

    lax.fori_loop(0, qi + 1, slc_body, 0)
    flash_finish(1)

    flash_init()
    n_win = WINDOW // tq

    def win_body(kb, carry):
        d = qi - kb
        didx = jnp.where(d == 0, 2, jnp.where(d == 1, 1, jnp.where(d == n_win, 3, 0)))
        flash_block(wk_ref, wvt_ref, kb, didx)
        return carry

    lax.fori_loop(jnp.maximum(qi - n_win, 0), qi + 1, win_body, 0)
    flash_finish(2)

    for k in range(NSA_GROUP // 2):
        o_ref[:, k * LANES:(k + 1) * LANES] = jnp.concatenate(
            [out_ref[:, (2 * k) * tq:(2 * k + 1) * tq], out_ref[:, (2 * k + 1) * tq:(2 * k + 2) * tq]], axis=0).T


LOG2E = 1.0 / math.log(2.0)
FAR_BLOCKS = 4
NEAR_BLOCKS = 5
FULL_MASK_TILE = 4


def _nsa_prompt_kernel2(rel_ref, q_ref, gate_ref, gq_ref, bg_ref, ck_ref, cvt_ref, band_ref, far_ref, mt_ref,
                        ka_ref, vat_ref, dl_ref, o_ref, qa_ref, m_ref, acc_ref, cb_ref, *, n_cmp, n_tiles):
    tq = NSA_TQ
    cols = NSA_GROUP * tq
    qi = pl.program_id(1)
    lane = lax.broadcasted_iota(jnp.int32, (tq, LANES), 1)
    lo = lane < HEAD_DIM
    nidx = lax.broadcasted_iota(jnp.int32, (LANES, cols), 0)
    r_cb = lax.broadcasted_iota(jnp.int32, (LANES + CMP_BAND, cols), 0)
    jidx = lax.broadcasted_iota(jnp.int32, (N_SEL_LANES, tq), 0)
    tpos = qi * tq + lax.broadcasted_iota(jnp.int32, (N_SEL_LANES, tq), 1)
    cur = tpos // SLC_BLOCK
    forced = (jidx == 0) | (jidx == cur) | (jidx == cur - 1)
    half_band = CMP_BAND // 2

    def gate_rows(g):
        sig_t = jax.nn.sigmoid(gate_ref[:, g * LANES:(g + 1) * LANES] + bg_ref[:, g * LANES:(g + 1) * LANES]).T
        return [jnp.concatenate([sig_t[br * NSA_GROUP + z:br * NSA_GROUP + z + 1, :] for z in range(NSA_GROUP)], axis=1)
                for br in range(3)]

    def prologue(g):
        qn = []
        for k in range(NSA_GROUP // 2):
            kk = g * (NSA_GROUP // 2) + k
            y = _half_rms(q_ref[:, kk * LANES:(kk + 1) * LANES], gq_ref[...], lo) * (HEAD_DIM ** -0.5 * LOG2E)
            qn.append(jnp.where(lo, y, 0.0))
            qn.append(jnp.where(lo, pltpu.roll(y, HEAD_DIM, 1), 0.0))
        qc = jnp.concatenate(qn, axis=0).astype(BF16)

        cb_ref[g] = jnp.where(r_cb < half_band * qi, far_ref[g], MASK)
        cb_ref[g, pl.ds(pl.multiple_of(half_band * qi, half_band), CMP_BAND), :] = band_ref[g]
        cbias = jnp.where(nidx >= n_cmp, MASK, cb_ref[g, half_band:half_band + LANES, :])
        s = _dot_nt(ck_ref[g], qc) + cbias
        m = jnp.maximum(jnp.max(s, axis=0, keepdims=True), NEG)
        p = jnp.exp2(s - m)
        l = jnp.sum(p, axis=0, keepdims=True)
        pn = p * jnp.where(l > 0.0, 1.0 / l, 0.0)
        o_cmp = jnp.dot(cvt_ref[g], pn.astype(BF16), preferred_element_type=F32)

        psum = pn[:, 0:tq]
        for z in range(1, NSA_GROUP):
            psum = psum + pn[:, z * tq:(z + 1) * tq]
        p_hi = psum.astype(BF16)
        p_lo = (psum - p_hi.astype(F32)).astype(BF16)
        imp = (jnp.dot(mt_ref[...], p_hi, preferred_element_type=F32)
               + jnp.dot(mt_ref[...], p_lo, preferred_element_type=F32))
        score = jnp.where(forced, FORCED, jnp.where(jidx <= cur, imp, -1.0))
        rank = jnp.zeros((N_SEL_LANES, tq), F32)
        for i in range(N_SEL_LANES):
            ri = score[i:i + 1, :]
            beats = (ri > score) | ((ri == score) & (jidx > i))
            rank = rank + jnp.where(beats, 1.0, 0.0)
        selb = jnp.where((rank < N_SELECT) & (jidx <= cur), 0.0, MASK)
        selb = jnp.concatenate([selb, jnp.zeros((LANES - N_SEL_LANES, tq), F32)], axis=0)
        sel_cols = pltpu.roll(selb.T, SEL_LANE, 1)

        for z in range(NSA_GROUP):
            c = jnp.full((tq, LANES), rel_ref[REL_BUCKETS - 1, g * NSA_GROUP + z], F32)
            c_hi = c.astype(BF16).astype(F32)
            extra = jnp.where(lane == CONST_LANE, c_hi, jnp.where(lane == CONST_LANE + 1, c - c_hi, sel_cols))
            qa_ref[g, z * tq:(z + 1) * tq, :] = jnp.where(lo, qn[z], extra).astype(BF16)
        return o_cmp

    o_cmp = [prologue(g) for g in range(NSA_KV_HEADS)]

    n_far = jnp.maximum(qi - 1, 0) // FAR_BLOCKS
    far_rows = FAR_BLOCKS * tq
    m_ref[...] = jnp.full(m_ref.shape, NEG, F32)
    acc_ref[...] = jnp.zeros(acc_ref.shape, F32)

    def far_body(c, carry):
        start = pl.multiple_of(c * far_rows, far_rows)
        for g in range(NSA_KV_HEADS):
            s = _dot_nt(ka_ref[g, pl.ds(start, far_rows), :], qa_ref[g])
            m_prev = m_ref[g]
            m_next = jnp.maximum(m_prev, jnp.max(s, axis=0, keepdims=True))
            p = jnp.exp2(s - m_next).astype(BF16)
            acc_ref[g] = (jnp.exp2(m_prev - m_next) * acc_ref[g]
                          + jnp.dot(vat_ref[g, :, pl.ds(start, far_rows)], p, preferred_element_type=F32))
            m_ref[g] = m_next
        return carry

    lax.fori_loop(0, n_far, far_body, 0)

    def near_stage(idx, g, start_blk, tile_of, m_prev, acc_prev):
        start = pl.multiple_of(start_blk * tq, tq)
        s = _dot_nt(ka_ref[idx, pl.ds(start, NEAR_BLOCKS * tq), :], qa_ref[g])
        s = s + jnp.concatenate([dl_ref[tile_of(start_blk + i), g] for i in range(NEAR_BLOCKS)], axis=0)
        m_next = jnp.max(s, axis=0, keepdims=True)
        if m_prev is not None:
            m_next = jnp.maximum(m_prev, m_next)
        acc = jnp.dot(vat_ref[idx, :, pl.ds(start, NEAR_BLOCKS * tq)], jnp.exp2(s - m_next).astype(BF16),
                      preferred_element_type=F32)
        if acc_prev is not None:
            acc = acc + jnp.exp2(m_prev - m_next) * acc_prev
        return acc[:HEAD_DIM, :] / acc[HEAD_DIM:HEAD_DIM + 1, :]

    far_cov = FAR_BLOCKS * n_far
    slc_start = jnp.minimum(far_cov, n_tiles - NEAR_BLOCKS)
    win_start = jnp.maximum(qi - WINDOW // tq, 0)

    def slc_tile(j):
        d = qi - j
        return jnp.where((d < 0) | (j < far_cov), FULL_MASK_TILE, jnp.where(d == 0, 2, jnp.where(d == 1, 1, 0)))

    def win_tile(j):
        d = qi - j
        return jnp.where(d < 0, FULL_MASK_TILE,
                         jnp.where(d == 0, 2, jnp.where(d == 1, 1, jnp.where(d == WINDOW // tq, 3, 0))))

    outs = []
    for g in range(NSA_KV_HEADS):
        gr = gate_rows(g)
        o_slc = near_stage(g, g, slc_start, slc_tile, m_ref[g], acc_ref[g])
        o_win = near_stage(NSA_KV_HEADS + g, g, win_start, win_tile, None, None)
        outs.append(gr[0] * o_cmp[g] + gr[1] * o_slc + gr[2] * o_win)

    for g in range(NSA_KV_HEADS):
        for k in range(NSA_GROUP // 2):
            kk = g * (NSA_GROUP // 2) + k
            o_ref[:, kk * LANES:(kk + 1) * LANES] = jnp.concatenate(
                [outs[g][:, (2 * k) * tq:(2 * k + 1) * tq], outs[g][:, (2 * k + 1) * tq:(2 * k + 2) * tq]], axis=0).T


def nsa_prompt2(p, b, s, g_q, b_gate, rel_table, ck, cvt, ka, vat, tables):
    tq = NSA_TQ
    nq = s // tq
    cols = NSA_GROUP * tq
    n_cmp = (s - CMP_LEN) // CMP_STRIDE + 1
    n_slc = -(-s // SLC_BLOCK)
    assert s % tq == 0 and nq >= NEAR_BLOCKS and n_cmp <= LANES and n_slc <= N_SEL_LANES and tq == 8 * CMP_STRIDE
    tiles, band, far_row = tables
    tiles = jnp.concatenate([tiles, jnp.full((1,) + tiles.shape[1:], MASK, F32)], axis=0)
    mt = np.zeros((N_SEL_LANES, LANES), np.float32)
    mt[:n_slc, :n_cmp] = _overlap_matrix(n_cmp, n_slc).T
    bg = _gate_bias_slabs(b_gate).reshape(1, NSA_KV_HEADS * LANES)
    gcol = (MIX_W + MEM_W) // (NSA_KV_HEADS * LANES)
    ka4 = ka.reshape(2 * NSA_KV_HEADS, b, s, LANES)
    c2 = lambda bi, qi: (0, 0)
    c3 = lambda bi, qi: (0, 0, 0)
    return pl.pallas_call(
        partial(_nsa_prompt_kernel2, n_cmp=n_cmp, n_tiles=nq),
        grid=(b, nq),
        in_specs=[pl.BlockSpec(memory_space=pltpu.SMEM),
                  pl.BlockSpec((tq, MIX_W), lambda bi, qi: (bi * nq + qi, 0)),
                  pl.BlockSpec((tq, NSA_KV_HEADS * LANES), lambda bi, qi: (bi * nq + qi, gcol)),
                  pl.BlockSpec((1, LANES), c2),
                  pl.BlockSpec((1, NSA_KV_HEADS * LANES), c2),
                  pl.BlockSpec((None, NSA_KV_HEADS, LANES, LANES), lambda bi, qi: (bi, 0, 0, 0)),
                  pl.BlockSpec((None, NSA_KV_HEADS, HEAD_DIM, LANES), lambda bi, qi: (bi, 0, 0, 0)),
                  pl.BlockSpec((NSA_KV_HEADS, CMP_BAND, cols), c3),
                  pl.BlockSpec((NSA_KV_HEADS, 1, cols), c3),
                  pl.BlockSpec((N_SEL_LANES, LANES), c2),
                  pl.BlockSpec((2 * NSA_KV_HEADS, None, s, LANES), lambda bi, qi: (0, bi, 0, 0)),
                  pl.BlockSpec((2 * NSA_KV_HEADS, VT_ROWS, s), lambda bi, qi: (0, 0, bi)),
                  pl.BlockSpec((FULL_MASK_TILE + 1, NSA_KV_HEADS, tq, cols), lambda bi, qi: (0, 0, 0, 0))],
        out_specs=pl.BlockSpec((tq, MIX_W), lambda bi, qi: (bi * nq + qi, 0)),
        out_shape=jax.ShapeDtypeStruct((b * s, MIX_W), F32),
        scratch_shapes=[pltpu.VMEM((NSA_KV_HEADS, cols, LANES), BF16), pltpu.VMEM((NSA_KV_HEADS, 1, cols), F32),
                        pltpu.VMEM((NSA_KV_HEADS, VT_ROWS, cols), F32),
                        pltpu.VMEM((NSA_KV_HEADS, LANES + CMP_BAND, cols), F32)],
        compiler_params=_params("parallel", "arbitrary"),
        name="nsa_prompt",
    )(rel_table.astype(F32), p, p, _gain2(g_q), bg, ck, cvt, band, far_row, jnp.asarray(mt, BF16), ka4, vat, tiles)


def nsa_bias_tables(rel_table):
    tq = NSA_TQ
    table = rel_table.astype(F32)
    far = table[REL_BUCKETS - 1]
    r = np.arange(tq)[None, :]
    c = np.arange(tq)[:, None]

    def by_group(t):
        k = t.shape[0]
        return jnp.transpose(t.reshape(k, tq, NSA_KV_HEADS, NSA_GROUP), (2, 0, 3, 1)).reshape(NSA_KV_HEADS, k, -1)

    def tile(off, masked):
        d = off + np.arange(-(tq - 1), tq)
        v = jnp.where(jnp.asarray(masked(d))[:, None], MASK, table[_rel_bucket_np(d)] - far)
        return by_group(jnp.stack([v[tq - 1 - k:2 * tq - 1 - k] for k in range(tq)]))

    zero = jnp.zeros((NSA_KV_HEADS, tq, NSA_GROUP * tq), F32)
    tiles = jnp.stack([zero, tile(tq, lambda d: d < 0), tile(0, lambda d: d < 0),
                       tile(WINDOW, lambda d: d >= WINDOW)])
    n_rel = np.arange(CMP_BAND)[:, None] - CMP_BAND // 2
    dist = r - CMP_STRIDE * n_rel - (CMP_LEN - 1)
    band = by_group(jnp.where(jnp.asarray(dist < 0)[..., None], MASK, table[_rel_bucket_np(dist)]))
    far_row = by_group(jnp.broadcast_to(far, (1, tq, NSA_HEADS)))
    return tiles, band, far_row


def nsa_prompt_t(p, b, s, g_q, b_gate, rel_table, ck, cvt, ka, vat, tables):
    tq = NSA_TQ
    nq = s // tq
    cols = NSA_GROUP * tq
    n_cmp = (s - CMP_LEN) // CMP_STRIDE + 1
    n_slc = -(-s // SLC_BLOCK)
    assert s % tq == 0 and n_cmp <= LANES and n_slc <= N_SEL_LANES and tq == 8 * CMP_STRIDE
    tiles, band, far_row = tables
    mt = np.zeros((N_SEL_LANES, LANES), np.float32)
    mt[:n_slc, :n_cmp] = _overlap_matrix(n_cmp, n_slc).T
    bg = _gate_bias_slabs(b_gate).reshape(NSA_KV_HEADS, 1, LANES)
    qcol = (MIX_W + MEM_W) // LANES
    ka4 = ka.reshape(2 * NSA_KV_HEADS, b, s, LANES)
    kmap = lambda off: (lambda bi, g, qi: (off + g, bi, 0, 0))
    vmap = lambda off: (lambda bi, g, qi: (off + g, 0, bi))
    gmap = lambda bi, g, qi: (g, 0, 0)
    return pl.pallas_call(
        partial(_nsa_prompt_t_kernel, n_cmp=n_cmp),
        grid=(b, NSA_KV_HEADS, nq),
        in_specs=[pl.BlockSpec(memory_space=pltpu.SMEM),
                  pl.BlockSpec((tq, NSA_GROUP * HEAD_DIM), lambda bi, g, qi: (bi * nq + qi, g)),
                  pl.BlockSpec((tq, LANES), lambda bi, g, qi: (bi * nq + qi, qcol + g)),
                  pl.BlockSpec((1, LANES), lambda bi, g, qi: (0, 0)),
                  pl.BlockSpec((None, 1, LANES), gmap),
                  pl.BlockSpec((None, None, LANES, LANES), lambda bi, g, qi: (bi, g, 0, 0)),
                  pl.BlockSpec((None, None, HEAD_DIM, LANES), lambda bi, g, qi: (bi, g, 0, 0)),
                  pl.BlockSpec((None, CMP_BAND, cols), gmap),
                  pl.BlockSpec((None, 1, cols), gmap),
                  pl.BlockSpec((N_SEL_LANES, LANES), lambda bi, g, qi: (0, 0)),
                  pl.BlockSpec((None, None, s, LANES), kmap(0)),
                  pl.BlockSpec((None, VT_ROWS, s), vmap(0)),
                  pl.BlockSpec((None, None, s, LANES), kmap(NSA_KV_HEADS)),
                  pl.BlockSpec((None, VT_ROWS, s), vmap(NSA_KV_HEADS)),
                  pl.BlockSpec((4, None, tq, cols), lambda bi, g, qi: (0, g, 0, 0))],
        out_specs=pl.BlockSpec((tq, NSA_GROUP * HEAD_DIM), lambda bi, g, qi: (bi * nq + qi, g)),
        out_shape=jax.ShapeDtypeStruct((b * s, MIX_W), F32),
        scratch_shapes=[pltpu.VMEM((cols, LANES), BF16), pltpu.VMEM((1, cols), F32), pltpu.VMEM((VT_ROWS, cols), F32),
                        pltpu.VMEM((HEAD_DIM, cols), F32), pltpu.VMEM((LANES + CMP_BAND, cols), F32)],
        compiler_params=_params("parallel", "parallel", "arbitrary"),
        name="nsa_prompt",
    )(rel_table.astype(F32), p, p, _gain2(g_q), bg, ck, cvt, band, far_row, jnp.asarray(mt, BF16),
      ka4, vat, ka4, vat, tiles)


def nsa_bias_tiles(rel_table, s):
    tq = NSA_TQ
    table = rel_table.astype(F32)
    far = table[REL_BUCKETS - 1]
    r = np.arange(tq)[:, None]
    c = np.arange(tq)[None, :]

    def tile(off, masked):
        dist = off + r - c
        t = jnp.transpose(table[_rel_bucket_np(dist)], (2, 0, 1)) - far[:, None, None]
        return jnp.where(jnp.asarray(masked(dist))[None], MASK, t)

    zero = jnp.zeros((NSA_HEADS, tq, tq), F32)
    d128 = tile(tq, lambda d: d < 0)
    d0 = tile(0, lambda d: d < 0)
    dwin = tile(WINDOW, lambda d: d >= WINDOW)
    tiles = jnp.stack([zero, d128, d0, dwin])
    n_cmp = (s - CMP_LEN) // CMP_STRIDE + 1
    c_end = np.arange(LANES) * CMP_STRIDE + CMP_LEN - 1
    dist = np.arange(s)[:, None] - c_end[None, :]
    cb = jnp.transpose(table[_rel_bucket_np(dist)], (2, 0, 1))
    bad = (dist < 0) | (np.arange(LANES)[None, :] >= n_cmp)
    cb = jnp.where(jnp.asarray(bad)[None], MASK, cb)
    return tiles, cb


def _gate_bias_slabs(b_gate):
    bg = b_gate.astype(F32).reshape(3, NSA_KV_HEADS, NSA_GROUP).transpose(1, 0, 2).reshape(NSA_KV_HEADS, 3 * NSA_GROUP)
    return _pad_cols(bg, LANES)


def nsa_prompt(p, b, s, g_q, b_gate, rel_table, ck, cv, ka, va, tiles, cb):
    tq = NSA_TQ
    nq = s // tq
    rows = NSA_GROUP * tq
    n_cmp = (s - CMP_LEN) // CMP_STRIDE + 1
    n_slc = -(-s // SLC_BLOCK)
    assert s % tq == 0 and n_cmp <= LANES and n_slc <= N_SEL_LANES
    mt = np.zeros((N_SEL_LANES, LANES), np.float32)
    mt[:n_slc, :n_cmp] = _overlap_matrix(n_cmp, n_slc).T
    bg = _gate_bias_slabs(b_gate).reshape(NSA_KV_HEADS, 1, LANES)
    qcol = (MIX_W + MEM_W) // LANES
    kmap = lambda off: (lambda bi, g, qi: (off + g, bi, 0, 0))
    return pl.pallas_call(
        _nsa_prompt_kernel,
        grid=(b, NSA_KV_HEADS, nq),
        in_specs=[pl.BlockSpec(memory_space=pltpu.SMEM),
                  pl.BlockSpec((tq, NSA_GROUP * HEAD_DIM), lambda bi, g, qi: (bi * nq + qi, g)),
                  pl.BlockSpec((tq, LANES), lambda bi, g, qi: (bi * nq + qi, qcol + g)),
                  pl.BlockSpec((1, LANES), lambda bi, g, qi: (0, 0)),
                  pl.BlockSpec((None, 1, LANES), lambda bi, g, qi: (g, 0, 0)),
                  pl.BlockSpec((None, None, LANES, LANES), lambda bi, g, qi: (bi, g, 0, 0)),
                  pl.BlockSpec((None, None, LANES, LANES), lambda bi, g, qi: (bi, g, 0, 0)),
                  pl.BlockSpec((NSA_GROUP, tq, LANES), lambda bi, g, qi: (g, qi, 0)),
                  pl.BlockSpec((N_SEL_LANES, LANES), lambda bi, g, qi: (0, 0)),
                  pl.BlockSpec((None, None, s, LANES), kmap(0)),
                  pl.BlockSpec((None, None, s, LANES), kmap(0)),
                  pl.BlockSpec((None, None, s, LANES), kmap(NSA_KV_HEADS)),
                  pl.BlockSpec((None, None, s, LANES), kmap(NSA_KV_HEADS)),
                  pl.BlockSpec((4, NSA_GROUP, tq, tq), lambda bi, g, qi: (0, g, 0, 0))],
        out_specs=pl.BlockSpec((tq, NSA_GROUP * HEAD_DIM), lambda bi, g, qi: (bi * nq + qi, g)),
        out_shape=jax.ShapeDtypeStruct((b * s, MIX_W), F32),
        scratch_shapes=[pltpu.VMEM((rows, LANES), BF16)] + [pltpu.VMEM((rows, LANES), F32)] * 4,
        compiler_params=_params("parallel", "parallel", "arbitrary"),
        name="nsa_prompt",
    )(rel_table.astype(F32), p, p, _gain2(g_q), bg, ck, cv, cb, jnp.asarray(mt, BF16),
      ka.reshape(4, b, s, LANES), va.reshape(4, b, s, LANES),
      ka.reshape(4, b, s, LANES), va.reshape(4, b, s, LANES), tiles)


def split_cmp_ctx(ctx):
    b, n, _ = ctx.shape
    parts = ctx.reshape(b, n, 2, NSA_KV_HEADS, HEAD_DIM).transpose(2, 0, 3, 1, 4)
    parts = jnp.pad(parts, ((0, 0), (0, 0), (0, 0), (0, LANES - n), (0, 0)))
    ck = jnp.pad(parts[0], ((0, 0), (0, 0), (0, 0), (0, LANES - HEAD_DIM)))
    return ck, jnp.swapaxes(parts[1], -1, -2)


def _softmax_parts(s, s_new):
    m = jnp.maximum(jnp.max(s, axis=1, keepdims=True), s_new)
    p = jnp.exp(s - m)
    p_new = jnp.exp(s_new - m)
    inv = 1.0 / (jnp.sum(p, axis=1, keepdims=True) + p_new)
    return p * inv, p_new * inv


DECODE_SEQS = 2


def _nsa_decode_kernel(*refs, n_pages):
    (pt_ref, p_ref, gq_ref, bg_ref, cmp_ref, cb_ref, mt_ref) = refs[:7]
    n_in = DECODE_SEQS * n_pages
    page_refs = refs[7:7 + n_in]
    (snew_ref, sb_ref, e_ref, win_ref, wnew_ref, wb_ref, b0_ref, o_ref) = refs[7 + n_in:]
    del pt_ref
    for s in range(DECODE_SEQS):
        _nsa_decode_seq(p_ref.at[s], gq_ref, bg_ref, cmp_ref.at[s], cb_ref, mt_ref,
                        page_refs[s * n_pages:(s + 1) * n_pages], snew_ref.at[s], sb_ref, e_ref, win_ref.at[s],
                        wnew_ref.at[s], wb_ref, b0_ref, o_ref.at[s])


def _nsa_decode_seq(p_ref, gq_ref, bg_ref, cmp_ref, cb_ref, mt_ref, page_refs, snew_ref, sb_ref, e_ref, win_ref,
                    wnew_ref, wb_ref, b0_ref, o_ref):
    n_pages = len(page_refs)
    n_keys = n_pages * PAGE_SIZE
    lo1 = lax.broadcasted_iota(jnp.int32, (1, LANES), 1) < HEAD_DIM
    rowi = lax.broadcasted_iota(jnp.int32, (QROWS, LANES), 0)
    lanei = lax.broadcasted_iota(jnp.int32, (QROWS, LANES), 1)
    row_g = rowi // 8
    own = (lanei // HEAD_DIM) == row_g

    qbd = jnp.zeros((QROWS, LANES), F32)
    for k in range(NSA_HEADS // 2):
        y = _half_rms(p_ref[:, k * LANES:(k + 1) * LANES], gq_ref[...], lo1) * HEAD_DIM ** -0.5
        yr = pltpu.roll(y, HEAD_DIM, 1)
        for half in range(2):
            g, z = divmod(2 * k + half, NSA_GROUP)
            qbd = jnp.where(rowi == 8 * g + z, y if half == g else yr, qbd)
    qbd = jnp.where(own, qbd, 0.0)
    qb = qbd.astype(BF16)

    s = _dot_nt(qb, cmp_ref[:, :LANES]) + cb_ref[...]
    m = jnp.maximum(jnp.max(s, axis=1, keepdims=True), NEG)
    p = jnp.exp(s - m)
    l = jnp.sum(p, axis=1, keepdims=True)
    pn = p * jnp.where(l > 0.0, 1.0 / l, 0.0)
    o_cmp = jnp.dot(pn.astype(BF16), cmp_ref[:, LANES:], preferred_element_type=F32)

    valid_row = (rowi % 8) < NSA_GROUP
    ps0 = jnp.sum(jnp.where(valid_row & (row_g == 0), pn, 0.0), axis=0, keepdims=True)
    ps1 = jnp.sum(jnp.where(valid_row & (row_g == 1), pn, 0.0), axis=0, keepdims=True)
    r8 = lax.broadcasted_iota(jnp.int32, (8, LANES), 0)
    j8 = lax.broadcasted_iota(jnp.int32, (8, LANES), 1)
    psum = jnp.where(r8 == 0, ps0, jnp.where(r8 == 1, ps1, 0.0))
    p_hi = psum.astype(BF16)
    p_lo = (psum - p_hi.astype(F32)).astype(BF16)
    imp = (jnp.dot(p_hi, mt_ref[...], preferred_element_type=F32)
           + jnp.dot(p_lo, mt_ref[...], preferred_element_type=F32))
    cur = n_keys // SLC_BLOCK
    forced = (j8 == 0) | (j8 == cur) | (j8 == cur - 1)
    score = jnp.where(forced, FORCED, jnp.where(j8 <= cur, imp, -1.0))
    rank = jnp.zeros((8, LANES), F32)
    for i in range(cur + 1):
        ri = score[:, i:i + 1]
        beats = (ri > score) | ((ri == score) & (j8 > i))
        rank = rank + jnp.where(beats, 1.0, 0.0)
    selb = jnp.where(rank < N_SELECT, 0.0, MASK)
    sel16 = jnp.where(row_g == 0, selb[0:1, :], selb[1:2, :]).astype(BF16)
    blockmask = jnp.dot(sel16, e_ref[...], preferred_element_type=F32)

    s = jnp.concatenate([_dot_nt(qb, r[:, :LANES].astype(BF16)) for r in page_refs], axis=1)
    s = s + sb_ref[...] + blockmask
    s_new = jnp.sum(qbd * snew_ref[:, :LANES], axis=1, keepdims=True) + b0_ref[...]
    pp, p_new = _softmax_parts(s, s_new)
    o_slc = p_new * snew_ref[:, LANES:]
    for k, r in enumerate(page_refs):
        o_slc = o_slc + jnp.dot(pp[:, k * PAGE_SIZE:(k + 1) * PAGE_SIZE].astype(BF16), r[:, LANES:].astype(BF16),
                                preferred_element_type=F32)

    n_wt = win_ref.shape[0] // LANES
    s = jnp.concatenate([_dot_nt(qb, win_ref[k * LANES:(k + 1) * LANES, :LANES].astype(BF16)) for k in range(n_wt)],
                        axis=1) + wb_ref[...]
    s_new = jnp.sum(qbd * wnew_ref[:, :LANES], axis=1, keepdims=True) + b0_ref[...]
    pp, p_new = _softmax_parts(s, s_new)
    o_win = p_new * wnew_ref[:, LANES:]
    for k in range(n_wt):
        o_win = o_win + jnp.dot(pp[:, k * LANES:(k + 1) * LANES].astype(BF16),
                                win_ref[k * LANES:(k + 1) * LANES, LANES:].astype(BF16), preferred_element_type=F32)

    gbase = MIX_W + MEM_W
    sig = jax.nn.sigmoid(p_ref[:, gbase:gbase + NSA_KV_HEADS * LANES] + bg_ref[...])
    gate = [jnp.zeros((QROWS, 1), F32) for _ in range(3)]
    rcol = lax.broadcasted_iota(jnp.int32, (QROWS, 1), 0)
    for br in range(3):
        for g in range(NSA_KV_HEADS):
            for z in range(NSA_GROUP):
                c = g * LANES + br * NSA_GROUP + z
                gate[br] = jnp.where(rcol == 8 * g + z, sig[:, c:c + 1], gate[br])
    comb = jnp.where(own, gate[0] * o_cmp + gate[1] * o_slc + gate[2] * o_win, 0.0)
    comb_r = pltpu.roll(comb, HEAD_DIM, 1)
    for k in range(NSA_HEADS // 2):
        acc = jnp.zeros((1, LANES), F32)
        for half in range(2):
            g, z = divmod(2 * k + half, NSA_GROUP)
            src = comb if half == g else comb_r
            acc = acc + src[8 * g + z:8 * g + z + 1, :]
        o_ref[:, k * LANES:(k + 1) * LANES] = acc


def nsa_decode_bias(rel_table, past, wb):
    table = rel_table.astype(F32)

    def rows16(t):
        z = jnp.zeros((2, t.shape[1]), F32)
        return jnp.concatenate([t[:NSA_GROUP], z, t[NSA_GROUP:], z], axis=0)

    n_cmp = (past + 1 - CMP_LEN) // CMP_STRIDE + 1
    c_end = np.arange(LANES) * CMP_STRIDE + CMP_LEN - 1
    dist = past - c_end
    cb = jnp.where(jnp.asarray((dist < 0) | (np.arange(LANES) >= n_cmp))[None], MASK, table[_rel_bucket_np(dist)].T)
    sb = table[_rel_bucket_np(past - np.arange(past))].T
    wpos = past - wb + np.arange(wb)
    wbias = jnp.where(jnp.asarray(past - wpos >= WINDOW)[None], MASK, table[_rel_bucket_np(past - wpos)].T)
    b0 = table[0][:, None]
    return rows16(cb), rows16(sb), rows16(wbias), rows16(b0)


def nsa_decode(p, g_q, b_gate, cmp_ctx, page_table, cache_slc, slc_new, cache_win, win_new, biases):
    b, n_pages = page_table.shape
    past = n_pages * PAGE_SIZE
    wb = cache_win.shape[1]
    cb, sb, wbias, b0 = biases
    n_cmp = (past + 1 - CMP_LEN) // CMP_STRIDE + 1
    n_slc = -(-(past + 1) // SLC_BLOCK)
    assert n_slc <= LANES and n_cmp <= LANES and wb % LANES == 0
    mt = np.zeros((LANES, LANES), np.float32)
    mt[:n_cmp, :n_slc] = _overlap_matrix(n_cmp, n_slc)
    e = (np.arange(past)[None, :] // SLC_BLOCK == np.arange(LANES)[:, None]).astype(np.float32)
    bg = _gate_bias_slabs(b_gate).reshape(1, NSA_KV_HEADS * LANES)
    ns = DECODE_SEQS
    assert b % ns == 0
    c2 = lambda bi, pt: (0, 0)
    seq3 = lambda bi, pt: (bi, 0, 0)
    in_specs = ([pl.BlockSpec((ns, 1, p.shape[1]), seq3),
                 pl.BlockSpec((1, LANES), c2),
                 pl.BlockSpec((1, NSA_KV_HEADS * LANES), c2),
                 pl.BlockSpec((ns, LANES, KV_ROW), seq3),
                 pl.BlockSpec((QROWS, LANES), c2),
                 pl.BlockSpec((LANES, LANES), c2)]
                + [pl.BlockSpec((None, PAGE_SIZE, KV_ROW), partial(lambda s, k, bi, pt: (pt[ns * bi + s, k], 0, 0), s, k))
                   for s in range(ns) for k in range(n_pages)]
                + [pl.BlockSpec((ns, 1, KV_ROW), seq3),
                   pl.BlockSpec((QROWS, past), c2),
                   pl.BlockSpec((LANES, past), c2),
                   pl.BlockSpec((ns, wb, KV_ROW), seq3),
                   pl.BlockSpec((ns, 1, KV_ROW), seq3),
                   pl.BlockSpec((QROWS, wb), c2),
                   pl.BlockSpec((QROWS, 1), c2)])
    gs = pltpu.PrefetchScalarGridSpec(num_scalar_prefetch=1, grid=(b // ns,), in_specs=in_specs,
                                      out_specs=pl.BlockSpec((ns, 1, MIX_W), seq3))
    out = pl.pallas_call(
        partial(_nsa_decode_kernel, n_pages=n_pages),
        grid_spec=gs,
        out_shape=jax.ShapeDtypeStruct((b, 1, MIX_W), F32),
        compiler_params=_params("parallel"),
        name="nsa_decode",
    )(page_table, p.reshape(b, 1, -1), _gain2(g_q), bg, cmp_ctx, cb, jnp.asarray(mt, BF16),
      *([cache_slc] * (ns * n_pages)), slc_new.reshape(b, 1, KV_ROW), sb, jnp.asarray(e, BF16),
      cache_win, win_new.reshape(b, 1, KV_ROW), wbias, b0)
    return out.reshape(b, MIX_W)


GATE_PAD = LANES


def _pad_heads_cols(w):
    d = w.shape[0]
    return jnp.pad(w.reshape(d, MLSTM_HEADS, MLSTM_DH), ((0, 0), (0, 0), (0, MLSTM_PAD - MLSTM_DH))).reshape(d, MLSTM_HW)


def prep_weights(w_in_a, w_in_b, w_kv, w_mem_kv, w_out, w_gu, w_down):
    w = {}
    n_gate_a = 2 * MLSTM_HEADS
    n_gate_b = 3 * NSA_HEADS
    w['in_a'] = []
    for l in range(N_A_LAYERS):
        wa = w_in_a[l]
        qkvo = [_pad_heads_cols(wa[:, j * MIX_W:(j + 1) * MIX_W]) for j in range(4)]
        w['in_a'].append(jnp.concatenate(qkvo + [wa[:, 4 * MIX_W + n_gate_a:],
                                                 _pad_cols(wa[:, 4 * MIX_W:4 * MIX_W + n_gate_a], GATE_PAD)],
                                         axis=1).astype(BF16))
    w['in_b'] = []
    for j in range(N_B_LAYERS):
        wg = w_in_b[j][:, MIX_W:MIX_W + n_gate_b].reshape(D_MODEL, 3, NSA_KV_HEADS, NSA_GROUP)
        slabs = [_pad_cols(wg[:, :, g].reshape(D_MODEL, 3 * NSA_GROUP), GATE_PAD) for g in range(NSA_KV_HEADS)]
        w['in_b'].append(jnp.concatenate([w_in_b[j][:, :MIX_W], w_in_b[j][:, MIX_W + n_gate_b:]] + slabs,
                                         axis=1).astype(BF16))
    w['out_mix'] = []
    for l in range(DEPTH):
        wm = w_out[l][:MIX_W]
        if l < N_A_LAYERS:
            wm = jnp.pad(wm.reshape(MLSTM_HEADS, MLSTM_DH, D_MODEL),
                         ((0, 0), (0, MLSTM_PAD - MLSTM_DH), (0, 0))).reshape(MLSTM_HW, D_MODEL)
        w['out_mix'].append(wm.astype(BF16))
    w['out_mem'] = w_out[:, MIX_W:].astype(BF16)
    w['kv'] = w_kv.astype(BF16)
    w['mem_kv'] = w_mem_kv.astype(BF16)
    w['gu'] = w_gu.astype(BF16)
    w['down'] = w_down.astype(BF16)
    return w


def kernel(x_prompt, x_sample, state_mlstm_c, state_mlstm_n, state_mlstm_m, cache_mem_kv,
           cache_cmp_kv, cache_slc_kv, cache_win_kv, page_table, mem_prompt,
           g_mix, w_in_a, b_if, g_hnorm, w_in_b, g_q, b_gate, rel_table, g_kv, w_kv,
           g_k_slc, g_k_win, g_k_cmp, cmp_pos, cmp_w1, cmp_b1, cmp_w2, cmp_b2,
           w_mem_kv, g_mem_k, g_mem_q, w_out, g_ffn, w_gu, w_down):
    w = prep_weights(w_in_a, w_in_b, w_kv, w_mem_kv, w_out, w_gu, w_down)
    cw = compress_weights(cmp_w1, cmp_b1, cmp_w2, cmp_b2, cmp_pos, g_k_cmp)
    qcol_a = 4 * MLSTM_HW // MEM_W
    qcol_b = MIX_W // MEM_W

    def finish_layer(l, x, mix, mem_o):
        return layer_tail(mix, w['out_mix'][l], mem_o, w['out_mem'][l], x, g_ffn[l], w['gu'][l], w['down'][l])

    b_p, s_p, _ = x_prompt.shape
    m_tok = mem_prompt.shape[1]
    mem_kv_p = mem_kv_rows(mem_prompt.reshape(b_p * m_tok, D_MODEL), w['mem_kv'], g_mem_k)
    mem_kv_p = mem_kv_p.reshape(DEPTH, b_p, m_tok, 2 * MEM_W)
    rel2 = rel_table.astype(F32) * LOG2E
    tables = nsa_bias_tables(rel2)
    c0, m0 = pack_mlstm_state(jnp.zeros((b_p, MLSTM_HEADS, MLSTM_DH, MLSTM_DH), F32),
                              jnp.zeros((b_p, MLSTM_HEADS, MLSTM_DH), F32),
                              jnp.full((b_p, MLSTM_HEADS), NEG, F32))
    x = x_prompt.reshape(b_p * s_p, D_MODEL)
    st_p = []
    for l in range(N_A_LAYERS):
        p = norm_matmul(x, g_mix[l], w['in_a'][l])
        mix, cn, mm = mlstm_prompt(p, b_p, s_p, b_if[l], g_hnorm[l], c0, m0)
        st_p.append(unpack_mlstm_state(cn, mm))
        mem_o = mem_attn_prompt(p, qcol_a, b_p, s_p, g_mem_q[l], mem_kv_p, l)
        x = finish_layer(l, x, mix, mem_o)
    cmp_p, slc_p, win_p, ka, vat = kv_rows(x, g_kv, w['kv'], g_k_slc, g_k_win, s_p, aug=True)
    ck, cvt = split_cmp_ctx(compress_rows(cmp_p.reshape(b_p, 2 * s_p, LANES), cw))
    for j in range(N_B_LAYERS):
        l = N_A_LAYERS + j
        p = norm_matmul(x, g_mix[l], w['in_b'][j])
        mix = nsa_prompt2(p, b_p, s_p, g_q[j], b_gate[j], rel2, ck, cvt, ka, vat, tables)
        mem_o = mem_attn_prompt(p, qcol_b, b_p, s_p, g_mem_q[l], mem_kv_p, l)
        x = finish_layer(l, x, mix, mem_o)
    y_p = x.reshape(b_p, s_p, D_MODEL)

    b_s, s_s, _ = x_sample.shape
    assert s_s == 1
    n_pages = page_table.shape[1]
    past = n_pages * PAGE_SIZE
    wb = cache_win_kv.shape[1]
    n_pool = cache_cmp_kv.shape[0]
    x = x_sample.reshape(b_s, D_MODEL)
    cache_mem = cache_mem_kv.reshape(DEPTH, b_s, -1, 2 * MEM_W)
    st_s = []
    for l in range(N_A_LAYERS):
        p = norm_matmul(x, g_mix[l], w['in_a'][l])
        mix, c_new, n_new, m_new = mlstm_step(p, b_if[l], g_hnorm[l], state_mlstm_c, state_mlstm_n, state_mlstm_m, l)
        st_s.append((c_new, n_new, m_new))
        mem_o = mem_attn_decode(p, qcol_a, g_mem_q[l], cache_mem, l)
        x = finish_layer(l, x, mix, mem_o)
    cmp_s, slc_s, win_s = kv_rows(x, g_kv, w['kv'], g_k_slc, g_k_win, 1, aug=False)
    assert (past + 1 - CMP_LEN) // CMP_STRIDE + 1 == past // CMP_STRIDE - 1
    ctx_s = compress_rows(cache_cmp_kv.reshape(n_pool, 2 * PAGE_SIZE, LANES), cw, page_table)
    biases = nsa_decode_bias(rel_table, past, wb)
    cache_slc = cache_slc_kv.reshape(n_pool, PAGE_SIZE, KV_ROW)
    cache_win = cache_win_kv.reshape(b_s, wb, KV_ROW)
    for j in range(N_B_LAYERS):
        l = N_A_LAYERS + j
        p = norm_matmul(x, g_mix[l], w['in_b'][j])
        mix = nsa_decode(p, g_q[j], b_gate[j], ctx_s, page_table, cache_slc, slc_s, cache_win, win_s, biases)
        mem_o = mem_attn_decode(p, qcol_b, g_mem_q[l], cache_mem, l)
        x = finish_layer(l, x, mix, mem_o)
    y_s = x.reshape(b_s, 1, D_MODEL)

    rows5 = lambda r, b: r.reshape(b, -1, 2, NSA_KV_HEADS, HEAD_DIM)
    win_p5 = rows5(win_p, b_p)
    p_win = win_p5[:, s_p - min(WINDOW, s_p):]
    s_win = jnp.concatenate([cache_win_kv, rows5(win_s, b_s)], axis=1)[:, -wb:]
    stack = lambda sts, i: jnp.stack([st[i] for st in sts])
    return (y_p, y_s, stack(st_p, 0), stack(st_p, 1), stack(st_p, 2),
            mem_kv_p.reshape(DEPTH, b_p, m_tok, 2, MEM_HEADS, HEAD_DIM),
            rows5(cmp_p, b_p), rows5(slc_p, b_p), p_win,
            stack(st_s, 0), stack(st_s, 1), stack(st_s, 2),
            rows5(cmp_s, b_s), rows5(slc_s, b_s), s_win)
```

```python
import math
from functools import partial

import numpy as np
import jax
import jax.numpy as jnp
from jax import lax
from jax.experimental import pallas as pl
from jax.experimental.pallas import tpu as pltpu

D_MODEL = 1024
DEPTH = 4
PAGE_SIZE = 128
N_A_LAYERS = DEPTH // 2
N_B_LAYERS = DEPTH - N_A_LAYERS
MIX_W = 3 * D_MODEL // 4
MEM_W = D_MODEL - MIX_W
HEAD_DIM = 64
MEM_HEADS = MEM_W // HEAD_DIM
MLSTM_HEADS = 4
MLSTM_DH = MIX_W // MLSTM_HEADS
NSA_HEADS = MIX_W // HEAD_DIM
NSA_KV_HEADS = 2
NSA_GROUP = NSA_HEADS // NSA_KV_HEADS
CMP_LEN = 32
CMP_STRIDE = 16
CMP_HIDDEN = 128
SLC_BLOCK = 64
N_SELECT = 8
WINDOW = 512
REL_BUCKETS = 32
REL_MAX_EXACT = 16
REL_MAX_DIST = 128
D_FF = -(-(8 * D_MODEL) // (3 * 256)) * 256
EPS = 1e-6
NEG = -1e30
FORCED = 1e6

LANES = 128
VMEM_LIMIT = 48 * 1024 * 1024
TAIL_VMEM_LIMIT = 56 * 1024 * 1024

BF16 = jnp.bfloat16
F32 = jnp.float32


def _pick_tile(n, candidates):
    for c in candidates:
        if n % c == 0:
            return c
    return n


def _pad_cols(w, n):
    return jnp.pad(w, ((0, 0), (0, n - w.shape[1])))


def _dot_nt(a, b):
    return lax.dot_general(a, b, (((1,), (1,)), ((), ())), preferred_element_type=F32)


def _rms_rows(x, g):
    return x * lax.rsqrt(jnp.mean(x * x, axis=-1, keepdims=True) + EPS) * g


def _half_rms(x, g2, lo):
    x2 = x * x
    ss_lo = jnp.sum(jnp.where(lo, x2, 0.0), axis=1, keepdims=True)
    ss_hi = jnp.sum(jnp.where(lo, 0.0, x2), axis=1, keepdims=True)
    inv = jnp.where(lo, lax.rsqrt(ss_lo / HEAD_DIM + EPS), lax.rsqrt(ss_hi / HEAD_DIM + EPS))
    return x * inv * g2


def _gain2(g):
    return jnp.concatenate([g, g]).reshape(1, LANES).astype(F32)


def _params(*sem):
    return pltpu.CompilerParams(dimension_semantics=sem, vmem_limit_bytes=VMEM_LIMIT)


def _norm_matmul_kernel(x_ref, g_ref, w_ref, o_ref, xn_ref):
    @pl.when(pl.program_id(1) == 0)
    def _():
        xn_ref[...] = _rms_rows(x_ref[...], g_ref[...]).astype(BF16)

    o_ref[...] = jnp.dot(xn_ref[...], w_ref[...], preferred_element_type=F32)


def norm_matmul(x, g, w):
    m, k = x.shape
    n = w.shape[1]
    tm = _pick_tile(m, (1024, 512, 256, 128))
    tn = _pick_tile(n, (1280, 1152, 1024, 896, 768, 640, 512, 384, 256, 128))
    return pl.pallas_call(
        _norm_matmul_kernel,
        grid=(m // tm, n // tn),
        in_specs=[pl.BlockSpec((tm, k), lambda i, j: (i, 0)),
                  pl.BlockSpec((1, k), lambda i, j: (0, 0)),
                  pl.BlockSpec((k, tn), lambda i, j: (0, j))],
        out_specs=pl.BlockSpec((tm, tn), lambda i, j: (i, j)),
        out_shape=jax.ShapeDtypeStruct((m, n), F32),
        scratch_shapes=[pltpu.VMEM((tm, k), BF16)],
        compiler_params=_params("parallel", "arbitrary"),
        name="norm_matmul",
    )(x, g.reshape(1, k), w)


def _mem_kv_kernel(x_ref, w_ref, g_ref, o_ref):
    tm = x_ref.shape[0]
    kv = jnp.dot(x_ref[...].astype(BF16), w_ref[...], preferred_element_type=F32)
    lo = lax.broadcasted_iota(jnp.int32, (tm, LANES), 1) < HEAD_DIM
    for k in range(MEM_HEADS // 2):
        o_ref[:, k * LANES:(k + 1) * LANES] = _half_rms(kv[:, k * LANES:(k + 1) * LANES], g_ref[...], lo)
    o_ref[:, MEM_W:] = kv[:, MEM_W:]


def mem_kv_rows(mem, w, g_k):
    m, d = mem.shape
    depth, _, n = w.shape
    tm = _pick_tile(m, (512, 256, 128))
    g2 = jnp.concatenate([g_k, g_k], axis=1).reshape(depth, 1, LANES).astype(F32)
    return pl.pallas_call(
        _mem_kv_kernel,
        grid=(depth, m // tm),
        in_specs=[pl.BlockSpec((tm, d), lambda l, i: (i, 0)),
                  pl.BlockSpec((None, d, n), lambda l, i: (l, 0, 0)),
                  pl.BlockSpec((None, 1, LANES), lambda l, i: (l, 0, 0))],
        out_specs=pl.BlockSpec((None, tm, n), lambda l, i: (l, i, 0)),
        out_shape=jax.ShapeDtypeStruct((depth, m, n), F32),
        compiler_params=_params("parallel", "parallel"),
        name="mem_kv_rows",
    )(mem, w, g2)


def _layer_tail_kernel(a_ref, wa_ref, b_ref, wb_ref, x_ref, g_ref, wg_ref, wu_ref, wd_ref, o_ref, xn_ref):
    f = pl.program_id(1)

    @pl.when(f == 0)
    def _():
        x1 = (x_ref[...] + jnp.dot(a_ref[...].astype(BF16), wa_ref[...], preferred_element_type=F32)
              + jnp.dot(b_ref[...].astype(BF16), wb_ref[...], preferred_element_type=F32))
        o_ref[...] = x1
        xn_ref[...] = _rms_rows(x1, g_ref[...]).astype(BF16)

    xn = xn_ref[...]
    gate = jnp.dot(xn, wg_ref[...], preferred_element_type=F32)
    up = jnp.dot(xn, wu_ref[...], preferred_element_type=F32)
    act = (gate * jax.nn.sigmoid(gate) * up).astype(BF16)
    o_ref[...] += jnp.dot(act, wd_ref[...], preferred_element_type=F32)


def layer_tail(a, wa, b, wb, x, g, w_gu, w_down):
    m, d = x.shape
    tm = _pick_tile(m, (512, 256, 128))
    tf = 1408
    nf = D_FF // tf
    ka, kb = a.shape[1], b.shape[1]
    row = lambda k: pl.BlockSpec((tm, k), lambda i, f: (i, 0))
    const = lambda shape: pl.BlockSpec(shape, lambda i, f: (0, 0))
    return pl.pallas_call(
        _layer_tail_kernel,
        grid=(m // tm, nf),
        in_specs=[row(ka), const((ka, d)), row(kb), const((kb, d)), row(d), const((1, d)),
                  pl.BlockSpec((d, tf), lambda i, f: (0, f)),
                  pl.BlockSpec((d, tf), lambda i, f: (0, f + nf)),
                  pl.BlockSpec((tf, d), lambda i, f: (f, 0))],
        out_specs=row(d),
        out_shape=jax.ShapeDtypeStruct((m, d), F32),
        scratch_shapes=[pltpu.VMEM((tm, d), BF16)],
        compiler_params=pltpu.CompilerParams(dimension_semantics=("parallel", "arbitrary"),
                                             vmem_limit_bytes=TAIL_VMEM_LIMIT),
        name="layer_tail",
    )(a, wa, b, wb, x, g.reshape(1, d), w_gu, w_gu, w_down)


MLSTM_L = 128
MLSTM_PAD = 256
N_COL = MLSTM_DH
MLSTM_HW = MLSTM_HEADS * MLSTM_PAD
STEP_TB = 8


def _exact_tri_cumsum(tri, x):
    hi = x.astype(BF16)
    r1 = x - hi.astype(F32)
    mid = r1.astype(BF16)
    lo = (r1 - mid.astype(F32)).astype(BF16)
    return (jnp.dot(tri, hi, preferred_element_type=F32) + jnp.dot(tri, mid, preferred_element_type=F32)
            + jnp.dot(tri, lo, preferred_element_type=F32))


def _mlstm_kernel(q_ref, k_ref, v_ref, og_ref, gate_ref, bif_ref, gh_ref, c0_ref, m0_ref,
                  mix_ref, cout_ref, mout_ref, c_scr, m_scr):
    L = MLSTM_L
    P = MLSTM_PAD
    ci = pl.program_id(1)

    @pl.when(ci == 0)
    def _():
        c_scr[...] = c0_ref[...]
        m_scr[...] = m0_ref[...]

    row = lax.broadcasted_iota(jnp.int32, (L, L), 0)
    col = lax.broadcasted_iota(jnp.int32, (L, L), 1)
    causal = col <= row
    tri = jnp.where(causal, 1.0, 0.0).astype(BF16)
    lane_p = lax.broadcasted_iota(jnp.int32, (L, P), 1)

    gl = gate_ref[...] + bif_ref[...]
    logf = jax.nn.log_sigmoid(gl)
    bc = _exact_tri_cumsum(tri, logf)
    bt = bc.T
    gt = gl.T
    for h in range(MLSTM_HEADS):
        fh = MLSTM_HEADS + h
        b_col = bc[:, fh:fh + 1]
        b_row = bt[fh:fh + 1, :]
        i_row = gt[h:h + 1, :]
        i_col = gl[:, h:h + 1]
        m_prev = m_scr[0:1, h:h + 1]
        log_d = jnp.where(causal, b_col - b_row + i_row, -jnp.inf)
        log_inter = b_col + m_prev
        m_t = jnp.maximum(jnp.max(log_d, axis=1, keepdims=True), log_inter)
        w_intra = jnp.exp(log_d - m_t)
        w_inter = jnp.exp(log_inter - m_t)
        sl = slice(h * P, (h + 1) * P)
        q = q_ref[:, sl].astype(BF16)
        kf = k_ref[:, sl] * MLSTM_DH ** -0.5
        v_aug = jnp.where(lane_p == N_COL, 1.0, v_ref[:, sl])
        sc = _dot_nt(q, kf.astype(BF16)) * w_intra
        c = c_scr[h]
        nd = (jnp.dot(sc.astype(BF16), v_aug.astype(BF16), preferred_element_type=F32)
              + w_inter * jnp.dot(q, c.astype(BF16), preferred_element_type=F32))
        den = nd[:, N_COL:N_COL + 1]
        hh = jnp.where(lane_p < N_COL, nd / jnp.maximum(jnp.abs(den), jnp.exp(-m_t)), 0.0)
        y = hh * lax.rsqrt(jnp.sum(hh * hh, axis=1, keepdims=True) / MLSTM_DH + EPS) * gh_ref[...]
        mix_ref[:, sl] = jax.nn.sigmoid(og_ref[:, sl]) * y
        g_last = bc[L - 1:L, fh:fh + 1]
        m_new = m_t[L - 1:L, :]
        w_s = jnp.exp(g_last - b_col + i_col - m_new)
        decay = jnp.exp(g_last + m_prev - m_new)
        upd = jnp.dot(kf.T.astype(BF16), (w_s * v_aug).astype(BF16), preferred_element_type=F32)
        c_scr[h] = decay * c + upd
        m_scr[0:1, h:h + 1] = m_new

    @pl.when(ci == pl.num_programs(1) - 1)
    def _():
        cout_ref[...] = c_scr[...]
        mout_ref[...] = m_scr[...]


def mlstm_prompt(p, b, s, b_if, g_hnorm, c0, m0):
    L = MLSTM_L
    P = MLSTM_PAD
    hw = MLSTM_HW
    assert s % L == 0
    nc = s // L
    gcol = (4 * hw + MEM_W) // LANES
    bif = _pad_cols(b_if.astype(F32).reshape(1, -1), LANES)
    gh = _pad_cols(g_hnorm.astype(F32).reshape(1, -1), P)
    blk = lambda j: pl.BlockSpec((L, hw), lambda bi, ci: (bi * nc + ci, j))
    return pl.pallas_call(
        _mlstm_kernel,
        grid=(b, nc),
        in_specs=[blk(0), blk(1), blk(2), blk(3),
                  pl.BlockSpec((L, LANES), lambda bi, ci: (bi * nc + ci, gcol)),
                  pl.BlockSpec((1, LANES), lambda bi, ci: (0, 0)),
                  pl.BlockSpec((1, P), lambda bi, ci: (0, 0)),
                  pl.BlockSpec((None, MLSTM_HEADS, P, P), lambda bi, ci: (bi, 0, 0, 0)),
                  pl.BlockSpec((None, 1, LANES), lambda bi, ci: (bi, 0, 0))],
        out_specs=[pl.BlockSpec((L, hw), lambda bi, ci: (bi * nc + ci, 0)),
                   pl.BlockSpec((None, MLSTM_HEADS, P, P), lambda bi, ci: (bi, 0, 0, 0)),
                   pl.BlockSpec((None, 1, LANES), lambda bi, ci: (bi, 0, 0))],
        out_shape=[jax.ShapeDtypeStruct((b * s, hw), F32),
                   jax.ShapeDtypeStruct((b, MLSTM_HEADS, P, P), F32),
                   jax.ShapeDtypeStruct((b, 1, LANES), F32)],
        scratch_shapes=[pltpu.VMEM((MLSTM_HEADS, P, P), F32), pltpu.VMEM((1, LANES), F32)],
        compiler_params=_params("parallel", "arbitrary"),
        name="mlstm_prompt",
    )(p, p, p, p, p, bif, gh, c0, m0)


def pack_mlstm_state(c, n, m):
    pad = MLSTM_PAD - MLSTM_DH
    cn = jnp.concatenate([c, n[..., None]], axis=-1)
    cn = jnp.pad(cn, ((0, 0), (0, 0), (0, pad), (0, pad - 1)))
    return cn, _pad_cols(m, LANES)[:, None, :]


def unpack_mlstm_state(cn, m):
    return cn[:, :, :MLSTM_DH, :MLSTM_DH], cn[:, :, :MLSTM_DH, MLSTM_DH], m[:, 0, :MLSTM_HEADS]


def _mlstm_step_kernel(q_ref, v_ref, og_ref, gate_ref, kt_ref, bif_ref, gh_ref, c_ref, n_ref, m_ref,
                       mix_ref, cout_ref, nout_ref, mout_ref):
    tb = q_ref.shape[0]
    dh = MLSTM_DH
    gl = gate_ref[...] + bif_ref[...]
    logf = jax.nn.log_sigmoid(gl)
    i4 = gl[:, :MLSTM_HEADS]
    f4 = logf[:, MLSTM_HEADS:2 * MLSTM_HEADS]
    m_prev = m_ref[...]
    m_t = jnp.maximum(i4, f4 + m_prev)
    w_in = jnp.exp(i4 - m_t)
    w_dec = jnp.exp(f4 + m_prev - m_t)
    floor = jnp.exp(-m_t)
    mout_ref[...] = m_t
    mix_ref[...] = jnp.zeros(mix_ref.shape, F32)
    rowi = lax.broadcasted_iota(jnp.int32, (LANES, dh), 0)
    for j in range(tb):
        for h in range(MLSTM_HEADS):
            sl = slice(h * MLSTM_PAD, h * MLSTM_PAD + dh)
            wi = w_in[j:j + 1, h:h + 1]
            wd = w_dec[j:j + 1, h:h + 1]
            q_row = q_ref[j:j + 1, sl]
            v_row = v_ref[j:j + 1, sl]
            k_row = kt_ref[1, h, j:j + 1, :]
            c = c_ref[j, h]
            n_row = n_ref[j, h:h + 1, :]
            qk = jnp.sum(q_row * k_row, axis=1, keepdims=True) * wi
            qc = jnp.dot(q_ref[:, sl].astype(BF16), c.astype(BF16), preferred_element_type=F32)[j:j + 1, :]
            qn = jnp.sum(q_row * n_row, axis=1, keepdims=True)
            v_sel = jnp.where(rowi == j, wi * v_row, 0.0).astype(BF16)
            kv = jnp.dot(kt_ref[0, h, :, :LANES].astype(BF16), v_sel, preferred_element_type=F32)
            num = qk * v_row + wd * qc
            den = qk + wd * qn
            hh = num / jnp.maximum(jnp.abs(den), floor[j:j + 1, h:h + 1])
            y = hh * lax.rsqrt(jnp.sum(hh * hh, axis=1, keepdims=True) / dh + EPS) * gh_ref[...]
            mix_ref[j:j + 1, sl] = jax.nn.sigmoid(og_ref[j:j + 1, sl]) * y
            cout_ref[j, h] = wd * c + kv
            nout_ref[j, h:h + 1, :] = wd * n_row + wi * k_row


def mlstm_step(p, b_if, g_hnorm, c, n, m, layer):
    b = p.shape[0]
    tb = STEP_TB
    hw = MLSTM_HW
    dh = MLSTM_DH
    nb = b // tb
    gcol = (4 * hw + MEM_W) // LANES
    k = p[:, hw:2 * hw].reshape(nb, tb, MLSTM_HEADS, MLSTM_PAD)[..., :dh] * dh ** -0.5
    k_cols = jnp.pad(k.transpose(0, 2, 3, 1), ((0, 0), (0, 0), (0, 0), (0, dh - tb)))
    k_rows = jnp.pad(k.transpose(0, 2, 1, 3), ((0, 0), (0, 0), (0, dh - tb), (0, 0)))
    kt = jnp.stack([k_cols, k_rows], axis=1)
    bif = _pad_cols(b_if.astype(F32).reshape(1, -1), LANES)
    gh = g_hnorm.astype(F32).reshape(1, dh)
    blk = lambda j: pl.BlockSpec((tb, hw), lambda i: (i, j))
    st4 = pl.BlockSpec((tb, MLSTM_HEADS, dh, dh), lambda i: (i, 0, 0, 0))
    st3 = pl.BlockSpec((tb, MLSTM_HEADS, dh), lambda i: (i, 0, 0))
    st2 = pl.BlockSpec((tb, MLSTM_HEADS), lambda i: (i, 0))
    return pl.pallas_call(
        _mlstm_step_kernel,
        grid=(nb,),
        in_specs=[blk(0), blk(2), blk(3),
                  pl.BlockSpec((tb, LANES), lambda i: (i, gcol)),
                  pl.BlockSpec((None, 2, MLSTM_HEADS, dh, dh), lambda i: (i, 0, 0, 0, 0)),
                  pl.BlockSpec((1, LANES), lambda i: (0, 0)),
                  pl.BlockSpec((1, dh), lambda i: (0, 0)),
                  pl.BlockSpec((None, tb, MLSTM_HEADS, dh, dh), lambda i: (layer, i, 0, 0, 0)),
                  pl.BlockSpec((None, tb, MLSTM_HEADS, dh), lambda i: (layer, i, 0, 0)),
                  pl.BlockSpec((None, tb, MLSTM_HEADS), lambda i: (layer, i, 0))],
        out_specs=[pl.BlockSpec((tb, hw), lambda i: (i, 0)), st4, st3, st2],
        out_shape=[jax.ShapeDtypeStruct((b, hw), F32), jax.ShapeDtypeStruct(c.shape[1:], F32),
                   jax.ShapeDtypeStruct(n.shape[1:], F32), jax.ShapeDtypeStruct(m.shape[1:], F32)],
        compiler_params=_params("parallel"),
        name="mlstm_step",
    )(p, p, p, p, kt, bif, gh, c, n, m)


def _mem_attn_kernel(q_ref, gq_ref, kv_ref, o_ref):
    tq = q_ref.shape[0]
    lane = lax.broadcasted_iota(jnp.int32, (tq, LANES), 1)
    lo = lane < HEAD_DIM
    for k in range(MEM_HEADS // 2):
        y = _half_rms(q_ref[:, k * LANES:(k + 1) * LANES], gq_ref[...], lo) * HEAD_DIM ** -0.5
        kp = kv_ref[:, k * LANES:(k + 1) * LANES].astype(BF16)
        vp = kv_ref[:, MEM_W + k * LANES:MEM_W + (k + 1) * LANES].astype(BF16)
        outs = []
        for half in range(2):
            qh = jnp.where(lo if half == 0 else ~lo, y, 0.0).astype(BF16)
            s = _dot_nt(qh, kp)
            p = jnp.exp(s - jnp.max(s, axis=1, keepdims=True))
            p = p / jnp.sum(p, axis=1, keepdims=True)
            outs.append(jnp.dot(p.astype(BF16), vp, preferred_element_type=F32))
        o_ref[:, k * LANES:(k + 1) * LANES] = jnp.where(lo, outs[0], outs[1])


def mem_attn_prompt(p, qcol, b, s, g_q, mem_kv, layer):
    tq = _pick_tile(s, (512, 256, 128))
    nq = s // tq
    m_tok = mem_kv.shape[2]
    return pl.pallas_call(
        _mem_attn_kernel,
        grid=(b, nq),
        in_specs=[pl.BlockSpec((tq, MEM_W), lambda bi, qi: (bi * nq + qi, qcol)),
                  pl.BlockSpec((1, LANES), lambda bi, qi: (0, 0)),
                  pl.BlockSpec((None, None, m_tok, 2 * MEM_W), lambda bi, qi: (layer, bi, 0, 0))],
        out_specs=pl.BlockSpec((tq, MEM_W), lambda bi, qi: (bi * nq + qi, 0)),
        out_shape=jax.ShapeDtypeStruct((b * s, MEM_W), F32),
        compiler_params=_params("parallel", "parallel"),
        name="mem_attn",
    )(p, _gain2(g_q), mem_kv)


def _mem_decode_kernel(q_ref, gq_ref, kv_ref, o_ref):
    tb = q_ref.shape[0]
    lane = lax.broadcasted_iota(jnp.int32, (tb, LANES), 1)
    lo = lane < HEAD_DIM
    y = jnp.concatenate([_half_rms(q_ref[:, k * LANES:(k + 1) * LANES], gq_ref[...], lo)
                         for k in range(MEM_HEADS // 2)], axis=1) * HEAD_DIM ** -0.5
    rowi = lax.broadcasted_iota(jnp.int32, (8, MEM_W), 0)
    own = (lax.broadcasted_iota(jnp.int32, (8, MEM_W), 1) // HEAD_DIM) == rowi
    for j in range(tb):
        qbd = jnp.where(own, y[j:j + 1, :], 0.0).astype(BF16)
        s = _dot_nt(qbd, kv_ref[j, :, :MEM_W].astype(BF16))
        p = jnp.exp(s - jnp.max(s, axis=1, keepdims=True))
        p = p / jnp.sum(p, axis=1, keepdims=True)
        o = jnp.dot(p.astype(BF16), kv_ref[j, :, MEM_W:].astype(BF16), preferred_element_type=F32)
        o_ref[j:j + 1, :] = jnp.sum(jnp.where(own, o, 0.0), axis=0, keepdims=True)


def mem_attn_decode(p, qcol, g_q, mem_kv, layer):
    b = p.shape[0]
    tb = STEP_TB
    m_tok = mem_kv.shape[2]
    return pl.pallas_call(
        _mem_decode_kernel,
        grid=(b // tb,),
        in_specs=[pl.BlockSpec((tb, MEM_W), lambda i: (i, qcol)),
                  pl.BlockSpec((1, LANES), lambda i: (0, 0)),
                  pl.BlockSpec((None, tb, m_tok, 2 * MEM_W), lambda i: (layer, i, 0, 0))],
        out_specs=pl.BlockSpec((tb, MEM_W), lambda i: (i, 0)),
        out_shape=jax.ShapeDtypeStruct((b, MEM_W), F32),
        compiler_params=_params("parallel"),
        name="mem_attn_decode",
    )(p, _gain2(g_q), mem_kv)


NSA_TQ = 128
MASK = -(2.0 ** 100)
SEL_LANE = HEAD_DIM
N_SEL_LANES = 32
CONST_LANE = SEL_LANE + N_SEL_LANES
KV_ROW = 2 * NSA_KV_HEADS * HEAD_DIM
SEG_PER_PAGE = PAGE_SIZE // CMP_STRIDE
SEG_ROW = CMP_STRIDE * KV_ROW
CMP_OUT = 2 * NSA_KV_HEADS * CMP_HIDDEN
QROWS = 16


def _rel_bucket_np(dist):
    d = np.maximum(dist, 0)
    ratio = np.maximum(d, REL_MAX_EXACT).astype(np.float64) / REL_MAX_EXACT
    large = REL_MAX_EXACT + (np.log(ratio) / math.log(REL_MAX_DIST / REL_MAX_EXACT)
                             * (REL_BUCKETS - REL_MAX_EXACT)).astype(np.int32)
    return np.where(d < REL_MAX_EXACT, d, np.minimum(large, REL_BUCKETS - 1)).astype(np.int32)


def _overlap_matrix(n_cmp, n_slc):
    c0 = np.arange(n_cmp)[:, None] * CMP_STRIDE
    s0 = np.arange(n_slc)[None, :] * SLC_BLOCK
    return np.clip(np.minimum(c0 + CMP_LEN, s0 + SLC_BLOCK) - np.maximum(c0, s0), 0, None) / CMP_STRIDE


def _kv_rows_kernel(x_ref, g_ref, w_ref, gs_ref, gw_ref, cmp_ref, slc_ref, win_ref, *aug_refs, seq_len):
    tm = x_ref.shape[0]
    xn = _rms_rows(x_ref[...], g_ref[...]).astype(BF16)
    kv = jnp.dot(xn, w_ref[...], preferred_element_type=F32)
    cmp_ref[...] = kv[:, :KV_ROW]
    lane = lax.broadcasted_iota(jnp.int32, (tm, LANES), 1)
    lo = lane < HEAD_DIM
    if aug_refs:
        ka_ref, va_ref = aug_refs
        pos = (pl.program_id(0) * tm + lax.broadcasted_iota(jnp.int32, (tm, LANES), 0)) % seq_len
        ones = (lane == CONST_LANE) | (lane == CONST_LANE + 1)
        onehot = (lane - SEL_LANE) == pos // SLC_BLOCK
    for t, (rows_ref, gk_ref) in enumerate(((slc_ref, gs_ref), (win_ref, gw_ref))):
        base = KV_ROW * (t + 1)
        kn = _half_rms(kv[:, base:base + LANES], gk_ref[...], lo)
        vv = kv[:, base + LANES:base + 2 * LANES]
        rows_ref[:, :LANES] = kn
        rows_ref[:, LANES:] = vv
        if aug_refs:
            extra = jnp.where((ones | onehot) if t == 0 else ones, 1.0, 0.0)
            rowt = lax.broadcasted_iota(jnp.int32, (VT_ROWS, tm), 0)
            for g in range(NSA_KV_HEADS):
                kg = kn if g == 0 else pltpu.roll(kn, HEAD_DIM, 1)
                vg = vv if g == 0 else pltpu.roll(vv, HEAD_DIM, 1)
                ka_ref[NSA_KV_HEADS * t + g] = jnp.where(lo, kg, extra).astype(BF16)
                vt = jnp.where(lo, vg, 0.0).T[:VT_ROWS, :]
                va_ref[NSA_KV_HEADS * t + g] = jnp.where(rowt == HEAD_DIM, 1.0, vt).astype(BF16)


def kv_rows(x, g_kv, w_kv, g_k_slc, g_k_win, seq_len, aug):
    m, d = x.shape
    n = w_kv.shape[1]
    tm = _pick_tile(m, (512, 256, 128))
    out_specs = [pl.BlockSpec((tm, KV_ROW), lambda i: (i, 0))] * 3
    out_shape = [jax.ShapeDtypeStruct((m, KV_ROW), F32)] * 3
    if aug:
        out_specs += [pl.BlockSpec((2 * NSA_KV_HEADS, tm, LANES), lambda i: (0, i, 0)),
                      pl.BlockSpec((2 * NSA_KV_HEADS, VT_ROWS, tm), lambda i: (0, 0, i))]
        out_shape += [jax.ShapeDtypeStruct((2 * NSA_KV_HEADS, m, LANES), BF16),
                      jax.ShapeDtypeStruct((2 * NSA_KV_HEADS, VT_ROWS, m), BF16)]
    return pl.pallas_call(
        partial(_kv_rows_kernel, seq_len=seq_len),
        grid=(m // tm,),
        in_specs=[pl.BlockSpec((tm, d), lambda i: (i, 0)),
                  pl.BlockSpec((1, d), lambda i: (0, 0)),
                  pl.BlockSpec((d, n), lambda i: (0, 0)),
                  pl.BlockSpec((1, LANES), lambda i: (0, 0)),
                  pl.BlockSpec((1, LANES), lambda i: (0, 0))],
        out_specs=out_specs,
        out_shape=out_shape,
        compiler_params=_params("parallel"),
        name="kv_rows",
    )(x, g_kv.reshape(1, d), w_kv, _gain2(g_k_slc), _gain2(g_k_win))


def _compress_kernel(*refs, n_x, paged):
    if paged:
        refs = refs[1:]
    x_refs = refs[:n_x]
    w1_ref, b1_ref, w2_ref, b2_ref, gk_ref, o_ref = refs[n_x:]
    n_seg = o_ref.shape[0]
    r_n = CMP_LEN // CMP_STRIDE
    lane = lax.broadcasted_iota(jnp.int32, (n_seg, LANES), 1)
    for c in range(2):
        acc = [jnp.zeros((n_seg, CMP_OUT // 2), F32) for _ in range(r_n)]
        for u in range(CMP_STRIDE):
            lo_lane = u * KV_ROW + c * LANES
            pieces = [r[:, lo_lane:lo_lane + LANES] for r in x_refs]
            xu = (pieces[0] if n_x == 1 else jnp.concatenate(pieces, axis=0)).astype(BF16)
            for r in range(r_n):
                acc[r] = acc[r] + jnp.dot(xu, w1_ref[r, u, c], preferred_element_type=F32)
        pre = acc[0] + pltpu.roll(acc[1], n_seg - 1, 0) + b1_ref[c]
        out = jnp.dot(jax.nn.gelu(pre).astype(BF16), w2_ref[c], preferred_element_type=F32) + b2_ref[c]
        if c == 0:
            out = _half_rms(out, gk_ref[...], lane < HEAD_DIM)
        o_ref[:, c * LANES:(c + 1) * LANES] = out.astype(BF16)


def compress_weights(w1, b1, w2, b2, pos_enc, g_k_cmp):
    r_n = CMP_LEN // CMP_STRIDE
    eye = jnp.eye(NSA_KV_HEADS, dtype=F32)
    w1r = w1.reshape(2, r_n, CMP_STRIDE, HEAD_DIM, CMP_HIDDEN)
    w1b = jnp.einsum('crudh,gy->rucgdyh', w1r, eye).reshape(r_n, CMP_STRIDE, 2, LANES, CMP_OUT // 2).astype(BF16)
    pe = jnp.einsum('pcd,cpdh->ch', pos_enc, w1)
    b1b = jnp.broadcast_to((b1 + pe)[:, None, :], (2, NSA_KV_HEADS, CMP_HIDDEN)).reshape(2, 1, CMP_OUT // 2)
    w2b = jnp.einsum('chd,gy->cghyd', w2, eye).reshape(2, CMP_OUT // 2, LANES).astype(BF16)
    b2b = jnp.broadcast_to(b2[:, None, :], (2, NSA_KV_HEADS, HEAD_DIM)).reshape(2, 1, LANES)
    return w1b, b1b, w2b, b2b, _gain2(g_k_cmp)


def compress_rows(x, cw, page_table=None):
    w1b, b1b, w2b, b2b, gk2 = cw
    paged = page_table is not None
    if paged:
        n_seq, n_x = page_table.shape
        x_specs = [pl.BlockSpec((None, SEG_PER_PAGE, SEG_ROW), partial(lambda k, b, pt: (pt[b, k], 0, 0), k))
                   for k in range(n_x)]
        n_seg = n_x * SEG_PER_PAGE
        cmap = lambda *idx: lambda b, pt: idx
    else:
        n_seq, n_seg, n_x = x.shape[0], x.shape[1], 1
        x_specs = [pl.BlockSpec((None, n_seg, SEG_ROW), lambda b: (b, 0, 0))]
        cmap = lambda *idx: lambda b: idx
    w_specs = [pl.BlockSpec(w1b.shape, cmap(0, 0, 0, 0, 0)), pl.BlockSpec(b1b.shape, cmap(0, 0, 0)),
               pl.BlockSpec(w2b.shape, cmap(0, 0, 0)), pl.BlockSpec(b2b.shape, cmap(0, 0, 0)),
               pl.BlockSpec(gk2.shape, cmap(0, 0))]
    out_spec = pl.BlockSpec((None, n_seg, KV_ROW), (lambda b, pt: (b, 0, 0)) if paged else (lambda b: (b, 0, 0)))
    out_shape = jax.ShapeDtypeStruct((n_seq, n_seg, KV_ROW), BF16)
    kern = partial(_compress_kernel, n_x=n_x, paged=paged)
    if paged:
        gs = pltpu.PrefetchScalarGridSpec(num_scalar_prefetch=1, grid=(n_seq,), in_specs=x_specs + w_specs,
                                          out_specs=out_spec)
        return pl.pallas_call(kern, grid_spec=gs, out_shape=out_shape, compiler_params=_params("parallel"),
                              name="compress_paged")(page_table, *([x] * n_x), w1b, b1b, w2b, b2b, gk2)
    return pl.pallas_call(kern, grid=(n_seq,), in_specs=x_specs + w_specs, out_specs=out_spec, out_shape=out_shape,
                          compiler_params=_params("parallel"), name="compress")(x, w1b, b1b, w2b, b2b, gk2)


VT_ROWS = 80
CMP_BAND = 16


LOG2E = 1.0 / math.log(2.0)
FAR_BLOCKS = 4
NEAR_BLOCKS = 5
FULL_MASK_TILE = 4


def _nsa_prompt_kernel(rel_ref, q_ref, gate_ref, gq_ref, bg_ref, ck_ref, cvt_ref, band_ref, far_ref, mt_ref,
                        ka_ref, vat_ref, dl_ref, o_ref, qa_ref, m_ref, acc_ref, cb_ref, *, n_cmp, n_tiles):
    tq = NSA_TQ
    cols = NSA_GROUP * tq
    qi = pl.program_id(1)
    lane = lax.broadcasted_iota(jnp.int32, (tq, LANES), 1)
    lo = lane < HEAD_DIM
    nidx = lax.broadcasted_iota(jnp.int32, (LANES, cols), 0)
    r_cb = lax.broadcasted_iota(jnp.int32, (LANES + CMP_BAND, cols), 0)
    jidx = lax.broadcasted_iota(jnp.int32, (N_SEL_LANES, tq), 0)
    tpos = qi * tq + lax.broadcasted_iota(jnp.int32, (N_SEL_LANES, tq), 1)
    cur = tpos // SLC_BLOCK
    forced = (jidx == 0) | (jidx == cur) | (jidx == cur - 1)
    half_band = CMP_BAND // 2

    def gate_rows(g):
        sig_t = jax.nn.sigmoid(gate_ref[:, g * LANES:(g + 1) * LANES] + bg_ref[:, g * LANES:(g + 1) * LANES]).T
        return [jnp.concatenate([sig_t[br * NSA_GROUP + z:br * NSA_GROUP + z + 1, :] for z in range(NSA_GROUP)], axis=1)
                for br in range(3)]

    def prologue(g):
        qn = []
        for k in range(NSA_GROUP // 2):
            kk = g * (NSA_GROUP // 2) + k
            y = _half_rms(q_ref[:, kk * LANES:(kk + 1) * LANES], gq_ref[...], lo) * (HEAD_DIM ** -0.5 * LOG2E)
            qn.append(jnp.where(lo, y, 0.0))
            qn.append(jnp.where(lo, pltpu.roll(y, HEAD_DIM, 1), 0.0))
        qc = jnp.concatenate(qn, axis=0).astype(BF16)

        cb_ref[g] = jnp.where(r_cb < half_band * qi, far_ref[g], MASK)
        cb_ref[g, pl.ds(pl.multiple_of(half_band * qi, half_band), CMP_BAND), :] = band_ref[g]
        cbias = jnp.where(nidx >= n_cmp, MASK, cb_ref[g, half_band:half_band + LANES, :])
        s = _dot_nt(ck_ref[g], qc) + cbias
        m = jnp.maximum(jnp.max(s, axis=0, keepdims=True), NEG)
        p = jnp.exp2(s - m)
        l = jnp.sum(p, axis=0, keepdims=True)
        pn = p * jnp.where(l > 0.0, 1.0 / l, 0.0)
        o_cmp = jnp.dot(cvt_ref[g], pn.astype(BF16), preferred_element_type=F32)

        psum = pn[:, 0:tq]
        for z in range(1, NSA_GROUP):
            psum = psum + pn[:, z * tq:(z + 1) * tq]
        p_hi = psum.astype(BF16)
        p_lo = (psum - p_hi.astype(F32)).astype(BF16)
        imp = (jnp.dot(mt_ref[...], p_hi, preferred_element_type=F32)
               + jnp.dot(mt_ref[...], p_lo, preferred_element_type=F32))
        score = jnp.where(forced, FORCED, jnp.where(jidx <= cur, imp, -1.0))
        rank = jnp.zeros((N_SEL_LANES, tq), F32)
        for i in range(N_SEL_LANES):
            ri = score[i:i + 1, :]
            beats = (ri > score) | ((ri == score) & (jidx > i))
            rank = rank + jnp.where(beats, 1.0, 0.0)
        selb = jnp.where((rank < N_SELECT) & (jidx <= cur), 0.0, MASK)
        selb = jnp.concatenate([selb, jnp.zeros((LANES - N_SEL_LANES, tq), F32)], axis=0)
        sel_cols = pltpu.roll(selb.T, SEL_LANE, 1)

        for z in range(NSA_GROUP):
            c = jnp.full((tq, LANES), rel_ref[REL_BUCKETS - 1, g * NSA_GROUP + z], F32)
            c_hi = c.astype(BF16).astype(F32)
            extra = jnp.where(lane == CONST_LANE, c_hi, jnp.where(lane == CONST_LANE + 1, c - c_hi, sel_cols))
            qa_ref[g, z * tq:(z + 1) * tq, :] = jnp.where(lo, qn[z], extra).astype(BF16)
        return o_cmp

    o_cmp = [prologue(g) for g in range(NSA_KV_HEADS)]

    n_far = jnp.maximum(qi - 1, 0) // FAR_BLOCKS
    far_rows = FAR_BLOCKS * tq
    m_ref[...] = jnp.full(m_ref.shape, NEG, F32)
    acc_ref[...] = jnp.zeros(acc_ref.shape, F32)

    def far_body(c, carry):
        start = pl.multiple_of(c * far_rows, far_rows)
        for g in range(NSA_KV_HEADS):
            s = _dot_nt(ka_ref[g, pl.ds(start, far_rows), :], qa_ref[g])
            m_prev = m_ref[g]
            m_next = jnp.maximum(m_prev, jnp.max(s, axis=0, keepdims=True))
            p = jnp.exp2(s - m_next).astype(BF16)
            acc_ref[g] = (jnp.exp2(m_prev - m_next) * acc_ref[g]
                          + jnp.dot(vat_ref[g, :, pl.ds(start, far_rows)], p, preferred_element_type=F32))
            m_ref[g] = m_next
        return carry

    lax.fori_loop(0, n_far, far_body, 0)

    def near_stage(idx, g, start_blk, tile_of, m_prev, acc_prev):
        start = pl.multiple_of(start_blk * tq, tq)
        s = _dot_nt(ka_ref[idx, pl.ds(start, NEAR_BLOCKS * tq), :], qa_ref[g])
        s = s + jnp.concatenate([dl_ref[tile_of(start_blk + i), g] for i in range(NEAR_BLOCKS)], axis=0)
        m_next = jnp.max(s, axis=0, keepdims=True)
        if m_prev is not None:
            m_next = jnp.maximum(m_prev, m_next)
        acc = jnp.dot(vat_ref[idx, :, pl.ds(start, NEAR_BLOCKS * tq)], jnp.exp2(s - m_next).astype(BF16),
                      preferred_element_type=F32)
        if acc_prev is not None:
            acc = acc + jnp.exp2(m_prev - m_next) * acc_prev
        return acc[:HEAD_DIM, :] / acc[HEAD_DIM:HEAD_DIM + 1, :]

    far_cov = FAR_BLOCKS * n_far
    slc_start = jnp.minimum(far_cov, n_tiles - NEAR_BLOCKS)
    win_start = jnp.maximum(qi - WINDOW // tq, 0)

    def slc_tile(j):
        d = qi - j
        return jnp.where((d < 0) | (j < far_cov), FULL_MASK_TILE, jnp.where(d == 0, 2, jnp.where(d == 1, 1, 0)))

    def win_tile(j):
        d = qi - j
        return jnp.where(d < 0, FULL_MASK_TILE,
                         jnp.where(d == 0, 2, jnp.where(d == 1, 1, jnp.where(d == WINDOW // tq, 3, 0))))

    outs = []
    for g in range(NSA_KV_HEADS):
        gr = gate_rows(g)
        o_slc = near_stage(g, g, slc_start, slc_tile, m_ref[g], acc_ref[g])
        o_win = near_stage(NSA_KV_HEADS + g, g, win_start, win_tile, None, None)
        outs.append(gr[0] * o_cmp[g] + gr[1] * o_slc + gr[2] * o_win)

    for g in range(NSA_KV_HEADS):
        for k in range(NSA_GROUP // 2):
            kk = g * (NSA_GROUP // 2) + k
            o_ref[:, kk * LANES:(kk + 1) * LANES] = jnp.concatenate(
                [outs[g][:, (2 * k) * tq:(2 * k + 1) * tq], outs[g][:, (2 * k + 1) * tq:(2 * k + 2) * tq]], axis=0).T


def nsa_prompt(p, b, s, g_q, b_gate, rel_table, ck, cvt, ka, vat, tables):
    tq = NSA_TQ
    nq = s // tq
    cols = NSA_GROUP * tq
    n_cmp = (s - CMP_LEN) // CMP_STRIDE + 1
    n_slc = -(-s // SLC_BLOCK)
    assert s % tq == 0 and nq >= NEAR_BLOCKS and n_cmp <= LANES and n_slc <= N_SEL_LANES and tq == 8 * CMP_STRIDE
    tiles, band, far_row = tables
    tiles = jnp.concatenate([tiles, jnp.full((1,) + tiles.shape[1:], MASK, F32)], axis=0)
    mt = np.zeros((N_SEL_LANES, LANES), np.float32)
    mt[:n_slc, :n_cmp] = _overlap_matrix(n_cmp, n_slc).T
    bg = _gate_bias_slabs(b_gate).reshape(1, NSA_KV_HEADS * LANES)
    gcol = (MIX_W + MEM_W) // (NSA_KV_HEADS * LANES)
    ka4 = ka.reshape(2 * NSA_KV_HEADS, b, s, LANES)
    c2 = lambda bi, qi: (0, 0)
    c3 = lambda bi, qi: (0, 0, 0)
    return pl.pallas_call(
        partial(_nsa_prompt_kernel, n_cmp=n_cmp, n_tiles=nq),
        grid=(b, nq),
        in_specs=[pl.BlockSpec(memory_space=pltpu.SMEM),
                  pl.BlockSpec((tq, MIX_W), lambda bi, qi: (bi * nq + qi, 0)),
                  pl.BlockSpec((tq, NSA_KV_HEADS * LANES), lambda bi, qi: (bi * nq + qi, gcol)),
                  pl.BlockSpec((1, LANES), c2),
                  pl.BlockSpec((1, NSA_KV_HEADS * LANES), c2),
                  pl.BlockSpec((None, NSA_KV_HEADS, LANES, LANES), lambda bi, qi: (bi, 0, 0, 0)),
                  pl.BlockSpec((None, NSA_KV_HEADS, HEAD_DIM, LANES), lambda bi, qi: (bi, 0, 0, 0)),
                  pl.BlockSpec((NSA_KV_HEADS, CMP_BAND, cols), c3),
                  pl.BlockSpec((NSA_KV_HEADS, 1, cols), c3),
                  pl.BlockSpec((N_SEL_LANES, LANES), c2),
                  pl.BlockSpec((2 * NSA_KV_HEADS, None, s, LANES), lambda bi, qi: (0, bi, 0, 0)),
                  pl.BlockSpec((2 * NSA_KV_HEADS, VT_ROWS, s), lambda bi, qi: (0, 0, bi)),
                  pl.BlockSpec((FULL_MASK_TILE + 1, NSA_KV_HEADS, tq, cols), lambda bi, qi: (0, 0, 0, 0))],
        out_specs=pl.BlockSpec((tq, MIX_W), lambda bi, qi: (bi * nq + qi, 0)),
        out_shape=jax.ShapeDtypeStruct((b * s, MIX_W), F32),
        scratch_shapes=[pltpu.VMEM((NSA_KV_HEADS, cols, LANES), BF16), pltpu.VMEM((NSA_KV_HEADS, 1, cols), F32),
                        pltpu.VMEM((NSA_KV_HEADS, VT_ROWS, cols), F32),
                        pltpu.VMEM((NSA_KV_HEADS, LANES + CMP_BAND, cols), F32)],
        compiler_params=_params("parallel", "arbitrary"),
        name="nsa_prompt",
    )(rel_table.astype(F32), p, p, _gain2(g_q), bg, ck, cvt, band, far_row, jnp.asarray(mt, BF16), ka4, vat, tiles)


def nsa_bias_tables(rel_table):
    tq = NSA_TQ
    table = rel_table.astype(F32)
    far = table[REL_BUCKETS - 1]
    r = np.arange(tq)[None, :]
    c = np.arange(tq)[:, None]

    def by_group(t):
        k = t.shape[0]
        return jnp.transpose(t.reshape(k, tq, NSA_KV_HEADS, NSA_GROUP), (2, 0, 3, 1)).reshape(NSA_KV_HEADS, k, -1)

    spread = (jnp.arange(2 * tq - 1)[:, None, None]
              == (tq - 1 - jnp.arange(tq)[None, :, None] + jnp.arange(tq)[None, None, :])).astype(F32)

    def tile(off, masked):
        d = off + np.arange(-(tq - 1), tq)
        v = jnp.where(jnp.asarray(masked(d))[:, None], MASK, table[_rel_bucket_np(d)] - far)
        return by_group(jnp.einsum('dh,dcr->crh', v, spread, precision=lax.Precision.HIGHEST))

    zero = jnp.zeros((NSA_KV_HEADS, tq, NSA_GROUP * tq), F32)
    tiles = jnp.stack([zero, tile(tq, lambda d: d < 0), tile(0, lambda d: d < 0),
                       tile(WINDOW, lambda d: d >= WINDOW)])
    n_rel = np.arange(CMP_BAND)[:, None] - CMP_BAND // 2
    dist = r - CMP_STRIDE * n_rel - (CMP_LEN - 1)
    band = by_group(jnp.where(jnp.asarray(dist < 0)[..., None], MASK, table[_rel_bucket_np(dist)]))
    far_row = by_group(jnp.broadcast_to(far, (1, tq, NSA_HEADS)))
    return tiles, band, far_row


def _gate_bias_slabs(b_gate):
    bg = b_gate.astype(F32).reshape(3, NSA_KV_HEADS, NSA_GROUP).transpose(1, 0, 2).reshape(NSA_KV_HEADS, 3 * NSA_GROUP)
    return _pad_cols(bg, LANES)


def split_cmp_ctx(ctx):
    b, n, _ = ctx.shape
    parts = ctx.reshape(b, n, 2, NSA_KV_HEADS, HEAD_DIM).transpose(2, 0, 3, 1, 4)
    parts = jnp.pad(parts, ((0, 0), (0, 0), (0, 0), (0, LANES - n), (0, 0)))
    ck = jnp.pad(parts[0], ((0, 0), (0, 0), (0, 0), (0, LANES - HEAD_DIM)))
    return ck, jnp.swapaxes(parts[1], -1, -2)


def _softmax_parts(s, s_new):
    m = jnp.maximum(jnp.max(s, axis=1, keepdims=True), s_new)
    p = jnp.exp(s - m)
    p_new = jnp.exp(s_new - m)
    inv = 1.0 / (jnp.sum(p, axis=1, keepdims=True) + p_new)
    return p * inv, p_new * inv


DECODE_SEQS = 2


def _nsa_decode_kernel(*refs, n_pages):
    (pt_ref, p_ref, gq_ref, bg_ref, cmp_ref, cb_ref, mt_ref) = refs[:7]
    n_in = DECODE_SEQS * n_pages
    page_refs = refs[7:7 + n_in]
    (snew_ref, sb_ref, e_ref, win_ref, wnew_ref, wb_ref, b0_ref, o_ref) = refs[7 + n_in:]
    del pt_ref
    for s in range(DECODE_SEQS):
        _nsa_decode_seq(p_ref.at[s], gq_ref, bg_ref, cmp_ref.at[s], cb_ref, mt_ref,
                        page_refs[s * n_pages:(s + 1) * n_pages], snew_ref.at[s], sb_ref, e_ref, win_ref.at[s],
                        wnew_ref.at[s], wb_ref, b0_ref, o_ref.at[s])


def _nsa_decode_seq(p_ref, gq_ref, bg_ref, cmp_ref, cb_ref, mt_ref, page_refs, snew_ref, sb_ref, e_ref, win_ref,
                    wnew_ref, wb_ref, b0_ref, o_ref):
    n_pages = len(page_refs)
    n_keys = n_pages * PAGE_SIZE
    lo1 = lax.broadcasted_iota(jnp.int32, (1, LANES), 1) < HEAD_DIM
    rowi = lax.broadcasted_iota(jnp.int32, (QROWS, LANES), 0)
    lanei = lax.broadcasted_iota(jnp.int32, (QROWS, LANES), 1)
    row_g = rowi // 8
    own = (lanei // HEAD_DIM) == row_g

    qbd = jnp.zeros((QROWS, LANES), F32)
    for k in range(NSA_HEADS // 2):
        y = _half_rms(p_ref[:, k * LANES:(k + 1) * LANES], gq_ref[...], lo1) * HEAD_DIM ** -0.5
        yr = pltpu.roll(y, HEAD_DIM, 1)
        for half in range(2):
            g, z = divmod(2 * k + half, NSA_GROUP)
            qbd = jnp.where(rowi == 8 * g + z, y if half == g else yr, qbd)
    qbd = jnp.where(own, qbd, 0.0)
    qb = qbd.astype(BF16)

    s = _dot_nt(qb, cmp_ref[:, :LANES]) + cb_ref[...]
    m = jnp.maximum(jnp.max(s, axis=1, keepdims=True), NEG)
    p = jnp.exp(s - m)
    l = jnp.sum(p, axis=1, keepdims=True)
    pn = p * jnp.where(l > 0.0, 1.0 / l, 0.0)
    o_cmp = jnp.dot(pn.astype(BF16), cmp_ref[:, LANES:], preferred_element_type=F32)

    valid_row = (rowi % 8) < NSA_GROUP
    ps0 = jnp.sum(jnp.where(valid_row & (row_g == 0), pn, 0.0), axis=0, keepdims=True)
    ps1 = jnp.sum(jnp.where(valid_row & (row_g == 1), pn, 0.0), axis=0, keepdims=True)
    r8 = lax.broadcasted_iota(jnp.int32, (8, LANES), 0)
    j8 = lax.broadcasted_iota(jnp.int32, (8, LANES), 1)
    psum = jnp.where(r8 == 0, ps0, jnp.where(r8 == 1, ps1, 0.0))
    p_hi = psum.astype(BF16)
    p_lo = (psum - p_hi.astype(F32)).astype(BF16)
    imp = (jnp.dot(p_hi, mt_ref[...], preferred_element_type=F32)
           + jnp.dot(p_lo, mt_ref[...], preferred_element_type=F32))
    cur = n_keys // SLC_BLOCK
    forced = (j8 == 0) | (j8 == cur) | (j8 == cur - 1)
    score = jnp.where(forced, FORCED, jnp.where(j8 <= cur, imp, -1.0))
    rank = jnp.zeros((8, LANES), F32)
    for i in range(cur + 1):
        ri = score[:, i:i + 1]
        beats = (ri > score) | ((ri == score) & (j8 > i))
        rank = rank + jnp.where(beats, 1.0, 0.0)
    selb = jnp.where(rank < N_SELECT, 0.0, MASK)
    sel16 = jnp.where(row_g == 0, selb[0:1, :], selb[1:2, :]).astype(BF16)
    blockmask = jnp.dot(sel16, e_ref[...], preferred_element_type=F32)

    s = jnp.concatenate([_dot_nt(qb, r[:, :LANES].astype(BF16)) for r in page_refs], axis=1)
    s = s + sb_ref[...] + blockmask
    s_new = jnp.sum(qbd * snew_ref[:, :LANES], axis=1, keepdims=True) + b0_ref[...]
    pp, p_new = _softmax_parts(s, s_new)
    o_slc = p_new * snew_ref[:, LANES:]
    for k, r in enumerate(page_refs):
        o_slc = o_slc + jnp.dot(pp[:, k * PAGE_SIZE:(k + 1) * PAGE_SIZE].astype(BF16), r[:, LANES:].astype(BF16),
                                preferred_element_type=F32)

    n_wt = win_ref.shape[0] // LANES
    s = jnp.concatenate([_dot_nt(qb, win_ref[k * LANES:(k + 1) * LANES, :LANES].astype(BF16)) for k in range(n_wt)],
                        axis=1) + wb_ref[...]
    s_new = jnp.sum(qbd * wnew_ref[:, :LANES], axis=1, keepdims=True) + b0_ref[...]
    pp, p_new = _softmax_parts(s, s_new)
    o_win = p_new * wnew_ref[:, LANES:]
    for k in range(n_wt):
        o_win = o_win + jnp.dot(pp[:, k * LANES:(k + 1) * LANES].astype(BF16),
                                win_ref[k * LANES:(k + 1) * LANES, LANES:].astype(BF16), preferred_element_type=F32)

    gbase = MIX_W + MEM_W
    sig = jax.nn.sigmoid(p_ref[:, gbase:gbase + NSA_KV_HEADS * LANES] + bg_ref[...])
    gate = [jnp.zeros((QROWS, 1), F32) for _ in range(3)]
    rcol = lax.broadcasted_iota(jnp.int32, (QROWS, 1), 0)
    for br in range(3):
        for g in range(NSA_KV_HEADS):
            for z in range(NSA_GROUP):
                c = g * LANES + br * NSA_GROUP + z
                gate[br] = jnp.where(rcol == 8 * g + z, sig[:, c:c + 1], gate[br])
    comb = jnp.where(own, gate[0] * o_cmp + gate[1] * o_slc + gate[2] * o_win, 0.0)
    comb_r = pltpu.roll(comb, HEAD_DIM, 1)
    for k in range(NSA_HEADS // 2):
        acc = jnp.zeros((1, LANES), F32)
        for half in range(2):
            g, z = divmod(2 * k + half, NSA_GROUP)
            src = comb if half == g else comb_r
            acc = acc + src[8 * g + z:8 * g + z + 1, :]
        o_ref[:, k * LANES:(k + 1) * LANES] = acc


def nsa_decode_bias(rel_table, past, wb):
    table = rel_table.astype(F32)

    def rows16(t):
        z = jnp.zeros((2, t.shape[1]), F32)
        return jnp.concatenate([t[:NSA_GROUP], z, t[NSA_GROUP:], z], axis=0)

    n_cmp = (past + 1 - CMP_LEN) // CMP_STRIDE + 1
    c_end = np.arange(LANES) * CMP_STRIDE + CMP_LEN - 1
    dist = past - c_end
    cb = jnp.where(jnp.asarray((dist < 0) | (np.arange(LANES) >= n_cmp))[None], MASK, table[_rel_bucket_np(dist)].T)
    sb = table[_rel_bucket_np(past - np.arange(past))].T
    wpos = past - wb + np.arange(wb)
    wbias = jnp.where(jnp.asarray(past - wpos >= WINDOW)[None], MASK, table[_rel_bucket_np(past - wpos)].T)
    b0 = table[0][:, None]
    return rows16(cb), rows16(sb), rows16(wbias), rows16(b0)


def nsa_decode(p, g_q, b_gate, cmp_ctx, page_table, cache_slc, slc_new, cache_win, win_new, biases):
    b, n_pages = page_table.shape
    past = n_pages * PAGE_SIZE
    wb = cache_win.shape[1]
    cb, sb, wbias, b0 = biases
    n_cmp = (past + 1 - CMP_LEN) // CMP_STRIDE + 1
    n_slc = -(-(past + 1) // SLC_BLOCK)
    assert n_slc <= LANES and n_cmp <= LANES and wb % LANES == 0
    mt = np.zeros((LANES, LANES), np.float32)
    mt[:n_cmp, :n_slc] = _overlap_matrix(n_cmp, n_slc)
    e = (np.arange(past)[None, :] // SLC_BLOCK == np.arange(LANES)[:, None]).astype(np.float32)
    bg = _gate_bias_slabs(b_gate).reshape(1, NSA_KV_HEADS * LANES)
    ns = DECODE_SEQS
    assert b % ns == 0
    c2 = lambda bi, pt: (0, 0)
    seq3 = lambda bi, pt: (bi, 0, 0)
    in_specs = ([pl.BlockSpec((ns, 1, p.shape[1]), seq3),
                 pl.BlockSpec((1, LANES), c2),
                 pl.BlockSpec((1, NSA_KV_HEADS * LANES), c2),
                 pl.BlockSpec((ns, LANES, KV_ROW), seq3),
                 pl.BlockSpec((QROWS, LANES), c2),
                 pl.BlockSpec((LANES, LANES), c2)]
                + [pl.BlockSpec((None, PAGE_SIZE, KV_ROW), partial(lambda s, k, bi, pt: (pt[ns * bi + s, k], 0, 0), s, k))
                   for s in range(ns) for k in range(n_pages)]
                + [pl.BlockSpec((ns, 1, KV_ROW), seq3),
                   pl.BlockSpec((QROWS, past), c2),
                   pl.BlockSpec((LANES, past), c2),
                   pl.BlockSpec((ns, wb, KV_ROW), seq3),
                   pl.BlockSpec((ns, 1, KV_ROW), seq3),
                   pl.BlockSpec((QROWS, wb), c2),
                   pl.BlockSpec((QROWS, 1), c2)])
    gs = pltpu.PrefetchScalarGridSpec(num_scalar_prefetch=1, grid=(b // ns,), in_specs=in_specs,
                                      out_specs=pl.BlockSpec((ns, 1, MIX_W), seq3))
    out = pl.pallas_call(
        partial(_nsa_decode_kernel, n_pages=n_pages),
        grid_spec=gs,
        out_shape=jax.ShapeDtypeStruct((b, 1, MIX_W), F32),
        compiler_params=_params("parallel"),
        name="nsa_decode",
    )(page_table, p.reshape(b, 1, -1), _gain2(g_q), bg, cmp_ctx, cb, jnp.asarray(mt, BF16),
      *([cache_slc] * (ns * n_pages)), slc_new.reshape(b, 1, KV_ROW), sb, jnp.asarray(e, BF16),
      cache_win, win_new.reshape(b, 1, KV_ROW), wbias, b0)
    return out.reshape(b, MIX_W)


GATE_PAD = LANES


def _pad_heads_cols(w):
    d = w.shape[0]
    return jnp.pad(w.reshape(d, MLSTM_HEADS, MLSTM_DH), ((0, 0), (0, 0), (0, MLSTM_PAD - MLSTM_DH))).reshape(d, MLSTM_HW)


def prep_weights(w_in_a, w_in_b, w_kv, w_mem_kv, w_out, w_gu, w_down):
    w = {}
    n_gate_a = 2 * MLSTM_HEADS
    n_gate_b = 3 * NSA_HEADS
    w['in_a'] = []
    for l in range(N_A_LAYERS):
        wa = w_in_a[l]
        qkvo = [_pad_heads_cols(wa[:, j * MIX_W:(j + 1) * MIX_W]) for j in range(4)]
        w['in_a'].append(jnp.concatenate(qkvo + [wa[:, 4 * MIX_W + n_gate_a:],
                                                 _pad_cols(wa[:, 4 * MIX_W:4 * MIX_W + n_gate_a], GATE_PAD)],
                                         axis=1).astype(BF16))
    w['in_b'] = []
    for j in range(N_B_LAYERS):
        wg = w_in_b[j][:, MIX_W:MIX_W + n_gate_b].reshape(D_MODEL, 3, NSA_KV_HEADS, NSA_GROUP)
        slabs = [_pad_cols(wg[:, :, g].reshape(D_MODEL, 3 * NSA_GROUP), GATE_PAD) for g in range(NSA_KV_HEADS)]
        w['in_b'].append(jnp.concatenate([w_in_b[j][:, :MIX_W], w_in_b[j][:, MIX_W + n_gate_b:]] + slabs,
                                         axis=1).astype(BF16))
    w['out_mix'] = []
    for l in range(DEPTH):
        wm = w_out[l][:MIX_W]
        if l < N_A_LAYERS:
            wm = jnp.pad(wm.reshape(MLSTM_HEADS, MLSTM_DH, D_MODEL),
                         ((0, 0), (0, MLSTM_PAD - MLSTM_DH), (0, 0))).reshape(MLSTM_HW, D_MODEL)
        w['out_mix'].append(wm.astype(BF16))
    w['out_mem'] = w_out[:, MIX_W:].astype(BF16)
    w['kv'] = w_kv.astype(BF16)
    w['mem_kv'] = w_mem_kv.astype(BF16)
    w['gu'] = w_gu.astype(BF16)
    w['down'] = w_down.astype(BF16)
    return w


def kernel(x_prompt, x_sample, state_mlstm_c, state_mlstm_n, state_mlstm_m, cache_mem_kv,
           cache_cmp_kv, cache_slc_kv, cache_win_kv, page_table, mem_prompt,
           g_mix, w_in_a, b_if, g_hnorm, w_in_b, g_q, b_gate, rel_table, g_kv, w_kv,
           g_k_slc, g_k_win, g_k_cmp, cmp_pos, cmp_w1, cmp_b1, cmp_w2, cmp_b2,
           w_mem_kv, g_mem_k, g_mem_q, w_out, g_ffn, w_gu, w_down):
    w = prep_weights(w_in_a, w_in_b, w_kv, w_mem_kv, w_out, w_gu, w_down)
    cw = compress_weights(cmp_w1, cmp_b1, cmp_w2, cmp_b2, cmp_pos, g_k_cmp)
    qcol_a = 4 * MLSTM_HW // MEM_W
    qcol_b = MIX_W // MEM_W

    def finish_layer(l, x, mix, mem_o):
        return layer_tail(mix, w['out_mix'][l], mem_o, w['out_mem'][l], x, g_ffn[l], w['gu'][l], w['down'][l])

    b_p, s_p, _ = x_prompt.shape
    m_tok = mem_prompt.shape[1]
    mem_kv_p = mem_kv_rows(mem_prompt.reshape(b_p * m_tok, D_MODEL), w['mem_kv'], g_mem_k)
    mem_kv_p = mem_kv_p.reshape(DEPTH, b_p, m_tok, 2 * MEM_W)
    rel2 = rel_table.astype(F32) * LOG2E
    tables = nsa_bias_tables(rel2)
    c0, m0 = pack_mlstm_state(jnp.zeros((b_p, MLSTM_HEADS, MLSTM_DH, MLSTM_DH), F32),
                              jnp.zeros((b_p, MLSTM_HEADS, MLSTM_DH), F32),
                              jnp.full((b_p, MLSTM_HEADS), NEG, F32))
    x = x_prompt.reshape(b_p * s_p, D_MODEL)
    st_p = []
    for l in range(N_A_LAYERS):
        p = norm_matmul(x, g_mix[l], w['in_a'][l])
        mix, cn, mm = mlstm_prompt(p, b_p, s_p, b_if[l], g_hnorm[l], c0, m0)
        st_p.append(unpack_mlstm_state(cn, mm))
        mem_o = mem_attn_prompt(p, qcol_a, b_p, s_p, g_mem_q[l], mem_kv_p, l)
        x = finish_layer(l, x, mix, mem_o)
    cmp_p, slc_p, win_p, ka, vat = kv_rows(x, g_kv, w['kv'], g_k_slc, g_k_win, s_p, aug=True)
    ck, cvt = split_cmp_ctx(compress_rows(cmp_p.reshape(b_p, s_p // CMP_STRIDE, SEG_ROW), cw))
    for j in range(N_B_LAYERS):
        l = N_A_LAYERS + j
        p = norm_matmul(x, g_mix[l], w['in_b'][j])
        mix = nsa_prompt(p, b_p, s_p, g_q[j], b_gate[j], rel2, ck, cvt, ka, vat, tables)
        mem_o = mem_attn_prompt(p, qcol_b, b_p, s_p, g_mem_q[l], mem_kv_p, l)
        x = finish_layer(l, x, mix, mem_o)
    y_p = x.reshape(b_p, s_p, D_MODEL)

    b_s, s_s, _ = x_sample.shape
    assert s_s == 1
    n_pages = page_table.shape[1]
    past = n_pages * PAGE_SIZE
    wb = cache_win_kv.shape[1]
    n_pool = cache_cmp_kv.shape[0]
    x = x_sample.reshape(b_s, D_MODEL)
    cache_mem = cache_mem_kv.reshape(DEPTH, b_s, -1, 2 * MEM_W)
    st_s = []
    for l in range(N_A_LAYERS):
        p = norm_matmul(x, g_mix[l], w['in_a'][l])
        mix, c_new, n_new, m_new = mlstm_step(p, b_if[l], g_hnorm[l], state_mlstm_c, state_mlstm_n, state_mlstm_m, l)
        st_s.append((c_new, n_new, m_new))
        mem_o = mem_attn_decode(p, qcol_a, g_mem_q[l], cache_mem, l)
        x = finish_layer(l, x, mix, mem_o)
    cmp_s, slc_s, win_s = kv_rows(x, g_kv, w['kv'], g_k_slc, g_k_win, 1, aug=False)
    assert (past + 1 - CMP_LEN) // CMP_STRIDE + 1 == past // CMP_STRIDE - 1
    ctx_s = compress_rows(cache_cmp_kv.reshape(n_pool, SEG_PER_PAGE, SEG_ROW), cw, page_table)
    biases = nsa_decode_bias(rel_table, past, wb)
    cache_slc = cache_slc_kv.reshape(n_pool, PAGE_SIZE, KV_ROW)
    cache_win = cache_win_kv.reshape(b_s, wb, KV_ROW)
    for j in range(N_B_LAYERS):
        l = N_A_LAYERS + j
        p = norm_matmul(x, g_mix[l], w['in_b'][j])
        mix = nsa_decode(p, g_q[j], b_gate[j], ctx_s, page_table, cache_slc, slc_s, cache_win, win_s, biases)
        mem_o = mem_attn_decode(p, qcol_b, g_mem_q[l], cache_mem, l)
        x = finish_layer(l, x, mix, mem_o)
    y_s = x.reshape(b_s, 1, D_MODEL)

    rows5 = lambda r, b: r.reshape(b, -1, 2, NSA_KV_HEADS, HEAD_DIM)
    win_p5 = rows5(win_p, b_p)
    p_win = win_p5[:, s_p - min(WINDOW, s_p):]
    s_win = jnp.concatenate([cache_win_kv, rows5(win_s, b_s)], axis=1)[:, -wb:]
    stack = lambda sts, i: jnp.stack([st[i] for st in sts])
    return (y_p, y_s, stack(st_p, 0), stack(st_p, 1), stack(st_p, 2),
            mem_kv_p.reshape(DEPTH, b_p, m_tok, 2, MEM_HEADS, HEAD_DIM),
            rows5(cmp_p, b_p), rows5(slc_p, b_p), p_win,
            stack(st_s, 0), stack(st_s, 1), stack(st_s, 2),
            rows5(cmp_s, b_s), rows5(slc_s, b_s), s_win)
```

```python
import math
from functools import partial

import numpy as np
import jax
import jax.numpy as jnp
from jax import lax
from jax.experimental import pallas as pl
from jax.experimental.pallas import tpu as pltpu

D_MODEL = 1024
DEPTH = 4
PAGE_SIZE = 128
N_A_LAYERS = DEPTH // 2
N_B_LAYERS = DEPTH - N_A_LAYERS
MIX_W = 3 * D_MODEL // 4
MEM_W = D_MODEL - MIX_W
HEAD_DIM = 64
MEM_HEADS = MEM_W // HEAD_DIM
MLSTM_HEADS = 4
MLSTM_DH = MIX_W // MLSTM_HEADS
NSA_HEADS = MIX_W // HEAD_DIM
NSA_KV_HEADS = 2
NSA_GROUP = NSA_HEADS // NSA_KV_HEADS
CMP_LEN = 32
CMP_STRIDE = 16
CMP_HIDDEN = 128
SLC_BLOCK = 64
N_SELECT = 8
WINDOW = 512
REL_BUCKETS = 32
REL_MAX_EXACT = 16
REL_MAX_DIST = 128
D_FF = -(-(8 * D_MODEL) // (3 * 256)) * 256
EPS = 1e-6
NEG = -1e30
FORCED = 1e6

LANES = 128
VMEM_LIMIT = 48 * 1024 * 1024
TAIL_VMEM_LIMIT = 56 * 1024 * 1024

BF16 = jnp.bfloat16
F32 = jnp.float32


def _pick_tile(n, candidates):
    for c in candidates:
        if n % c == 0:
            return c
    return n


def _pad_cols(w, n):
    return jnp.pad(w, ((0, 0), (0, n - w.shape[1])))


def _dot_nt(a, b):
    return lax.dot_general(a, b, (((1,), (1,)), ((), ())), preferred_element_type=F32)


def _rms_rows(x, g):
    return x * lax.rsqrt(jnp.mean(x * x, axis=-1, keepdims=True) + EPS) * g


def _half_rms(x, g2, lo):
    x2 = x * x
    ss_lo = jnp.sum(jnp.where(lo, x2, 0.0), axis=1, keepdims=True)
    ss_hi = jnp.sum(jnp.where(lo, 0.0, x2), axis=1, keepdims=True)
    inv = jnp.where(lo, lax.rsqrt(ss_lo / HEAD_DIM + EPS), lax.rsqrt(ss_hi / HEAD_DIM + EPS))
    return x * inv * g2


def _gain2(g):
    return jnp.concatenate([g, g]).reshape(1, LANES).astype(F32)


def _params(*sem):
    return pltpu.CompilerParams(dimension_semantics=sem, vmem_limit_bytes=VMEM_LIMIT)


def _norm_matmul_kernel(x_ref, g_ref, w_ref, o_ref, xn_ref):
    @pl.when(pl.program_id(1) == 0)
    def _():
        xn_ref[...] = _rms_rows(x_ref[...], g_ref[...]).astype(BF16)

    o_ref[...] = jnp.dot(xn_ref[...], w_ref[...], preferred_element_type=F32)


def norm_matmul(x, g, w):
    m, k = x.shape
    n = w.shape[1]
    tm = _pick_tile(m, (1024, 512, 256, 128))
    tn = _pick_tile(n, (1280, 1152, 1024, 896, 768, 640, 512, 384, 256, 128))
    return pl.pallas_call(
        _norm_matmul_kernel,
        grid=(m // tm, n // tn),
        in_specs=[pl.BlockSpec((tm, k), lambda i, j: (i, 0)),
                  pl.BlockSpec((1, k), lambda i, j: (0, 0)),
                  pl.BlockSpec((k, tn), lambda i, j: (0, j))],
        out_specs=pl.BlockSpec((tm, tn), lambda i, j: (i, j)),
        out_shape=jax.ShapeDtypeStruct((m, n), F32),
        scratch_shapes=[pltpu.VMEM((tm, k), BF16)],
        compiler_params=_params("parallel", "arbitrary"),
        name="norm_matmul",
    )(x, g.reshape(1, k), w)


def _mem_kv_kernel(x_ref, w_ref, g_ref, o_ref):
    tm = x_ref.shape[0]
    kv = jnp.dot(x_ref[...].astype(BF16), w_ref[...], preferred_element_type=F32)
    lo = lax.broadcasted_iota(jnp.int32, (tm, LANES), 1) < HEAD_DIM
    for k in range(MEM_HEADS // 2):
        o_ref[:, k * LANES:(k + 1) * LANES] = _half_rms(kv[:, k * LANES:(k + 1) * LANES], g_ref[...], lo)
    o_ref[:, MEM_W:] = kv[:, MEM_W:]


def mem_kv_rows(mem, w, g_k):
    m, d = mem.shape
    depth, _, n = w.shape
    tm = _pick_tile(m, (512, 256, 128))
    g2 = jnp.concatenate([g_k, g_k], axis=1).reshape(depth, 1, LANES).astype(F32)
    return pl.pallas_call(
        _mem_kv_kernel,
        grid=(depth, m // tm),
        in_specs=[pl.BlockSpec((tm, d), lambda l, i: (i, 0)),
                  pl.BlockSpec((None, d, n), lambda l, i: (l, 0, 0)),
                  pl.BlockSpec((None, 1, LANES), lambda l, i: (l, 0, 0))],
        out_specs=pl.BlockSpec((None, tm, n), lambda l, i: (l, i, 0)),
        out_shape=jax.ShapeDtypeStruct((depth, m, n), F32),
        compiler_params=_params("parallel", "parallel"),
        name="mem_kv_rows",
    )(mem, w, g2)


def _layer_tail_kernel(a_ref, wa_ref, b_ref, wb_ref, x_ref, g_ref, wg_ref, wu_ref, wd_ref, o_ref, xn_ref):
    f = pl.program_id(1)

    @pl.when(f == 0)
    def _():
        x1 = (x_ref[...] + jnp.dot(a_ref[...].astype(BF16), wa_ref[...], preferred_element_type=F32)
              + jnp.dot(b_ref[...].astype(BF16), wb_ref[...], preferred_element_type=F32))
        o_ref[...] = x1
        xn_ref[...] = _rms_rows(x1, g_ref[...]).astype(BF16)

    xn = xn_ref[...]
    gate = jnp.dot(xn, wg_ref[...], preferred_element_type=F32)
    up = jnp.dot(xn, wu_ref[...], preferred_element_type=F32)
    act = (gate * jax.nn.sigmoid(gate) * up).astype(BF16)
    o_ref[...] += jnp.dot(act, wd_ref[...], preferred_element_type=F32)


def layer_tail(a, wa, b, wb, x, g, w_gu, w_down):
    m, d = x.shape
    tm = _pick_tile(m, (512, 256, 128))
    tf = 1408
    nf = D_FF // tf
    ka, kb = a.shape[1], b.shape[1]
    row = lambda k: pl.BlockSpec((tm, k), lambda i, f: (i, 0))
    const = lambda shape: pl.BlockSpec(shape, lambda i, f: (0, 0))
    return pl.pallas_call(
        _layer_tail_kernel,
        grid=(m // tm, nf),
        in_specs=[row(ka), const((ka, d)), row(kb), const((kb, d)), row(d), const((1, d)),
                  pl.BlockSpec((d, tf), lambda i, f: (0, f)),
                  pl.BlockSpec((d, tf), lambda i, f: (0, f + nf)),
                  pl.BlockSpec((tf, d), lambda i, f: (f, 0))],
        out_specs=row(d),
        out_shape=jax.ShapeDtypeStruct((m, d), F32),
        scratch_shapes=[pltpu.VMEM((tm, d), BF16)],
        compiler_params=pltpu.CompilerParams(dimension_semantics=("parallel", "arbitrary"),
                                             vmem_limit_bytes=TAIL_VMEM_LIMIT),
        name="layer_tail",
    )(a, wa, b, wb, x, g.reshape(1, d), w_gu, w_gu, w_down)


MLSTM_L = 128
MLSTM_PAD = 256
N_COL = MLSTM_DH
MLSTM_HW = MLSTM_HEADS * MLSTM_PAD
STEP_TB = 8


def _exact_tri_cumsum(tri, x):
    hi = x.astype(BF16)
    r1 = x - hi.astype(F32)
    mid = r1.astype(BF16)
    lo = (r1 - mid.astype(F32)).astype(BF16)
    return (jnp.dot(tri, hi, preferred_element_type=F32) + jnp.dot(tri, mid, preferred_element_type=F32)
            + jnp.dot(tri, lo, preferred_element_type=F32))


def _mlstm_kernel(q_ref, k_ref, v_ref, og_ref, gate_ref, bif_ref, gh_ref, c0_ref, m0_ref,
                  mix_ref, cout_ref, mout_ref, c_scr, m_scr):
    L = MLSTM_L
    P = MLSTM_PAD
    ci = pl.program_id(1)

    @pl.when(ci == 0)
    def _():
        c_scr[...] = c0_ref[...]
        m_scr[...] = m0_ref[...]

    row = lax.broadcasted_iota(jnp.int32, (L, L), 0)
    col = lax.broadcasted_iota(jnp.int32, (L, L), 1)
    causal = col <= row
    tri = jnp.where(causal, 1.0, 0.0).astype(BF16)
    lane_p = lax.broadcasted_iota(jnp.int32, (L, P), 1)

    gl = gate_ref[...] + bif_ref[...]
    logf = jax.nn.log_sigmoid(gl)
    bc = _exact_tri_cumsum(tri, logf)
    bt = bc.T
    gt = gl.T
    for h in range(MLSTM_HEADS):
        fh = MLSTM_HEADS + h
        b_col = bc[:, fh:fh + 1]
        b_row = bt[fh:fh + 1, :]
        i_row = gt[h:h + 1, :]
        i_col = gl[:, h:h + 1]
        m_prev = m_scr[0:1, h:h + 1]
        log_d = jnp.where(causal, b_col - b_row + i_row, -jnp.inf)
        log_inter = b_col + m_prev
        m_t = jnp.maximum(jnp.max(log_d, axis=1, keepdims=True), log_inter)
        w_intra = jnp.exp(log_d - m_t)
        w_inter = jnp.exp(log_inter - m_t)
        sl = slice(h * P, (h + 1) * P)
        q = q_ref[:, sl].astype(BF16)
        kf = k_ref[:, sl] * MLSTM_DH ** -0.5
        v_aug = jnp.where(lane_p == N_COL, 1.0, v_ref[:, sl])
        sc = _dot_nt(q, kf.astype(BF16)) * w_intra
        c = c_scr[h]
        nd = (jnp.dot(sc.astype(BF16), v_aug.astype(BF16), preferred_element_type=F32)
              + w_inter * jnp.dot(q, c.astype(BF16), preferred_element_type=F32))
        den = nd[:, N_COL:N_COL + 1]
        hh = jnp.where(lane_p < N_COL, nd / jnp.maximum(jnp.abs(den), jnp.exp(-m_t)), 0.0)
        y = hh * lax.rsqrt(jnp.sum(hh * hh, axis=1, keepdims=True) / MLSTM_DH + EPS) * gh_ref[...]
        mix_ref[:, sl] = jax.nn.sigmoid(og_ref[:, sl]) * y
        g_last = bc[L - 1:L, fh:fh + 1]
        m_new = m_t[L - 1:L, :]
        w_s = jnp.exp(g_last - b_col + i_col - m_new)
        decay = jnp.exp(g_last + m_prev - m_new)
        upd = jnp.dot(kf.T.astype(BF16), (w_s * v_aug).astype(BF16), preferred_element_type=F32)
        c_scr[h] = decay * c + upd
        m_scr[0:1, h:h + 1] = m_new

    @pl.when(ci == pl.num_programs(1) - 1)
    def _():
        cout_ref[...] = c_scr[...]
        mout_ref[...] = m_scr[...]


def mlstm_prompt(p, b, s, b_if, g_hnorm, c0, m0):
    L = MLSTM_L
    P = MLSTM_PAD
    hw = MLSTM_HW
    assert s % L == 0
    nc = s // L
    gcol = (4 * hw + MEM_W) // LANES
    bif = _pad_cols(b_if.astype(F32).reshape(1, -1), LANES)
    gh = _pad_cols(g_hnorm.astype(F32).reshape(1, -1), P)
    blk = lambda j: pl.BlockSpec((L, hw), lambda bi, ci: (bi * nc + ci, j))
    return pl.pallas_call(
        _mlstm_kernel,
        grid=(b, nc),
        in_specs=[blk(0), blk(1), blk(2), blk(3),
                  pl.BlockSpec((L, LANES), lambda bi, ci: (bi * nc + ci, gcol)),
                  pl.BlockSpec((1, LANES), lambda bi, ci: (0, 0)),
                  pl.BlockSpec((1, P), lambda bi, ci: (0, 0)),
                  pl.BlockSpec((None, MLSTM_HEADS, P, P), lambda bi, ci: (bi, 0, 0, 0)),
                  pl.BlockSpec((None, 1, LANES), lambda bi, ci: (bi, 0, 0))],
        out_specs=[pl.BlockSpec((L, hw), lambda bi, ci: (bi * nc + ci, 0)),
                   pl.BlockSpec((None, MLSTM_HEADS, P, P), lambda bi, ci: (bi, 0, 0, 0)),
                   pl.BlockSpec((None, 1, LANES), lambda bi, ci: (bi, 0, 0))],
        out_shape=[jax.ShapeDtypeStruct((b * s, hw), F32),
                   jax.ShapeDtypeStruct((b, MLSTM_HEADS, P, P), F32),
                   jax.ShapeDtypeStruct((b, 1, LANES), F32)],
        scratch_shapes=[pltpu.VMEM((MLSTM_HEADS, P, P), F32), pltpu.VMEM((1, LANES), F32)],
        compiler_params=_params("parallel", "arbitrary"),
        name="mlstm_prompt",
    )(p, p, p, p, p, bif, gh, c0, m0)


def pack_mlstm_state(c, n, m):
    pad = MLSTM_PAD - MLSTM_DH
    cn = jnp.concatenate([c, n[..., None]], axis=-1)
    cn = jnp.pad(cn, ((0, 0), (0, 0), (0, pad), (0, pad - 1)))
    return cn, _pad_cols(m, LANES)[:, None, :]


def unpack_mlstm_state(cn, m):
    return cn[:, :, :MLSTM_DH, :MLSTM_DH], cn[:, :, :MLSTM_DH, MLSTM_DH], m[:, 0, :MLSTM_HEADS]


def _mlstm_step_kernel(q_ref, v_ref, og_ref, gate_ref, kt_ref, bif_ref, gh_ref, c_ref, n_ref, m_ref,
                       mix_ref, cout_ref, nout_ref, mout_ref):
    tb = q_ref.shape[0]
    dh = MLSTM_DH
    gl = gate_ref[...] + bif_ref[...]
    logf = jax.nn.log_sigmoid(gl)
    i4 = gl[:, :MLSTM_HEADS]
    f4 = logf[:, MLSTM_HEADS:2 * MLSTM_HEADS]
    m_prev = m_ref[...]
    m_t = jnp.maximum(i4, f4 + m_prev)
    w_in = jnp.exp(i4 - m_t)
    w_dec = jnp.exp(f4 + m_prev - m_t)
    floor = jnp.exp(-m_t)
    mout_ref[...] = m_t
    mix_ref[...] = jnp.zeros(mix_ref.shape, F32)
    rowi = lax.broadcasted_iota(jnp.int32, (LANES, dh), 0)
    for j in range(tb):
        for h in range(MLSTM_HEADS):
            sl = slice(h * MLSTM_PAD, h * MLSTM_PAD + dh)
            wi = w_in[j:j + 1, h:h + 1]
            wd = w_dec[j:j + 1, h:h + 1]
            q_row = q_ref[j:j + 1, sl]
            v_row = v_ref[j:j + 1, sl]
            k_row = kt_ref[1, h, j:j + 1, :]
            c = c_ref[j, h]
            n_row = n_ref[j, h:h + 1, :]
            qk = jnp.sum(q_row * k_row, axis=1, keepdims=True) * wi
            qc = jnp.dot(q_ref[:, sl].astype(BF16), c.astype(BF16), preferred_element_type=F32)[j:j + 1, :]
            qn = jnp.sum(q_row * n_row, axis=1, keepdims=True)
            v_sel = jnp.where(rowi == j, wi * v_row, 0.0).astype(BF16)
            kv = jnp.dot(kt_ref[0, h, :, :LANES].astype(BF16), v_sel, preferred_element_type=F32)
            num = qk * v_row + wd * qc
            den = qk + wd * qn
            hh = num / jnp.maximum(jnp.abs(den), floor[j:j + 1, h:h + 1])
            y = hh * lax.rsqrt(jnp.sum(hh * hh, axis=1, keepdims=True) / dh + EPS) * gh_ref[...]
            mix_ref[j:j + 1, sl] = jax.nn.sigmoid(og_ref[j:j + 1, sl]) * y
            cout_ref[j, h] = wd * c + kv
            nout_ref[j, h:h + 1, :] = wd * n_row + wi * k_row


def mlstm_step(p, b_if, g_hnorm, c, n, m, layer):
    b = p.shape[0]
    tb = STEP_TB
    hw = MLSTM_HW
    dh = MLSTM_DH
    nb = b // tb
    gcol = (4 * hw + MEM_W) // LANES
    k = p[:, hw:2 * hw].reshape(nb, tb, MLSTM_HEADS, MLSTM_PAD)[..., :dh] * dh ** -0.5
    k_cols = jnp.pad(k.transpose(0, 2, 3, 1), ((0, 0), (0, 0), (0, 0), (0, dh - tb)))
    k_rows = jnp.pad(k.transpose(0, 2, 1, 3), ((0, 0), (0, 0), (0, dh - tb), (0, 0)))
    kt = jnp.stack([k_cols, k_rows], axis=1)
    bif = _pad_cols(b_if.astype(F32).reshape(1, -1), LANES)
    gh = g_hnorm.astype(F32).reshape(1, dh)
    blk = lambda j: pl.BlockSpec((tb, hw), lambda i: (i, j))
    st4 = pl.BlockSpec((tb, MLSTM_HEADS, dh, dh), lambda i: (i, 0, 0, 0))
    st3 = pl.BlockSpec((tb, MLSTM_HEADS, dh), lambda i: (i, 0, 0))
    st2 = pl.BlockSpec((tb, MLSTM_HEADS), lambda i: (i, 0))
    return pl.pallas_call(
        _mlstm_step_kernel,
        grid=(nb,),
        in_specs=[blk(0), blk(2), blk(3),
                  pl.BlockSpec((tb, LANES), lambda i: (i, gcol)),
                  pl.BlockSpec((None, 2, MLSTM_HEADS, dh, dh), lambda i: (i, 0, 0, 0, 0)),
                  pl.BlockSpec((1, LANES), lambda i: (0, 0)),
                  pl.BlockSpec((1, dh), lambda i: (0, 0)),
                  pl.BlockSpec((None, tb, MLSTM_HEADS, dh, dh), lambda i: (layer, i, 0, 0, 0)),
                  pl.BlockSpec((None, tb, MLSTM_HEADS, dh), lambda i: (layer, i, 0, 0)),
                  pl.BlockSpec((None, tb, MLSTM_HEADS), lambda i: (layer, i, 0))],
        out_specs=[pl.BlockSpec((tb, hw), lambda i: (i, 0)), st4, st3, st2],
        out_shape=[jax.ShapeDtypeStruct((b, hw), F32), jax.ShapeDtypeStruct(c.shape[1:], F32),
                   jax.ShapeDtypeStruct(n.shape[1:], F32), jax.ShapeDtypeStruct(m.shape[1:], F32)],
        compiler_params=_params("parallel"),
        name="mlstm_step",
    )(p, p, p, p, kt, bif, gh, c, n, m)


def _mem_attn_kernel(q_ref, gq_ref, kv_ref, o_ref):
    tq = q_ref.shape[0]
    lane = lax.broadcasted_iota(jnp.int32, (tq, LANES), 1)
    lo = lane < HEAD_DIM
    for k in range(MEM_HEADS // 2):
        y = _half_rms(q_ref[:, k * LANES:(k + 1) * LANES], gq_ref[...], lo) * HEAD_DIM ** -0.5
        kp = kv_ref[:, k * LANES:(k + 1) * LANES].astype(BF16)
        vp = kv_ref[:, MEM_W + k * LANES:MEM_W + (k + 1) * LANES].astype(BF16)
        outs = []
        for half in range(2):
            qh = jnp.where(lo if half == 0 else ~lo, y, 0.0).astype(BF16)
            s = _dot_nt(qh, kp)
            p = jnp.exp(s - jnp.max(s, axis=1, keepdims=True))
            p = p / jnp.sum(p, axis=1, keepdims=True)
            outs.append(jnp.dot(p.astype(BF16), vp, preferred_element_type=F32))
        o_ref[:, k * LANES:(k + 1) * LANES] = jnp.where(lo, outs[0], outs[1])


def mem_attn_prompt(p, qcol, b, s, g_q, mem_kv, layer):
    tq = _pick_tile(s, (512, 256, 128))
    nq = s // tq
    m_tok = mem_kv.shape[2]
    return pl.pallas_call(
        _mem_attn_kernel,
        grid=(b, nq),
        in_specs=[pl.BlockSpec((tq, MEM_W), lambda bi, qi: (bi * nq + qi, qcol)),
                  pl.BlockSpec((1, LANES), lambda bi, qi: (0, 0)),
                  pl.BlockSpec((None, None, m_tok, 2 * MEM_W), lambda bi, qi: (layer, bi, 0, 0))],
        out_specs=pl.BlockSpec((tq, MEM_W), lambda bi, qi: (bi * nq + qi, 0)),
        out_shape=jax.ShapeDtypeStruct((b * s, MEM_W), F32),
        compiler_params=_params("parallel", "parallel"),
        name="mem_attn",
    )(p, _gain2(g_q), mem_kv)


def _mem_decode_kernel(q_ref, gq_ref, kv_ref, o_ref):
    tb = q_ref.shape[0]
    lane = lax.broadcasted_iota(jnp.int32, (tb, LANES), 1)
    lo = lane < HEAD_DIM
    y = jnp.concatenate([_half_rms(q_ref[:, k * LANES:(k + 1) * LANES], gq_ref[...], lo)
                         for k in range(MEM_HEADS // 2)], axis=1) * HEAD_DIM ** -0.5
    rowi = lax.broadcasted_iota(jnp.int32, (8, MEM_W), 0)
    own = (lax.broadcasted_iota(jnp.int32, (8, MEM_W), 1) // HEAD_DIM) == rowi
    for j in range(tb):
        qbd = jnp.where(own, y[j:j + 1, :], 0.0).astype(BF16)
        s = _dot_nt(qbd, kv_ref[j, :, :MEM_W].astype(BF16))
        p = jnp.exp(s - jnp.max(s, axis=1, keepdims=True))
        p = p / jnp.sum(p, axis=1, keepdims=True)
        o = jnp.dot(p.astype(BF16), kv_ref[j, :, MEM_W:].astype(BF16), preferred_element_type=F32)
        o_ref[j:j + 1, :] = jnp.sum(jnp.where(own, o, 0.0), axis=0, keepdims=True)


def mem_attn_decode(p, qcol, g_q, mem_kv, layer):
    b = p.shape[0]
    tb = STEP_TB
    m_tok = mem_kv.shape[2]
    return pl.pallas_call(
        _mem_decode_kernel,
        grid=(b // tb,),
        in_specs=[pl.BlockSpec((tb, MEM_W), lambda i: (i, qcol)),
                  pl.BlockSpec((1, LANES), lambda i: (0, 0)),
                  pl.BlockSpec((None, tb, m_tok, 2 * MEM_W), lambda i: (layer, i, 0, 0))],
        out_specs=pl.BlockSpec((tb, MEM_W), lambda i: (i, 0)),
        out_shape=jax.ShapeDtypeStruct((b, MEM_W), F32),
        compiler_params=_params("parallel"),
        name="mem_attn_decode",
    )(p, _gain2(g_q), mem_kv)


NSA_TQ = 128
MASK = -(2.0 ** 100)
SEL_LANE = HEAD_DIM
N_SEL_LANES = 32
CONST_LANE = SEL_LANE + N_SEL_LANES
KV_ROW = 2 * NSA_KV_HEADS * HEAD_DIM
SEG_PER_PAGE = PAGE_SIZE // CMP_STRIDE
SEG_ROW = CMP_STRIDE * KV_ROW
CMP_OUT = 2 * NSA_KV_HEADS * CMP_HIDDEN
QROWS = 16


def _rel_bucket_np(dist):
    d = np.maximum(dist, 0)
    ratio = np.maximum(d, REL_MAX_EXACT).astype(np.float64) / REL_MAX_EXACT
    large = REL_MAX_EXACT + (np.log(ratio) / math.log(REL_MAX_DIST / REL_MAX_EXACT)
                             * (REL_BUCKETS - REL_MAX_EXACT)).astype(np.int32)
    return np.where(d < REL_MAX_EXACT, d, np.minimum(large, REL_BUCKETS - 1)).astype(np.int32)


def _overlap_matrix(n_cmp, n_slc):
    c0 = np.arange(n_cmp)[:, None] * CMP_STRIDE
    s0 = np.arange(n_slc)[None, :] * SLC_BLOCK
    return np.clip(np.minimum(c0 + CMP_LEN, s0 + SLC_BLOCK) - np.maximum(c0, s0), 0, None) / CMP_STRIDE


def _kv_rows_kernel(x_ref, g_ref, w_ref, gs_ref, gw_ref, cmp_ref, slc_ref, win_ref, *aug_refs, seq_len):
    tm = x_ref.shape[0]
    xn = _rms_rows(x_ref[...], g_ref[...]).astype(BF16)
    kv = jnp.dot(xn, w_ref[...], preferred_element_type=F32)
    cmp_ref[...] = kv[:, :KV_ROW]
    lane = lax.broadcasted_iota(jnp.int32, (tm, LANES), 1)
    lo = lane < HEAD_DIM
    if aug_refs:
        ka_ref, va_ref = aug_refs
        pos = (pl.program_id(0) * tm + lax.broadcasted_iota(jnp.int32, (tm, LANES), 0)) % seq_len
        ones = (lane == CONST_LANE) | (lane == CONST_LANE + 1)
        onehot = (lane - SEL_LANE) == pos // SLC_BLOCK
    for t, (rows_ref, gk_ref) in enumerate(((slc_ref, gs_ref), (win_ref, gw_ref))):
        base = KV_ROW * (t + 1)
        kn = _half_rms(kv[:, base:base + LANES], gk_ref[...], lo)
        vv = kv[:, base + LANES:base + 2 * LANES]
        rows_ref[:, :LANES] = kn
        rows_ref[:, LANES:] = vv
        if aug_refs:
            extra = jnp.where((ones | onehot) if t == 0 else ones, 1.0, 0.0)
            rowt = lax.broadcasted_iota(jnp.int32, (VT_ROWS, tm), 0)
            for g in range(NSA_KV_HEADS):
                kg = kn if g == 0 else pltpu.roll(kn, HEAD_DIM, 1)
                vg = vv if g == 0 else pltpu.roll(vv, HEAD_DIM, 1)
                ka_ref[NSA_KV_HEADS * t + g] = jnp.where(lo, kg, extra).astype(BF16)
                vt = jnp.where(lo, vg, 0.0).T[:VT_ROWS, :]
                va_ref[NSA_KV_HEADS * t + g] = jnp.where(rowt == HEAD_DIM, 1.0, vt).astype(BF16)


def kv_rows(x, g_kv, w_kv, g_k_slc, g_k_win, seq_len, aug):
    m, d = x.shape
    n = w_kv.shape[1]
    tm = _pick_tile(m, (512, 256, 128))
    out_specs = [pl.BlockSpec((tm, KV_ROW), lambda i: (i, 0))] * 3
    out_shape = [jax.ShapeDtypeStruct((m, KV_ROW), F32)] * 3
    if aug:
        out_specs += [pl.BlockSpec((2 * NSA_KV_HEADS, tm, LANES), lambda i: (0, i, 0)),
                      pl.BlockSpec((2 * NSA_KV_HEADS, VT_ROWS, tm), lambda i: (0, 0, i))]
        out_shape += [jax.ShapeDtypeStruct((2 * NSA_KV_HEADS, m, LANES), BF16),
                      jax.ShapeDtypeStruct((2 * NSA_KV_HEADS, VT_ROWS, m), BF16)]
    return pl.pallas_call(
        partial(_kv_rows_kernel, seq_len=seq_len),
        grid=(m // tm,),
        in_specs=[pl.BlockSpec((tm, d), lambda i: (i, 0)),
                  pl.BlockSpec((1, d), lambda i: (0, 0)),
                  pl.BlockSpec((d, n), lambda i: (0, 0)),
                  pl.BlockSpec((1, LANES), lambda i: (0, 0)),
                  pl.BlockSpec((1, LANES), lambda i: (0, 0))],
        out_specs=out_specs,
        out_shape=out_shape,
        compiler_params=_params("parallel"),
        name="kv_rows",
    )(x, g_kv.reshape(1, d), w_kv, _gain2(g_k_slc), _gain2(g_k_win))


def _compress_kernel(*refs, n_x, paged):
    if paged:
        refs = refs[1:]
    x_refs = refs[:n_x]
    w1_ref, b1_ref, w2_ref, b2_ref, gk_ref, o_ref = refs[n_x:]
    n_seg = o_ref.shape[0]
    r_n = CMP_LEN // CMP_STRIDE
    lane = lax.broadcasted_iota(jnp.int32, (n_seg, LANES), 1)
    for c in range(2):
        acc = [jnp.zeros((n_seg, CMP_OUT // 2), F32) for _ in range(r_n)]
        refs_c = x_refs[c::2]
        for u in range(CMP_STRIDE):
            pieces = [r[pl.ds(u, r.shape[0] // CMP_STRIDE, stride=CMP_STRIDE), :] for r in refs_c]
            xu = (pieces[0] if len(pieces) == 1 else jnp.concatenate(pieces, axis=0)).astype(BF16)
            for r in range(r_n):
                acc[r] = acc[r] + jnp.dot(xu, w1_ref[r, u, c], preferred_element_type=F32)
        pre = acc[0] + pltpu.roll(acc[1], n_seg - 1, 0) + b1_ref[c]
        out = jnp.dot(jax.nn.gelu(pre).astype(BF16), w2_ref[c], preferred_element_type=F32) + b2_ref[c]
        if c == 0:
            out = _half_rms(out, gk_ref[...], lane < HEAD_DIM)
        o_ref[:, c * LANES:(c + 1) * LANES] = out.astype(BF16)


def compress_weights(w1, b1, w2, b2, pos_enc, g_k_cmp):
    r_n = CMP_LEN // CMP_STRIDE
    eye = jnp.eye(NSA_KV_HEADS, dtype=F32)
    w1r = w1.reshape(2, r_n, CMP_STRIDE, HEAD_DIM, CMP_HIDDEN)
    w1b = jnp.einsum('crudh,gy->rucgdyh', w1r, eye).reshape(r_n, CMP_STRIDE, 2, LANES, CMP_OUT // 2).astype(BF16)
    pe = jnp.einsum('pcd,cpdh->ch', pos_enc, w1)
    b1b = jnp.broadcast_to((b1 + pe)[:, None, :], (2, NSA_KV_HEADS, CMP_HIDDEN)).reshape(2, 1, CMP_OUT // 2)
    w2b = jnp.einsum('chd,gy->cghyd', w2, eye).reshape(2, CMP_OUT // 2, LANES).astype(BF16)
    b2b = jnp.broadcast_to(b2[:, None, :], (2, NSA_KV_HEADS, HEAD_DIM)).reshape(2, 1, LANES)
    return w1b, b1b, w2b, b2b, _gain2(g_k_cmp)


def compress_rows(x, cw, page_table=None):
    w1b, b1b, w2b, b2b, gk2 = cw
    paged = page_table is not None
    if paged:
        n_seq, n_pages = page_table.shape
        x_specs = [pl.BlockSpec((None, PAGE_SIZE, LANES), partial(lambda k, c, b, pt: (pt[b, k], 0, c), k, c))
                   for k in range(n_pages) for c in range(2)]
        n_seg = n_pages * SEG_PER_PAGE
        cmap = lambda *idx: lambda b, pt: idx
    else:
        n_seq, n_seg = x.shape[0], x.shape[1] // CMP_STRIDE
        x_specs = [pl.BlockSpec((None, x.shape[1], LANES), partial(lambda c, b: (b, 0, c), c)) for c in range(2)]
        cmap = lambda *idx: lambda b: idx
    n_x = len(x_specs)
    w_specs = [pl.BlockSpec(w1b.shape, cmap(0, 0, 0, 0, 0)), pl.BlockSpec(b1b.shape, cmap(0, 0, 0)),
               pl.BlockSpec(w2b.shape, cmap(0, 0, 0)), pl.BlockSpec(b2b.shape, cmap(0, 0, 0)),
               pl.BlockSpec(gk2.shape, cmap(0, 0))]
    out_spec = pl.BlockSpec((None, n_seg, KV_ROW), (lambda b, pt: (b, 0, 0)) if paged else (lambda b: (b, 0, 0)))
    out_shape = jax.ShapeDtypeStruct((n_seq, n_seg, KV_ROW), BF16)
    kern = partial(_compress_kernel, n_x=n_x, paged=paged)
    if paged:
        gs = pltpu.PrefetchScalarGridSpec(num_scalar_prefetch=1, grid=(n_seq,), in_specs=x_specs + w_specs,
                                          out_specs=out_spec)
        return pl.pallas_call(kern, grid_spec=gs, out_shape=out_shape, compiler_params=_params("parallel"),
                              name="compress_paged")(page_table, *([x] * n_x), w1b, b1b, w2b, b2b, gk2)
    return pl.pallas_call(kern, grid=(n_seq,), in_specs=x_specs + w_specs, out_specs=out_spec, out_shape=out_shape,
                          compiler_params=_params("parallel"), name="compress")(x, x, w1b, b1b, w2b, b2b, gk2)


VT_ROWS = 80
CMP_BAND = 16


LOG2E = 1.0 / math.log(2.0)
FAR_BLOCKS = 4
NEAR_BLOCKS = 5
FULL_MASK_TILE = 4


def _nsa_prompt_kernel(rel_ref, q_ref, gate_ref, gq_ref, bg_ref, ck_ref, cvt_ref, band_ref, far_ref, mt_ref,
                        ka_ref, vat_ref, dl_ref, o_ref, qa_ref, m_ref, acc_ref, cb_ref, *, n_cmp, n_tiles):
    tq = NSA_TQ
    cols = NSA_GROUP * tq
    qi = pl.program_id(1)
    lane = lax.broadcasted_iota(jnp.int32, (tq, LANES), 1)
    lo = lane < HEAD_DIM
    nidx = lax.broadcasted_iota(jnp.int32, (LANES, cols), 0)
    r_cb = lax.broadcasted_iota(jnp.int32, (LANES + CMP_BAND, cols), 0)
    jidx = lax.broadcasted_iota(jnp.int32, (N_SEL_LANES, tq), 0)
    tpos = qi * tq + lax.broadcasted_iota(jnp.int32, (N_SEL_LANES, tq), 1)
    cur = tpos // SLC_BLOCK
    forced = (jidx == 0) | (jidx == cur) | (jidx == cur - 1)
    half_band = CMP_BAND // 2

    def gate_rows(g):
        sig_t = jax.nn.sigmoid(gate_ref[:, g * LANES:(g + 1) * LANES] + bg_ref[:, g * LANES:(g + 1) * LANES]).T
        return [jnp.concatenate([sig_t[br * NSA_GROUP + z:br * NSA_GROUP + z + 1, :] for z in range(NSA_GROUP)], axis=1)
                for br in range(3)]

    def prologue(g):
        qn = []
        for k in range(NSA_GROUP // 2):
            kk = g * (NSA_GROUP // 2) + k
            y = _half_rms(q_ref[:, kk * LANES:(kk + 1) * LANES], gq_ref[...], lo) * (HEAD_DIM ** -0.5 * LOG2E)
            qn.append(jnp.where(lo, y, 0.0))
            qn.append(jnp.where(lo, pltpu.roll(y, HEAD_DIM, 1), 0.0))
        qc = jnp.concatenate(qn, axis=0).astype(BF16)

        cb_ref[g] = jnp.where(r_cb < half_band * qi, far_ref[g], MASK)
        cb_ref[g, pl.ds(pl.multiple_of(half_band * qi, half_band), CMP_BAND), :] = band_ref[g]
        cbias = jnp.where(nidx >= n_cmp, MASK, cb_ref[g, half_band:half_band + LANES, :])
        s = _dot_nt(ck_ref[g], qc) + cbias
        m = jnp.maximum(jnp.max(s, axis=0, keepdims=True), NEG)
        p = jnp.exp2(s - m)
        l = jnp.sum(p, axis=0, keepdims=True)
        pn = p * jnp.where(l > 0.0, 1.0 / l, 0.0)
        o_cmp = jnp.dot(cvt_ref[g], pn.astype(BF16), preferred_element_type=F32)

        psum = pn[:, 0:tq]
        for z in range(1, NSA_GROUP):
            psum = psum + pn[:, z * tq:(z + 1) * tq]
        p_hi = psum.astype(BF16)
        p_lo = (psum - p_hi.astype(F32)).astype(BF16)
        imp = (jnp.dot(mt_ref[...], p_hi, preferred_element_type=F32)
               + jnp.dot(mt_ref[...], p_lo, preferred_element_type=F32))
        score = jnp.where(forced, FORCED, jnp.where(jidx <= cur, imp, -1.0))
        rank = jnp.zeros((N_SEL_LANES, tq), F32)
        for i in range(N_SEL_LANES):
            ri = score[i:i + 1, :]
            beats = (ri > score) | ((ri == score) & (jidx > i))
            rank = rank + jnp.where(beats, 1.0, 0.0)
        selb = jnp.where((rank < N_SELECT) & (jidx <= cur), 0.0, MASK)
        selb = jnp.concatenate([selb, jnp.zeros((LANES - N_SEL_LANES, tq), F32)], axis=0)
        sel_cols = pltpu.roll(selb.T, SEL_LANE, 1)

        for z in range(NSA_GROUP):
            c = jnp.full((tq, LANES), rel_ref[REL_BUCKETS - 1, g * NSA_GROUP + z], F32)
            c_hi = c.astype(BF16).astype(F32)
            extra = jnp.where(lane == CONST_LANE, c_hi, jnp.where(lane == CONST_LANE + 1, c - c_hi, sel_cols))
            qa_ref[g, z * tq:(z + 1) * tq, :] = jnp.where(lo, qn[z], extra).astype(BF16)
        return o_cmp

    o_cmp = [prologue(g) for g in range(NSA_KV_HEADS)]

    n_far = jnp.maximum(qi - 1, 0) // FAR_BLOCKS
    far_rows = FAR_BLOCKS * tq
    m_ref[...] = jnp.full(m_ref.shape, NEG, F32)
    acc_ref[...] = jnp.zeros(acc_ref.shape, F32)

    def far_body(c, carry):
        start = pl.multiple_of(c * far_rows, far_rows)
        for g in range(NSA_KV_HEADS):
            s = _dot_nt(ka_ref[g, pl.ds(start, far_rows), :], qa_ref[g])
            m_prev = m_ref[g]
            m_next = jnp.maximum(m_prev, jnp.max(s, axis=0, keepdims=True))
            p = jnp.exp2(s - m_next).astype(BF16)
            acc_ref[g] = (jnp.exp2(m_prev - m_next) * acc_ref[g]
                          + jnp.dot(vat_ref[g, :, pl.ds(start, far_rows)], p, preferred_element_type=F32))
            m_ref[g] = m_next
        return carry

    lax.fori_loop(0, n_far, far_body, 0)

    def near_stage(idx, g, start_blk, tile_of, m_prev, acc_prev):
        start = pl.multiple_of(start_blk * tq, tq)
        s = _dot_nt(ka_ref[idx, pl.ds(start, NEAR_BLOCKS * tq), :], qa_ref[g])
        s = s + jnp.concatenate([dl_ref[tile_of(start_blk + i), g] for i in range(NEAR_BLOCKS)], axis=0)
        m_next = jnp.max(s, axis=0, keepdims=True)
        if m_prev is not None:
            m_next = jnp.maximum(m_prev, m_next)
        acc = jnp.dot(vat_ref[idx, :, pl.ds(start, NEAR_BLOCKS * tq)], jnp.exp2(s - m_next).astype(BF16),
                      preferred_element_type=F32)
        if acc_prev is not None:
            acc = acc + jnp.exp2(m_prev - m_next) * acc_prev
        return acc[:HEAD_DIM, :] / acc[HEAD_DIM:HEAD_DIM + 1, :]

    far_cov = FAR_BLOCKS * n_far
    slc_start = jnp.minimum(far_cov, n_tiles - NEAR_BLOCKS)
    win_start = jnp.maximum(qi - WINDOW // tq, 0)

    def slc_tile(j):
        d = qi - j
        return jnp.where((d < 0) | (j < far_cov), FULL_MASK_TILE, jnp.where(d == 0, 2, jnp.where(d == 1, 1, 0)))

    def win_tile(j):
        d = qi - j
        return jnp.where(d < 0, FULL_MASK_TILE,
                         jnp.where(d == 0, 2, jnp.where(d == 1, 1, jnp.where(d == WINDOW // tq, 3, 0))))

    outs = []
    for g in range(NSA_KV_HEADS):
        gr = gate_rows(g)
        o_slc = near_stage(g, g, slc_start, slc_tile, m_ref[g], acc_ref[g])
        o_win = near_stage(NSA_KV_HEADS + g, g, win_start, win_tile, None, None)
        outs.append(gr[0] * o_cmp[g] + gr[1] * o_slc + gr[2] * o_win)

    for g in range(NSA_KV_HEADS):
        for k in range(NSA_GROUP // 2):
            kk = g * (NSA_GROUP // 2) + k
            o_ref[:, kk * LANES:(kk + 1) * LANES] = jnp.concatenate(
                [outs[g][:, (2 * k) * tq:(2 * k + 1) * tq], outs[g][:, (2 * k + 1) * tq:(2 * k + 2) * tq]], axis=0).T


def nsa_prompt(p, b, s, g_q, b_gate, rel_table, ck, cvt, ka, vat, tables):
    tq = NSA_TQ
    nq = s // tq
    cols = NSA_GROUP * tq
    n_cmp = (s - CMP_LEN) // CMP_STRIDE + 1
    n_slc = -(-s // SLC_BLOCK)
    assert s % tq == 0 and nq >= NEAR_BLOCKS and n_cmp <= LANES and n_slc <= N_SEL_LANES and tq == 8 * CMP_STRIDE
    tiles, band, far_row = tables
    tiles = jnp.concatenate([tiles, jnp.full((1,) + tiles.shape[1:], MASK, F32)], axis=0)
    mt = np.zeros((N_SEL_LANES, LANES), np.float32)
    mt[:n_slc, :n_cmp] = _overlap_matrix(n_cmp, n_slc).T
    bg = _gate_bias_slabs(b_gate).reshape(1, NSA_KV_HEADS * LANES)
    gcol = (MIX_W + MEM_W) // (NSA_KV_HEADS * LANES)
    ka4 = ka.reshape(2 * NSA_KV_HEADS, b, s, LANES)
    c2 = lambda bi, qi: (0, 0)
    c3 = lambda bi, qi: (0, 0, 0)
    return pl.pallas_call(
        partial(_nsa_prompt_kernel, n_cmp=n_cmp, n_tiles=nq),
        grid=(b, nq),
        in_specs=[pl.BlockSpec(memory_space=pltpu.SMEM),
                  pl.BlockSpec((tq, MIX_W), lambda bi, qi: (bi * nq + qi, 0)),
                  pl.BlockSpec((tq, NSA_KV_HEADS * LANES), lambda bi, qi: (bi * nq + qi, gcol)),
                  pl.BlockSpec((1, LANES), c2),
                  pl.BlockSpec((1, NSA_KV_HEADS * LANES), c2),
                  pl.BlockSpec((None, NSA_KV_HEADS, LANES, LANES), lambda bi, qi: (bi, 0, 0, 0)),
                  pl.BlockSpec((None, NSA_KV_HEADS, HEAD_DIM, LANES), lambda bi, qi: (bi, 0, 0, 0)),
                  pl.BlockSpec((NSA_KV_HEADS, CMP_BAND, cols), c3),
                  pl.BlockSpec((NSA_KV_HEADS, 1, cols), c3),
                  pl.BlockSpec((N_SEL_LANES, LANES), c2),
                  pl.BlockSpec((2 * NSA_KV_HEADS, None, s, LANES), lambda bi, qi: (0, bi, 0, 0)),
                  pl.BlockSpec((2 * NSA_KV_HEADS, VT_ROWS, s), lambda bi, qi: (0, 0, bi)),
                  pl.BlockSpec((FULL_MASK_TILE + 1, NSA_KV_HEADS, tq, cols), lambda bi, qi: (0, 0, 0, 0))],
        out_specs=pl.BlockSpec((tq, MIX_W), lambda bi, qi: (bi * nq + qi, 0)),
        out_shape=jax.ShapeDtypeStruct((b * s, MIX_W), F32),
        scratch_shapes=[pltpu.VMEM((NSA_KV_HEADS, cols, LANES), BF16), pltpu.VMEM((NSA_KV_HEADS, 1, cols), F32),
                        pltpu.VMEM((NSA_KV_HEADS, VT_ROWS, cols), F32),
                        pltpu.VMEM((NSA_KV_HEADS, LANES + CMP_BAND, cols), F32)],
        compiler_params=_params("parallel", "arbitrary"),
        name="nsa_prompt",
    )(rel_table.astype(F32), p, p, _gain2(g_q), bg, ck, cvt, band, far_row, jnp.asarray(mt, BF16), ka4, vat, tiles)


def nsa_bias_tables(rel_table):
    tq = NSA_TQ
    table = rel_table.astype(F32)
    far = table[REL_BUCKETS - 1]
    r = np.arange(tq)[None, :]
    c = np.arange(tq)[:, None]

    def by_group(t):
        k = t.shape[0]
        return jnp.transpose(t.reshape(k, tq, NSA_KV_HEADS, NSA_GROUP), (2, 0, 3, 1)).reshape(NSA_KV_HEADS, k, -1)

    spread = (jnp.arange(2 * tq - 1)[:, None, None]
              == (tq - 1 - jnp.arange(tq)[None, :, None] + jnp.arange(tq)[None, None, :])).astype(F32)

    def tile(off, masked):
        d = off + np.arange(-(tq - 1), tq)
        v = jnp.where(jnp.asarray(masked(d))[:, None], MASK, table[_rel_bucket_np(d)] - far)
        return by_group(jnp.einsum('dh,dcr->crh', v, spread, precision=lax.Precision.HIGHEST))

    zero = jnp.zeros((NSA_KV_HEADS, tq, NSA_GROUP * tq), F32)
    tiles = jnp.stack([zero, tile(tq, lambda d: d < 0), tile(0, lambda d: d < 0),
                       tile(WINDOW, lambda d: d >= WINDOW)])
    n_rel = np.arange(CMP_BAND)[:, None] - CMP_BAND // 2
    dist = r - CMP_STRIDE * n_rel - (CMP_LEN - 1)
    band = by_group(jnp.where(jnp.asarray(dist < 0)[..., None], MASK, table[_rel_bucket_np(dist)]))
    far_row = by_group(jnp.broadcast_to(far, (1, tq, NSA_HEADS)))
    return tiles, band, far_row


def _gate_bias_slabs(b_gate):
    bg = b_gate.astype(F32).reshape(3, NSA_KV_HEADS, NSA_GROUP).transpose(1, 0, 2).reshape(NSA_KV_HEADS, 3 * NSA_GROUP)
    return _pad_cols(bg, LANES)


def split_cmp_ctx(ctx):
    b, n, _ = ctx.shape
    parts = ctx.reshape(b, n, 2, NSA_KV_HEADS, HEAD_DIM).transpose(2, 0, 3, 1, 4)
    parts = jnp.pad(parts, ((0, 0), (0, 0), (0, 0), (0, LANES - n), (0, 0)))
    ck = jnp.pad(parts[0], ((0, 0), (0, 0), (0, 0), (0, LANES - HEAD_DIM)))
    return ck, jnp.swapaxes(parts[1], -1, -2)


def _softmax_parts(s, s_new):
    m = jnp.maximum(jnp.max(s, axis=1, keepdims=True), s_new)
    p = jnp.exp(s - m)
    p_new = jnp.exp(s_new - m)
    inv = 1.0 / (jnp.sum(p, axis=1, keepdims=True) + p_new)
    return p * inv, p_new * inv


DECODE_SEQS = 2


def _nsa_decode_kernel(*refs, n_pages):
    (pt_ref, p_ref, gq_ref, bg_ref, cmp_ref, cb_ref, mt_ref) = refs[:7]
    n_in = DECODE_SEQS * n_pages
    page_refs = refs[7:7 + n_in]
    (snew_ref, sb_ref, e_ref, win_ref, wnew_ref, wb_ref, b0_ref, o_ref) = refs[7 + n_in:]
    del pt_ref
    for s in range(DECODE_SEQS):
        _nsa_decode_seq(p_ref.at[s], gq_ref, bg_ref, cmp_ref.at[s], cb_ref, mt_ref,
                        page_refs[s * n_pages:(s + 1) * n_pages], snew_ref.at[s], sb_ref, e_ref, win_ref.at[s],
                        wnew_ref.at[s], wb_ref, b0_ref, o_ref.at[s])


def _nsa_decode_seq(p_ref, gq_ref, bg_ref, cmp_ref, cb_ref, mt_ref, page_refs, snew_ref, sb_ref, e_ref, win_ref,
                    wnew_ref, wb_ref, b0_ref, o_ref):
    n_pages = len(page_refs)
    n_keys = n_pages * PAGE_SIZE
    lo1 = lax.broadcasted_iota(jnp.int32, (1, LANES), 1) < HEAD_DIM
    rowi = lax.broadcasted_iota(jnp.int32, (QROWS, LANES), 0)
    lanei = lax.broadcasted_iota(jnp.int32, (QROWS, LANES), 1)
    row_g = rowi // 8
    own = (lanei // HEAD_DIM) == row_g

    qbd = jnp.zeros((QROWS, LANES), F32)
    for k in range(NSA_HEADS // 2):
        y = _half_rms(p_ref[:, k * LANES:(k + 1) * LANES], gq_ref[...], lo1) * HEAD_DIM ** -0.5
        yr = pltpu.roll(y, HEAD_DIM, 1)
        for half in range(2):
            g, z = divmod(2 * k + half, NSA_GROUP)
            qbd = jnp.where(rowi == 8 * g + z, y if half == g else yr, qbd)
    qbd = jnp.where(own, qbd, 0.0)
    qb = qbd.astype(BF16)

    s = _dot_nt(qb, cmp_ref[:, :LANES]) + cb_ref[...]
    m = jnp.maximum(jnp.max(s, axis=1, keepdims=True), NEG)
    p = jnp.exp(s - m)
    l = jnp.sum(p, axis=1, keepdims=True)
    pn = p * jnp.where(l > 0.0, 1.0 / l, 0.0)
    o_cmp = jnp.dot(pn.astype(BF16), cmp_ref[:, LANES:], preferred_element_type=F32)

    valid_row = (rowi % 8) < NSA_GROUP
    ps0 = jnp.sum(jnp.where(valid_row & (row_g == 0), pn, 0.0), axis=0, keepdims=True)
    ps1 = jnp.sum(jnp.where(valid_row & (row_g == 1), pn, 0.0), axis=0, keepdims=True)
    r8 = lax.broadcasted_iota(jnp.int32, (8, LANES), 0)
    j8 = lax.broadcasted_iota(jnp.int32, (8, LANES), 1)
    psum = jnp.where(r8 == 0, ps0, jnp.where(r8 == 1, ps1, 0.0))
    p_hi = psum.astype(BF16)
    p_lo = (psum - p_hi.astype(F32)).astype(BF16)
    imp = (jnp.dot(p_hi, mt_ref[...], preferred_element_type=F32)
           + jnp.dot(p_lo, mt_ref[...], preferred_element_type=F32))
    cur = n_keys // SLC_BLOCK
    forced = (j8 == 0) | (j8 == cur) | (j8 == cur - 1)
    score = jnp.where(forced, FORCED, jnp.where(j8 <= cur, imp, -1.0))
    rank = jnp.zeros((8, LANES), F32)
    for i in range(cur + 1):
        ri = score[:, i:i + 1]
        beats = (ri > score) | ((ri == score) & (j8 > i))
        rank = rank + jnp.where(beats, 1.0, 0.0)
    selb = jnp.where(rank < N_SELECT, 0.0, MASK)
    sel16 = jnp.where(row_g == 0, selb[0:1, :], selb[1:2, :]).astype(BF16)
    blockmask = jnp.dot(sel16, e_ref[...], preferred_element_type=F32)

    s = jnp.concatenate([_dot_nt(qb, r[:, :LANES].astype(BF16)) for r in page_refs], axis=1)
    s = s + sb_ref[...] + blockmask
    s_new = jnp.sum(qbd * snew_ref[:, :LANES], axis=1, keepdims=True) + b0_ref[...]
    pp, p_new = _softmax_parts(s, s_new)
    o_slc = p_new * snew_ref[:, LANES:]
    for k, r in enumerate(page_refs):
        o_slc = o_slc + jnp.dot(pp[:, k * PAGE_SIZE:(k + 1) * PAGE_SIZE].astype(BF16), r[:, LANES:].astype(BF16),
                                preferred_element_type=F32)

    n_wt = win_ref.shape[0] // LANES
    s = jnp.concatenate([_dot_nt(qb, win_ref[k * LANES:(k + 1) * LANES, :LANES].astype(BF16)) for k in range(n_wt)],
                        axis=1) + wb_ref[...]
    s_new = jnp.sum(qbd * wnew_ref[:, :LANES], axis=1, keepdims=True) + b0_ref[...]
    pp, p_new = _softmax_parts(s, s_new)
    o_win = p_new * wnew_ref[:, LANES:]
    for k in range(n_wt):
        o_win = o_win + jnp.dot(pp[:, k * LANES:(k + 1) * LANES].astype(BF16),
                                win_ref[k * LANES:(k + 1) * LANES, LANES:].astype(BF16), preferred_element_type=F32)

    gbase = MIX_W + MEM_W
    sig = jax.nn.sigmoid(p_ref[:, gbase:gbase + NSA_KV_HEADS * LANES] + bg_ref[...])
    gate = [jnp.zeros((QROWS, 1), F32) for _ in range(3)]
    rcol = lax.broadcasted_iota(jnp.int32, (QROWS, 1), 0)
    for br in range(3):
        for g in range(NSA_KV_HEADS):
            for z in range(NSA_GROUP):
                c = g * LANES + br * NSA_GROUP + z
                gate[br] = jnp.where(rcol == 8 * g + z, sig[:, c:c + 1], gate[br])
    comb = jnp.where(own, gate[0] * o_cmp + gate[1] * o_slc + gate[2] * o_win, 0.0)
    comb_r = pltpu.roll(comb, HEAD_DIM, 1)
    for k in range(NSA_HEADS // 2):
        acc = jnp.zeros((1, LANES), F32)
        for half in range(2):
            g, z = divmod(2 * k + half, NSA_GROUP)
            src = comb if half == g else comb_r
            acc = acc + src[8 * g + z:8 * g + z + 1, :]
        o_ref[:, k * LANES:(k + 1) * LANES] = acc


def nsa_decode_bias(rel_table, past, wb):
    table = rel_table.astype(F32)

    def rows16(t):
        z = jnp.zeros((2, t.shape[1]), F32)
        return jnp.concatenate([t[:NSA_GROUP], z, t[NSA_GROUP:], z], axis=0)

    n_cmp = (past + 1 - CMP_LEN) // CMP_STRIDE + 1
    c_end = np.arange(LANES) * CMP_STRIDE + CMP_LEN - 1
    dist = past - c_end
    cb = jnp.where(jnp.asarray((dist < 0) | (np.arange(LANES) >= n_cmp))[None], MASK, table[_rel_bucket_np(dist)].T)
    sb = table[_rel_bucket_np(past - np.arange(past))].T
    wpos = past - wb + np.arange(wb)
    wbias = jnp.where(jnp.asarray(past - wpos >= WINDOW)[None], MASK, table[_rel_bucket_np(past - wpos)].T)
    b0 = table[0][:, None]
    return rows16(cb), rows16(sb), rows16(wbias), rows16(b0)


def nsa_decode(p, g_q, b_gate, cmp_ctx, page_table, cache_slc, slc_new, cache_win, win_new, biases):
    b, n_pages = page_table.shape
    past = n_pages * PAGE_SIZE
    wb = cache_win.shape[1]
    cb, sb, wbias, b0 = biases
    n_cmp = (past + 1 - CMP_LEN) // CMP_STRIDE + 1
    n_slc = -(-(past + 1) // SLC_BLOCK)
    assert n_slc <= LANES and n_cmp <= LANES and wb % LANES == 0
    mt = np.zeros((LANES, LANES), np.float32)
    mt[:n_cmp, :n_slc] = _overlap_matrix(n_cmp, n_slc)
    e = (np.arange(past)[None, :] // SLC_BLOCK == np.arange(LANES)[:, None]).astype(np.float32)
    bg = _gate_bias_slabs(b_gate).reshape(1, NSA_KV_HEADS * LANES)
    ns = DECODE_SEQS
    assert b % ns == 0
    c2 = lambda bi, pt: (0, 0)
    seq3 = lambda bi, pt: (bi, 0, 0)
    in_specs = ([pl.BlockSpec((ns, 1, p.shape[1]), seq3),
                 pl.BlockSpec((1, LANES), c2),
                 pl.BlockSpec((1, NSA_KV_HEADS * LANES), c2),
                 pl.BlockSpec((ns, LANES, KV_ROW), seq3),
                 pl.BlockSpec((QROWS, LANES), c2),
                 pl.BlockSpec((LANES, LANES), c2)]
                + [pl.BlockSpec((None, PAGE_SIZE, KV_ROW), partial(lambda s, k, bi, pt: (pt[ns * bi + s, k], 0, 0), s, k))
                   for s in range(ns) for k in range(n_pages)]
                + [pl.BlockSpec((ns, 1, KV_ROW), seq3),
                   pl.BlockSpec((QROWS, past), c2),
                   pl.BlockSpec((LANES, past), c2),
                   pl.BlockSpec((ns, wb, KV_ROW), seq3),
                   pl.BlockSpec((ns, 1, KV_ROW), seq3),
                   pl.BlockSpec((QROWS, wb), c2),
                   pl.BlockSpec((QROWS, 1), c2)])
    gs = pltpu.PrefetchScalarGridSpec(num_scalar_prefetch=1, grid=(b // ns,), in_specs=in_specs,
                                      out_specs=pl.BlockSpec((ns, 1, MIX_W), seq3))
    out = pl.pallas_call(
        partial(_nsa_decode_kernel, n_pages=n_pages),
        grid_spec=gs,
        out_shape=jax.ShapeDtypeStruct((b, 1, MIX_W), F32),
        compiler_params=_params("parallel"),
        name="nsa_decode",
    )(page_table, p.reshape(b, 1, -1), _gain2(g_q), bg, cmp_ctx, cb, jnp.asarray(mt, BF16),
      *([cache_slc] * (ns * n_pages)), slc_new.reshape(b, 1, KV_ROW), sb, jnp.asarray(e, BF16),
      cache_win, win_new.reshape(b, 1, KV_ROW), wbias, b0)
    return out.reshape(b, MIX_W)


GATE_PAD = LANES


def _pad_heads_cols(w):
    d = w.shape[0]
    return jnp.pad(w.reshape(d, MLSTM_HEADS, MLSTM_DH), ((0, 0), (0, 0), (0, MLSTM_PAD - MLSTM_DH))).reshape(d, MLSTM_HW)


def prep_weights(w_in_a, w_in_b, w_kv, w_mem_kv, w_out, w_gu, w_down):
    w = {}
    n_gate_a = 2 * MLSTM_HEADS
    n_gate_b = 3 * NSA_HEADS
    w['in_a'] = []
    for l in range(N_A_LAYERS):
        wa = w_in_a[l]
        qkvo = [_pad_heads_cols(wa[:, j * MIX_W:(j + 1) * MIX_W]) for j in range(4)]
        w['in_a'].append(jnp.concatenate(qkvo + [wa[:, 4 * MIX_W + n_gate_a:],
                                                 _pad_cols(wa[:, 4 * MIX_W:4 * MIX_W + n_gate_a], GATE_PAD)],
                                         axis=1).astype(BF16))
    w['in_b'] = []
    for j in range(N_B_LAYERS):
        wg = w_in_b[j][:, MIX_W:MIX_W + n_gate_b].reshape(D_MODEL, 3, NSA_KV_HEADS, NSA_GROUP)
        slabs = [_pad_cols(wg[:, :, g].reshape(D_MODEL, 3 * NSA_GROUP), GATE_PAD) for g in range(NSA_KV_HEADS)]
        w['in_b'].append(jnp.concatenate([w_in_b[j][:, :MIX_W], w_in_b[j][:, MIX_W + n_gate_b:]] + slabs,
                                         axis=1).astype(BF16))
    w['out_mix'] = []
    for l in range(DEPTH):
        wm = w_out[l][:MIX_W]
        if l < N_A_LAYERS:
            wm = jnp.pad(wm.reshape(MLSTM_HEADS, MLSTM_DH, D_MODEL),
                         ((0, 0), (0, MLSTM_PAD - MLSTM_DH), (0, 0))).reshape(MLSTM_HW, D_MODEL)
        w['out_mix'].append(wm.astype(BF16))
    w['out_mem'] = w_out[:, MIX_W:].astype(BF16)
    w['kv'] = w_kv.astype(BF16)
    w['mem_kv'] = w_mem_kv.astype(BF16)
    w['gu'] = w_gu.astype(BF16)
    w['down'] = w_down.astype(BF16)
    return w


def kernel(x_prompt, x_sample, state_mlstm_c, state_mlstm_n, state_mlstm_m, cache_mem_kv,
           cache_cmp_kv, cache_slc_kv, cache_win_kv, page_table, mem_prompt,
           g_mix, w_in_a, b_if, g_hnorm, w_in_b, g_q, b_gate, rel_table, g_kv, w_kv,
           g_k_slc, g_k_win, g_k_cmp, cmp_pos, cmp_w1, cmp_b1, cmp_w2, cmp_b2,
           w_mem_kv, g_mem_k, g_mem_q, w_out, g_ffn, w_gu, w_down):
    w = prep_weights(w_in_a, w_in_b, w_kv, w_mem_kv, w_out, w_gu, w_down)
    cw = compress_weights(cmp_w1, cmp_b1, cmp_w2, cmp_b2, cmp_pos, g_k_cmp)
    qcol_a = 4 * MLSTM_HW // MEM_W
    qcol_b = MIX_W // MEM_W

    def finish_layer(l, x, mix, mem_o):
        return layer_tail(mix, w['out_mix'][l], mem_o, w['out_mem'][l], x, g_ffn[l], w['gu'][l], w['down'][l])

    b_p, s_p, _ = x_prompt.shape
    m_tok = mem_prompt.shape[1]
    mem_kv_p = mem_kv_rows(mem_prompt.reshape(b_p * m_tok, D_MODEL), w['mem_kv'], g_mem_k)
    mem_kv_p = mem_kv_p.reshape(DEPTH, b_p, m_tok, 2 * MEM_W)
    rel2 = rel_table.astype(F32) * LOG2E
    tables = nsa_bias_tables(rel2)
    c0, m0 = pack_mlstm_state(jnp.zeros((b_p, MLSTM_HEADS, MLSTM_DH, MLSTM_DH), F32),
                              jnp.zeros((b_p, MLSTM_HEADS, MLSTM_DH), F32),
                              jnp.full((b_p, MLSTM_HEADS), NEG, F32))
    x = x_prompt.reshape(b_p * s_p, D_MODEL)
    st_p = []
    for l in range(N_A_LAYERS):
        p = norm_matmul(x, g_mix[l], w['in_a'][l])
        mix, cn, mm = mlstm_prompt(p, b_p, s_p, b_if[l], g_hnorm[l], c0, m0)
        st_p.append(unpack_mlstm_state(cn, mm))
        mem_o = mem_attn_prompt(p, qcol_a, b_p, s_p, g_mem_q[l], mem_kv_p, l)
        x = finish_layer(l, x, mix, mem_o)
    cmp_p, slc_p, win_p, ka, vat = kv_rows(x, g_kv, w['kv'], g_k_slc, g_k_win, s_p, aug=True)
    ck, cvt = split_cmp_ctx(compress_rows(cmp_p.reshape(b_p, s_p, KV_ROW), cw))
    for j in range(N_B_LAYERS):
        l = N_A_LAYERS + j
        p = norm_matmul(x, g_mix[l], w['in_b'][j])
        mix = nsa_prompt(p, b_p, s_p, g_q[j], b_gate[j], rel2, ck, cvt, ka, vat, tables)
        mem_o = mem_attn_prompt(p, qcol_b, b_p, s_p, g_mem_q[l], mem_kv_p, l)
        x = finish_layer(l, x, mix, mem_o)
    y_p = x.reshape(b_p, s_p, D_MODEL)

    b_s, s_s, _ = x_sample.shape
    assert s_s == 1
    n_pages = page_table.shape[1]
    past = n_pages * PAGE_SIZE
    wb = cache_win_kv.shape[1]
    n_pool = cache_cmp_kv.shape[0]
    x = x_sample.reshape(b_s, D_MODEL)
    cache_mem = cache_mem_kv.reshape(DEPTH, b_s, -1, 2 * MEM_W)
    st_s = []
    for l in range(N_A_LAYERS):
        p = norm_matmul(x, g_mix[l], w['in_a'][l])
        mix, c_new, n_new, m_new = mlstm_step(p, b_if[l], g_hnorm[l], state_mlstm_c, state_mlstm_n, state_mlstm_m, l)
        st_s.append((c_new, n_new, m_new))
        mem_o = mem_attn_decode(p, qcol_a, g_mem_q[l], cache_mem, l)
        x = finish_layer(l, x, mix, mem_o)
    cmp_s, slc_s, win_s = kv_rows(x, g_kv, w['kv'], g_k_slc, g_k_win, 1, aug=False)
    assert (past + 1 - CMP_LEN) // CMP_STRIDE + 1 == past // CMP_STRIDE - 1
    ctx_s = compress_rows(cache_cmp_kv.reshape(n_pool, PAGE_SIZE, KV_ROW), cw, page_table)
    biases = nsa_decode_bias(rel_table, past, wb)
    cache_slc = cache_slc_kv.reshape(n_pool, PAGE_SIZE, KV_ROW)
    cache_win = cache_win_kv.reshape(b_s, wb, KV_ROW)
    for j in range(N_B_LAYERS):
        l = N_A_LAYERS + j
        p = norm_matmul(x, g_mix[l], w['in_b'][j])
        mix = nsa_decode(p, g_q[j], b_gate[j], ctx_s, page_table, cache_slc, slc_s, cache_win, win_s, biases)
        mem_o = mem_attn_decode(p, qcol_b, g_mem_q[l], cache_mem, l)
        x = finish_layer(l, x, mix, mem_o)
    y_s = x.reshape(b_s, 1, D_MODEL)

    rows5 = lambda r, b: r.reshape(b, -1, 2, NSA_KV_HEADS, HEAD_DIM)
    win_p5 = rows5(win_p, b_p)
    p_win = win_p5[:, s_p - min(WINDOW, s_p):]
    s_win = jnp.concatenate([cache_win_kv, rows5(win_s, b_s)], axis=1)[:, -wb:]
    stack = lambda sts, i: jnp.stack([st[i] for st in sts])
    return (y_p, y_s, stack(st_p, 0), stack(st_p, 1), stack(st_p, 2),
            mem_kv_p.reshape(DEPTH, b_p, m_tok, 2, MEM_HEADS, HEAD_DIM),
            rows5(cmp_p, b_p), rows5(slc_p, b_p), p_win,
            stack(st_s, 0), stack(st_s, 1), stack(st_s, 2),
            rows5(cmp_s, b_s), rows5(slc_s, b_s), s_win)
```

```python
import math
from functools import partial

import numpy as np
import jax
import jax.numpy as jnp
from jax import lax
from jax.experimental import pallas as pl
from jax.experimental.pallas import tpu as pltpu

D_MODEL = 1024
DEPTH = 4
PAGE_SIZE = 128
N_A_LAYERS = DEPTH // 2
N_B_LAYERS = DEPTH - N_A_LAYERS
MIX_W = 3 * D_MODEL // 4
MEM_W = D_MODEL - MIX_W
HEAD_DIM = 64
MEM_HEADS = MEM_W // HEAD_DIM
MLSTM_HEADS = 4
MLSTM_DH = MIX_W // MLSTM_HEADS
NSA_HEADS = MIX_W // HEAD_DIM
NSA_KV_HEADS = 2
NSA_GROUP = NSA_HEADS // NSA_KV_HEADS
CMP_LEN = 32
CMP_STRIDE = 16
CMP_HIDDEN = 128
SLC_BLOCK = 64
N_SELECT = 8
WINDOW = 512
REL_BUCKETS = 32
REL_MAX_EXACT = 16
REL_MAX_DIST = 128
D_FF = -(-(8 * D_MODEL) // (3 * 256)) * 256
EPS = 1e-6
NEG = -1e30
FORCED = 1e6

LANES = 128
VMEM_LIMIT = 48 * 1024 * 1024
TAIL_VMEM_LIMIT = 56 * 1024 * 1024

BF16 = jnp.bfloat16
F32 = jnp.float32


def _pick_tile(n, candidates):
    for c in candidates:
        if n % c == 0:
            return c
    return n


def _pad_cols(w, n):
    return jnp.pad(w, ((0, 0), (0, n - w.shape[1])))


def _dot_nt(a, b):
    return lax.dot_general(a, b, (((1,), (1,)), ((), ())), preferred_element_type=F32)


def _rms_rows(x, g):
    return x * lax.rsqrt(jnp.mean(x * x, axis=-1, keepdims=True) + EPS) * g


def _half_rms(x, g2, lo):
    x2 = x * x
    ss_lo = jnp.sum(jnp.where(lo, x2, 0.0), axis=1, keepdims=True)
    ss_hi = jnp.sum(jnp.where(lo, 0.0, x2), axis=1, keepdims=True)
    inv = jnp.where(lo, lax.rsqrt(ss_lo / HEAD_DIM + EPS), lax.rsqrt(ss_hi / HEAD_DIM + EPS))
    return x * inv * g2


def _gain2(g):
    return jnp.concatenate([g, g]).reshape(1, LANES).astype(F32)


def _params(*sem):
    return pltpu.CompilerParams(dimension_semantics=sem, vmem_limit_bytes=VMEM_LIMIT)


def _norm_matmul_kernel(x_ref, g_ref, w_ref, o_ref, xn_ref):
    @pl.when(pl.program_id(1) == 0)
    def _():
        xn_ref[...] = _rms_rows(x_ref[...], g_ref[...]).astype(BF16)

    o_ref[...] = jnp.dot(xn_ref[...], w_ref[...], preferred_element_type=F32)


def norm_matmul(x, g, w):
    m, k = x.shape
    n = w.shape[1]
    tm = _pick_tile(m, (1024, 512, 256, 128))
    tn = _pick_tile(n, (1280, 1152, 1024, 896, 768, 640, 512, 384, 256, 128))
    return pl.pallas_call(
        _norm_matmul_kernel,
        grid=(m // tm, n // tn),
        in_specs=[pl.BlockSpec((tm, k), lambda i, j: (i, 0)),
                  pl.BlockSpec((1, k), lambda i, j: (0, 0)),
                  pl.BlockSpec((k, tn), lambda i, j: (0, j))],
        out_specs=pl.BlockSpec((tm, tn), lambda i, j: (i, j)),
        out_shape=jax.ShapeDtypeStruct((m, n), F32),
        scratch_shapes=[pltpu.VMEM((tm, k), BF16)],
        compiler_params=_params("parallel", "arbitrary"),
        name="norm_matmul",
    )(x, g.reshape(1, k), w)


def _mem_kv_kernel(x_ref, w_ref, g_ref, o_ref):
    tm = x_ref.shape[0]
    kv = jnp.dot(x_ref[...].astype(BF16), w_ref[...], preferred_element_type=F32)
    lo = lax.broadcasted_iota(jnp.int32, (tm, LANES), 1) < HEAD_DIM
    for k in range(MEM_HEADS // 2):
        o_ref[:, k * LANES:(k + 1) * LANES] = _half_rms(kv[:, k * LANES:(k + 1) * LANES], g_ref[...], lo)
    o_ref[:, MEM_W:] = kv[:, MEM_W:]


def mem_kv_rows(mem, w, g_k):
    m, d = mem.shape
    depth, _, n = w.shape
    tm = _pick_tile(m, (512, 256, 128))
    g2 = jnp.concatenate([g_k, g_k], axis=1).reshape(depth, 1, LANES).astype(F32)
    return pl.pallas_call(
        _mem_kv_kernel,
        grid=(depth, m // tm),
        in_specs=[pl.BlockSpec((tm, d), lambda l, i: (i, 0)),
                  pl.BlockSpec((None, d, n), lambda l, i: (l, 0, 0)),
                  pl.BlockSpec((None, 1, LANES), lambda l, i: (l, 0, 0))],
        out_specs=pl.BlockSpec((None, tm, n), lambda l, i: (l, i, 0)),
        out_shape=jax.ShapeDtypeStruct((depth, m, n), F32),
        compiler_params=_params("parallel", "parallel"),
        name="mem_kv_rows",
    )(mem, w, g2)


def _layer_tail_kernel(a_ref, wa_ref, b_ref, wb_ref, x_ref, g_ref, wg_ref, wu_ref, wd_ref, o_ref, xn_ref):
    f = pl.program_id(1)

    @pl.when(f == 0)
    def _():
        x1 = (x_ref[...] + jnp.dot(a_ref[...].astype(BF16), wa_ref[...], preferred_element_type=F32)
              + jnp.dot(b_ref[...].astype(BF16), wb_ref[...], preferred_element_type=F32))
        o_ref[...] = x1
        xn_ref[...] = _rms_rows(x1, g_ref[...]).astype(BF16)

    xn = xn_ref[...]
    gate = jnp.dot(xn, wg_ref[...], preferred_element_type=F32)
    up = jnp.dot(xn, wu_ref[...], preferred_element_type=F32)
    act = (gate * jax.nn.sigmoid(gate) * up).astype(BF16)
    o_ref[...] += jnp.dot(act, wd_ref[...], preferred_element_type=F32)


def layer_tail(a, wa, b, wb, x, g, w_gu, w_down):
    m, d = x.shape
    tm = _pick_tile(m, (512, 256, 128))
    tf = 1408
    nf = D_FF // tf
    ka, kb = a.shape[1], b.shape[1]
    row = lambda k: pl.BlockSpec((tm, k), lambda i, f: (i, 0))
    const = lambda shape: pl.BlockSpec(shape, lambda i, f: (0, 0))
    return pl.pallas_call(
        _layer_tail_kernel,
        grid=(m // tm, nf),
        in_specs=[row(ka), const((ka, d)), row(kb), const((kb, d)), row(d), const((1, d)),
                  pl.BlockSpec((d, tf), lambda i, f: (0, f)),
                  pl.BlockSpec((d, tf), lambda i, f: (0, f + nf)),
                  pl.BlockSpec((tf, d), lambda i, f: (f, 0))],
        out_specs=row(d),
        out_shape=jax.ShapeDtypeStruct((m, d), F32),
        scratch_shapes=[pltpu.VMEM((tm, d), BF16)],
        compiler_params=pltpu.CompilerParams(dimension_semantics=("parallel", "arbitrary"),
                                             vmem_limit_bytes=TAIL_VMEM_LIMIT),
        name="layer_tail",
    )(a, wa, b, wb, x, g.reshape(1, d), w_gu, w_gu, w_down)


MLSTM_L = 128
MLSTM_PAD = 256
N_COL = MLSTM_DH
MLSTM_HW = MLSTM_HEADS * MLSTM_PAD
STEP_TB = 8


def _exact_tri_cumsum(tri, x):
    hi = x.astype(BF16)
    r1 = x - hi.astype(F32)
    mid = r1.astype(BF16)
    lo = (r1 - mid.astype(F32)).astype(BF16)
    return (jnp.dot(tri, hi, preferred_element_type=F32) + jnp.dot(tri, mid, preferred_element_type=F32)
            + jnp.dot(tri, lo, preferred_element_type=F32))


def _mlstm_kernel(q_ref, k_ref, v_ref, og_ref, gate_ref, bif_ref, gh_ref, c0_ref, m0_ref,
                  mix_ref, cout_ref, mout_ref, c_scr, m_scr):
    L = MLSTM_L
    P = MLSTM_PAD
    ci = pl.program_id(1)

    @pl.when(ci == 0)
    def _():
        c_scr[...] = c0_ref[...]
        m_scr[...] = m0_ref[...]

    row = lax.broadcasted_iota(jnp.int32, (L, L), 0)
    col = lax.broadcasted_iota(jnp.int32, (L, L), 1)
    causal = col <= row
    tri = jnp.where(causal, 1.0, 0.0).astype(BF16)
    lane_p = lax.broadcasted_iota(jnp.int32, (L, P), 1)

    gl = gate_ref[...] + bif_ref[...]
    logf = jax.nn.log_sigmoid(gl)
    bc = _exact_tri_cumsum(tri, logf)
    bt = bc.T
    gt = gl.T
    for h in range(MLSTM_HEADS):
        fh = MLSTM_HEADS + h
        b_col = bc[:, fh:fh + 1]
        b_row = bt[fh:fh + 1, :]
        i_row = gt[h:h + 1, :]
        i_col = gl[:, h:h + 1]
        m_prev = m_scr[0:1, h:h + 1]
        log_d = jnp.where(causal, b_col - b_row + i_row, -jnp.inf)
        log_inter = b_col + m_prev
        m_t = jnp.maximum(jnp.max(log_d, axis=1, keepdims=True), log_inter)
        w_intra = jnp.exp(log_d - m_t)
        w_inter = jnp.exp(log_inter - m_t)
        sl = slice(h * P, (h + 1) * P)
        q = q_ref[:, sl].astype(BF16)
        kf = k_ref[:, sl] * MLSTM_DH ** -0.5
        v_aug = jnp.where(lane_p == N_COL, 1.0, v_ref[:, sl])
        sc = _dot_nt(q, kf.astype(BF16)) * w_intra
        c = c_scr[h]
        nd = (jnp.dot(sc.astype(BF16), v_aug.astype(BF16), preferred_element_type=F32)
              + w_inter * jnp.dot(q, c.astype(BF16), preferred_element_type=F32))
        den = nd[:, N_COL:N_COL + 1]
        hh = jnp.where(lane_p < N_COL, nd / jnp.maximum(jnp.abs(den), jnp.exp(-m_t)), 0.0)
        y = hh * lax.rsqrt(jnp.sum(hh * hh, axis=1, keepdims=True) / MLSTM_DH + EPS) * gh_ref[...]
        mix_ref[:, sl] = jax.nn.sigmoid(og_ref[:, sl]) * y
        g_last = bc[L - 1:L, fh:fh + 1]
        m_new = m_t[L - 1:L, :]
        w_s = jnp.exp(g_last - b_col + i_col - m_new)
        decay = jnp.exp(g_last + m_prev - m_new)
        upd = jnp.dot(kf.T.astype(BF16), (w_s * v_aug).astype(BF16), preferred_element_type=F32)
        c_scr[h] = decay * c + upd
        m_scr[0:1, h:h + 1] = m_new

    @pl.when(ci == pl.num_programs(1) - 1)
    def _():
        cout_ref[...] = c_scr[...]
        mout_ref[...] = m_scr[...]


def mlstm_prompt(p, b, s, b_if, g_hnorm, c0, m0):
    L = MLSTM_L
    P = MLSTM_PAD
    hw = MLSTM_HW
    assert s % L == 0
    nc = s // L
    gcol = (4 * hw + MEM_W) // LANES
    bif = _pad_cols(b_if.astype(F32).reshape(1, -1), LANES)
    gh = _pad_cols(g_hnorm.astype(F32).reshape(1, -1), P)
    blk = lambda j: pl.BlockSpec((L, hw), lambda bi, ci: (bi * nc + ci, j))
    return pl.pallas_call(
        _mlstm_kernel,
        grid=(b, nc),
        in_specs=[blk(0), blk(1), blk(2), blk(3),
                  pl.BlockSpec((L, LANES), lambda bi, ci: (bi * nc + ci, gcol)),
                  pl.BlockSpec((1, LANES), lambda bi, ci: (0, 0)),
                  pl.BlockSpec((1, P), lambda bi, ci: (0, 0)),
                  pl.BlockSpec((None, MLSTM_HEADS, P, P), lambda bi, ci: (bi, 0, 0, 0)),
                  pl.BlockSpec((None, 1, LANES), lambda bi, ci: (bi, 0, 0))],
        out_specs=[pl.BlockSpec((L, hw), lambda bi, ci: (bi * nc + ci, 0)),
                   pl.BlockSpec((None, MLSTM_HEADS, P, P), lambda bi, ci: (bi, 0, 0, 0)),
                   pl.BlockSpec((None, 1, LANES), lambda bi, ci: (bi, 0, 0))],
        out_shape=[jax.ShapeDtypeStruct((b * s, hw), F32),
                   jax.ShapeDtypeStruct((b, MLSTM_HEADS, P, P), F32),
                   jax.ShapeDtypeStruct((b, 1, LANES), F32)],
        scratch_shapes=[pltpu.VMEM((MLSTM_HEADS, P, P), F32), pltpu.VMEM((1, LANES), F32)],
        compiler_params=_params("parallel", "arbitrary"),
        name="mlstm_prompt",
    )(p, p, p, p, p, bif, gh, c0, m0)


def pack_mlstm_state(c, n, m):
    pad = MLSTM_PAD - MLSTM_DH
    cn = jnp.concatenate([c, n[..., None]], axis=-1)
    cn = jnp.pad(cn, ((0, 0), (0, 0), (0, pad), (0, pad - 1)))
    return cn, _pad_cols(m, LANES)[:, None, :]


def unpack_mlstm_state(cn, m):
    return cn[:, :, :MLSTM_DH, :MLSTM_DH], cn[:, :, :MLSTM_DH, MLSTM_DH], m[:, 0, :MLSTM_HEADS]


def _mlstm_step_kernel(q_ref, v_ref, og_ref, gate_ref, kt_ref, bif_ref, gh_ref, c_ref, n_ref, m_ref,
                       mix_ref, cout_ref, nout_ref, mout_ref):
    tb = q_ref.shape[0]
    dh = MLSTM_DH
    gl = gate_ref[...] + bif_ref[...]
    logf = jax.nn.log_sigmoid(gl)
    i4 = gl[:, :MLSTM_HEADS]
    f4 = logf[:, MLSTM_HEADS:2 * MLSTM_HEADS]
    m_prev = m_ref[...]
    m_t = jnp.maximum(i4, f4 + m_prev)
    w_in = jnp.exp(i4 - m_t)
    w_dec = jnp.exp(f4 + m_prev - m_t)
    floor = jnp.exp(-m_t)
    mout_ref[...] = m_t
    mix_ref[...] = jnp.zeros(mix_ref.shape, F32)
    rowi = lax.broadcasted_iota(jnp.int32, (LANES, dh), 0)
    for j in range(tb):
        for h in range(MLSTM_HEADS):
            sl = slice(h * MLSTM_PAD, h * MLSTM_PAD + dh)
            wi = w_in[j:j + 1, h:h + 1]
            wd = w_dec[j:j + 1, h:h + 1]
            q_row = q_ref[j:j + 1, sl]
            v_row = v_ref[j:j + 1, sl]
            k_row = kt_ref[1, h, j:j + 1, :]
            c = c_ref[j, h]
            n_row = n_ref[j, h:h + 1, :]
            qk = jnp.sum(q_row * k_row, axis=1, keepdims=True) * wi
            qc = jnp.dot(q_ref[:, sl].astype(BF16), c.astype(BF16), preferred_element_type=F32)[j:j + 1, :]
            qn = jnp.sum(q_row * n_row, axis=1, keepdims=True)
            v_sel = jnp.where(rowi == j, wi * v_row, 0.0).astype(BF16)
            kv = jnp.dot(kt_ref[0, h, :, :LANES].astype(BF16), v_sel, preferred_element_type=F32)
            num = qk * v_row + wd * qc
            den = qk + wd * qn
            hh = num / jnp.maximum(jnp.abs(den), floor[j:j + 1, h:h + 1])
            y = hh * lax.rsqrt(jnp.sum(hh * hh, axis=1, keepdims=True) / dh + EPS) * gh_ref[...]
            mix_ref[j:j + 1, sl] = jax.nn.sigmoid(og_ref[j:j + 1, sl]) * y
            cout_ref[j, h] = wd * c + kv
            nout_ref[j, h:h + 1, :] = wd * n_row + wi * k_row


def mlstm_step(p, b_if, g_hnorm, c, n, m, layer):
    b = p.shape[0]
    tb = STEP_TB
    hw = MLSTM_HW
    dh = MLSTM_DH
    nb = b // tb
    gcol = (4 * hw + MEM_W) // LANES
    k = p[:, hw:2 * hw].reshape(nb, tb, MLSTM_HEADS, MLSTM_PAD)[..., :dh] * dh ** -0.5
    k_cols = jnp.pad(k.transpose(0, 2, 3, 1), ((0, 0), (0, 0), (0, 0), (0, dh - tb)))
    k_rows = jnp.pad(k.transpose(0, 2, 1, 3), ((0, 0), (0, 0), (0, dh - tb), (0, 0)))
    kt = jnp.stack([k_cols, k_rows], axis=1)
    bif = _pad_cols(b_if.astype(F32).reshape(1, -1), LANES)
    gh = g_hnorm.astype(F32).reshape(1, dh)
    blk = lambda j: pl.BlockSpec((tb, hw), lambda i: (i, j))
    st4 = pl.BlockSpec((tb, MLSTM_HEADS, dh, dh), lambda i: (i, 0, 0, 0))
    st3 = pl.BlockSpec((tb, MLSTM_HEADS, dh), lambda i: (i, 0, 0))
    st2 = pl.BlockSpec((tb, MLSTM_HEADS), lambda i: (i, 0))
    return pl.pallas_call(
        _mlstm_step_kernel,
        grid=(nb,),
        in_specs=[blk(0), blk(2), blk(3),
                  pl.BlockSpec((tb, LANES), lambda i: (i, gcol)),
                  pl.BlockSpec((None, 2, MLSTM_HEADS, dh, dh), lambda i: (i, 0, 0, 0, 0)),
                  pl.BlockSpec((1, LANES), lambda i: (0, 0)),
                  pl.BlockSpec((1, dh), lambda i: (0, 0)),
                  pl.BlockSpec((None, tb, MLSTM_HEADS, dh, dh), lambda i: (layer, i, 0, 0, 0)),
                  pl.BlockSpec((None, tb, MLSTM_HEADS, dh), lambda i: (layer, i, 0, 0)),
                  pl.BlockSpec((None, tb, MLSTM_HEADS), lambda i: (layer, i, 0))],
        out_specs=[pl.BlockSpec((tb, hw), lambda i: (i, 0)), st4, st3, st2],
        out_shape=[jax.ShapeDtypeStruct((b, hw), F32), jax.ShapeDtypeStruct(c.shape[1:], F32),
                   jax.ShapeDtypeStruct(n.shape[1:], F32), jax.ShapeDtypeStruct(m.shape[1:], F32)],
        compiler_params=_params("parallel"),
        name="mlstm_step",
    )(p, p, p, p, kt, bif, gh, c, n, m)


def _mem_attn_kernel(q_ref, gq_ref, kv_ref, o_ref):
    tq = q_ref.shape[0]
    lane = lax.broadcasted_iota(jnp.int32, (tq, LANES), 1)
    lo = lane < HEAD_DIM
    for k in range(MEM_HEADS // 2):
        y = _half_rms(q_ref[:, k * LANES:(k + 1) * LANES], gq_ref[...], lo) * HEAD_DIM ** -0.5
        kp = kv_ref[:, k * LANES:(k + 1) * LANES].astype(BF16)
        vp = kv_ref[:, MEM_W + k * LANES:MEM_W + (k + 1) * LANES].astype(BF16)
        outs = []
        for half in range(2):
            qh = jnp.where(lo if half == 0 else ~lo, y, 0.0).astype(BF16)
            s = _dot_nt(qh, kp)
            p = jnp.exp(s - jnp.max(s, axis=1, keepdims=True))
            p = p / jnp.sum(p, axis=1, keepdims=True)
            outs.append(jnp.dot(p.astype(BF16), vp, preferred_element_type=F32))
        o_ref[:, k * LANES:(k + 1) * LANES] = jnp.where(lo, outs[0], outs[1])


def mem_attn_prompt(p, qcol, b, s, g_q, mem_kv, layer):
    tq = _pick_tile(s, (512, 256, 128))
    nq = s // tq
    m_tok = mem_kv.shape[2]
    return pl.pallas_call(
        _mem_attn_kernel,
        grid=(b, nq),
        in_specs=[pl.BlockSpec((tq, MEM_W), lambda bi, qi: (bi * nq + qi, qcol)),
                  pl.BlockSpec((1, LANES), lambda bi, qi: (0, 0)),
                  pl.BlockSpec((None, None, m_tok, 2 * MEM_W), lambda bi, qi: (layer, bi, 0, 0))],
        out_specs=pl.BlockSpec((tq, MEM_W), lambda bi, qi: (bi * nq + qi, 0)),
        out_shape=jax.ShapeDtypeStruct((b * s, MEM_W), F32),
        compiler_params=_params("parallel", "parallel"),
        name="mem_attn",
    )(p, _gain2(g_q), mem_kv)


def _mem_decode_kernel(q_ref, gq_ref, *refs):
    kv_refs, o_ref = refs[:-1], refs[-1]
    m_tok = kv_refs[0].shape[1] // 2
    tb = q_ref.shape[0]
    lane = lax.broadcasted_iota(jnp.int32, (tb, LANES), 1)
    lo = lane < HEAD_DIM
    y = jnp.concatenate([_half_rms(q_ref[:, k * LANES:(k + 1) * LANES], gq_ref[...], lo)
                         for k in range(MEM_HEADS // 2)], axis=1) * HEAD_DIM ** -0.5
    rowi = lax.broadcasted_iota(jnp.int32, (8, MEM_W), 0)
    own = (lax.broadcasted_iota(jnp.int32, (8, MEM_W), 1) // HEAD_DIM) == rowi
    for j in range(tb):
        qbd = jnp.where(own, y[j:j + 1, :], 0.0).astype(BF16)
        keys = [r[j, pl.ds(0, m_tok, stride=2), :].astype(BF16) for r in kv_refs]
        vals = [r[j, pl.ds(1, m_tok, stride=2), :].astype(BF16) for r in kv_refs]
        s = sum(_dot_nt(qbd[:, k * LANES:(k + 1) * LANES], keys[k]) for k in range(len(keys)))
        p = jnp.exp(s - jnp.max(s, axis=1, keepdims=True))
        p = (p / jnp.sum(p, axis=1, keepdims=True)).astype(BF16)
        o = jnp.concatenate([jnp.dot(p, v, preferred_element_type=F32) for v in vals], axis=1)
        o_ref[j:j + 1, :] = jnp.sum(jnp.where(own, o, 0.0), axis=0, keepdims=True)


def mem_attn_decode(p, qcol, g_q, mem_kv, layer):
    b = p.shape[0]
    tb = STEP_TB
    rows = mem_kv.shape[2]
    halves = [pl.BlockSpec((None, tb, rows, LANES), partial(lambda k, i: (layer, i, 0, k), k))
              for k in range(MEM_W // LANES)]
    return pl.pallas_call(
        _mem_decode_kernel,
        grid=(b // tb,),
        in_specs=[pl.BlockSpec((tb, MEM_W), lambda i: (i, qcol)),
                  pl.BlockSpec((1, LANES), lambda i: (0, 0))] + halves,
        out_specs=pl.BlockSpec((tb, MEM_W), lambda i: (i, 0)),
        out_shape=jax.ShapeDtypeStruct((b, MEM_W), F32),
        compiler_params=_params("parallel"),
        name="mem_attn_decode",
    )(p, _gain2(g_q), *([mem_kv] * len(halves)))


NSA_TQ = 128
MASK = -(2.0 ** 100)
SEL_LANE = HEAD_DIM
N_SEL_LANES = 32
CONST_LANE = SEL_LANE + N_SEL_LANES
KV_ROW = 2 * NSA_KV_HEADS * HEAD_DIM
SEG_PER_PAGE = PAGE_SIZE // CMP_STRIDE
SEG_ROW = CMP_STRIDE * KV_ROW
CMP_OUT = 2 * NSA_KV_HEADS * CMP_HIDDEN
QROWS = 16


def _rel_bucket_np(dist):
    d = np.maximum(dist, 0)
    ratio = np.maximum(d, REL_MAX_EXACT).astype(np.float64) / REL_MAX_EXACT
    large = REL_MAX_EXACT + (np.log(ratio) / math.log(REL_MAX_DIST / REL_MAX_EXACT)
                             * (REL_BUCKETS - REL_MAX_EXACT)).astype(np.int32)
    return np.where(d < REL_MAX_EXACT, d, np.minimum(large, REL_BUCKETS - 1)).astype(np.int32)


def _overlap_matrix(n_cmp, n_slc):
    c0 = np.arange(n_cmp)[:, None] * CMP_STRIDE
    s0 = np.arange(n_slc)[None, :] * SLC_BLOCK
    return np.clip(np.minimum(c0 + CMP_LEN, s0 + SLC_BLOCK) - np.maximum(c0, s0), 0, None) / CMP_STRIDE


def _kv_rows_kernel(x_ref, g_ref, w_ref, gs_ref, gw_ref, cmp_ref, slc_ref, win_ref, *aug_refs, seq_len):
    tm = x_ref.shape[0]
    xn = _rms_rows(x_ref[...], g_ref[...]).astype(BF16)
    kv = jnp.dot(xn, w_ref[...], preferred_element_type=F32)
    cmp_ref[...] = kv[:, :KV_ROW]
    lane = lax.broadcasted_iota(jnp.int32, (tm, LANES), 1)
    lo = lane < HEAD_DIM
    if aug_refs:
        ka_ref, va_ref = aug_refs
        pos = (pl.program_id(0) * tm + lax.broadcasted_iota(jnp.int32, (tm, LANES), 0)) % seq_len
        ones = (lane == CONST_LANE) | (lane == CONST_LANE + 1)
        onehot = (lane - SEL_LANE) == pos // SLC_BLOCK
    for t, (rows_ref, gk_ref) in enumerate(((slc_ref, gs_ref), (win_ref, gw_ref))):
        base = KV_ROW * (t + 1)
        kn = _half_rms(kv[:, base:base + LANES], gk_ref[...], lo)
        vv = kv[:, base + LANES:base + 2 * LANES]
        rows_ref[:, :LANES] = kn
        rows_ref[:, LANES:] = vv
        if aug_refs:
            extra = jnp.where((ones | onehot) if t == 0 else ones, 1.0, 0.0)
            rowt = lax.broadcasted_iota(jnp.int32, (VT_ROWS, tm), 0)
            for g in range(NSA_KV_HEADS):
                kg = kn if g == 0 else pltpu.roll(kn, HEAD_DIM, 1)
                vg = vv if g == 0 else pltpu.roll(vv, HEAD_DIM, 1)
                ka_ref[NSA_KV_HEADS * t + g] = jnp.where(lo, kg, extra).astype(BF16)
                vt = jnp.where(lo, vg, 0.0).T[:VT_ROWS, :]
                va_ref[NSA_KV_HEADS * t + g] = jnp.where(rowt == HEAD_DIM, 1.0, vt).astype(BF16)


def kv_rows(x, g_kv, w_kv, g_k_slc, g_k_win, seq_len, aug):
    m, d = x.shape
    n = w_kv.shape[1]
    tm = _pick_tile(m, (512, 256, 128))
    out_specs = [pl.BlockSpec((tm, KV_ROW), lambda i: (i, 0))] * 3
    out_shape = [jax.ShapeDtypeStruct((m, KV_ROW), F32)] * 3
    if aug:
        out_specs += [pl.BlockSpec((2 * NSA_KV_HEADS, tm, LANES), lambda i: (0, i, 0)),
                      pl.BlockSpec((2 * NSA_KV_HEADS, VT_ROWS, tm), lambda i: (0, 0, i))]
        out_shape += [jax.ShapeDtypeStruct((2 * NSA_KV_HEADS, m, LANES), BF16),
                      jax.ShapeDtypeStruct((2 * NSA_KV_HEADS, VT_ROWS, m), BF16)]
    return pl.pallas_call(
        partial(_kv_rows_kernel, seq_len=seq_len),
        grid=(m // tm,),
        in_specs=[pl.BlockSpec((tm, d), lambda i: (i, 0)),
                  pl.BlockSpec((1, d), lambda i: (0, 0)),
                  pl.BlockSpec((d, n), lambda i: (0, 0)),
                  pl.BlockSpec((1, LANES), lambda i: (0, 0)),
                  pl.BlockSpec((1, LANES), lambda i: (0, 0))],
        out_specs=out_specs,
        out_shape=out_shape,
        compiler_params=_params("parallel"),
        name="kv_rows",
    )(x, g_kv.reshape(1, d), w_kv, _gain2(g_k_slc), _gain2(g_k_win))


def _compress_kernel(*refs, n_x, paged):
    if paged:
        refs = refs[1:]
    x_refs = refs[:n_x]
    w1_ref, b1_ref, w2_ref, b2_ref, gk_ref, o_ref = refs[n_x:]
    n_seg = o_ref.shape[0]
    r_n = CMP_LEN // CMP_STRIDE
    lane = lax.broadcasted_iota(jnp.int32, (n_seg, LANES), 1)
    for c in range(2):
        acc = [jnp.zeros((n_seg, CMP_OUT // 2), F32) for _ in range(r_n)]
        refs_c = x_refs[c::2]
        for u in range(CMP_STRIDE):
            pieces = [r[pl.ds(u, r.shape[0] // CMP_STRIDE, stride=CMP_STRIDE), :] for r in refs_c]
            xu = (pieces[0] if len(pieces) == 1 else jnp.concatenate(pieces, axis=0)).astype(BF16)
            for r in range(r_n):
                acc[r] = acc[r] + jnp.dot(xu, w1_ref[r, u, c], preferred_element_type=F32)
        pre = acc[0] + pltpu.roll(acc[1], n_seg - 1, 0) + b1_ref[c]
        out = jnp.dot(jax.nn.gelu(pre).astype(BF16), w2_ref[c], preferred_element_type=F32) + b2_ref[c]
        if c == 0:
            out = _half_rms(out, gk_ref[...], lane < HEAD_DIM)
        o_ref[:, c * LANES:(c + 1) * LANES] = out.astype(BF16)


def compress_weights(w1, b1, w2, b2, pos_enc, g_k_cmp):
    r_n = CMP_LEN // CMP_STRIDE
    eye = jnp.eye(NSA_KV_HEADS, dtype=F32)
    w1r = w1.reshape(2, r_n, CMP_STRIDE, HEAD_DIM, CMP_HIDDEN)
    w1b = jnp.einsum('crudh,gy->rucgdyh', w1r, eye).reshape(r_n, CMP_STRIDE, 2, LANES, CMP_OUT // 2).astype(BF16)
    pe = jnp.einsum('pcd,cpdh->ch', pos_enc, w1)
    b1b = jnp.broadcast_to((b1 + pe)[:, None, :], (2, NSA_KV_HEADS, CMP_HIDDEN)).reshape(2, 1, CMP_OUT // 2)
    w2b = jnp.einsum('chd,gy->cghyd', w2, eye).reshape(2, CMP_OUT // 2, LANES).astype(BF16)
    b2b = jnp.broadcast_to(b2[:, None, :], (2, NSA_KV_HEADS, HEAD_DIM)).reshape(2, 1, LANES)
    return w1b, b1b, w2b, b2b, _gain2(g_k_cmp)


def compress_rows(x, cw, page_table=None):
    w1b, b1b, w2b, b2b, gk2 = cw
    paged = page_table is not None
    if paged:
        n_seq, n_pages = page_table.shape
        x_specs = [pl.BlockSpec((None, PAGE_SIZE, LANES), partial(lambda k, c, b, pt: (pt[b, k], 0, c), k, c))
                   for k in range(n_pages) for c in range(2)]
        n_seg = n_pages * SEG_PER_PAGE
        cmap = lambda *idx: lambda b, pt: idx
    else:
        n_seq, n_seg = x.shape[0], x.shape[1] // CMP_STRIDE
        x_specs = [pl.BlockSpec((None, x.shape[1], LANES), partial(lambda c, b: (b, 0, c), c)) for c in range(2)]
        cmap = lambda *idx: lambda b: idx
    n_x = len(x_specs)
    w_specs = [pl.BlockSpec(w1b.shape, cmap(0, 0, 0, 0, 0)), pl.BlockSpec(b1b.shape, cmap(0, 0, 0)),
               pl.BlockSpec(w2b.shape, cmap(0, 0, 0)), pl.BlockSpec(b2b.shape, cmap(0, 0, 0)),
               pl.BlockSpec(gk2.shape, cmap(0, 0))]
    out_spec = pl.BlockSpec((None, n_seg, KV_ROW), (lambda b, pt: (b, 0, 0)) if paged else (lambda b: (b, 0, 0)))
    out_shape = jax.ShapeDtypeStruct((n_seq, n_seg, KV_ROW), BF16)
    kern = partial(_compress_kernel, n_x=n_x, paged=paged)
    if paged:
        gs = pltpu.PrefetchScalarGridSpec(num_scalar_prefetch=1, grid=(n_seq,), in_specs=x_specs + w_specs,
                                          out_specs=out_spec)
        return pl.pallas_call(kern, grid_spec=gs, out_shape=out_shape, compiler_params=_params("parallel"),
                              name="compress_paged")(page_table, *([x] * n_x), w1b, b1b, w2b, b2b, gk2)
    return pl.pallas_call(kern, grid=(n_seq,), in_specs=x_specs + w_specs, out_specs=out_spec, out_shape=out_shape,
                          compiler_params=_params("parallel"), name="compress")(x, x, w1b, b1b, w2b, b2b, gk2)


VT_ROWS = 80
CMP_BAND = 16


LOG2E = 1.0 / math.log(2.0)
FAR_BLOCKS = 4
NEAR_BLOCKS = 5
FULL_MASK_TILE = 4


def _nsa_prompt_kernel(rel_ref, q_ref, gate_ref, gq_ref, bg_ref, ck_ref, cvt_ref, band_ref, far_ref, mt_ref,
                        ka_ref, vat_ref, dl_ref, o_ref, qa_ref, m_ref, acc_ref, cb_ref, *, n_cmp, n_tiles):
    tq = NSA_TQ
    cols = NSA_GROUP * tq
    qi = pl.program_id(1)
    lane = lax.broadcasted_iota(jnp.int32, (tq, LANES), 1)
    lo = lane < HEAD_DIM
    nidx = lax.broadcasted_iota(jnp.int32, (LANES, cols), 0)
    r_cb = lax.broadcasted_iota(jnp.int32, (LANES + CMP_BAND, cols), 0)
    jidx = lax.broadcasted_iota(jnp.int32, (N_SEL_LANES, tq), 0)
    tpos = qi * tq + lax.broadcasted_iota(jnp.int32, (N_SEL_LANES, tq), 1)
    cur = tpos // SLC_BLOCK
    forced = (jidx == 0) | (jidx == cur) | (jidx == cur - 1)
    half_band = CMP_BAND // 2

    def gate_rows(g):
        sig_t = jax.nn.sigmoid(gate_ref[:, g * LANES:(g + 1) * LANES] + bg_ref[:, g * LANES:(g + 1) * LANES]).T
        return [jnp.concatenate([sig_t[br * NSA_GROUP + z:br * NSA_GROUP + z + 1, :] for z in range(NSA_GROUP)], axis=1)
                for br in range(3)]

    def prologue(g):
        qn = []
        for k in range(NSA_GROUP // 2):
            kk = g * (NSA_GROUP // 2) + k
            y = _half_rms(q_ref[:, kk * LANES:(kk + 1) * LANES], gq_ref[...], lo) * (HEAD_DIM ** -0.5 * LOG2E)
            qn.append(jnp.where(lo, y, 0.0))
            qn.append(jnp.where(lo, pltpu.roll(y, HEAD_DIM, 1), 0.0))
        qc = jnp.concatenate(qn, axis=0).astype(BF16)

        cb_ref[g] = jnp.where(r_cb < half_band * qi, far_ref[g], MASK)
        cb_ref[g, pl.ds(pl.multiple_of(half_band * qi, half_band), CMP_BAND), :] = band_ref[g]
        cbias = jnp.where(nidx >= n_cmp, MASK, cb_ref[g, half_band:half_band + LANES, :])
        s = _dot_nt(ck_ref[g], qc) + cbias
        m = jnp.maximum(jnp.max(s, axis=0, keepdims=True), NEG)
        p = jnp.exp2(s - m)
        l = jnp.sum(p, axis=0, keepdims=True)
        pn = p * jnp.where(l > 0.0, 1.0 / l, 0.0)
        o_cmp = jnp.dot(cvt_ref[g], pn.astype(BF16), preferred_element_type=F32)

        psum = pn[:, 0:tq]
        for z in range(1, NSA_GROUP):
            psum = psum + pn[:, z * tq:(z + 1) * tq]
        p_hi = psum.astype(BF16)
        p_lo = (psum - p_hi.astype(F32)).astype(BF16)
        imp = (jnp.dot(mt_ref[...], p_hi, preferred_element_type=F32)
               + jnp.dot(mt_ref[...], p_lo, preferred_element_type=F32))
        score = jnp.where(forced, FORCED, jnp.where(jidx <= cur, imp, -1.0))
        rank = jnp.zeros((N_SEL_LANES, tq), F32)
        for i in range(N_SEL_LANES):
            ri = score[i:i + 1, :]
            beats = (ri > score) | ((ri == score) & (jidx > i))
            rank = rank + jnp.where(beats, 1.0, 0.0)
        selb = jnp.where((rank < N_SELECT) & (jidx <= cur), 0.0, MASK)
        selb = jnp.concatenate([selb, jnp.zeros((LANES - N_SEL_LANES, tq), F32)], axis=0)
        sel_cols = pltpu.roll(selb.T, SEL_LANE, 1)

        for z in range(NSA_GROUP):
            c = jnp.full((tq, LANES), rel_ref[REL_BUCKETS - 1, g * NSA_GROUP + z], F32)
            c_hi = c.astype(BF16).astype(F32)
            extra = jnp.where(lane == CONST_LANE, c_hi, jnp.where(lane == CONST_LANE + 1, c - c_hi, sel_cols))
            qa_ref[g, z * tq:(z + 1) * tq, :] = jnp.where(lo, qn[z], extra).astype(BF16)
        return o_cmp

    o_cmp = [prologue(g) for g in range(NSA_KV_HEADS)]

    n_far = jnp.maximum(qi - 1, 0) // FAR_BLOCKS
    far_rows = FAR_BLOCKS * tq
    m_ref[...] = jnp.full(m_ref.shape, NEG, F32)
    acc_ref[...] = jnp.zeros(acc_ref.shape, F32)

    def far_body(c, carry):
        start = pl.multiple_of(c * far_rows, far_rows)
        for g in range(NSA_KV_HEADS):
            s = _dot_nt(ka_ref[g, pl.ds(start, far_rows), :], qa_ref[g])
            m_prev = m_ref[g]
            m_next = jnp.maximum(m_prev, jnp.max(s, axis=0, keepdims=True))
            p = jnp.exp2(s - m_next).astype(BF16)
            acc_ref[g] = (jnp.exp2(m_prev - m_next) * acc_ref[g]
                          + jnp.dot(vat_ref[g, :, pl.ds(start, far_rows)], p, preferred_element_type=F32))
            m_ref[g] = m_next
        return carry

    lax.fori_loop(0, n_far, far_body, 0)

    def near_stage(idx, g, start_blk, tile_of, m_prev, acc_prev):
        start = pl.multiple_of(start_blk * tq, tq)
        s = _dot_nt(ka_ref[idx, pl.ds(start, NEAR_BLOCKS * tq), :], qa_ref[g])
        s = s + jnp.concatenate([dl_ref[tile_of(start_blk + i), g] for i in range(NEAR_BLOCKS)], axis=0)
        m_next = jnp.max(s, axis=0, keepdims=True)
        if m_prev is not None:
            m_next = jnp.maximum(m_prev, m_next)
        acc = jnp.dot(vat_ref[idx, :, pl.ds(start, NEAR_BLOCKS * tq)], jnp.exp2(s - m_next).astype(BF16),
                      preferred_element_type=F32)
        if acc_prev is not None:
            acc = acc + jnp.exp2(m_prev - m_next) * acc_prev
        return acc[:HEAD_DIM, :] / acc[HEAD_DIM:HEAD_DIM + 1, :]

    far_cov = FAR_BLOCKS * n_far
    slc_start = jnp.minimum(far_cov, n_tiles - NEAR_BLOCKS)
    win_start = jnp.maximum(qi - WINDOW // tq, 0)

    def slc_tile(j):
        d = qi - j
        return jnp.where((d < 0) | (j < far_cov), FULL_MASK_TILE, jnp.where(d == 0, 2, jnp.where(d == 1, 1, 0)))

    def win_tile(j):
        d = qi - j
        return jnp.where(d < 0, FULL_MASK_TILE,
                         jnp.where(d == 0, 2, jnp.where(d == 1, 1, jnp.where(d == WINDOW // tq, 3, 0))))

    outs = []
    for g in range(NSA_KV_HEADS):
        gr = gate_rows(g)
        o_slc = near_stage(g, g, slc_start, slc_tile, m_ref[g], acc_ref[g])
        o_win = near_stage(NSA_KV_HEADS + g, g, win_start, win_tile, None, None)
        outs.append(gr[0] * o_cmp[g] + gr[1] * o_slc + gr[2] * o_win)

    for g in range(NSA_KV_HEADS):
        for k in range(NSA_GROUP // 2):
            kk = g * (NSA_GROUP // 2) + k
            o_ref[:, kk * LANES:(kk + 1) * LANES] = jnp.concatenate(
                [outs[g][:, (2 * k) * tq:(2 * k + 1) * tq], outs[g][:, (2 * k + 1) * tq:(2 * k + 2) * tq]], axis=0).T


def nsa_prompt(p, b, s, g_q, b_gate, rel_table, ck, cvt, ka, vat, tables):
    tq = NSA_TQ
    nq = s // tq
    cols = NSA_GROUP * tq
    n_cmp = (s - CMP_LEN) // CMP_STRIDE + 1
    n_slc = -(-s // SLC_BLOCK)
    assert s % tq == 0 and nq >= NEAR_BLOCKS and n_cmp <= LANES and n_slc <= N_SEL_LANES and tq == 8 * CMP_STRIDE
    tiles, band, far_row = tables
    tiles = jnp.concatenate([tiles, jnp.full((1,) + tiles.shape[1:], MASK, F32)], axis=0)
    mt = np.zeros((N_SEL_LANES, LANES), np.float32)
    mt[:n_slc, :n_cmp] = _overlap_matrix(n_cmp, n_slc).T
    bg = _gate_bias_slabs(b_gate).reshape(1, NSA_KV_HEADS * LANES)
    gcol = (MIX_W + MEM_W) // (NSA_KV_HEADS * LANES)
    ka4 = ka.reshape(2 * NSA_KV_HEADS, b, s, LANES)
    c2 = lambda bi, qi: (0, 0)
    c3 = lambda bi, qi: (0, 0, 0)
    return pl.pallas_call(
        partial(_nsa_prompt_kernel, n_cmp=n_cmp, n_tiles=nq),
        grid=(b, nq),
        in_specs=[pl.BlockSpec(memory_space=pltpu.SMEM),
                  pl.BlockSpec((tq, MIX_W), lambda bi, qi: (bi * nq + qi, 0)),
                  pl.BlockSpec((tq, NSA_KV_HEADS * LANES), lambda bi, qi: (bi * nq + qi, gcol)),
                  pl.BlockSpec((1, LANES), c2),
                  pl.BlockSpec((1, NSA_KV_HEADS * LANES), c2),
                  pl.BlockSpec((None, NSA_KV_HEADS, LANES, LANES), lambda bi, qi: (bi, 0, 0, 0)),
                  pl.BlockSpec((None, NSA_KV_HEADS, HEAD_DIM, LANES), lambda bi, qi: (bi, 0, 0, 0)),
                  pl.BlockSpec((NSA_KV_HEADS, CMP_BAND, cols), c3),
                  pl.BlockSpec((NSA_KV_HEADS, 1, cols), c3),
                  pl.BlockSpec((N_SEL_LANES, LANES), c2),
                  pl.BlockSpec((2 * NSA_KV_HEADS, None, s, LANES), lambda bi, qi: (0, bi, 0, 0)),
                  pl.BlockSpec((2 * NSA_KV_HEADS, VT_ROWS, s), lambda bi, qi: (0, 0, bi)),
                  pl.BlockSpec((FULL_MASK_TILE + 1, NSA_KV_HEADS, tq, cols), lambda bi, qi: (0, 0, 0, 0))],
        out_specs=pl.BlockSpec((tq, MIX_W), lambda bi, qi: (bi * nq + qi, 0)),
        out_shape=jax.ShapeDtypeStruct((b * s, MIX_W), F32),
        scratch_shapes=[pltpu.VMEM((NSA_KV_HEADS, cols, LANES), BF16), pltpu.VMEM((NSA_KV_HEADS, 1, cols), F32),
                        pltpu.VMEM((NSA_KV_HEADS, VT_ROWS, cols), F32),
                        pltpu.VMEM((NSA_KV_HEADS, LANES + CMP_BAND, cols), F32)],
        compiler_params=_params("parallel", "arbitrary"),
        name="nsa_prompt",
    )(rel_table.astype(F32), p, p, _gain2(g_q), bg, ck, cvt, band, far_row, jnp.asarray(mt, BF16), ka4, vat, tiles)


def nsa_bias_tables(rel_table):
    tq = NSA_TQ
    table = rel_table.astype(F32)
    far = table[REL_BUCKETS - 1]
    r = np.arange(tq)[None, :]
    c = np.arange(tq)[:, None]

    def by_group(t):
        k = t.shape[0]
        return jnp.transpose(t.reshape(k, tq, NSA_KV_HEADS, NSA_GROUP), (2, 0, 3, 1)).reshape(NSA_KV_HEADS, k, -1)

    spread = (jnp.arange(2 * tq - 1)[:, None, None]
              == (tq - 1 - jnp.arange(tq)[None, :, None] + jnp.arange(tq)[None, None, :])).astype(F32)

    def tile(off, masked):
        d = off + np.arange(-(tq - 1), tq)
        v = jnp.where(jnp.asarray(masked(d))[:, None], MASK, table[_rel_bucket_np(d)] - far)
        return by_group(jnp.einsum('dh,dcr->crh', v, spread, precision=lax.Precision.HIGHEST))

    zero = jnp.zeros((NSA_KV_HEADS, tq, NSA_GROUP * tq), F32)
    tiles = jnp.stack([zero, tile(tq, lambda d: d < 0), tile(0, lambda d: d < 0),
                       tile(WINDOW, lambda d: d >= WINDOW)])
    n_rel = np.arange(CMP_BAND)[:, None] - CMP_BAND // 2
    dist = r - CMP_STRIDE * n_rel - (CMP_LEN - 1)
    band = by_group(jnp.where(jnp.asarray(dist < 0)[..., None], MASK, table[_rel_bucket_np(dist)]))
    far_row = by_group(jnp.broadcast_to(far, (1, tq, NSA_HEADS)))
    return tiles, band, far_row


def _gate_bias_slabs(b_gate):
    bg = b_gate.astype(F32).reshape(3, NSA_KV_HEADS, NSA_GROUP).transpose(1, 0, 2).reshape(NSA_KV_HEADS, 3 * NSA_GROUP)
    return _pad_cols(bg, LANES)


def split_cmp_ctx(ctx):
    b, n, _ = ctx.shape
    parts = ctx.reshape(b, n, 2, NSA_KV_HEADS, HEAD_DIM).transpose(2, 0, 3, 1, 4)
    parts = jnp.pad(parts, ((0, 0), (0, 0), (0, 0), (0, LANES - n), (0, 0)))
    ck = jnp.pad(parts[0], ((0, 0), (0, 0), (0, 0), (0, LANES - HEAD_DIM)))
    return ck, jnp.swapaxes(parts[1], -1, -2)


def _softmax_parts(s, s_new):
    m = jnp.maximum(jnp.max(s, axis=1, keepdims=True), s_new)
    p = jnp.exp(s - m)
    p_new = jnp.exp(s_new - m)
    inv = 1.0 / (jnp.sum(p, axis=1, keepdims=True) + p_new)
    return p * inv, p_new * inv


DECODE_SEQS = 2


def _nsa_decode_kernel(*refs, n_pages):
    (pt_ref, p_ref, gq_ref, bg_ref, cmp_ref, cb_ref, mt_ref) = refs[:7]
    n_in = DECODE_SEQS * n_pages
    page_refs = refs[7:7 + n_in]
    (snew_ref, sb_ref, e_ref, win_ref, wnew_ref, wb_ref, b0_ref, o_ref) = refs[7 + n_in:]
    del pt_ref
    for s in range(DECODE_SEQS):
        _nsa_decode_seq(p_ref.at[s], gq_ref, bg_ref, cmp_ref.at[s], cb_ref, mt_ref,
                        page_refs[s * n_pages:(s + 1) * n_pages], snew_ref.at[s], sb_ref, e_ref, win_ref.at[s],
                        wnew_ref.at[s], wb_ref, b0_ref, o_ref.at[s])


def _nsa_decode_seq(p_ref, gq_ref, bg_ref, cmp_ref, cb_ref, mt_ref, page_refs, snew_ref, sb_ref, e_ref, win_ref,
                    wnew_ref, wb_ref, b0_ref, o_ref):
    n_pages = len(page_refs)
    n_keys = n_pages * PAGE_SIZE
    lo1 = lax.broadcasted_iota(jnp.int32, (1, LANES), 1) < HEAD_DIM
    rowi = lax.broadcasted_iota(jnp.int32, (QROWS, LANES), 0)
    lanei = lax.broadcasted_iota(jnp.int32, (QROWS, LANES), 1)
    row_g = rowi // 8
    own = (lanei // HEAD_DIM) == row_g

    qbd = jnp.zeros((QROWS, LANES), F32)
    for k in range(NSA_HEADS // 2):
        y = _half_rms(p_ref[:, k * LANES:(k + 1) * LANES], gq_ref[...], lo1) * HEAD_DIM ** -0.5
        yr = pltpu.roll(y, HEAD_DIM, 1)
        for half in range(2):
            g, z = divmod(2 * k + half, NSA_GROUP)
            qbd = jnp.where(rowi == 8 * g + z, y if half == g else yr, qbd)
    qbd = jnp.where(own, qbd, 0.0)
    qb = qbd.astype(BF16)

    s = _dot_nt(qb, cmp_ref[:, :LANES]) + cb_ref[...]
    m = jnp.maximum(jnp.max(s, axis=1, keepdims=True), NEG)
    p = jnp.exp(s - m)
    l = jnp.sum(p, axis=1, keepdims=True)
    pn = p * jnp.where(l > 0.0, 1.0 / l, 0.0)
    o_cmp = jnp.dot(pn.astype(BF16), cmp_ref[:, LANES:], preferred_element_type=F32)

    valid_row = (rowi % 8) < NSA_GROUP
    ps0 = jnp.sum(jnp.where(valid_row & (row_g == 0), pn, 0.0), axis=0, keepdims=True)
    ps1 = jnp.sum(jnp.where(valid_row & (row_g == 1), pn, 0.0), axis=0, keepdims=True)
    r8 = lax.broadcasted_iota(jnp.int32, (8, LANES), 0)
    j8 = lax.broadcasted_iota(jnp.int32, (8, LANES), 1)
    psum = jnp.where(r8 == 0, ps0, jnp.where(r8 == 1, ps1, 0.0))
    p_hi = psum.astype(BF16)
    p_lo = (psum - p_hi.astype(F32)).astype(BF16)
    imp = (jnp.dot(p_hi, mt_ref[...], preferred_element_type=F32)
           + jnp.dot(p_lo, mt_ref[...], preferred_element_type=F32))
    cur = n_keys // SLC_BLOCK
    forced = (j8 == 0) | (j8 == cur) | (j8 == cur - 1)
    score = jnp.where(forced, FORCED, jnp.where(j8 <= cur, imp, -1.0))
    rank = jnp.zeros((8, LANES), F32)
    for i in range(cur + 1):
        ri = score[:, i:i + 1]
        beats = (ri > score) | ((ri == score) & (j8 > i))
        rank = rank + jnp.where(beats, 1.0, 0.0)
    selb = jnp.where(rank < N_SELECT, 0.0, MASK)
    sel16 = jnp.where(row_g == 0, selb[0:1, :], selb[1:2, :]).astype(BF16)
    blockmask = jnp.dot(sel16, e_ref[...], preferred_element_type=F32)

    s = jnp.concatenate([_dot_nt(qb, r[:, :LANES].astype(BF16)) for r in page_refs], axis=1)
    s = s + sb_ref[...] + blockmask
    s_new = jnp.sum(qbd * snew_ref[:, :LANES], axis=1, keepdims=True) + b0_ref[...]
    pp, p_new = _softmax_parts(s, s_new)
    o_slc = p_new * snew_ref[:, LANES:]
    for k, r in enumerate(page_refs):
        o_slc = o_slc + jnp.dot(pp[:, k * PAGE_SIZE:(k + 1) * PAGE_SIZE].astype(BF16), r[:, LANES:].astype(BF16),
                                preferred_element_type=F32)

    n_wt = win_ref.shape[0] // LANES
    s = jnp.concatenate([_dot_nt(qb, win_ref[k * LANES:(k + 1) * LANES, :LANES].astype(BF16)) for k in range(n_wt)],
                        axis=1) + wb_ref[...]
    s_new = jnp.sum(qbd * wnew_ref[:, :LANES], axis=1, keepdims=True) + b0_ref[...]
    pp, p_new = _softmax_parts(s, s_new)
    o_win = p_new * wnew_ref[:, LANES:]
    for k in range(n_wt):
        o_win = o_win + jnp.dot(pp[:, k * LANES:(k + 1) * LANES].astype(BF16),
                                win_ref[k * LANES:(k + 1) * LANES, LANES:].astype(BF16), preferred_element_type=F32)

    gbase = MIX_W + MEM_W
    sig = jax.nn.sigmoid(p_ref[:, gbase:gbase + NSA_KV_HEADS * LANES] + bg_ref[...])
    gate = [jnp.zeros((QROWS, 1), F32) for _ in range(3)]
    rcol = lax.broadcasted_iota(jnp.int32, (QROWS, 1), 0)
    for br in range(3):
        for g in range(NSA_KV_HEADS):
            for z in range(NSA_GROUP):
                c = g * LANES + br * NSA_GROUP + z
                gate[br] = jnp.where(rcol == 8 * g + z, sig[:, c:c + 1], gate[br])
    comb = jnp.where(own, gate[0] * o_cmp + gate[1] * o_slc + gate[2] * o_win, 0.0)
    comb_r = pltpu.roll(comb, HEAD_DIM, 1)
    for k in range(NSA_HEADS // 2):
        acc = jnp.zeros((1, LANES), F32)
        for half in range(2):
            g, z = divmod(2 * k + half, NSA_GROUP)
            src = comb if half == g else comb_r
            acc = acc + src[8 * g + z:8 * g + z + 1, :]
        o_ref[:, k * LANES:(k + 1) * LANES] = acc


def nsa_decode_bias(rel_table, past, wb):
    table = rel_table.astype(F32)

    def rows16(t):
        z = jnp.zeros((2, t.shape[1]), F32)
        return jnp.concatenate([t[:NSA_GROUP], z, t[NSA_GROUP:], z], axis=0)

    n_cmp = (past + 1 - CMP_LEN) // CMP_STRIDE + 1
    c_end = np.arange(LANES) * CMP_STRIDE + CMP_LEN - 1
    dist = past - c_end
    cb = jnp.where(jnp.asarray((dist < 0) | (np.arange(LANES) >= n_cmp))[None], MASK, table[_rel_bucket_np(dist)].T)
    sb = table[_rel_bucket_np(past - np.arange(past))].T
    wpos = past - wb + np.arange(wb)
    wbias = jnp.where(jnp.asarray(past - wpos >= WINDOW)[None], MASK, table[_rel_bucket_np(past - wpos)].T)
    b0 = table[0][:, None]
    return rows16(cb), rows16(sb), rows16(wbias), rows16(b0)


def nsa_decode(p, g_q, b_gate, cmp_ctx, page_table, cache_slc, slc_new, cache_win, win_new, biases):
    b, n_pages = page_table.shape
    past = n_pages * PAGE_SIZE
    wb = cache_win.shape[1]
    cb, sb, wbias, b0 = biases
    n_cmp = (past + 1 - CMP_LEN) // CMP_STRIDE + 1
    n_slc = -(-(past + 1) // SLC_BLOCK)
    assert n_slc <= LANES and n_cmp <= LANES and wb % LANES == 0
    mt = np.zeros((LANES, LANES), np.float32)
    mt[:n_cmp, :n_slc] = _overlap_matrix(n_cmp, n_slc)
    e = (np.arange(past)[None, :] // SLC_BLOCK == np.arange(LANES)[:, None]).astype(np.float32)
    bg = _gate_bias_slabs(b_gate).reshape(1, NSA_KV_HEADS * LANES)
    ns = DECODE_SEQS
    assert b % ns == 0
    c2 = lambda bi, pt: (0, 0)
    seq3 = lambda bi, pt: (bi, 0, 0)
    in_specs = ([pl.BlockSpec((ns, 1, p.shape[1]), seq3),
                 pl.BlockSpec((1, LANES), c2),
                 pl.BlockSpec((1, NSA_KV_HEADS * LANES), c2),
                 pl.BlockSpec((ns, LANES, KV_ROW), seq3),
                 pl.BlockSpec((QROWS, LANES), c2),
                 pl.BlockSpec((LANES, LANES), c2)]
                + [pl.BlockSpec((None, PAGE_SIZE, KV_ROW), partial(lambda s, k, bi, pt: (pt[ns * bi + s, k], 0, 0), s, k))
                   for s in range(ns) for k in range(n_pages)]
                + [pl.BlockSpec((ns, 1, KV_ROW), seq3),
                   pl.BlockSpec((QROWS, past), c2),
                   pl.BlockSpec((LANES, past), c2),
                   pl.BlockSpec((ns, wb, KV_ROW), seq3),
                   pl.BlockSpec((ns, 1, KV_ROW), seq3),
                   pl.BlockSpec((QROWS, wb), c2),
                   pl.BlockSpec((QROWS, 1), c2)])
    gs = pltpu.PrefetchScalarGridSpec(num_scalar_prefetch=1, grid=(b // ns,), in_specs=in_specs,
                                      out_specs=pl.BlockSpec((ns, 1, MIX_W), seq3))
    out = pl.pallas_call(
        partial(_nsa_decode_kernel, n_pages=n_pages),
        grid_spec=gs,
        out_shape=jax.ShapeDtypeStruct((b, 1, MIX_W), F32),
        compiler_params=_params("parallel"),
        name="nsa_decode",
    )(page_table, p.reshape(b, 1, -1), _gain2(g_q), bg, cmp_ctx, cb, jnp.asarray(mt, BF16),
      *([cache_slc] * (ns * n_pages)), slc_new.reshape(b, 1, KV_ROW), sb, jnp.asarray(e, BF16),
      cache_win, win_new.reshape(b, 1, KV_ROW), wbias, b0)
    return out.reshape(b, MIX_W)


GATE_PAD = LANES


def _pad_heads_cols(w):
    d = w.shape[0]
    return jnp.pad(w.reshape(d, MLSTM_HEADS, MLSTM_DH), ((0, 0), (0, 0), (0, MLSTM_PAD - MLSTM_DH))).reshape(d, MLSTM_HW)


def prep_weights(w_in_a, w_in_b, w_kv, w_mem_kv, w_out, w_gu, w_down):
    w = {}
    n_gate_a = 2 * MLSTM_HEADS
    n_gate_b = 3 * NSA_HEADS
    w['in_a'] = []
    for l in range(N_A_LAYERS):
        wa = w_in_a[l]
        qkvo = [_pad_heads_cols(wa[:, j * MIX_W:(j + 1) * MIX_W]) for j in range(4)]
        w['in_a'].append(jnp.concatenate(qkvo + [wa[:, 4 * MIX_W + n_gate_a:],
                                                 _pad_cols(wa[:, 4 * MIX_W:4 * MIX_W + n_gate_a], GATE_PAD)],
                                         axis=1).astype(BF16))
    w['in_b'] = []
    for j in range(N_B_LAYERS):
        wg = w_in_b[j][:, MIX_W:MIX_W + n_gate_b].reshape(D_MODEL, 3, NSA_KV_HEADS, NSA_GROUP)
        slabs = [_pad_cols(wg[:, :, g].reshape(D_MODEL, 3 * NSA_GROUP), GATE_PAD) for g in range(NSA_KV_HEADS)]
        w['in_b'].append(jnp.concatenate([w_in_b[j][:, :MIX_W], w_in_b[j][:, MIX_W + n_gate_b:]] + slabs,
                                         axis=1).astype(BF16))
    w['out_mix'] = []
    for l in range(DEPTH):
        wm = w_out[l][:MIX_W]
        if l < N_A_LAYERS:
            wm = jnp.pad(wm.reshape(MLSTM_HEADS, MLSTM_DH, D_MODEL),
                         ((0, 0), (0, MLSTM_PAD - MLSTM_DH), (0, 0))).reshape(MLSTM_HW, D_MODEL)
        w['out_mix'].append(wm.astype(BF16))
    w['out_mem'] = w_out[:, MIX_W:].astype(BF16)
    w['kv'] = w_kv.astype(BF16)
    w['mem_kv'] = w_mem_kv.astype(BF16)
    w['gu'] = w_gu.astype(BF16)
    w['down'] = w_down.astype(BF16)
    return w


def kernel(x_prompt, x_sample, state_mlstm_c, state_mlstm_n, state_mlstm_m, cache_mem_kv,
           cache_cmp_kv, cache_slc_kv, cache_win_kv, page_table, mem_prompt,
           g_mix, w_in_a, b_if, g_hnorm, w_in_b, g_q, b_gate, rel_table, g_kv, w_kv,
           g_k_slc, g_k_win, g_k_cmp, cmp_pos, cmp_w1, cmp_b1, cmp_w2, cmp_b2,
           w_mem_kv, g_mem_k, g_mem_q, w_out, g_ffn, w_gu, w_down):
    w = prep_weights(w_in_a, w_in_b, w_kv, w_mem_kv, w_out, w_gu, w_down)
    cw = compress_weights(cmp_w1, cmp_b1, cmp_w2, cmp_b2, cmp_pos, g_k_cmp)
    qcol_a = 4 * MLSTM_HW // MEM_W
    qcol_b = MIX_W // MEM_W

    def finish_layer(l, x, mix, mem_o):
        return layer_tail(mix, w['out_mix'][l], mem_o, w['out_mem'][l], x, g_ffn[l], w['gu'][l], w['down'][l])

    b_p, s_p, _ = x_prompt.shape
    m_tok = mem_prompt.shape[1]
    mem_kv_p = mem_kv_rows(mem_prompt.reshape(b_p * m_tok, D_MODEL), w['mem_kv'], g_mem_k)
    mem_kv_p = mem_kv_p.reshape(DEPTH, b_p, m_tok, 2 * MEM_W)
    rel2 = rel_table.astype(F32) * LOG2E
    tables = nsa_bias_tables(rel2)
    c0, m0 = pack_mlstm_state(jnp.zeros((b_p, MLSTM_HEADS, MLSTM_DH, MLSTM_DH), F32),
                              jnp.zeros((b_p, MLSTM_HEADS, MLSTM_DH), F32),
                              jnp.full((b_p, MLSTM_HEADS), NEG, F32))
    x = x_prompt.reshape(b_p * s_p, D_MODEL)
    st_p = []
    for l in range(N_A_LAYERS):
        p = norm_matmul(x, g_mix[l], w['in_a'][l])
        mix, cn, mm = mlstm_prompt(p, b_p, s_p, b_if[l], g_hnorm[l], c0, m0)
        st_p.append(unpack_mlstm_state(cn, mm))
        mem_o = mem_attn_prompt(p, qcol_a, b_p, s_p, g_mem_q[l], mem_kv_p, l)
        x = finish_layer(l, x, mix, mem_o)
    cmp_p, slc_p, win_p, ka, vat = kv_rows(x, g_kv, w['kv'], g_k_slc, g_k_win, s_p, aug=True)
    ck, cvt = split_cmp_ctx(compress_rows(cmp_p.reshape(b_p, s_p, KV_ROW), cw))
    for j in range(N_B_LAYERS):
        l = N_A_LAYERS + j
        p = norm_matmul(x, g_mix[l], w['in_b'][j])
        mix = nsa_prompt(p, b_p, s_p, g_q[j], b_gate[j], rel2, ck, cvt, ka, vat, tables)
        mem_o = mem_attn_prompt(p, qcol_b, b_p, s_p, g_mem_q[l], mem_kv_p, l)
        x = finish_layer(l, x, mix, mem_o)
    y_p = x.reshape(b_p, s_p, D_MODEL)

    b_s, s_s, _ = x_sample.shape
    assert s_s == 1
    n_pages = page_table.shape[1]
    past = n_pages * PAGE_SIZE
    wb = cache_win_kv.shape[1]
    n_pool = cache_cmp_kv.shape[0]
    x = x_sample.reshape(b_s, D_MODEL)
    cache_mem = cache_mem_kv.reshape(DEPTH, b_s, -1, MEM_W)
    st_s = []
    for l in range(N_A_LAYERS):
        p = norm_matmul(x, g_mix[l], w['in_a'][l])
        mix, c_new, n_new, m_new = mlstm_step(p, b_if[l], g_hnorm[l], state_mlstm_c, state_mlstm_n, state_mlstm_m, l)
        st_s.append((c_new, n_new, m_new))
        mem_o = mem_attn_decode(p, qcol_a, g_mem_q[l], cache_mem, l)
        x = finish_layer(l, x, mix, mem_o)
    cmp_s, slc_s, win_s = kv_rows(x, g_kv, w['kv'], g_k_slc, g_k_win, 1, aug=False)
    assert (past + 1 - CMP_LEN) // CMP_STRIDE + 1 == past // CMP_STRIDE - 1
    ctx_s = compress_rows(cache_cmp_kv.reshape(n_pool, PAGE_SIZE, KV_ROW), cw, page_table)
    biases = nsa_decode_bias(rel_table, past, wb)
    cache_slc = cache_slc_kv.reshape(n_pool, PAGE_SIZE, KV_ROW)
    cache_win = cache_win_kv.reshape(b_s, wb, KV_ROW)
    for j in range(N_B_LAYERS):
        l = N_A_LAYERS + j
        p = norm_matmul(x, g_mix[l], w['in_b'][j])
        mix = nsa_decode(p, g_q[j], b_gate[j], ctx_s, page_table, cache_slc, slc_s, cache_win, win_s, biases)
        mem_o = mem_attn_decode(p, qcol_b, g_mem_q[l], cache_mem, l)
        x = finish_layer(l, x, mix, mem_o)
    y_s = x.reshape(b_s, 1, D_MODEL)

    rows5 = lambda r, b: r.reshape(b, -1, 2, NSA_KV_HEADS, HEAD_DIM)
    win_p5 = rows5(win_p, b_p)
    p_win = win_p5[:, s_p - min(WINDOW, s_p):]
    s_win = jnp.concatenate([cache_win_kv, rows5(win_s, b_s)], axis=1)[:, -wb:]
    stack = lambda sts, i: jnp.stack([st[i] for st in sts])
    return (y_p, y_s, stack(st_p, 0), stack(st_p, 1), stack(st_p, 2),
            mem_kv_p.reshape(DEPTH, b_p, m_tok, 2, MEM_HEADS, HEAD_DIM),
            rows5(cmp_p, b_p), rows5(slc_p, b_p), p_win,
            stack(st_s, 0), stack(st_s, 1), stack(st_s, 2),
            rows5(cmp_s, b_s), rows5(slc_s, b_s), s_win)
```

```python
import math
from functools import partial

import numpy as np
import jax
import jax.numpy as jnp
from jax import lax
from jax.experimental import pallas as pl
from jax.experimental.pallas import tpu as pltpu

D_MODEL = 1024
DEPTH = 4
PAGE_SIZE = 128
N_A_LAYERS = DEPTH // 2
N_B_LAYERS = DEPTH - N_A_LAYERS
MIX_W = 3 * D_MODEL // 4
MEM_W = D_MODEL - MIX_W
HEAD_DIM = 64
MEM_HEADS = MEM_W // HEAD_DIM
MLSTM_HEADS = 4
MLSTM_DH = MIX_W // MLSTM_HEADS
NSA_HEADS = MIX_W // HEAD_DIM
NSA_KV_HEADS = 2
NSA_GROUP = NSA_HEADS // NSA_KV_HEADS
CMP_LEN = 32
CMP_STRIDE = 16
CMP_HIDDEN = 128
SLC_BLOCK = 64
N_SELECT = 8
WINDOW = 512
REL_BUCKETS = 32
REL_MAX_EXACT = 16
REL_MAX_DIST = 128
D_FF = -(-(8 * D_MODEL) // (3 * 256)) * 256
EPS = 1e-6
NEG = -1e30
FORCED = 1e6

LANES = 128
VMEM_LIMIT = 48 * 1024 * 1024
TAIL_VMEM_LIMIT = 56 * 1024 * 1024

BF16 = jnp.bfloat16
F32 = jnp.float32


def _pick_tile(n, candidates):
    for c in candidates:
        if n % c == 0:
            return c
    return n


def _pad_cols(w, n):
    return jnp.pad(w, ((0, 0), (0, n - w.shape[1])))


def _dot_nt(a, b):
    return lax.dot_general(a, b, (((1,), (1,)), ((), ())), preferred_element_type=F32)


def _rms_rows(x, g):
    return x * lax.rsqrt(jnp.mean(x * x, axis=-1, keepdims=True) + EPS) * g


def _half_rms(x, g2, lo):
    x2 = x * x
    ss_lo = jnp.sum(jnp.where(lo, x2, 0.0), axis=1, keepdims=True)
    ss_hi = jnp.sum(jnp.where(lo, 0.0, x2), axis=1, keepdims=True)
    inv = jnp.where(lo, lax.rsqrt(ss_lo / HEAD_DIM + EPS), lax.rsqrt(ss_hi / HEAD_DIM + EPS))
    return x * inv * g2


def _gain2(g):
    return jnp.concatenate([g, g]).reshape(1, LANES).astype(F32)


def _params(*sem):
    return pltpu.CompilerParams(dimension_semantics=sem, vmem_limit_bytes=VMEM_LIMIT)


def _norm_matmul_kernel(x_ref, g_ref, w_ref, o_ref, xn_ref):
    @pl.when(pl.program_id(1) == 0)
    def _():
        xn_ref[...] = _rms_rows(x_ref[...], g_ref[...]).astype(BF16)

    o_ref[...] = jnp.dot(xn_ref[...], w_ref[...], preferred_element_type=F32)


def norm_matmul(x, g, w):
    m, k = x.shape
    n = w.shape[1]
    tm = _pick_tile(m, (1024, 512, 256, 128))
    tn = _pick_tile(n, (1280, 1152, 1024, 896, 768, 640, 512, 384, 256, 128))
    return pl.pallas_call(
        _norm_matmul_kernel,
        grid=(m // tm, n // tn),
        in_specs=[pl.BlockSpec((tm, k), lambda i, j: (i, 0)),
                  pl.BlockSpec((1, k), lambda i, j: (0, 0)),
                  pl.BlockSpec((k, tn), lambda i, j: (0, j))],
        out_specs=pl.BlockSpec((tm, tn), lambda i, j: (i, j)),
        out_shape=jax.ShapeDtypeStruct((m, n), F32),
        scratch_shapes=[pltpu.VMEM((tm, k), BF16)],
        compiler_params=_params("parallel", "arbitrary"),
        name="norm_matmul",
    )(x, g.reshape(1, k), w)


def _mem_kv_kernel(x_ref, w_ref, g_ref, o_ref):
    tm = x_ref.shape[0]
    kv = jnp.dot(x_ref[...].astype(BF16), w_ref[...], preferred_element_type=F32)
    lo = lax.broadcasted_iota(jnp.int32, (tm, LANES), 1) < HEAD_DIM
    for k in range(MEM_HEADS // 2):
        o_ref[:, k * LANES:(k + 1) * LANES] = _half_rms(kv[:, k * LANES:(k + 1) * LANES], g_ref[...], lo)
    o_ref[:, MEM_W:] = kv[:, MEM_W:]


def mem_kv_rows(mem, w, g_k):
    m, d = mem.shape
    depth, _, n = w.shape
    tm = _pick_tile(m, (512, 256, 128))
    g2 = jnp.concatenate([g_k, g_k], axis=1).reshape(depth, 1, LANES).astype(F32)
    return pl.pallas_call(
        _mem_kv_kernel,
        grid=(depth, m // tm),
        in_specs=[pl.BlockSpec((tm, d), lambda l, i: (i, 0)),
                  pl.BlockSpec((None, d, n), lambda l, i: (l, 0, 0)),
                  pl.BlockSpec((None, 1, LANES), lambda l, i: (l, 0, 0))],
        out_specs=pl.BlockSpec((None, tm, n), lambda l, i: (l, i, 0)),
        out_shape=jax.ShapeDtypeStruct((depth, m, n), F32),
        compiler_params=_params("parallel", "parallel"),
        name="mem_kv_rows",
    )(mem, w, g2)


def _layer_tail_kernel(a_ref, wa_ref, b_ref, wb_ref, x_ref, g_ref, wg_ref, wu_ref, wd_ref, o_ref, xn_ref):
    f = pl.program_id(1)

    @pl.when(f == 0)
    def _():
        x1 = (x_ref[...] + jnp.dot(a_ref[...].astype(BF16), wa_ref[...], preferred_element_type=F32)
              + jnp.dot(b_ref[...].astype(BF16), wb_ref[...], preferred_element_type=F32))
        o_ref[...] = x1
        xn_ref[...] = _rms_rows(x1, g_ref[...]).astype(BF16)

    xn = xn_ref[...]
    gate = jnp.dot(xn, wg_ref[...], preferred_element_type=F32)
    up = jnp.dot(xn, wu_ref[...], preferred_element_type=F32)
    act = (gate * jax.nn.sigmoid(gate) * up).astype(BF16)
    o_ref[...] += jnp.dot(act, wd_ref[...], preferred_element_type=F32)


def layer_tail(a, wa, b, wb, x, g, w_gu, w_down):
    m, d = x.shape
    tm = _pick_tile(m, (512, 256, 128))
    tf = 1408
    nf = D_FF // tf
    ka, kb = a.shape[1], b.shape[1]
    row = lambda k: pl.BlockSpec((tm, k), lambda i, f: (i, 0))
    const = lambda shape: pl.BlockSpec(shape, lambda i, f: (0, 0))
    return pl.pallas_call(
        _layer_tail_kernel,
        grid=(m // tm, nf),
        in_specs=[row(ka), const((ka, d)), row(kb), const((kb, d)), row(d), const((1, d)),
                  pl.BlockSpec((d, tf), lambda i, f: (0, f)),
                  pl.BlockSpec((d, tf), lambda i, f: (0, f + nf)),
                  pl.BlockSpec((tf, d), lambda i, f: (f, 0))],
        out_specs=row(d),
        out_shape=jax.ShapeDtypeStruct((m, d), F32),
        scratch_shapes=[pltpu.VMEM((tm, d), BF16)],
        compiler_params=pltpu.CompilerParams(dimension_semantics=("parallel", "arbitrary"),
                                             vmem_limit_bytes=TAIL_VMEM_LIMIT),
        name="layer_tail",
    )(a, wa, b, wb, x, g.reshape(1, d), w_gu, w_gu, w_down)


MLSTM_L = 128
MLSTM_PAD = 256
N_COL = MLSTM_DH
MLSTM_HW = MLSTM_HEADS * MLSTM_PAD
STEP_TB = 8


def _exact_tri_cumsum(tri, x):
    hi = x.astype(BF16)
    r1 = x - hi.astype(F32)
    mid = r1.astype(BF16)
    lo = (r1 - mid.astype(F32)).astype(BF16)
    return (jnp.dot(tri, hi, preferred_element_type=F32) + jnp.dot(tri, mid, preferred_element_type=F32)
            + jnp.dot(tri, lo, preferred_element_type=F32))


def _mlstm_kernel(q_ref, k_ref, v_ref, og_ref, gate_ref, bif_ref, gh_ref, c0_ref, m0_ref,
                  mix_ref, cout_ref, mout_ref, c_scr, m_scr):
    L = MLSTM_L
    P = MLSTM_PAD
    ci = pl.program_id(1)

    @pl.when(ci == 0)
    def _():
        c_scr[...] = c0_ref[...]
        m_scr[...] = m0_ref[...]

    row = lax.broadcasted_iota(jnp.int32, (L, L), 0)
    col = lax.broadcasted_iota(jnp.int32, (L, L), 1)
    causal = col <= row
    tri = jnp.where(causal, 1.0, 0.0).astype(BF16)
    lane_p = lax.broadcasted_iota(jnp.int32, (L, P), 1)

    gl = gate_ref[...] + bif_ref[...]
    logf = jax.nn.log_sigmoid(gl)
    bc = _exact_tri_cumsum(tri, logf)
    bt = bc.T
    gt = gl.T
    for h in range(MLSTM_HEADS):
        fh = MLSTM_HEADS + h
        b_col = bc[:, fh:fh + 1]
        b_row = bt[fh:fh + 1, :]
        i_row = gt[h:h + 1, :]
        i_col = gl[:, h:h + 1]
        m_prev = m_scr[0:1, h:h + 1]
        log_d = jnp.where(causal, b_col - b_row + i_row, -jnp.inf)
        log_inter = b_col + m_prev
        m_t = jnp.maximum(jnp.max(log_d, axis=1, keepdims=True), log_inter)
        w_intra = jnp.exp(log_d - m_t)
        w_inter = jnp.exp(log_inter - m_t)
        sl = slice(h * P, (h + 1) * P)
        q = q_ref[:, sl].astype(BF16)
        kf = k_ref[:, sl] * MLSTM_DH ** -0.5
        v_aug = jnp.where(lane_p == N_COL, 1.0, v_ref[:, sl])
        sc = _dot_nt(q, kf.astype(BF16)) * w_intra
        c = c_scr[h]
        nd = (jnp.dot(sc.astype(BF16), v_aug.astype(BF16), preferred_element_type=F32)
              + w_inter * jnp.dot(q, c.astype(BF16), preferred_element_type=F32))
        den = nd[:, N_COL:N_COL + 1]
        hh = jnp.where(lane_p < N_COL, nd / jnp.maximum(jnp.abs(den), jnp.exp(-m_t)), 0.0)
        y = hh * lax.rsqrt(jnp.sum(hh * hh, axis=1, keepdims=True) / MLSTM_DH + EPS) * gh_ref[...]
        mix_ref[:, sl] = jax.nn.sigmoid(og_ref[:, sl]) * y
        g_last = bc[L - 1:L, fh:fh + 1]
        m_new = m_t[L - 1:L, :]
        w_s = jnp.exp(g_last - b_col + i_col - m_new)
        decay = jnp.exp(g_last + m_prev - m_new)
        upd = jnp.dot(kf.T.astype(BF16), (w_s * v_aug).astype(BF16), preferred_element_type=F32)
        c_scr[h] = decay * c + upd
        m_scr[0:1, h:h + 1] = m_new

    @pl.when(ci == pl.num_programs(1) - 1)
    def _():
        cout_ref[...] = c_scr[...]
        mout_ref[...] = m_scr[...]


def mlstm_prompt(p, b, s, b_if, g_hnorm, c0, m0):
    L = MLSTM_L
    P = MLSTM_PAD
    hw = MLSTM_HW
    assert s % L == 0
    nc = s // L
    gcol = (4 * hw + MEM_W) // LANES
    bif = _pad_cols(b_if.astype(F32).reshape(1, -1), LANES)
    gh = _pad_cols(g_hnorm.astype(F32).reshape(1, -1), P)
    blk = lambda j: pl.BlockSpec((L, hw), lambda bi, ci: (bi * nc + ci, j))
    return pl.pallas_call(
        _mlstm_kernel,
        grid=(b, nc),
        in_specs=[blk(0), blk(1), blk(2), blk(3),
                  pl.BlockSpec((L, LANES), lambda bi, ci: (bi * nc + ci, gcol)),
                  pl.BlockSpec((1, LANES), lambda bi, ci: (0, 0)),
                  pl.BlockSpec((1, P), lambda bi, ci: (0, 0)),
                  pl.BlockSpec((None, MLSTM_HEADS, P, P), lambda bi, ci: (bi, 0, 0, 0)),
                  pl.BlockSpec((None, 1, LANES), lambda bi, ci: (bi, 0, 0))],
        out_specs=[pl.BlockSpec((L, hw), lambda bi, ci: (bi * nc + ci, 0)),
                   pl.BlockSpec((None, MLSTM_HEADS, P, P), lambda bi, ci: (bi, 0, 0, 0)),
                   pl.BlockSpec((None, 1, LANES), lambda bi, ci: (bi, 0, 0))],
        out_shape=[jax.ShapeDtypeStruct((b * s, hw), F32),
                   jax.ShapeDtypeStruct((b, MLSTM_HEADS, P, P), F32),
                   jax.ShapeDtypeStruct((b, 1, LANES), F32)],
        scratch_shapes=[pltpu.VMEM((MLSTM_HEADS, P, P), F32), pltpu.VMEM((1, LANES), F32)],
        compiler_params=_params("parallel", "arbitrary"),
        name="mlstm_prompt",
    )(p, p, p, p, p, bif, gh, c0, m0)


def pack_mlstm_state(c, n, m):
    pad = MLSTM_PAD - MLSTM_DH
    cn = jnp.concatenate([c, n[..., None]], axis=-1)
    cn = jnp.pad(cn, ((0, 0), (0, 0), (0, pad), (0, pad - 1)))
    return cn, _pad_cols(m, LANES)[:, None, :]


def unpack_mlstm_state(cn, m):
    return cn[:, :, :MLSTM_DH, :MLSTM_DH], cn[:, :, :MLSTM_DH, MLSTM_DH], m[:, 0, :MLSTM_HEADS]


def _mlstm_step_kernel(q_ref, v_ref, og_ref, gate_ref, kt_ref, bif_ref, gh_ref, c_ref, n_ref, m_ref,
                       mix_ref, cout_ref, nout_ref, mout_ref):
    tb = q_ref.shape[0]
    dh = MLSTM_DH
    gl = gate_ref[...] + bif_ref[...]
    logf = jax.nn.log_sigmoid(gl)
    i4 = gl[:, :MLSTM_HEADS]
    f4 = logf[:, MLSTM_HEADS:2 * MLSTM_HEADS]
    m_prev = m_ref[...]
    m_t = jnp.maximum(i4, f4 + m_prev)
    w_in = jnp.exp(i4 - m_t)
    w_dec = jnp.exp(f4 + m_prev - m_t)
    floor = jnp.exp(-m_t)
    mout_ref[...] = m_t
    mix_ref[...] = jnp.zeros(mix_ref.shape, F32)
    rowi = lax.broadcasted_iota(jnp.int32, (LANES, dh), 0)
    for j in range(tb):
        for h in range(MLSTM_HEADS):
            sl = slice(h * MLSTM_PAD, h * MLSTM_PAD + dh)
            wi = w_in[j:j + 1, h:h + 1]
            wd = w_dec[j:j + 1, h:h + 1]
            q_row = q_ref[j:j + 1, sl]
            v_row = v_ref[j:j + 1, sl]
            k_row = kt_ref[1, h, j:j + 1, :]
            c = c_ref[j, h]
            n_row = n_ref[j, h:h + 1, :]
            qk = jnp.sum(q_row * k_row, axis=1, keepdims=True) * wi
            qc = jnp.dot(q_ref[:, sl].astype(BF16), c.astype(BF16), preferred_element_type=F32)[j:j + 1, :]
            qn = jnp.sum(q_row * n_row, axis=1, keepdims=True)
            v_sel = jnp.where(rowi == j, wi * v_row, 0.0).astype(BF16)
            kv = jnp.dot(kt_ref[0, h, :, :LANES].astype(BF16), v_sel, preferred_element_type=F32)
            num = qk * v_row + wd * qc
            den = qk + wd * qn
            hh = num / jnp.maximum(jnp.abs(den), floor[j:j + 1, h:h + 1])
            y = hh * lax.rsqrt(jnp.sum(hh * hh, axis=1, keepdims=True) / dh + EPS) * gh_ref[...]
            mix_ref[j:j + 1, sl] = jax.nn.sigmoid(og_ref[j:j + 1, sl]) * y
            cout_ref[j, h] = wd * c + kv
            nout_ref[j, h:h + 1, :] = wd * n_row + wi * k_row


def mlstm_step(p, b_if, g_hnorm, c, n, m, layer):
    b = p.shape[0]
    tb = STEP_TB
    hw = MLSTM_HW
    dh = MLSTM_DH
    nb = b // tb
    gcol = (4 * hw + MEM_W) // LANES
    k = p[:, hw:2 * hw].reshape(nb, tb, MLSTM_HEADS, MLSTM_PAD)[..., :dh] * dh ** -0.5
    k_cols = jnp.pad(k.transpose(0, 2, 3, 1), ((0, 0), (0, 0), (0, 0), (0, dh - tb)))
    k_rows = jnp.pad(k.transpose(0, 2, 1, 3), ((0, 0), (0, 0), (0, dh - tb), (0, 0)))
    kt = jnp.stack([k_cols, k_rows], axis=1)
    bif = _pad_cols(b_if.astype(F32).reshape(1, -1), LANES)
    gh = g_hnorm.astype(F32).reshape(1, dh)
    blk = lambda j: pl.BlockSpec((tb, hw), lambda i: (i, j))
    st4 = pl.BlockSpec((tb, MLSTM_HEADS, dh, dh), lambda i: (i, 0, 0, 0))
    st3 = pl.BlockSpec((tb, MLSTM_HEADS, dh), lambda i: (i, 0, 0))
    st2 = pl.BlockSpec((tb, MLSTM_HEADS), lambda i: (i, 0))
    return pl.pallas_call(
        _mlstm_step_kernel,
        grid=(nb,),
        in_specs=[blk(0), blk(2), blk(3),
                  pl.BlockSpec((tb, LANES), lambda i: (i, gcol)),
                  pl.BlockSpec((None, 2, MLSTM_HEADS, dh, dh), lambda i: (i, 0, 0, 0, 0)),
                  pl.BlockSpec((1, LANES), lambda i: (0, 0)),
                  pl.BlockSpec((1, dh), lambda i: (0, 0)),
                  pl.BlockSpec((None, tb, MLSTM_HEADS, dh, dh), lambda i: (layer, i, 0, 0, 0)),
                  pl.BlockSpec((None, tb, MLSTM_HEADS, dh), lambda i: (layer, i, 0, 0)),
                  pl.BlockSpec((None, tb, MLSTM_HEADS), lambda i: (layer, i, 0))],
        out_specs=[pl.BlockSpec((tb, hw), lambda i: (i, 0)), st4, st3, st2],
        out_shape=[jax.ShapeDtypeStruct((b, hw), F32), jax.ShapeDtypeStruct(c.shape[1:], F32),
                   jax.ShapeDtypeStruct(n.shape[1:], F32), jax.ShapeDtypeStruct(m.shape[1:], F32)],
        compiler_params=_params("parallel"),
        name="mlstm_step",
    )(p, p, p, p, kt, bif, gh, c, n, m)


def _mem_attn_kernel(q_ref, gq_ref, kv_ref, o_ref):
    tq = q_ref.shape[0]
    lane = lax.broadcasted_iota(jnp.int32, (tq, LANES), 1)
    lo = lane < HEAD_DIM
    for k in range(MEM_HEADS // 2):
        y = _half_rms(q_ref[:, k * LANES:(k + 1) * LANES], gq_ref[...], lo) * HEAD_DIM ** -0.5
        kp = kv_ref[:, k * LANES:(k + 1) * LANES].astype(BF16)
        vp = kv_ref[:, MEM_W + k * LANES:MEM_W + (k + 1) * LANES].astype(BF16)
        outs = []
        for half in range(2):
            qh = jnp.where(lo if half == 0 else ~lo, y, 0.0).astype(BF16)
            s = _dot_nt(qh, kp)
            p = jnp.exp(s - jnp.max(s, axis=1, keepdims=True))
            p = p / jnp.sum(p, axis=1, keepdims=True)
            outs.append(jnp.dot(p.astype(BF16), vp, preferred_element_type=F32))
        o_ref[:, k * LANES:(k + 1) * LANES] = jnp.where(lo, outs[0], outs[1])


def mem_attn_prompt(p, qcol, b, s, g_q, mem_kv, layer):
    tq = _pick_tile(s, (512, 256, 128))
    nq = s // tq
    m_tok = mem_kv.shape[2]
    return pl.pallas_call(
        _mem_attn_kernel,
        grid=(b, nq),
        in_specs=[pl.BlockSpec((tq, MEM_W), lambda bi, qi: (bi * nq + qi, qcol)),
                  pl.BlockSpec((1, LANES), lambda bi, qi: (0, 0)),
                  pl.BlockSpec((None, None, m_tok, 2 * MEM_W), lambda bi, qi: (layer, bi, 0, 0))],
        out_specs=pl.BlockSpec((tq, MEM_W), lambda bi, qi: (bi * nq + qi, 0)),
        out_shape=jax.ShapeDtypeStruct((b * s, MEM_W), F32),
        compiler_params=_params("parallel", "parallel"),
        name="mem_attn",
    )(p, _gain2(g_q), mem_kv)


def _mem_decode_kernel(q_ref, gq_ref, kv_ref, o_ref):
    tb = q_ref.shape[0]
    lane = lax.broadcasted_iota(jnp.int32, (tb, LANES), 1)
    lo = lane < HEAD_DIM
    y = jnp.concatenate([_half_rms(q_ref[:, k * LANES:(k + 1) * LANES], gq_ref[...], lo)
                         for k in range(MEM_HEADS // 2)], axis=1) * HEAD_DIM ** -0.5
    rowi = lax.broadcasted_iota(jnp.int32, (8, MEM_W), 0)
    own = (lax.broadcasted_iota(jnp.int32, (8, MEM_W), 1) // HEAD_DIM) == rowi
    for j in range(tb):
        qbd = jnp.where(own, y[j:j + 1, :], 0.0).astype(BF16)
        s = _dot_nt(qbd, kv_ref[j, :, :MEM_W].astype(BF16))
        p = jnp.exp(s - jnp.max(s, axis=1, keepdims=True))
        p = p / jnp.sum(p, axis=1, keepdims=True)
        o = jnp.dot(p.astype(BF16), kv_ref[j, :, MEM_W:].astype(BF16), preferred_element_type=F32)
        o_ref[j:j + 1, :] = jnp.sum(jnp.where(own, o, 0.0), axis=0, keepdims=True)


def mem_attn_decode(p, qcol, g_q, mem_kv, layer):
    b = p.shape[0]
    tb = STEP_TB
    m_tok = mem_kv.shape[2]
    return pl.pallas_call(
        _mem_decode_kernel,
        grid=(b // tb,),
        in_specs=[pl.BlockSpec((tb, MEM_W), lambda i: (i, qcol)),
                  pl.BlockSpec((1, LANES), lambda i: (0, 0)),
                  pl.BlockSpec((None, tb, m_tok, 2 * MEM_W), lambda i: (layer, i, 0, 0))],
        out_specs=pl.BlockSpec((tb, MEM_W), lambda i: (i, 0)),
        out_shape=jax.ShapeDtypeStruct((b, MEM_W), F32),
        compiler_params=_params("parallel"),
        name="mem_attn_decode",
    )(p, _gain2(g_q), mem_kv)


NSA_TQ = 128
MASK = -(2.0 ** 100)
SEL_LANE = HEAD_DIM
N_SEL_LANES = 32
CONST_LANE = SEL_LANE + N_SEL_LANES
KV_ROW = 2 * NSA_KV_HEADS * HEAD_DIM
SEG_PER_PAGE = PAGE_SIZE // CMP_STRIDE
SEG_ROW = CMP_STRIDE * KV_ROW
CMP_OUT = 2 * NSA_KV_HEADS * CMP_HIDDEN
CMP_SEQS = 2
QROWS = 16


def _rel_bucket_np(dist):
    d = np.maximum(dist, 0)
    ratio = np.maximum(d, REL_MAX_EXACT).astype(np.float64) / REL_MAX_EXACT
    large = REL_MAX_EXACT + (np.log(ratio) / math.log(REL_MAX_DIST / REL_MAX_EXACT)
                             * (REL_BUCKETS - REL_MAX_EXACT)).astype(np.int32)
    return np.where(d < REL_MAX_EXACT, d, np.minimum(large, REL_BUCKETS - 1)).astype(np.int32)


def _overlap_matrix(n_cmp, n_slc):
    c0 = np.arange(n_cmp)[:, None] * CMP_STRIDE
    s0 = np.arange(n_slc)[None, :] * SLC_BLOCK
    return np.clip(np.minimum(c0 + CMP_LEN, s0 + SLC_BLOCK) - np.maximum(c0, s0), 0, None) / CMP_STRIDE


def _kv_rows_kernel(x_ref, g_ref, w_ref, gs_ref, gw_ref, cmp_ref, slc_ref, win_ref, *aug_refs, seq_len):
    tm = x_ref.shape[0]
    xn = _rms_rows(x_ref[...], g_ref[...]).astype(BF16)
    kv = jnp.dot(xn, w_ref[...], preferred_element_type=F32)
    cmp_ref[...] = kv[:, :KV_ROW]
    lane = lax.broadcasted_iota(jnp.int32, (tm, LANES), 1)
    lo = lane < HEAD_DIM
    if aug_refs:
        ka_ref, va_ref = aug_refs
        pos = (pl.program_id(0) * tm + lax.broadcasted_iota(jnp.int32, (tm, LANES), 0)) % seq_len
        ones = (lane == CONST_LANE) | (lane == CONST_LANE + 1)
        onehot = (lane - SEL_LANE) == pos // SLC_BLOCK
    for t, (rows_ref, gk_ref) in enumerate(((slc_ref, gs_ref), (win_ref, gw_ref))):
        base = KV_ROW * (t + 1)
        kn = _half_rms(kv[:, base:base + LANES], gk_ref[...], lo)
        vv = kv[:, base + LANES:base + 2 * LANES]
        rows_ref[:, :LANES] = kn
        rows_ref[:, LANES:] = vv
        if aug_refs:
            extra = jnp.where((ones | onehot) if t == 0 else ones, 1.0, 0.0)
            rowt = lax.broadcasted_iota(jnp.int32, (VT_ROWS, tm), 0)
            for g in range(NSA_KV_HEADS):
                kg = kn if g == 0 else pltpu.roll(kn, HEAD_DIM, 1)
                vg = vv if g == 0 else pltpu.roll(vv, HEAD_DIM, 1)
                ka_ref[NSA_KV_HEADS * t + g] = jnp.where(lo, kg, extra).astype(BF16)
                vt = jnp.where(lo, vg, 0.0).T[:VT_ROWS, :]
                va_ref[NSA_KV_HEADS * t + g] = jnp.where(rowt == HEAD_DIM, 1.0, vt).astype(BF16)


def kv_rows(x, g_kv, w_kv, g_k_slc, g_k_win, seq_len, aug):
    m, d = x.shape
    n = w_kv.shape[1]
    tm = _pick_tile(m, (512, 256, 128))
    out_specs = [pl.BlockSpec((tm, KV_ROW), lambda i: (i, 0))] * 3
    out_shape = [jax.ShapeDtypeStruct((m, KV_ROW), F32)] * 3
    if aug:
        out_specs += [pl.BlockSpec((2 * NSA_KV_HEADS, tm, LANES), lambda i: (0, i, 0)),
                      pl.BlockSpec((2 * NSA_KV_HEADS, VT_ROWS, tm), lambda i: (0, 0, i))]
        out_shape += [jax.ShapeDtypeStruct((2 * NSA_KV_HEADS, m, LANES), BF16),
                      jax.ShapeDtypeStruct((2 * NSA_KV_HEADS, VT_ROWS, m), BF16)]
    return pl.pallas_call(
        partial(_kv_rows_kernel, seq_len=seq_len),
        grid=(m // tm,),
        in_specs=[pl.BlockSpec((tm, d), lambda i: (i, 0)),
                  pl.BlockSpec((1, d), lambda i: (0, 0)),
                  pl.BlockSpec((d, n), lambda i: (0, 0)),
                  pl.BlockSpec((1, LANES), lambda i: (0, 0)),
                  pl.BlockSpec((1, LANES), lambda i: (0, 0))],
        out_specs=out_specs,
        out_shape=out_shape,
        compiler_params=_params("parallel"),
        name="kv_rows",
    )(x, g_kv.reshape(1, d), w_kv, _gain2(g_k_slc), _gain2(g_k_win))


def _compress_kernel(*refs, n_x, paged):
    if paged:
        refs = refs[1:]
    x_refs = refs[:n_x]
    w1_ref, b1_ref, w2_ref, b2_ref, gk_ref, o_ref = refs[n_x:]
    n_seq, seg_per_seq = o_ref.shape[0], o_ref.shape[1]
    n_seg = n_seq * seg_per_seq
    r_n = CMP_LEN // CMP_STRIDE
    lane = lax.broadcasted_iota(jnp.int32, (n_seg, LANES), 1)
    for c in range(2):
        acc = [jnp.zeros((n_seg, CMP_OUT // 2), F32) for _ in range(r_n)]
        refs_c = x_refs[c::2]
        for u in range(CMP_STRIDE):
            pieces = [r[pl.ds(u, r.shape[0] // CMP_STRIDE, stride=CMP_STRIDE), :] for r in refs_c]
            xu = (pieces[0] if len(pieces) == 1 else jnp.concatenate(pieces, axis=0)).astype(BF16)
            for r in range(r_n):
                acc[r] = acc[r] + jnp.dot(xu, w1_ref[r, u, c], preferred_element_type=F32)
        pre = acc[0] + pltpu.roll(acc[1], n_seg - 1, 0) + b1_ref[c]
        out = jnp.dot(jax.nn.gelu(pre).astype(BF16), w2_ref[c], preferred_element_type=F32) + b2_ref[c]
        if c == 0:
            out = _half_rms(out, gk_ref[...], lane < HEAD_DIM)
        for s in range(n_seq):
            o_ref[s, :, c * LANES:(c + 1) * LANES] = out[s * seg_per_seq:(s + 1) * seg_per_seq].astype(BF16)


def compress_weights(w1, b1, w2, b2, pos_enc, g_k_cmp):
    r_n = CMP_LEN // CMP_STRIDE
    eye = jnp.eye(NSA_KV_HEADS, dtype=F32)
    w1r = w1.reshape(2, r_n, CMP_STRIDE, HEAD_DIM, CMP_HIDDEN)
    w1b = jnp.einsum('crudh,gy->rucgdyh', w1r, eye).reshape(r_n, CMP_STRIDE, 2, LANES, CMP_OUT // 2).astype(BF16)
    pe = jnp.einsum('pcd,cpdh->ch', pos_enc, w1)
    b1b = jnp.broadcast_to((b1 + pe)[:, None, :], (2, NSA_KV_HEADS, CMP_HIDDEN)).reshape(2, 1, CMP_OUT // 2)
    w2b = jnp.einsum('chd,gy->cghyd', w2, eye).reshape(2, CMP_OUT // 2, LANES).astype(BF16)
    b2b = jnp.broadcast_to(b2[:, None, :], (2, NSA_KV_HEADS, HEAD_DIM)).reshape(2, 1, LANES)
    return w1b, b1b, w2b, b2b, _gain2(g_k_cmp)


def compress_rows(x, cw, page_table=None):
    w1b, b1b, w2b, b2b, gk2 = cw
    paged = page_table is not None
    ns = CMP_SEQS
    if paged:
        n_seq, n_pages = page_table.shape
        x_specs = [pl.BlockSpec((None, PAGE_SIZE, LANES),
                                partial(lambda s, k, c, b, pt: (pt[ns * b + s, k], 0, c), s, k, c))
                   for s in range(ns) for k in range(n_pages) for c in range(2)]
        n_seg = n_pages * SEG_PER_PAGE
        cmap = lambda *idx: lambda b, pt: idx
    else:
        n_seq, n_seg = x.shape[0], x.shape[1] // CMP_STRIDE
        x_specs = [pl.BlockSpec((None, x.shape[1], LANES), partial(lambda s, c, b: (ns * b + s, 0, c), s, c))
                   for s in range(ns) for c in range(2)]
        cmap = lambda *idx: lambda b: idx
    assert n_seq % ns == 0
    n_x = len(x_specs)
    w_specs = [pl.BlockSpec(w1b.shape, cmap(0, 0, 0, 0, 0)), pl.BlockSpec(b1b.shape, cmap(0, 0, 0)),
               pl.BlockSpec(w2b.shape, cmap(0, 0, 0)), pl.BlockSpec(b2b.shape, cmap(0, 0, 0)),
               pl.BlockSpec(gk2.shape, cmap(0, 0))]
    out_spec = pl.BlockSpec((ns, n_seg, KV_ROW), (lambda b, pt: (b, 0, 0)) if paged else (lambda b: (b, 0, 0)))
    out_shape = jax.ShapeDtypeStruct((n_seq, n_seg, KV_ROW), BF16)
    kern = partial(_compress_kernel, n_x=n_x, paged=paged)
    if paged:
        gs = pltpu.PrefetchScalarGridSpec(num_scalar_prefetch=1, grid=(n_seq // ns,), in_specs=x_specs + w_specs,
                                          out_specs=out_spec)
        return pl.pallas_call(kern, grid_spec=gs, out_shape=out_shape, compiler_params=_params("parallel"),
                              name="compress_paged")(page_table, *([x] * n_x), w1b, b1b, w2b, b2b, gk2)
    return pl.pallas_call(kern, grid=(n_seq // ns,), in_specs=x_specs + w_specs, out_specs=out_spec,
                          out_shape=out_shape, compiler_params=_params("parallel"),
                          name="compress")(*([x] * n_x), w1b, b1b, w2b, b2b, gk2)


VT_ROWS = 80
CMP_BAND = 16


LOG2E = 1.0 / math.log(2.0)
FAR_BLOCKS = 4
NEAR_BLOCKS = 5
FULL_MASK_TILE = 4


def _nsa_prompt_kernel(rel_ref, q_ref, gate_ref, gq_ref, bg_ref, ck_ref, cvt_ref, band_ref, far_ref, mt_ref,
                        ka_ref, vat_ref, dl_ref, o_ref, qa_ref, m_ref, acc_ref, cb_ref, *, n_cmp, n_tiles):
    tq = NSA_TQ
    cols = NSA_GROUP * tq
    qi = pl.program_id(1)
    lane = lax.broadcasted_iota(jnp.int32, (tq, LANES), 1)
    lo = lane < HEAD_DIM
    nidx = lax.broadcasted_iota(jnp.int32, (LANES, cols), 0)
    r_cb = lax.broadcasted_iota(jnp.int32, (LANES + CMP_BAND, cols), 0)
    jidx = lax.broadcasted_iota(jnp.int32, (N_SEL_LANES, tq), 0)
    tpos = qi * tq + lax.broadcasted_iota(jnp.int32, (N_SEL_LANES, tq), 1)
    cur = tpos // SLC_BLOCK
    forced = (jidx == 0) | (jidx == cur) | (jidx == cur - 1)
    half_band = CMP_BAND // 2

    def gate_rows(g):
        sig_t = jax.nn.sigmoid(gate_ref[:, g * LANES:(g + 1) * LANES] + bg_ref[:, g * LANES:(g + 1) * LANES]).T
        return [jnp.concatenate([sig_t[br * NSA_GROUP + z:br * NSA_GROUP + z + 1, :] for z in range(NSA_GROUP)], axis=1)
                for br in range(3)]

    def prologue(g):
        qn = []
        for k in range(NSA_GROUP // 2):
            kk = g * (NSA_GROUP // 2) + k
            y = _half_rms(q_ref[:, kk * LANES:(kk + 1) * LANES], gq_ref[...], lo) * (HEAD_DIM ** -0.5 * LOG2E)
            qn.append(jnp.where(lo, y, 0.0))
            qn.append(jnp.where(lo, pltpu.roll(y, HEAD_DIM, 1), 0.0))
        qc = jnp.concatenate(qn, axis=0).astype(BF16)

        cb_ref[g] = jnp.where(r_cb < half_band * qi, far_ref[g], MASK)
        cb_ref[g, pl.ds(pl.multiple_of(half_band * qi, half_band), CMP_BAND), :] = band_ref[g]
        cbias = jnp.where(nidx >= n_cmp, MASK, cb_ref[g, half_band:half_band + LANES, :])
        s = _dot_nt(ck_ref[g], qc) + cbias
        m = jnp.maximum(jnp.max(s, axis=0, keepdims=True), NEG)
        p = jnp.exp2(s - m)
        l = jnp.sum(p, axis=0, keepdims=True)
        pn = p * jnp.where(l > 0.0, 1.0 / l, 0.0)
        o_cmp = jnp.dot(cvt_ref[g], pn.astype(BF16), preferred_element_type=F32)

        psum = pn[:, 0:tq]
        for z in range(1, NSA_GROUP):
            psum = psum + pn[:, z * tq:(z + 1) * tq]
        p_hi = psum.astype(BF16)
        p_lo = (psum - p_hi.astype(F32)).astype(BF16)
        imp = (jnp.dot(mt_ref[...], p_hi, preferred_element_type=F32)
               + jnp.dot(mt_ref[...], p_lo, preferred_element_type=F32))
        score = jnp.where(forced, FORCED, jnp.where(jidx <= cur, imp, -1.0))
        rank = jnp.zeros((N_SEL_LANES, tq), F32)
        for i in range(N_SEL_LANES):
            ri = score[i:i + 1, :]
            beats = (ri > score) | ((ri == score) & (jidx > i))
            rank = rank + jnp.where(beats, 1.0, 0.0)
        selb = jnp.where((rank < N_SELECT) & (jidx <= cur), 0.0, MASK)
        selb = jnp.concatenate([selb, jnp.zeros((LANES - N_SEL_LANES, tq), F32)], axis=0)
        sel_cols = pltpu.roll(selb.T, SEL_LANE, 1)

        for z in range(NSA_GROUP):
            c = jnp.full((tq, LANES), rel_ref[REL_BUCKETS - 1, g * NSA_GROUP + z], F32)
            c_hi = c.astype(BF16).astype(F32)
            extra = jnp.where(lane == CONST_LANE, c_hi, jnp.where(lane == CONST_LANE + 1, c - c_hi, sel_cols))
            qa_ref[g, z * tq:(z + 1) * tq, :] = jnp.where(lo, qn[z], extra).astype(BF16)
        return o_cmp

    o_cmp = [prologue(g) for g in range(NSA_KV_HEADS)]

    n_far = jnp.maximum(qi - 1, 0) // FAR_BLOCKS
    far_rows = FAR_BLOCKS * tq
    m_ref[...] = jnp.full(m_ref.shape, NEG, F32)
    acc_ref[...] = jnp.zeros(acc_ref.shape, F32)

    def far_body(c, carry):
        start = pl.multiple_of(c * far_rows, far_rows)
        for g in range(NSA_KV_HEADS):
            s = _dot_nt(ka_ref[g, pl.ds(start, far_rows), :], qa_ref[g])
            m_prev = m_ref[g]
            m_next = jnp.maximum(m_prev, jnp.max(s, axis=0, keepdims=True))
            p = jnp.exp2(s - m_next).astype(BF16)
            acc_ref[g] = (jnp.exp2(m_prev - m_next) * acc_ref[g]
                          + jnp.dot(vat_ref[g, :, pl.ds(start, far_rows)], p, preferred_element_type=F32))
            m_ref[g] = m_next
        return carry

    lax.fori_loop(0, n_far, far_body, 0)

    def near_stage(idx, g, start_blk, tile_of, m_prev, acc_prev):
        start = pl.multiple_of(start_blk * tq, tq)
        s = _dot_nt(ka_ref[idx, pl.ds(start, NEAR_BLOCKS * tq), :], qa_ref[g])
        s = s + jnp.concatenate([dl_ref[tile_of(start_blk + i), g] for i in range(NEAR_BLOCKS)], axis=0)
        m_next = jnp.max(s, axis=0, keepdims=True)
        if m_prev is not None:
            m_next = jnp.maximum(m_prev, m_next)
        acc = jnp.dot(vat_ref[idx, :, pl.ds(start, NEAR_BLOCKS * tq)], jnp.exp2(s - m_next).astype(BF16),
                      preferred_element_type=F32)
        if acc_prev is not None:
            acc = acc + jnp.exp2(m_prev - m_next) * acc_prev
        return acc[:HEAD_DIM, :] / acc[HEAD_DIM:HEAD_DIM + 1, :]

    far_cov = FAR_BLOCKS * n_far
    slc_start = jnp.minimum(far_cov, n_tiles - NEAR_BLOCKS)
    win_start = jnp.maximum(qi - WINDOW // tq, 0)

    def slc_tile(j):
        d = qi - j
        return jnp.where((d < 0) | (j < far_cov), FULL_MASK_TILE, jnp.where(d == 0, 2, jnp.where(d == 1, 1, 0)))

    def win_tile(j):
        d = qi - j
        return jnp.where(d < 0, FULL_MASK_TILE,
                         jnp.where(d == 0, 2, jnp.where(d == 1, 1, jnp.where(d == WINDOW // tq, 3, 0))))

    outs = []
    for g in range(NSA_KV_HEADS):
        gr = gate_rows(g)
        o_slc = near_stage(g, g, slc_start, slc_tile, m_ref[g], acc_ref[g])
        o_win = near_stage(NSA_KV_HEADS + g, g, win_start, win_tile, None, None)
        outs.append(gr[0] * o_cmp[g] + gr[1] * o_slc + gr[2] * o_win)

    for g in range(NSA_KV_HEADS):
        for k in range(NSA_GROUP // 2):
            kk = g * (NSA_GROUP // 2) + k
            o_ref[:, kk * LANES:(kk + 1) * LANES] = jnp.concatenate(
                [outs[g][:, (2 * k) * tq:(2 * k + 1) * tq], outs[g][:, (2 * k + 1) * tq:(2 * k + 2) * tq]], axis=0).T


def nsa_prompt(p, b, s, g_q, b_gate, rel_table, ck, cvt, ka, vat, tables):
    tq = NSA_TQ
    nq = s // tq
    cols = NSA_GROUP * tq
    n_cmp = (s - CMP_LEN) // CMP_STRIDE + 1
    n_slc = -(-s // SLC_BLOCK)
    assert s % tq == 0 and nq >= NEAR_BLOCKS and n_cmp <= LANES and n_slc <= N_SEL_LANES and tq == 8 * CMP_STRIDE
    tiles, band, far_row = tables
    tiles = jnp.concatenate([tiles, jnp.full((1,) + tiles.shape[1:], MASK, F32)], axis=0)
    mt = np.zeros((N_SEL_LANES, LANES), np.float32)
    mt[:n_slc, :n_cmp] = _overlap_matrix(n_cmp, n_slc).T
    bg = _gate_bias_slabs(b_gate).reshape(1, NSA_KV_HEADS * LANES)
    gcol = (MIX_W + MEM_W) // (NSA_KV_HEADS * LANES)
    ka4 = ka.reshape(2 * NSA_KV_HEADS, b, s, LANES)
    c2 = lambda bi, qi: (0, 0)
    c3 = lambda bi, qi: (0, 0, 0)
    return pl.pallas_call(
        partial(_nsa_prompt_kernel, n_cmp=n_cmp, n_tiles=nq),
        grid=(b, nq),
        in_specs=[pl.BlockSpec(memory_space=pltpu.SMEM),
                  pl.BlockSpec((tq, MIX_W), lambda bi, qi: (bi * nq + qi, 0)),
                  pl.BlockSpec((tq, NSA_KV_HEADS * LANES), lambda bi, qi: (bi * nq + qi, gcol)),
                  pl.BlockSpec((1, LANES), c2),
                  pl.BlockSpec((1, NSA_KV_HEADS * LANES), c2),
                  pl.BlockSpec((None, NSA_KV_HEADS, LANES, LANES), lambda bi, qi: (bi, 0, 0, 0)),
                  pl.BlockSpec((None, NSA_KV_HEADS, HEAD_DIM, LANES), lambda bi, qi: (bi, 0, 0, 0)),
                  pl.BlockSpec((NSA_KV_HEADS, CMP_BAND, cols), c3),
                  pl.BlockSpec((NSA_KV_HEADS, 1, cols), c3),
                  pl.BlockSpec((N_SEL_LANES, LANES), c2),
                  pl.BlockSpec((2 * NSA_KV_HEADS, None, s, LANES), lambda bi, qi: (0, bi, 0, 0)),
                  pl.BlockSpec((2 * NSA_KV_HEADS, VT_ROWS, s), lambda bi, qi: (0, 0, bi)),
                  pl.BlockSpec((FULL_MASK_TILE + 1, NSA_KV_HEADS, tq, cols), lambda bi, qi: (0, 0, 0, 0))],
        out_specs=pl.BlockSpec((tq, MIX_W), lambda bi, qi: (bi * nq + qi, 0)),
        out_shape=jax.ShapeDtypeStruct((b * s, MIX_W), F32),
        scratch_shapes=[pltpu.VMEM((NSA_KV_HEADS, cols, LANES), BF16), pltpu.VMEM((NSA_KV_HEADS, 1, cols), F32),
                        pltpu.VMEM((NSA_KV_HEADS, VT_ROWS, cols), F32),
                        pltpu.VMEM((NSA_KV_HEADS, LANES + CMP_BAND, cols), F32)],
        compiler_params=_params("parallel", "arbitrary"),
        name="nsa_prompt",
    )(rel_table.astype(F32), p, p, _gain2(g_q), bg, ck, cvt, band, far_row, jnp.asarray(mt, BF16), ka4, vat, tiles)


def nsa_bias_tables(rel_table):
    tq = NSA_TQ
    table = rel_table.astype(F32)
    far = table[REL_BUCKETS - 1]
    r = np.arange(tq)[None, :]
    c = np.arange(tq)[:, None]

    def by_group(t):
        k = t.shape[0]
        return jnp.transpose(t.reshape(k, tq, NSA_KV_HEADS, NSA_GROUP), (2, 0, 3, 1)).reshape(NSA_KV_HEADS, k, -1)

    spread = (jnp.arange(2 * tq - 1)[:, None, None]
              == (tq - 1 - jnp.arange(tq)[None, :, None] + jnp.arange(tq)[None, None, :])).astype(F32)

    def tile(off, masked):
        d = off + np.arange(-(tq - 1), tq)
        v = jnp.where(jnp.asarray(masked(d))[:, None], MASK, table[_rel_bucket_np(d)] - far)
        return by_group(jnp.einsum('dh,dcr->crh', v, spread, precision=lax.Precision.HIGHEST))

    zero = jnp.zeros((NSA_KV_HEADS, tq, NSA_GROUP * tq), F32)
    tiles = jnp.stack([zero, tile(tq, lambda d: d < 0), tile(0, lambda d: d < 0),
                       tile(WINDOW, lambda d: d >= WINDOW)])
    n_rel = np.arange(CMP_BAND)[:, None] - CMP_BAND // 2
    dist = r - CMP_STRIDE * n_rel - (CMP_LEN - 1)
    band = by_group(jnp.where(jnp.asarray(dist < 0)[..., None], MASK, table[_rel_bucket_np(dist)]))
    far_row = by_group(jnp.broadcast_to(far, (1, tq, NSA_HEADS)))
    return tiles, band, far_row


def _gate_bias_slabs(b_gate):
    bg = b_gate.astype(F32).reshape(3, NSA_KV_HEADS, NSA_GROUP).transpose(1, 0, 2).reshape(NSA_KV_HEADS, 3 * NSA_GROUP)
    return _pad_cols(bg, LANES)


def split_cmp_ctx(ctx):
    b, n, _ = ctx.shape
    parts = ctx.reshape(b, n, 2, NSA_KV_HEADS, HEAD_DIM).transpose(2, 0, 3, 1, 4)
    parts = jnp.pad(parts, ((0, 0), (0, 0), (0, 0), (0, LANES - n), (0, 0)))
    ck = jnp.pad(parts[0], ((0, 0), (0, 0), (0, 0), (0, LANES - HEAD_DIM)))
    return ck, jnp.swapaxes(parts[1], -1, -2)


def _softmax_parts(s, s_new):
    m = jnp.maximum(jnp.max(s, axis=1, keepdims=True), s_new)
    p = jnp.exp(s - m)
    p_new = jnp.exp(s_new - m)
    inv = 1.0 / (jnp.sum(p, axis=1, keepdims=True) + p_new)
    return p * inv, p_new * inv


DECODE_SEQS = 2


def _nsa_decode_kernel(*refs, n_pages):
    (pt_ref, p_ref, gq_ref, bg_ref, cmp_ref, cb_ref, mt_ref) = refs[:7]
    n_in = DECODE_SEQS * n_pages
    page_refs = refs[7:7 + n_in]
    (snew_ref, sb_ref, e_ref, win_ref, wnew_ref, wb_ref, b0_ref, o_ref) = refs[7 + n_in:]
    del pt_ref
    for s in range(DECODE_SEQS):
        _nsa_decode_seq(p_ref.at[s], gq_ref, bg_ref, cmp_ref.at[s], cb_ref, mt_ref,
                        page_refs[s * n_pages:(s + 1) * n_pages], snew_ref.at[s], sb_ref, e_ref, win_ref.at[s],
                        wnew_ref.at[s], wb_ref, b0_ref, o_ref.at[s])


def _nsa_decode_seq(p_ref, gq_ref, bg_ref, cmp_ref, cb_ref, mt_ref, page_refs, snew_ref, sb_ref, e_ref, win_ref,
                    wnew_ref, wb_ref, b0_ref, o_ref):
    n_pages = len(page_refs)
    n_keys = n_pages * PAGE_SIZE
    lo1 = lax.broadcasted_iota(jnp.int32, (1, LANES), 1) < HEAD_DIM
    rowi = lax.broadcasted_iota(jnp.int32, (QROWS, LANES), 0)
    lanei = lax.broadcasted_iota(jnp.int32, (QROWS, LANES), 1)
    row_g = rowi // 8
    own = (lanei // HEAD_DIM) == row_g

    qbd = jnp.zeros((QROWS, LANES), F32)
    for k in range(NSA_HEADS // 2):
        y = _half_rms(p_ref[:, k * LANES:(k + 1) * LANES], gq_ref[...], lo1) * HEAD_DIM ** -0.5
        yr = pltpu.roll(y, HEAD_DIM, 1)
        for half in range(2):
            g, z = divmod(2 * k + half, NSA_GROUP)
            qbd = jnp.where(rowi == 8 * g + z, y if half == g else yr, qbd)
    qbd = jnp.where(own, qbd, 0.0)
    qb = qbd.astype(BF16)

    s = _dot_nt(qb, cmp_ref[:, :LANES]) + cb_ref[...]
    m = jnp.maximum(jnp.max(s, axis=1, keepdims=True), NEG)
    p = jnp.exp(s - m)
    l = jnp.sum(p, axis=1, keepdims=True)
    pn = p * jnp.where(l > 0.0, 1.0 / l, 0.0)
    o_cmp = jnp.dot(pn.astype(BF16), cmp_ref[:, LANES:], preferred_element_type=F32)

    valid_row = (rowi % 8) < NSA_GROUP
    ps0 = jnp.sum(jnp.where(valid_row & (row_g == 0), pn, 0.0), axis=0, keepdims=True)
    ps1 = jnp.sum(jnp.where(valid_row & (row_g == 1), pn, 0.0), axis=0, keepdims=True)
    r8 = lax.broadcasted_iota(jnp.int32, (8, LANES), 0)
    j8 = lax.broadcasted_iota(jnp.int32, (8, LANES), 1)
    psum = jnp.where(r8 == 0, ps0, jnp.where(r8 == 1, ps1, 0.0))
    p_hi = psum.astype(BF16)
    p_lo = (psum - p_hi.astype(F32)).astype(BF16)
    imp = (jnp.dot(p_hi, mt_ref[...], preferred_element_type=F32)
           + jnp.dot(p_lo, mt_ref[...], preferred_element_type=F32))
    cur = n_keys // SLC_BLOCK
    forced = (j8 == 0) | (j8 == cur) | (j8 == cur - 1)
    score = jnp.where(forced, FORCED, jnp.where(j8 <= cur, imp, -1.0))
    rank = jnp.zeros((8, LANES), F32)
    for i in range(cur + 1):
        ri = score[:, i:i + 1]
        beats = (ri > score) | ((ri == score) & (j8 > i))
        rank = rank + jnp.where(beats, 1.0, 0.0)
    selb = jnp.where(rank < N_SELECT, 0.0, MASK)
    sel16 = jnp.where(row_g == 0, selb[0:1, :], selb[1:2, :]).astype(BF16)
    blockmask = jnp.dot(sel16, e_ref[...], preferred_element_type=F32)

    s = jnp.concatenate([_dot_nt(qb, r[:, :LANES].astype(BF16)) for r in page_refs], axis=1)
    s = s + sb_ref[...] + blockmask
    s_new = jnp.sum(qbd * snew_ref[:, :LANES], axis=1, keepdims=True) + b0_ref[...]
    pp, p_new = _softmax_parts(s, s_new)
    o_slc = p_new * snew_ref[:, LANES:]
    for k, r in enumerate(page_refs):
        o_slc = o_slc + jnp.dot(pp[:, k * PAGE_SIZE:(k + 1) * PAGE_SIZE].astype(BF16), r[:, LANES:].astype(BF16),
                                preferred_element_type=F32)

    n_wt = win_ref.shape[0] // LANES
    s = jnp.concatenate([_dot_nt(qb, win_ref[k * LANES:(k + 1) * LANES, :LANES].astype(BF16)) for k in range(n_wt)],
                        axis=1) + wb_ref[...]
    s_new = jnp.sum(qbd * wnew_ref[:, :LANES], axis=1, keepdims=True) + b0_ref[...]
    pp, p_new = _softmax_parts(s, s_new)
    o_win = p_new * wnew_ref[:, LANES:]
    for k in range(n_wt):
        o_win = o_win + jnp.dot(pp[:, k * LANES:(k + 1) * LANES].astype(BF16),
                                win_ref[k * LANES:(k + 1) * LANES, LANES:].astype(BF16), preferred_element_type=F32)

    gbase = MIX_W + MEM_W
    sig = jax.nn.sigmoid(p_ref[:, gbase:gbase + NSA_KV_HEADS * LANES] + bg_ref[...])
    gate = [jnp.zeros((QROWS, 1), F32) for _ in range(3)]
    rcol = lax.broadcasted_iota(jnp.int32, (QROWS, 1), 0)
    for br in range(3):
        for g in range(NSA_KV_HEADS):
            for z in range(NSA_GROUP):
                c = g * LANES + br * NSA_GROUP + z
                gate[br] = jnp.where(rcol == 8 * g + z, sig[:, c:c + 1], gate[br])
    comb = jnp.where(own, gate[0] * o_cmp + gate[1] * o_slc + gate[2] * o_win, 0.0)
    comb_r = pltpu.roll(comb, HEAD_DIM, 1)
    for k in range(NSA_HEADS // 2):
        acc = jnp.zeros((1, LANES), F32)
        for half in range(2):
            g, z = divmod(2 * k + half, NSA_GROUP)
            src = comb if half == g else comb_r
            acc = acc + src[8 * g + z:8 * g + z + 1, :]
        o_ref[:, k * LANES:(k + 1) * LANES] = acc


def nsa_decode_bias(rel_table, past, wb):
    table = rel_table.astype(F32)

    def rows16(t):
        z = jnp.zeros((2, t.shape[1]), F32)
        return jnp.concatenate([t[:NSA_GROUP], z, t[NSA_GROUP:], z], axis=0)

    n_cmp = (past + 1 - CMP_LEN) // CMP_STRIDE + 1
    c_end = np.arange(LANES) * CMP_STRIDE + CMP_LEN - 1
    dist = past - c_end
    cb = jnp.where(jnp.asarray((dist < 0) | (np.arange(LANES) >= n_cmp))[None], MASK, table[_rel_bucket_np(dist)].T)
    sb = table[_rel_bucket_np(past - np.arange(past))].T
    wpos = past - wb + np.arange(wb)
    wbias = jnp.where(jnp.asarray(past - wpos >= WINDOW)[None], MASK, table[_rel_bucket_np(past - wpos)].T)
    b0 = table[0][:, None]
    return rows16(cb), rows16(sb), rows16(wbias), rows16(b0)


def nsa_decode(p, g_q, b_gate, cmp_ctx, page_table, cache_slc, slc_new, cache_win, win_new, biases):
    b, n_pages = page_table.shape
    past = n_pages * PAGE_SIZE
    wb = cache_win.shape[1]
    cb, sb, wbias, b0 = biases
    n_cmp = (past + 1 - CMP_LEN) // CMP_STRIDE + 1
    n_slc = -(-(past + 1) // SLC_BLOCK)
    assert n_slc <= LANES and n_cmp <= LANES and wb % LANES == 0
    mt = np.zeros((LANES, LANES), np.float32)
    mt[:n_cmp, :n_slc] = _overlap_matrix(n_cmp, n_slc)
    e = (np.arange(past)[None, :] // SLC_BLOCK == np.arange(LANES)[:, None]).astype(np.float32)
    bg = _gate_bias_slabs(b_gate).reshape(1, NSA_KV_HEADS * LANES)
    ns = DECODE_SEQS
    assert b % ns == 0
    c2 = lambda bi, pt: (0, 0)
    seq3 = lambda bi, pt: (bi, 0, 0)
    in_specs = ([pl.BlockSpec((ns, 1, p.shape[1]), seq3),
                 pl.BlockSpec((1, LANES), c2),
                 pl.BlockSpec((1, NSA_KV_HEADS * LANES), c2),
                 pl.BlockSpec((ns, LANES, KV_ROW), seq3),
                 pl.BlockSpec((QROWS, LANES), c2),
                 pl.BlockSpec((LANES, LANES), c2)]
                + [pl.BlockSpec((None, PAGE_SIZE, KV_ROW), partial(lambda s, k, bi, pt: (pt[ns * bi + s, k], 0, 0), s, k))
                   for s in range(ns) for k in range(n_pages)]
                + [pl.BlockSpec((ns, 1, KV_ROW), seq3),
                   pl.BlockSpec((QROWS, past), c2),
                   pl.BlockSpec((LANES, past), c2),
                   pl.BlockSpec((ns, wb, KV_ROW), seq3),
                   pl.BlockSpec((ns, 1, KV_ROW), seq3),
                   pl.BlockSpec((QROWS, wb), c2),
                   pl.BlockSpec((QROWS, 1), c2)])
    gs = pltpu.PrefetchScalarGridSpec(num_scalar_prefetch=1, grid=(b // ns,), in_specs=in_specs,
                                      out_specs=pl.BlockSpec((ns, 1, MIX_W), seq3))
    out = pl.pallas_call(
        partial(_nsa_decode_kernel, n_pages=n_pages),
        grid_spec=gs,
        out_shape=jax.ShapeDtypeStruct((b, 1, MIX_W), F32),
        compiler_params=_params("parallel"),
        name="nsa_decode",
    )(page_table, p.reshape(b, 1, -1), _gain2(g_q), bg, cmp_ctx, cb, jnp.asarray(mt, BF16),
      *([cache_slc] * (ns * n_pages)), slc_new.reshape(b, 1, KV_ROW), sb, jnp.asarray(e, BF16),
      cache_win, win_new.reshape(b, 1, KV_ROW), wbias, b0)
    return out.reshape(b, MIX_W)


GATE_PAD = LANES


def _pad_heads_cols(w):
    d = w.shape[0]
    return jnp.pad(w.reshape(d, MLSTM_HEADS, MLSTM_DH), ((0, 0), (0, 0), (0, MLSTM_PAD - MLSTM_DH))).reshape(d, MLSTM_HW)


def prep_weights(w_in_a, w_in_b, w_kv, w_mem_kv, w_out, w_gu, w_down):
    w = {}
    n_gate_a = 2 * MLSTM_HEADS
    n_gate_b = 3 * NSA_HEADS
    w['in_a'] = []
    for l in range(N_A_LAYERS):
        wa = w_in_a[l]
        qkvo = [_pad_heads_cols(wa[:, j * MIX_W:(j + 1) * MIX_W]) for j in range(4)]
        w['in_a'].append(jnp.concatenate(qkvo + [wa[:, 4 * MIX_W + n_gate_a:],
                                                 _pad_cols(wa[:, 4 * MIX_W:4 * MIX_W + n_gate_a], GATE_PAD)],
                                         axis=1).astype(BF16))
    w['in_b'] = []
    for j in range(N_B_LAYERS):
        wg = w_in_b[j][:, MIX_W:MIX_W + n_gate_b].reshape(D_MODEL, 3, NSA_KV_HEADS, NSA_GROUP)
        slabs = [_pad_cols(wg[:, :, g].reshape(D_MODEL, 3 * NSA_GROUP), GATE_PAD) for g in range(NSA_KV_HEADS)]
        w['in_b'].append(jnp.concatenate([w_in_b[j][:, :MIX_W], w_in_b[j][:, MIX_W + n_gate_b:]] + slabs,
                                         axis=1).astype(BF16))
    w['out_mix'] = []
    for l in range(DEPTH):
        wm = w_out[l][:MIX_W]
        if l < N_A_LAYERS:
            wm = jnp.pad(wm.reshape(MLSTM_HEADS, MLSTM_DH, D_MODEL),
                         ((0, 0), (0, MLSTM_PAD - MLSTM_DH), (0, 0))).reshape(MLSTM_HW, D_MODEL)
        w['out_mix'].append(wm.astype(BF16))
    w['out_mem'] = w_out[:, MIX_W:].astype(BF16)
    w['kv'] = w_kv.astype(BF16)
    w['mem_kv'] = w_mem_kv.astype(BF16)
    w['gu'] = w_gu.astype(BF16)
    w['down'] = w_down.astype(BF16)
    return w


def kernel(x_prompt, x_sample, state_mlstm_c, state_mlstm_n, state_mlstm_m, cache_mem_kv,
           cache_cmp_kv, cache_slc_kv, cache_win_kv, page_table, mem_prompt,
           g_mix, w_in_a, b_if, g_hnorm, w_in_b, g_q, b_gate, rel_table, g_kv, w_kv,
           g_k_slc, g_k_win, g_k_cmp, cmp_pos, cmp_w1, cmp_b1, cmp_w2, cmp_b2,
           w_mem_kv, g_mem_k, g_mem_q, w_out, g_ffn, w_gu, w_down):
    w = prep_weights(w_in_a, w_in_b, w_kv, w_mem_kv, w_out, w_gu, w_down)
    cw = compress_weights(cmp_w1, cmp_b1, cmp_w2, cmp_b2, cmp_pos, g_k_cmp)
    qcol_a = 4 * MLSTM_HW // MEM_W
    qcol_b = MIX_W // MEM_W

    def finish_layer(l, x, mix, mem_o):
        return layer_tail(mix, w['out_mix'][l], mem_o, w['out_mem'][l], x, g_ffn[l], w['gu'][l], w['down'][l])

    b_p, s_p, _ = x_prompt.shape
    m_tok = mem_prompt.shape[1]
    mem_kv_p = mem_kv_rows(mem_prompt.reshape(b_p * m_tok, D_MODEL), w['mem_kv'], g_mem_k)
    mem_kv_p = mem_kv_p.reshape(DEPTH, b_p, m_tok, 2 * MEM_W)
    rel2 = rel_table.astype(F32) * LOG2E
    tables = nsa_bias_tables(rel2)
    c0, m0 = pack_mlstm_state(jnp.zeros((b_p, MLSTM_HEADS, MLSTM_DH, MLSTM_DH), F32),
                              jnp.zeros((b_p, MLSTM_HEADS, MLSTM_DH), F32),
                              jnp.full((b_p, MLSTM_HEADS), NEG, F32))
    x = x_prompt.reshape(b_p * s_p, D_MODEL)
    st_p = []
    for l in range(N_A_LAYERS):
        p = norm_matmul(x, g_mix[l], w['in_a'][l])
        mix, cn, mm = mlstm_prompt(p, b_p, s_p, b_if[l], g_hnorm[l], c0, m0)
        st_p.append(unpack_mlstm_state(cn, mm))
        mem_o = mem_attn_prompt(p, qcol_a, b_p, s_p, g_mem_q[l], mem_kv_p, l)
        x = finish_layer(l, x, mix, mem_o)
    cmp_p, slc_p, win_p, ka, vat = kv_rows(x, g_kv, w['kv'], g_k_slc, g_k_win, s_p, aug=True)
    ck, cvt = split_cmp_ctx(compress_rows(cmp_p.reshape(b_p, s_p, KV_ROW), cw))
    for j in range(N_B_LAYERS):
        l = N_A_LAYERS + j
        p = norm_matmul(x, g_mix[l], w['in_b'][j])
        mix = nsa_prompt(p, b_p, s_p, g_q[j], b_gate[j], rel2, ck, cvt, ka, vat, tables)
        mem_o = mem_attn_prompt(p, qcol_b, b_p, s_p, g_mem_q[l], mem_kv_p, l)
        x = finish_layer(l, x, mix, mem_o)
    y_p = x.reshape(b_p, s_p, D_MODEL)

    b_s, s_s, _ = x_sample.shape
    assert s_s == 1
    n_pages = page_table.shape[1]
    past = n_pages * PAGE_SIZE
    wb = cache_win_kv.shape[1]
    n_pool = cache_cmp_kv.shape[0]
    x = x_sample.reshape(b_s, D_MODEL)
    cache_mem = cache_mem_kv.reshape(DEPTH, b_s, -1, 2 * MEM_W)
    st_s = []
    for l in range(N_A_LAYERS):
        p = norm_matmul(x, g_mix[l], w['in_a'][l])
        mix, c_new, n_new, m_new = mlstm_step(p, b_if[l], g_hnorm[l], state_mlstm_c, state_mlstm_n, state_mlstm_m, l)
        st_s.append((c_new, n_new, m_new))
        mem_o = mem_attn_decode(p, qcol_a, g_mem_q[l], cache_mem, l)
        x = finish_layer(l, x, mix, mem_o)
    cmp_s, slc_s, win_s = kv_rows(x, g_kv, w['kv'], g_k_slc, g_k_win, 1, aug=False)
    assert (past + 1 - CMP_LEN) // CMP_STRIDE + 1 == past // CMP_STRIDE - 1
    ctx_s = compress_rows(cache_cmp_kv.reshape(n_pool, PAGE_SIZE, KV_ROW), cw, page_table)
    biases = nsa_decode_bias(rel_table, past, wb)
    cache_slc = cache_slc_kv.reshape(n_pool, PAGE_SIZE, KV_ROW)
    cache_win = cache_win_kv.reshape(b_s, wb, KV_ROW)
    for j in range(N_B_LAYERS):
        l = N_A_LAYERS + j
        p = norm_matmul(x, g_mix[l], w['in_b'][j])
        mix = nsa_decode(p, g_q[j], b_gate[j], ctx_s, page_table, cache_slc, slc_s, cache_win, win_s, biases)
        mem_o = mem_attn_decode(p, qcol_b, g_mem_q[l], cache_mem, l)
        x = finish_layer(l, x, mix, mem_o)
    y_s = x.reshape(b_s, 1, D_MODEL)

    rows5 = lambda r, b: r.reshape(b, -1, 2, NSA_KV_HEADS, HEAD_DIM)
    win_p5 = rows5(win_p, b_p)
    p_win = win_p5[:, s_p - min(WINDOW, s_p):]
    s_win = jnp.concatenate([cache_win_kv, rows5(win_s, b_s)], axis=1)[:, -wb:]
    stack = lambda sts, i: jnp.stack([st[i] for st in sts])
    return (y_p, y_s, stack(st_p, 0), stack(st_p, 1), stack(st_p, 2),
            mem_kv_p.reshape(DEPTH, b_p, m_tok, 2, MEM_HEADS, HEAD_DIM),
            rows5(cmp_p, b_p), rows5(slc_p, b_p), p_win,
            stack(st_s, 0), stack(st_s, 1), stack(st_s, 2),
            rows5(cmp_s, b_s), rows5(slc_s, b_s), s_win)
```
